```python
import jax, jax.numpy as jnp
from jax import lax
import numpy as np

D_MODEL = 2048
BATCH = 4
SEQ = 2048
DEPTH = 1

RET_HEADS = 8
RET_HEAD_DIM = D_MODEL // 16
RET_WIDTH = RET_HEADS * RET_HEAD_DIM
RET_CHUNK = 128
DSA_HEADS = 8
DSA_HEAD_DIM = D_MODEL // 16
DSA_WIDTH = DSA_HEADS * DSA_HEAD_DIM
IDX_HEADS = 16
IDX_DIM = 64
IDX_TOPK_MAX = 256
Q_BLOCK = 128
MEM_TOKENS = 256
MEM_HEADS = 4
MEM_HEAD_DIM = D_MODEL // 8
MEM_WIDTH = MEM_HEADS * MEM_HEAD_DIM
N_BRANCHES = 3
NEG_INF = -1e30
EPS = 1e-6

W_IN_SPLITS = (RET_WIDTH, RET_WIDTH, RET_WIDTH, RET_WIDTH,
               DSA_WIDTH, DSA_WIDTH, DSA_WIDTH, DSA_WIDTH,
               IDX_HEADS * IDX_DIM, IDX_DIM, IDX_HEADS,
               MEM_WIDTH, MEM_WIDTH,
               D_MODEL, D_MODEL, D_MODEL)
W_IN_COLS = sum(W_IN_SPLITS)

kernel_name = "hybrid_retention_dsa_memory_gated_block"


def rms_norm(x, w):
    xf = x.astype(jnp.float32)
    y = xf * lax.rsqrt(jnp.mean(xf * xf, axis=-1, keepdims=True) + EPS)
    return (y * w.astype(jnp.float32)).astype(x.dtype)


def alibi_slopes(n):
    return 2.0 ** (-8.0 * (jnp.arange(n, dtype=jnp.float32) + 1.0) / n)


def retention(q, k, v):
    B, H, L, dk = q.shape
    dv = v.shape[-1]
    C = RET_CHUNK
    N = L // C
    q, k, v = (t.astype(jnp.float32) for t in (q, k, v))
    log_g = jnp.log1p(-(2.0 ** (-5.0 - jnp.arange(H, dtype=jnp.float32))))
    idx = jnp.arange(C, dtype=jnp.float32)
    diff = idx[:, None] - idx[None, :]
    decay = jnp.where(diff >= 0, jnp.exp(log_g[:, None, None] * jnp.maximum(diff, 0.0)), 0.0)
    q_dec = jnp.exp(log_g[:, None] * (idx + 1.0))[None, :, :, None]
    k_dec = jnp.exp(log_g[:, None] * (C - 1.0 - idx))[None, :, :, None]
    chunk_dec = jnp.exp(log_g * C)[None, :, None, None]

    def to_chunks(t):
        return t.reshape(B, H, N, C, t.shape[-1]).transpose(2, 0, 1, 3, 4)

    def step(state, inp):
        qi, ki, vi = inp
        inner = jnp.einsum('bhid,bhjd->bhij', qi, ki) * decay[None]
        o = (jnp.einsum('bhij,bhjv->bhiv', inner, vi)
             + jnp.einsum('bhid,bhdv->bhiv', qi * q_dec, state))
        state = state * chunk_dec + jnp.einsum('bhjd,bhjv->bhdv', ki * k_dec, vi)
        return state, o

    s0 = jnp.zeros((B, H, dk, dv), jnp.float32)
    _, o = lax.scan(step, s0, (to_chunks(q), to_chunks(k), to_chunks(v)))
    return o.transpose(1, 2, 0, 3, 4).reshape(B, H, L, dv)


def head_group_norm(o, w):
    mu = jnp.mean(o, axis=-1, keepdims=True)
    var = jnp.mean(jnp.square(o - mu), axis=-1, keepdims=True)
    on = (o - mu) * lax.rsqrt(var + 1e-5)
    B, H, L, dv = o.shape
    return on.transpose(0, 2, 1, 3).reshape(B, L, H * dv) * w.astype(jnp.float32)


def dsa_attention(q, k, v, q_idx, k_idx, w_idx, top_k):
    B, L, H, D = q.shape
    nb = L // Q_BLOCK
    slopes = alibi_slopes(H)
    key_pos = jnp.arange(L)
    k_idx_f = k_idx.astype(jnp.float32)

    def blocks(t):
        return t.reshape((B, nb, Q_BLOCK) + t.shape[2:]).swapaxes(0, 1)

    def one_block(args):
        qb, qib, wb, qpos = args
        s = jnp.einsum('bqhe,bse->bqhs', qib.astype(jnp.float32), k_idx_f) * (IDX_DIM ** -0.5)
        score = jnp.einsum('bqh,bqhs->bqs', wb.astype(jnp.float32), jax.nn.relu(s))
        causal = key_pos[None, :] <= qpos[:, None]
        score = jnp.where(causal[None], score, NEG_INF)
        _, sel = lax.top_k(score, top_k)
        valid = sel <= qpos[None, :, None]
        k_sel = jax.vmap(lambda kb, ib: kb[ib])(k, sel)
        v_sel = jax.vmap(lambda vb, ib: vb[ib])(v, sel)
        logits = jnp.einsum('bqhd,bqkhd->bqhk', qb, k_sel).astype(jnp.float32) * (D ** -0.5)
        dist = (qpos[None, :, None] - sel).astype(jnp.float32)
        logits = logits - slopes[None, None, :, None] * dist[:, :, None, :]
        logits = jnp.where(valid[:, :, None, :], logits, NEG_INF)
        p = jax.nn.softmax(logits, axis=-1).astype(v.dtype)
        return jnp.einsum('bqhk,bqkhd->bqhd', p, v_sel)

    qpos_blocks = jnp.arange(L).reshape(nb, Q_BLOCK)
    out = lax.map(one_block, (blocks(q), blocks(q_idx), blocks(w_idx), qpos_blocks))
    return out.swapaxes(0, 1).reshape(B, L, H, D)


def memory_cross_attention(q, mem, mem_norm_w, w_mem_kv):
    B, L, _ = q.shape
    M = mem.shape[1]
    kv = rms_norm(mem, mem_norm_w) @ w_mem_kv
    km, vm = jnp.split(kv, 2, axis=-1)
    km = km.reshape(B, M, MEM_HEADS, MEM_HEAD_DIM)
    vm = vm.reshape(B, M, MEM_HEADS, MEM_HEAD_DIM)
    qh = q.reshape(B, L, MEM_HEADS, MEM_HEAD_DIM)
    logits = jnp.einsum('blhd,bmhd->bhlm', qh, km).astype(jnp.float32) * (MEM_HEAD_DIM ** -0.5)
    p = jax.nn.softmax(logits, axis=-1).astype(vm.dtype)
    return jnp.einsum('bhlm,bmhd->blhd', p, vm).reshape(B, L, MEM_WIDTH)


def hybrid_layer(x, mem, pre_norm_w, w_in, ret_gn_w, mem_norm_w, w_mem_kv,
                 w_up_ret, w_up_dsa, w_up_mem, w_o, post_norm_w):
    B, L, _ = x.shape
    top_k = min(IDX_TOPK_MAX, L // 4)
    h = rms_norm(x, pre_norm_w)
    z = h @ w_in
    points = np.cumsum(W_IN_SPLITS)[:-1].tolist()
    (rq, rk, rv, rg, aq, ak, av, ag, iq, ik, iw, mq, mg,
     g_ret, g_dsa, g_mem) = jnp.split(z, points, axis=-1)

    def heads_first(t, n):
        return t.reshape(B, L, n, -1).transpose(0, 2, 1, 3)
    o_ret = retention(heads_first(rq, RET_HEADS),
                      heads_first(rk, RET_HEADS) * (RET_HEAD_DIM ** -0.5),
                      heads_first(rv, RET_HEADS))
    y_ret = jax.nn.silu(rg) * head_group_norm(o_ret, ret_gn_w).astype(x.dtype)

    o_dsa = dsa_attention(aq.reshape(B, L, DSA_HEADS, DSA_HEAD_DIM),
                          ak.reshape(B, L, DSA_HEADS, DSA_HEAD_DIM),
                          av.reshape(B, L, DSA_HEADS, DSA_HEAD_DIM),
                          iq.reshape(B, L, IDX_HEADS, IDX_DIM), ik,
                          iw * (IDX_HEADS ** -0.5), top_k)
    y_dsa = jax.nn.silu(ag) * o_dsa.reshape(B, L, DSA_WIDTH)

    y_mem = jax.nn.silu(mg) * memory_cross_attention(mq, mem, mem_norm_w, w_mem_kv)

    merged = (jax.nn.sigmoid(g_ret) * (y_ret @ w_up_ret)
              + jax.nn.sigmoid(g_dsa) * (y_dsa @ w_up_dsa)
              + jax.nn.sigmoid(g_mem) * (y_mem @ w_up_mem))
    out = merged @ w_o
    return x + rms_norm(out, post_norm_w)


def setup_inputs(seed: int = 0) -> dict:
    key = jax.random.key(seed)
    ks = jax.random.split(key, 14)
    f32 = jnp.float32

    def nrm(k, shape, fan_in):
        return jax.random.normal(k, shape, f32) * (fan_in ** -0.5)

    def gain(k, shape):
        return 1.0 + 0.02 * jax.random.normal(k, shape, f32)

    return {
        "x": jax.random.normal(ks[0], (BATCH, SEQ, D_MODEL), f32),
        "mem": jax.random.normal(ks[1], (BATCH, MEM_TOKENS, D_MODEL), f32),
        "pre_norm_w": gain(ks[2], (DEPTH, D_MODEL)),
        "w_in": nrm(ks[3], (DEPTH, D_MODEL, W_IN_COLS), D_MODEL),
        "ret_gn_w": gain(ks[4], (DEPTH, RET_WIDTH)),
        "mem_norm_w": gain(ks[5], (DEPTH, D_MODEL)),
        "w_mem_kv": nrm(ks[6], (DEPTH, D_MODEL, 2 * MEM_WIDTH), D_MODEL),
        "w_up_ret": nrm(ks[7], (DEPTH, RET_WIDTH, D_MODEL), RET_WIDTH),
        "w_up_dsa": nrm(ks[8], (DEPTH, DSA_WIDTH, D_MODEL), DSA_WIDTH),
        "w_up_mem": nrm(ks[9], (DEPTH, MEM_WIDTH, D_MODEL), MEM_WIDTH),
        "w_o": nrm(ks[10], (DEPTH, D_MODEL, D_MODEL), D_MODEL),
        "post_norm_w": gain(ks[11], (DEPTH, D_MODEL)),
    }


def reference(x, mem, pre_norm_w, w_in, ret_gn_w, mem_norm_w, w_mem_kv,
              w_up_ret, w_up_dsa, w_up_mem, w_o, post_norm_w):
    for layer in range(DEPTH):
        x = hybrid_layer(x, mem, pre_norm_w[layer], w_in[layer], ret_gn_w[layer],
                         mem_norm_w[layer], w_mem_kv[layer], w_up_ret[layer],
                         w_up_dsa[layer], w_up_mem[layer], w_o[layer], post_norm_w[layer])
    return x
```

```python
import functools

import jax
import jax.numpy as jnp
from jax import lax
from jax.experimental import pallas as pl
from jax.experimental.pallas import tpu as pltpu

F32 = jnp.float32
BF16 = jnp.bfloat16
I32 = jnp.int32

LANE = 128
D_MODEL = 2048
RET_HEADS = 8
HEAD_DIM = 128
DSA_HEADS = 8
IDX_HEADS = 16
IDX_DIM = 64
IDX_TOPK_MAX = 256
MEM_HEADS = 4
MEM_HEAD_DIM = 256
NEG_INF = -1e30
EPS = 1e-6
INT_MIN = -2**31

CB_RQ, CB_RK, CB_RV, CB_RG = 0, 8, 16, 24
CB_AQ, CB_AK, CB_AV, CB_AG = 32, 40, 48, 56
CB_IQ = 64
CB_MQ, CB_MG = 72, 80
CB_GRET, CB_GDSA, CB_GMEM = 88, 104, 120
CB_IKW = 136
N_COL_BLOCKS = 138
Z_COLS = N_COL_BLOCKS * LANE

PROJ_TM = 2048
PROJ_TN = 768
PROJ_ROWS = 512
VMEM_LIMIT = 56 * 1024 * 1024

NT = (((1,), (1,)), ((), ()))
TN = (((0,), (0,)), ((), ()))


def _sigmoid(v):
    return 1.0 / (1.0 + jnp.exp(-v))


def _silu(v):
    return v * _sigmoid(v)


def _rmsnorm_kernel(x_ref, w_ref, o_ref):
    x = x_ref[...]
    ms = jnp.mean(x * x, axis=-1, keepdims=True)
    o_ref[...] = (x * lax.rsqrt(ms + EPS) * w_ref[...]).astype(o_ref.dtype)


def _rmsnorm(x2d, w, tm=512):
    m, d = x2d.shape
    return pl.pallas_call(
        _rmsnorm_kernel,
        grid=(m // tm,),
        in_specs=[pl.BlockSpec((tm, d), lambda i: (i, 0)),
                  pl.BlockSpec((1, d), lambda i: (0, 0))],
        out_specs=pl.BlockSpec((tm, d), lambda i: (i, 0)),
        out_shape=jax.ShapeDtypeStruct((m, d), BF16),
        compiler_params=pltpu.CompilerParams(dimension_semantics=("arbitrary",)),
        name="rmsnorm_bf16",
    )(x2d, w.reshape(1, d))


def _proj_kernel(h_ref, w_ref, z_ref):
    w = w_ref[...]
    for r in range(0, h_ref.shape[0], PROJ_ROWS):
        z_ref[r:r + PROJ_ROWS, :] = jnp.dot(
            h_ref[r:r + PROJ_ROWS, :], w, preferred_element_type=F32).astype(z_ref.dtype)


def _project(h, wp):
    m, d = h.shape
    n = wp.shape[1]
    tm = min(PROJ_TM, m)
    return pl.pallas_call(
        _proj_kernel,
        grid=(m // tm, n // PROJ_TN),
        in_specs=[pl.BlockSpec((tm, d), lambda i, j: (i, 0)),
                  pl.BlockSpec((d, PROJ_TN), lambda i, j: (0, j))],
        out_specs=pl.BlockSpec((tm, PROJ_TN), lambda i, j: (i, j)),
        out_shape=jax.ShapeDtypeStruct((m, n), BF16),
        compiler_params=pltpu.CompilerParams(
            dimension_semantics=("arbitrary", "arbitrary"), vmem_limit_bytes=VMEM_LIMIT),
        name="in_proj",
    )(h, wp)


RET_C = 256


def _retention_kernel(lg_ref, q_ref, k_ref, v_ref, g_ref, gw_ref, o_ref):
    h = pl.program_id(1)
    lg = lg_ref[h]
    c = RET_C
    seq = q_ref.shape[0]
    scale = HEAD_DIM ** -0.5
    ri = lax.broadcasted_iota(I32, (c, c), 0)
    ci = lax.broadcasted_iota(I32, (c, c), 1)
    diff = (ri - ci).astype(F32)
    decay = jnp.where(diff >= 0, jnp.exp(lg * jnp.maximum(diff, 0.0)), 0.0) * scale
    idx = lax.broadcasted_iota(I32, (c, 1), 0).astype(F32)
    q_dec = jnp.exp(lg * (idx + 1.0))
    k_dec = jnp.exp(lg * (c - 1.0 - idx)) * scale
    chunk_dec = jnp.exp(lg * jnp.full((1, HEAD_DIM), float(c), F32))
    gw = gw_ref[...]

    def body(i, state):
        r = pl.multiple_of(i * c, c)
        qi = q_ref[pl.ds(r, c), :]
        ki = k_ref[pl.ds(r, c), :]
        vi = v_ref[pl.ds(r, c), :]
        inner = lax.dot_general(qi, ki, NT, preferred_element_type=F32) * decay
        o = (jnp.dot(inner.astype(BF16), vi, preferred_element_type=F32)
             + jnp.dot(qi, state.astype(BF16), preferred_element_type=F32) * q_dec)
        vs = (vi.astype(F32) * k_dec).astype(BF16)
        state = state * chunk_dec + lax.dot_general(ki, vs, TN, preferred_element_type=F32)
        mu = jnp.mean(o, axis=-1, keepdims=True)
        oc = o - mu
        var = jnp.mean(oc * oc, axis=-1, keepdims=True)
        on = oc * lax.rsqrt(var + 1e-5) * gw
        g = g_ref[pl.ds(r, c), :].astype(F32)
        o_ref[pl.ds(r, c), :] = (_silu(g) * on).astype(o_ref.dtype)
        return state

    lax.fori_loop(0, seq // c, body, jnp.zeros((HEAD_DIM, HEAD_DIM), F32))


def _retention(z, log_g, gn_w, batch, seq):
    blk = lambda off: pl.BlockSpec((seq, HEAD_DIM), lambda b, h: (b, off + h))
    return pl.pallas_call(
        _retention_kernel,
        grid=(batch, RET_HEADS),
        in_specs=[pl.BlockSpec(memory_space=pltpu.SMEM),
                  blk(CB_RQ), blk(CB_RK), blk(CB_RV), blk(CB_RG),
                  pl.BlockSpec((1, HEAD_DIM), lambda b, h: (0, h))],
        out_specs=pl.BlockSpec((seq, HEAD_DIM), lambda b, h: (b, h)),
        out_shape=jax.ShapeDtypeStruct((batch * seq, RET_HEADS * HEAD_DIM), BF16),
        compiler_params=pltpu.CompilerParams(
            dimension_semantics=("arbitrary", "arbitrary"), vmem_limit_bytes=VMEM_LIMIT),
        name="retention",
    )(log_g, z, z, z, z, gn_w.reshape(1, -1))


DSA_T = 256


def _dsa_kernel(iq_ref, ikw_ref, aq_ref, ak_ref, av_ref, ag_ref, o_ref,
                ika_ref, ikb_ref, wb_ref, keys_ref, bias_ref, logit_ref, *, top_k):
    t = DSA_T
    qb = pl.program_id(1)
    nk = qb + 1
    q0 = pl.multiple_of(qb * t, t)

    @pl.when(qb == 0)
    def _():
        ikw = ikw_ref[...].astype(F32)
        lane = lax.broadcasted_iota(I32, ikw.shape, 1)
        a = jnp.where(lane < IDX_DIM, ikw, 0.0)
        ika_ref[...] = a.astype(BF16)
        ikb_ref[...] = pltpu.roll(a, IDX_DIM, axis=1).astype(BF16)

    w = ikw_ref[pl.ds(q0, t), :].astype(F32) * ((IDX_DIM ** -0.5) * (IDX_HEADS ** -0.5))
    for hh in range(IDX_HEADS):
        wb_ref[hh] = jnp.broadcast_to(w[:, IDX_DIM + hh:IDX_DIM + hh + 1], (t, t))

    row = lax.broadcasted_iota(I32, (t, t), 0)
    col = lax.broadcasted_iota(I32, (t, t), 1)

    def score_tile(kt, carry):
        k0 = pl.multiple_of(kt * t, t)
        ka = ika_ref[pl.ds(k0, t), :]
        kb = ikb_ref[pl.ds(k0, t), :]
        acc = jnp.zeros((t, t), F32)
        for j in range(IDX_HEADS // 2):
            qj = iq_ref[:, j * LANE:(j + 1) * LANE]
            sa = lax.dot_general(qj, ka, NT, preferred_element_type=F32)
            acc = acc + jnp.maximum(sa, 0.0) * wb_ref[2 * j]
            sb = lax.dot_general(qj, kb, NT, preferred_element_type=F32)
            acc = acc + jnp.maximum(sb, 0.0) * wb_ref[2 * j + 1]
        bits = lax.bitcast_convert_type(acc, I32)
        key = bits ^ ((bits >> 31) & 0x7FFFFFFF)
        causal = (col + k0) <= (row + q0)
        keys_ref[kt] = jnp.where(causal, key, INT_MIN)
        return carry

    lax.fori_loop(0, nk, score_tile, 0)

    def bit_step(i, thr):
        bit = lax.shift_left(jnp.int32(1), 31 - i)
        cand = thr ^ bit

        def count_tile(kt, cnt):
            ge = jnp.where(keys_ref[kt] >= cand, 1, 0)
            return cnt + ge[:, :LANE] + ge[:, LANE:]

        cnt = lax.fori_loop(0, nk, count_tile, jnp.zeros((t, LANE), I32))
        total = jnp.sum(cnt, axis=-1, keepdims=True)
        return jnp.where(total >= top_k, cand, thr)

    thr = lax.fori_loop(0, 32, bit_step, jnp.full((t, 1), INT_MIN, I32))
    thr = jnp.maximum(thr, INT_MIN + 1)

    def bias_tile(kt, carry):
        bias_ref[kt] = jnp.where(keys_ref[kt] >= thr, 0.0, NEG_INF)
        return carry

    lax.fori_loop(0, nk, bias_tile, 0)

    scale = HEAD_DIM ** -0.5
    lane_pos = lax.broadcasted_iota(I32, (1, t), 1)
    for h in range(DSA_HEADS):
        slope = 2.0 ** (-8.0 * (h + 1) / DSA_HEADS)
        hs = slice(h * HEAD_DIM, (h + 1) * HEAD_DIM)
        qh = aq_ref[:, hs]

        def logit_tile(kt, m):
            k0 = pl.multiple_of(kt * t, t)
            s = lax.dot_general(qh, ak_ref[pl.ds(k0, t), hs], NT, preferred_element_type=F32)
            kbias = (lane_pos + (k0 - q0)).astype(F32) * slope
            lg = s * scale + kbias + bias_ref[kt]
            logit_ref[kt] = lg
            return jnp.maximum(m, jnp.maximum(lg[:, :LANE], lg[:, LANE:]))

        m = lax.fori_loop(0, nk, logit_tile, jnp.full((t, LANE), NEG_INF, F32))
        m = jnp.max(m, axis=-1, keepdims=True)

        def pv_tile(kt, carry):
            l, acc = carry
            k0 = pl.multiple_of(kt * t, t)
            p = jnp.exp(logit_ref[kt] - m)
            acc = acc + jnp.dot(p.astype(BF16), av_ref[pl.ds(k0, t), hs],
                                preferred_element_type=F32)
            return l + p[:, :LANE] + p[:, LANE:], acc

        l, acc = lax.fori_loop(0, nk, pv_tile,
                               (jnp.zeros((t, LANE), F32), jnp.zeros((t, HEAD_DIM), F32)))
        l = jnp.sum(l, axis=-1, keepdims=True)
        g = ag_ref[:, hs].astype(F32)
        o_ref[:, hs] = (_silu(g) * (acc / l)).astype(o_ref.dtype)


def _dsa(z, batch, seq, top_k):
    t = DSA_T
    nq = seq // t
    width = DSA_HEADS * HEAD_DIM
    wblk = width // LANE
    qspec = lambda off: pl.BlockSpec((t, width), lambda b, i: (b * nq + i, off // wblk))
    kvspec = lambda off: pl.BlockSpec((seq, width), lambda b, i: (b, off // wblk))
    return pl.pallas_call(
        functools.partial(_dsa_kernel, top_k=top_k),
        grid=(batch, nq),
        in_specs=[qspec(CB_IQ),
                  pl.BlockSpec((seq, LANE), lambda b, i: (b, CB_IKW)),
                  qspec(CB_AQ), kvspec(CB_AK), kvspec(CB_AV), qspec(CB_AG)],
        out_specs=pl.BlockSpec((t, width), lambda b, i: (b * nq + i, 0)),
        out_shape=jax.ShapeDtypeStruct((batch * seq, width), BF16),
        scratch_shapes=[pltpu.VMEM((seq, LANE), BF16),
                        pltpu.VMEM((seq, LANE), BF16),
                        pltpu.VMEM((IDX_HEADS, t, t), F32),
                        pltpu.VMEM((nq, t, t), I32),
                        pltpu.VMEM((nq, t, t), F32),
                        pltpu.VMEM((nq, t, t), F32)],
        compiler_params=pltpu.CompilerParams(
            dimension_semantics=("arbitrary", "arbitrary"), vmem_limit_bytes=VMEM_LIMIT),
        name="dsa",
    )(z, z, z, z, z, z)


def _memkv_kernel(mem_ref, nw_ref, w_ref, o_ref):
    x = mem_ref[...]
    ms = jnp.mean(x * x, axis=-1, keepdims=True)
    hn = (x * lax.rsqrt(ms + EPS) * nw_ref[...]).astype(BF16)
    o_ref[...] = jnp.dot(hn, w_ref[...], preferred_element_type=F32).astype(o_ref.dtype)


def _memkv(mem2d, norm_w, w_kv, tm=256, tn=1024):
    m, d = mem2d.shape
    n = w_kv.shape[1]
    return pl.pallas_call(
        _memkv_kernel,
        grid=(n // tn, m // tm),
        in_specs=[pl.BlockSpec((tm, d), lambda j, i: (i, 0)),
                  pl.BlockSpec((1, d), lambda j, i: (0, 0)),
                  pl.BlockSpec((d, tn), lambda j, i: (0, j))],
        out_specs=pl.BlockSpec((tm, tn), lambda j, i: (i, j)),
        out_shape=jax.ShapeDtypeStruct((m, n), BF16),
        compiler_params=pltpu.CompilerParams(
            dimension_semantics=("arbitrary", "arbitrary"), vmem_limit_bytes=VMEM_LIMIT),
        name="mem_kv",
    )(mem2d, norm_w.reshape(1, d), w_kv)


def _memattn_kernel(q_ref, g_ref, k_ref, v_ref, o_ref):
    scale = MEM_HEAD_DIM ** -0.5
    for h in range(MEM_HEADS):
        hs = slice(h * MEM_HEAD_DIM, (h + 1) * MEM_HEAD_DIM)
        s = lax.dot_general(q_ref[:, hs], k_ref[:, hs], NT, preferred_element_type=F32) * scale
        p = jnp.exp(s - jnp.max(s, axis=-1, keepdims=True))
        l = jnp.sum(p, axis=-1, keepdims=True)
        o = jnp.dot(p.astype(BF16), v_ref[:, hs], preferred_element_type=F32) / l
        o_ref[:, hs] = (_silu(g_ref[:, hs].astype(F32)) * o).astype(o_ref.dtype)


def _memattn(z, kv, batch, seq, mem_tokens, tl=512):
    width = MEM_HEADS * MEM_HEAD_DIM
    wblk = width // LANE
    nl = seq // tl
    return pl.pallas_call(
        _memattn_kernel,
        grid=(batch, nl),
        in_specs=[pl.BlockSpec((tl, width), lambda b, i: (b * nl + i, CB_MQ // wblk)),
                  pl.BlockSpec((tl, width), lambda b, i: (b * nl + i, CB_MG // wblk)),
                  pl.BlockSpec((mem_tokens, width), lambda b, i: (b, 0)),
                  pl.BlockSpec((mem_tokens, width), lambda b, i: (b, 1))],
        out_specs=pl.BlockSpec((tl, width), lambda b, i: (b * nl + i, 0)),
        out_shape=jax.ShapeDtypeStruct((batch * seq, width), BF16),
        compiler_params=pltpu.CompilerParams(
            dimension_semantics=("arbitrary", "arbitrary"), vmem_limit_bytes=VMEM_LIMIT),
        name="mem_attn",
    )(z, z, kv, kv)


def _merge_kernel(yr_ref, yd_ref, ym_ref, wr_ref, wd_ref, wm_ref, gr_ref, gd_ref, gm_ref, o_ref):
    def branch(y_ref, w_ref, g_ref):
        up = jnp.dot(y_ref[...], w_ref[...], preferred_element_type=F32)
        return _sigmoid(g_ref[...].astype(F32)) * up

    merged = (branch(yr_ref, wr_ref, gr_ref) + branch(yd_ref, wd_ref, gd_ref)
              + branch(ym_ref, wm_ref, gm_ref))
    o_ref[...] = merged.astype(o_ref.dtype)


def _merge(y_ret, y_dsa, y_mem, w_ret, w_dsa, w_mem, z, tm=512, tn=1024):
    m, kdim = y_ret.shape
    n = w_ret.shape[1]
    gblk = tn // LANE
    yspec = pl.BlockSpec((tm, kdim), lambda j, i: (i, 0))
    wspec = pl.BlockSpec((kdim, tn), lambda j, i: (0, j))
    gspec = lambda off: pl.BlockSpec((tm, tn), lambda j, i: (i, off // gblk + j))
    return pl.pallas_call(
        _merge_kernel,
        grid=(n // tn, m // tm),
        in_specs=[yspec, yspec, yspec, wspec, wspec, wspec,
                  gspec(CB_GRET), gspec(CB_GDSA), gspec(CB_GMEM)],
        out_specs=pl.BlockSpec((tm, tn), lambda j, i: (i, j)),
        out_shape=jax.ShapeDtypeStruct((m, n), BF16),
        compiler_params=pltpu.CompilerParams(
            dimension_semantics=("arbitrary", "arbitrary"), vmem_limit_bytes=VMEM_LIMIT),
        name="merge",
    )(y_ret, y_dsa, y_mem, w_ret, w_dsa, w_mem, z, z, z)


def _out_kernel(m_ref, w_ref, x_ref, nw_ref, o_ref):
    out = jnp.dot(m_ref[...], w_ref[...], preferred_element_type=F32)
    ms = jnp.mean(out * out, axis=-1, keepdims=True)
    o_ref[...] = x_ref[...] + out * lax.rsqrt(ms + EPS) * nw_ref[...]


def _out_proj(merged, w_o, x2d, post_w, tm=512):
    m, d = x2d.shape
    return pl.pallas_call(
        _out_kernel,
        grid=(m // tm,),
        in_specs=[pl.BlockSpec((tm, d), lambda i: (i, 0)),
                  pl.BlockSpec((d, d), lambda i: (0, 0)),
                  pl.BlockSpec((tm, d), lambda i: (i, 0)),
                  pl.BlockSpec((1, d), lambda i: (0, 0))],
        out_specs=pl.BlockSpec((tm, d), lambda i: (i, 0)),
        out_shape=jax.ShapeDtypeStruct((m, d), F32),
        compiler_params=pltpu.CompilerParams(
            dimension_semantics=("arbitrary",), vmem_limit_bytes=VMEM_LIMIT),
        name="out_proj_postnorm",
    )(merged, w_o, x2d, post_w.reshape(1, d))


def _pack_w_in(w):
    d = w.shape[0]
    idx_end = (CB_IQ + IDX_HEADS * IDX_DIM // LANE) * LANE
    ikw_end = idx_end + IDX_DIM + IDX_HEADS
    pad = Z_COLS - w.shape[1]
    return jnp.concatenate(
        [w[:, :idx_end], w[:, ikw_end:], w[:, idx_end:ikw_end], jnp.zeros((d, pad), w.dtype)],
        axis=1).astype(BF16)


def _layer(x, mem, pre_norm_w, w_in, ret_gn_w, mem_norm_w, w_mem_kv,
           w_up_ret, w_up_dsa, w_up_mem, w_o, post_norm_w):
    batch, seq, d = x.shape
    mem_tokens = mem.shape[1]
    top_k = min(IDX_TOPK_MAX, seq // 4)
    x2d = x.reshape(batch * seq, d)

    h = _rmsnorm(x2d, pre_norm_w)
    z = _project(h, _pack_w_in(w_in))

    log_g = jnp.log1p(-(2.0 ** (-5.0 - jnp.arange(RET_HEADS, dtype=F32))))
    y_ret = _retention(z, log_g, ret_gn_w, batch, seq)
    y_dsa = _dsa(z, batch, seq, top_k)
    kv = _memkv(mem.reshape(batch * mem_tokens, d), mem_norm_w, w_mem_kv.astype(BF16))
    y_mem = _memattn(z, kv, batch, seq, mem_tokens)

    merged = _merge(y_ret, y_dsa, y_mem, w_up_ret.astype(BF16), w_up_dsa.astype(BF16),
                    w_up_mem.astype(BF16), z)
    out = _out_proj(merged, w_o.astype(BF16), x2d, post_norm_w)
    return out.reshape(batch, seq, d)


def kernel(x, mem, pre_norm_w, w_in, ret_gn_w, mem_norm_w, w_mem_kv, w_up_ret, w_up_dsa,
           w_up_mem, w_o, post_norm_w):
    for layer in range(w_in.shape[0]):
        x = _layer(x, mem, pre_norm_w[layer], w_in[layer], ret_gn_w[layer], mem_norm_w[layer],
                   w_mem_kv[layer], w_up_ret[layer], w_up_dsa[layer], w_up_mem[layer],
                   w_o[layer], post_norm_w[layer])
    return x
```

```python
import functools

import jax
import jax.numpy as jnp
from jax import lax
from jax.experimental import pallas as pl
from jax.experimental.pallas import tpu as pltpu

F32 = jnp.float32
BF16 = jnp.bfloat16
I32 = jnp.int32

LANE = 128
D_MODEL = 2048
RET_HEADS = 8
HEAD_DIM = 128
DSA_HEADS = 8
IDX_HEADS = 16
IDX_DIM = 64
IDX_TOPK_MAX = 256
MEM_HEADS = 4
MEM_HEAD_DIM = 256
NEG_INF = -1e30
EPS = 1e-6
INT_MIN = -2**31

CB_RQ, CB_RK, CB_RV, CB_RG = 0, 8, 16, 24
CB_AQ, CB_AK, CB_AV, CB_AG = 32, 40, 48, 56
CB_IQ = 64
CB_MQ, CB_MG = 72, 80
CB_GRET, CB_GDSA, CB_GMEM = 88, 104, 120
CB_IKW = 136
N_COL_BLOCKS = 138
Z_COLS = N_COL_BLOCKS * LANE

PROJ_TM = 2048
PROJ_TN = 768
PROJ_ROWS = 512
VMEM_LIMIT = 56 * 1024 * 1024

NT = (((1,), (1,)), ((), ()))
TN = (((0,), (0,)), ((), ()))


def _sigmoid(v):
    return 1.0 / (1.0 + jnp.exp(-v))


def _silu(v):
    return v * _sigmoid(v)


def _rmsnorm_kernel(x_ref, w_ref, o_ref):
    x = x_ref[...]
    ms = jnp.mean(x * x, axis=-1, keepdims=True)
    o_ref[...] = (x * lax.rsqrt(ms + EPS) * w_ref[...]).astype(o_ref.dtype)


def _rmsnorm(x2d, w, tm=512):
    m, d = x2d.shape
    return pl.pallas_call(
        _rmsnorm_kernel,
        grid=(m // tm,),
        in_specs=[pl.BlockSpec((tm, d), lambda i: (i, 0)),
                  pl.BlockSpec((1, d), lambda i: (0, 0))],
        out_specs=pl.BlockSpec((tm, d), lambda i: (i, 0)),
        out_shape=jax.ShapeDtypeStruct((m, d), BF16),
        compiler_params=pltpu.CompilerParams(dimension_semantics=("arbitrary",)),
        name="rmsnorm_bf16",
    )(x2d, w.reshape(1, d))


def _proj_kernel(h_ref, w_ref, z_ref):
    w = w_ref[...]
    for r in range(0, h_ref.shape[0], PROJ_ROWS):
        z_ref[r:r + PROJ_ROWS, :] = jnp.dot(
            h_ref[r:r + PROJ_ROWS, :], w, preferred_element_type=F32).astype(z_ref.dtype)


def _project(h, wp):
    m, d = h.shape
    n = wp.shape[1]
    tm = min(PROJ_TM, m)
    return pl.pallas_call(
        _proj_kernel,
        grid=(m // tm, n // PROJ_TN),
        in_specs=[pl.BlockSpec((tm, d), lambda i, j: (i, 0)),
                  pl.BlockSpec((d, PROJ_TN), lambda i, j: (0, j))],
        out_specs=pl.BlockSpec((tm, PROJ_TN), lambda i, j: (i, j)),
        out_shape=jax.ShapeDtypeStruct((m, n), BF16),
        compiler_params=pltpu.CompilerParams(
            dimension_semantics=("arbitrary", "arbitrary"), vmem_limit_bytes=VMEM_LIMIT),
        name="in_proj",
    )(h, wp)


RET_C = 256


def _retention_kernel(lg_ref, q_ref, k_ref, v_ref, g_ref, gw_ref, o_ref):
    h = pl.program_id(1)
    lg = lg_ref[h]
    c = RET_C
    seq = q_ref.shape[0]
    scale = HEAD_DIM ** -0.5
    ri = lax.broadcasted_iota(I32, (c, c), 0)
    ci = lax.broadcasted_iota(I32, (c, c), 1)
    diff = (ri - ci).astype(F32)
    decay = jnp.where(diff >= 0, jnp.exp(lg * jnp.maximum(diff, 0.0)), 0.0) * scale
    idx = lax.broadcasted_iota(I32, (c, 1), 0).astype(F32)
    q_dec = jnp.exp(lg * (idx + 1.0))
    k_dec = jnp.exp(lg * (c - 1.0 - idx)) * scale
    chunk_dec = jnp.exp(lg * jnp.full((1, HEAD_DIM), float(c), F32))
    gw = gw_ref[...]

    def body(i, state):
        r = pl.multiple_of(i * c, c)
        qi = q_ref[pl.ds(r, c), :]
        ki = k_ref[pl.ds(r, c), :]
        vi = v_ref[pl.ds(r, c), :]
        inner = lax.dot_general(qi, ki, NT, preferred_element_type=F32) * decay
        o = (jnp.dot(inner.astype(BF16), vi, preferred_element_type=F32)
             + jnp.dot(qi, state.astype(BF16), preferred_element_type=F32) * q_dec)
        vs = (vi.astype(F32) * k_dec).astype(BF16)
        state = state * chunk_dec + lax.dot_general(ki, vs, TN, preferred_element_type=F32)
        mu = jnp.mean(o, axis=-1, keepdims=True)
        oc = o - mu
        var = jnp.mean(oc * oc, axis=-1, keepdims=True)
        on = oc * lax.rsqrt(var + 1e-5) * gw
        g = g_ref[pl.ds(r, c), :].astype(F32)
        o_ref[pl.ds(r, c), :] = (_silu(g) * on).astype(o_ref.dtype)
        return state

    lax.fori_loop(0, seq // c, body, jnp.zeros((HEAD_DIM, HEAD_DIM), F32))


def _retention(z, log_g, gn_w, batch, seq):
    blk = lambda off: pl.BlockSpec((seq, HEAD_DIM), lambda b, h: (b, off + h))
    return pl.pallas_call(
        _retention_kernel,
        grid=(batch, RET_HEADS),
        in_specs=[pl.BlockSpec(memory_space=pltpu.SMEM),
                  blk(CB_RQ), blk(CB_RK), blk(CB_RV), blk(CB_RG),
                  pl.BlockSpec((1, HEAD_DIM), lambda b, h: (0, h))],
        out_specs=pl.BlockSpec((seq, HEAD_DIM), lambda b, h: (b, h)),
        out_shape=jax.ShapeDtypeStruct((batch * seq, RET_HEADS * HEAD_DIM), BF16),
        compiler_params=pltpu.CompilerParams(
            dimension_semantics=("arbitrary", "arbitrary"), vmem_limit_bytes=VMEM_LIMIT),
        name="retention",
    )(log_g, z, z, z, z, gn_w.reshape(1, -1))


DSA_T = 256


SUBLANES = 8


def _fold_rows(x, op):
    while x.shape[0] > SUBLANES:
        half = x.shape[0] // 2
        x = op(x[:half], x[half:])
    return x


def _dsa_kernel(iq_ref, ikw_ref, aq_ref, ak_ref, av_ref, ag_ref, o_ref,
                ika_ref, ikb_ref, vt_ref, wt_ref, keys_ref, bias_ref, logit_ref,
                m_ref, l_ref, acc_ref, *, top_k):
    t = DSA_T
    qb = pl.program_id(1)
    nk = qb + 1
    q0 = pl.multiple_of(qb * t, t)
    nq = vt_ref.shape[0]

    @pl.when(qb == 0)
    def _():
        ikw = ikw_ref[...].astype(F32)
        lane = lax.broadcasted_iota(I32, ikw.shape, 1)
        a = jnp.where(lane < IDX_DIM, ikw, 0.0)
        ika_ref[...] = a.astype(BF16)
        ikb_ref[...] = pltpu.roll(a, IDX_DIM, axis=1).astype(BF16)

        def vt_tile(kt, carry):
            k0 = pl.multiple_of(kt * t, t)
            for h in range(DSA_HEADS):
                hs = slice(h * HEAD_DIM, (h + 1) * HEAD_DIM)
                vt_ref[kt, hs, :] = av_ref[pl.ds(k0, t), hs].astype(F32).T.astype(BF16)
            return carry

        lax.fori_loop(0, nq, vt_tile, 0)

    wt_ref[...] = ikw_ref[pl.ds(q0, t), :].astype(F32).T

    key_row = lax.broadcasted_iota(I32, (t, t), 0)
    qry_col = lax.broadcasted_iota(I32, (t, t), 1)

    def score_tile(kt, carry):
        k0 = pl.multiple_of(kt * t, t)
        ka = ika_ref[pl.ds(k0, t), :]
        kb = ikb_ref[pl.ds(k0, t), :]
        acc = jnp.zeros((t, t), F32)
        for j in range(IDX_HEADS // 2):
            qj = iq_ref[:, j * LANE:(j + 1) * LANE]
            for half, kx in enumerate((ka, kb)):
                hh = IDX_DIM + 2 * j + half
                s = lax.dot_general(kx, qj, NT, preferred_element_type=F32)
                acc = acc + jnp.maximum(s, 0.0) * wt_ref[hh:hh + 1, :]
        bits = lax.bitcast_convert_type(acc, I32)
        key = bits ^ ((bits >> 31) & 0x7FFFFFFF)
        causal = (key_row + k0) <= (qry_col + q0)
        keys_ref[kt] = jnp.where(causal, key, INT_MIN)
        return carry

    lax.fori_loop(0, nk, score_tile, 0)

    def bit_step(i, thr):
        bit = lax.shift_left(jnp.int32(1), 31 - i)
        cand = thr ^ bit

        def count_tile(kt, cnt):
            return cnt + _fold_rows(jnp.where(keys_ref[kt] >= cand, 1, 0), jnp.add)

        cnt = lax.fori_loop(0, nk, count_tile, jnp.zeros((SUBLANES, t), I32))
        total = jnp.sum(cnt, axis=0, keepdims=True)
        return jnp.where(total >= top_k, cand, thr)

    thr = lax.fori_loop(0, 32, bit_step, jnp.full((1, t), INT_MIN, I32))
    thr = jnp.maximum(thr, INT_MIN + 1)

    def bias_tile(kt, carry):
        bias_ref[kt] = jnp.where(keys_ref[kt] >= thr, 0.0, NEG_INF)
        return carry

    lax.fori_loop(0, nk, bias_tile, 0)

    m_ref[...] = jnp.full(m_ref.shape, NEG_INF, F32)
    l_ref[...] = jnp.zeros(l_ref.shape, F32)
    acc_ref[...] = jnp.zeros(acc_ref.shape, F32)
    log2e = 1.4426950408889634
    scale = (HEAD_DIM ** -0.5) * log2e

    def logit_tile(kt, carry):
        k0 = pl.multiple_of(kt * t, t)
        bias = bias_ref[kt]
        kpos = (key_row + (k0 - q0)).astype(F32)
        for h in range(DSA_HEADS):
            slope = (2.0 ** (-8.0 * (h + 1) / DSA_HEADS)) * log2e
            hs = slice(h * HEAD_DIM, (h + 1) * HEAD_DIM)
            s = lax.dot_general(ak_ref[pl.ds(k0, t), hs], aq_ref[:, hs], NT,
                                preferred_element_type=F32)
            lg = s * scale + (kpos * slope + bias)
            logit_ref[h, kt] = lg
            m_ref[h] = jnp.maximum(m_ref[h], _fold_rows(lg, jnp.maximum))
        return carry

    lax.fori_loop(0, nk, logit_tile, 0)

    for h in range(DSA_HEADS):
        m_ref[h] = jnp.broadcast_to(jnp.max(m_ref[h], axis=0, keepdims=True), (SUBLANES, t))

    def pv_tile(kt, carry):
        for h in range(DSA_HEADS):
            hs = slice(h * HEAD_DIM, (h + 1) * HEAD_DIM)
            p = jnp.exp2(logit_ref[h, kt] - m_ref[h][:1])
            l_ref[h] = l_ref[h] + _fold_rows(p, jnp.add)
            acc_ref[h] = acc_ref[h] + jnp.dot(vt_ref[kt, hs, :], p.astype(BF16),
                                              preferred_element_type=F32)
        return carry

    lax.fori_loop(0, nk, pv_tile, 0)

    for h in range(DSA_HEADS):
        hs = slice(h * HEAD_DIM, (h + 1) * HEAD_DIM)
        l = jnp.sum(l_ref[h], axis=0, keepdims=True)
        o = (acc_ref[h] / l).T
        o_ref[:, hs] = (_silu(ag_ref[:, hs].astype(F32)) * o).astype(o_ref.dtype)


def _dsa(z, batch, seq, top_k):
    t = DSA_T
    nq = seq // t
    width = DSA_HEADS * HEAD_DIM
    wblk = width // LANE
    qspec = lambda off: pl.BlockSpec((t, width), lambda b, i: (b * nq + i, off // wblk))
    kvspec = lambda off: pl.BlockSpec((seq, width), lambda b, i: (b, off // wblk))
    return pl.pallas_call(
        functools.partial(_dsa_kernel, top_k=top_k),
        grid=(batch, nq),
        in_specs=[qspec(CB_IQ),
                  pl.BlockSpec((seq, LANE), lambda b, i: (b, CB_IKW)),
                  qspec(CB_AQ), kvspec(CB_AK), kvspec(CB_AV), qspec(CB_AG)],
        out_specs=pl.BlockSpec((t, width), lambda b, i: (b * nq + i, 0)),
        out_shape=jax.ShapeDtypeStruct((batch * seq, width), BF16),
        scratch_shapes=[pltpu.VMEM((seq, LANE), BF16),
                        pltpu.VMEM((seq, LANE), BF16),
                        pltpu.VMEM((nq, width, t), BF16),
                        pltpu.VMEM((LANE, t), F32),
                        pltpu.VMEM((nq, t, t), I32),
                        pltpu.VMEM((nq, t, t), F32),
                        pltpu.VMEM((DSA_HEADS, nq, t, t), F32),
                        pltpu.VMEM((DSA_HEADS, SUBLANES, t), F32),
                        pltpu.VMEM((DSA_HEADS, SUBLANES, t), F32),
                        pltpu.VMEM((DSA_HEADS, HEAD_DIM, t), F32)],
        compiler_params=pltpu.CompilerParams(
            dimension_semantics=("arbitrary", "arbitrary"), vmem_limit_bytes=VMEM_LIMIT),
        name="dsa",
    )(z, z, z, z, z, z)


def _memkv_kernel(mem_ref, nw_ref, w_ref, o_ref):
    x = mem_ref[...]
    ms = jnp.mean(x * x, axis=-1, keepdims=True)
    hn = (x * lax.rsqrt(ms + EPS) * nw_ref[...]).astype(BF16)
    o_ref[...] = jnp.dot(hn, w_ref[...], preferred_element_type=F32).astype(o_ref.dtype)


def _memkv(mem2d, norm_w, w_kv, tm=256, tn=1024):
    m, d = mem2d.shape
    n = w_kv.shape[1]
    return pl.pallas_call(
        _memkv_kernel,
        grid=(n // tn, m // tm),
        in_specs=[pl.BlockSpec((tm, d), lambda j, i: (i, 0)),
                  pl.BlockSpec((1, d), lambda j, i: (0, 0)),
                  pl.BlockSpec((d, tn), lambda j, i: (0, j))],
        out_specs=pl.BlockSpec((tm, tn), lambda j, i: (i, j)),
        out_shape=jax.ShapeDtypeStruct((m, n), BF16),
        compiler_params=pltpu.CompilerParams(
            dimension_semantics=("arbitrary", "arbitrary"), vmem_limit_bytes=VMEM_LIMIT),
        name="mem_kv",
    )(mem2d, norm_w.reshape(1, d), w_kv)


def _memattn_kernel(q_ref, g_ref, k_ref, v_ref, o_ref):
    scale = MEM_HEAD_DIM ** -0.5
    for h in range(MEM_HEADS):
        hs = slice(h * MEM_HEAD_DIM, (h + 1) * MEM_HEAD_DIM)
        s = lax.dot_general(q_ref[:, hs], k_ref[:, hs], NT, preferred_element_type=F32) * scale
        p = jnp.exp(s - jnp.max(s, axis=-1, keepdims=True))
        l = jnp.sum(p, axis=-1, keepdims=True)
        o = jnp.dot(p.astype(BF16), v_ref[:, hs], preferred_element_type=F32) / l
        o_ref[:, hs] = (_silu(g_ref[:, hs].astype(F32)) * o).astype(o_ref.dtype)


def _memattn(z, kv, batch, seq, mem_tokens, tl=512):
    width = MEM_HEADS * MEM_HEAD_DIM
    wblk = width // LANE
    nl = seq // tl
    return pl.pallas_call(
        _memattn_kernel,
        grid=(batch, nl),
        in_specs=[pl.BlockSpec((tl, width), lambda b, i: (b * nl + i, CB_MQ // wblk)),
                  pl.BlockSpec((tl, width), lambda b, i: (b * nl + i, CB_MG // wblk)),
                  pl.BlockSpec((mem_tokens, width), lambda b, i: (b, 0)),
                  pl.BlockSpec((mem_tokens, width), lambda b, i: (b, 1))],
        out_specs=pl.BlockSpec((tl, width), lambda b, i: (b * nl + i, 0)),
        out_shape=jax.ShapeDtypeStruct((batch * seq, width), BF16),
        compiler_params=pltpu.CompilerParams(
            dimension_semantics=("arbitrary", "arbitrary"), vmem_limit_bytes=VMEM_LIMIT),
        name="mem_attn",
    )(z, z, kv, kv)


def _merge_kernel(yr_ref, yd_ref, ym_ref, wr_ref, wd_ref, wm_ref, gr_ref, gd_ref, gm_ref, o_ref):
    def branch(y_ref, w_ref, g_ref):
        up = jnp.dot(y_ref[...], w_ref[...], preferred_element_type=F32)
        return _sigmoid(g_ref[...].astype(F32)) * up

    merged = (branch(yr_ref, wr_ref, gr_ref) + branch(yd_ref, wd_ref, gd_ref)
              + branch(ym_ref, wm_ref, gm_ref))
    o_ref[...] = merged.astype(o_ref.dtype)


def _merge(y_ret, y_dsa, y_mem, w_ret, w_dsa, w_mem, z, tm=512, tn=1024):
    m, kdim = y_ret.shape
    n = w_ret.shape[1]
    gblk = tn // LANE
    yspec = pl.BlockSpec((tm, kdim), lambda j, i: (i, 0))
    wspec = pl.BlockSpec((kdim, tn), lambda j, i: (0, j))
    gspec = lambda off: pl.BlockSpec((tm, tn), lambda j, i: (i, off // gblk + j))
    return pl.pallas_call(
        _merge_kernel,
        grid=(n // tn, m // tm),
        in_specs=[yspec, yspec, yspec, wspec, wspec, wspec,
                  gspec(CB_GRET), gspec(CB_GDSA), gspec(CB_GMEM)],
        out_specs=pl.BlockSpec((tm, tn), lambda j, i: (i, j)),
        out_shape=jax.ShapeDtypeStruct((m, n), BF16),
        compiler_params=pltpu.CompilerParams(
            dimension_semantics=("arbitrary", "arbitrary"), vmem_limit_bytes=VMEM_LIMIT),
        name="merge",
    )(y_ret, y_dsa, y_mem, w_ret, w_dsa, w_mem, z, z, z)


def _out_kernel(m_ref, w_ref, x_ref, nw_ref, o_ref):
    out = jnp.dot(m_ref[...], w_ref[...], preferred_element_type=F32)
    ms = jnp.mean(out * out, axis=-1, keepdims=True)
    o_ref[...] = x_ref[...] + out * lax.rsqrt(ms + EPS) * nw_ref[...]


def _out_proj(merged, w_o, x2d, post_w, tm=512):
    m, d = x2d.shape
    return pl.pallas_call(
        _out_kernel,
        grid=(m // tm,),
        in_specs=[pl.BlockSpec((tm, d), lambda i: (i, 0)),
                  pl.BlockSpec((d, d), lambda i: (0, 0)),
                  pl.BlockSpec((tm, d), lambda i: (i, 0)),
                  pl.BlockSpec((1, d), lambda i: (0, 0))],
        out_specs=pl.BlockSpec((tm, d), lambda i: (i, 0)),
        out_shape=jax.ShapeDtypeStruct((m, d), F32),
        compiler_params=pltpu.CompilerParams(
            dimension_semantics=("arbitrary",), vmem_limit_bytes=VMEM_LIMIT),
        name="out_proj_postnorm",
    )(merged, w_o, x2d, post_w.reshape(1, d))


def _pack_w_in(w):
    d = w.shape[0]
    idx_end = (CB_IQ + IDX_HEADS * IDX_DIM // LANE) * LANE
    ikw_end = idx_end + IDX_DIM + IDX_HEADS
    pad = Z_COLS - w.shape[1]
    return jnp.concatenate(
        [w[:, :idx_end], w[:, ikw_end:], w[:, idx_end:ikw_end], jnp.zeros((d, pad), w.dtype)],
        axis=1).astype(BF16)


def _layer(x, mem, pre_norm_w, w_in, ret_gn_w, mem_norm_w, w_mem_kv,
           w_up_ret, w_up_dsa, w_up_mem, w_o, post_norm_w):
    batch, seq, d = x.shape
    mem_tokens = mem.shape[1]
    top_k = min(IDX_TOPK_MAX, seq // 4)
    x2d = x.reshape(batch * seq, d)

    h = _rmsnorm(x2d, pre_norm_w)
    z = _project(h, _pack_w_in(w_in))

    log_g = jnp.log1p(-(2.0 ** (-5.0 - jnp.arange(RET_HEADS, dtype=F32))))
    y_ret = _retention(z, log_g, ret_gn_w, batch, seq)
    y_dsa = _dsa(z, batch, seq, top_k)
    kv = _memkv(mem.reshape(batch * mem_tokens, d), mem_norm_w, w_mem_kv.astype(BF16))
    y_mem = _memattn(z, kv, batch, seq, mem_tokens)

    merged = _merge(y_ret, y_dsa, y_mem, w_up_ret.astype(BF16), w_up_dsa.astype(BF16),
                    w_up_mem.astype(BF16), z)
    out = _out_proj(merged, w_o.astype(BF16), x2d, post_norm_w)
    return out.reshape(batch, seq, d)


def kernel(x, mem, pre_norm_w, w_in, ret_gn_w, mem_norm_w, w_mem_kv, w_up_ret, w_up_dsa,
           w_up_mem, w_o, post_norm_w):
    for layer in range(w_in.shape[0]):
        x = _layer(x, mem, pre_norm_w[layer], w_in[layer], ret_gn_w[layer], mem_norm_w[layer],
                   w_mem_kv[layer], w_up_ret[layer], w_up_dsa[layer], w_up_mem[layer],
                   w_o[layer], post_norm_w[layer])
    return x
```

```python
import functools

import jax
import jax.numpy as jnp
from jax import lax
from jax.experimental import pallas as pl
from jax.experimental.pallas import tpu as pltpu

F32 = jnp.float32
BF16 = jnp.bfloat16
I32 = jnp.int32

LANE = 128
D_MODEL = 2048
RET_HEADS = 8
HEAD_DIM = 128
DSA_HEADS = 8
IDX_HEADS = 16
IDX_DIM = 64
IDX_TOPK_MAX = 256
MEM_HEADS = 4
MEM_HEAD_DIM = 256
NEG_INF = -1e30
EPS = 1e-6
INT_MIN = -2**31

CB_RQ, CB_RK, CB_RV, CB_RG = 0, 8, 16, 24
CB_AQ, CB_AK, CB_AV, CB_AG = 32, 40, 48, 56
CB_IQ = 64
CB_MQ, CB_MG = 72, 80
CB_GRET, CB_GDSA, CB_GMEM = 88, 104, 120
N_COL_BLOCKS = 136
Z_COLS = N_COL_BLOCKS * LANE
IDX_TAIL_START = (CB_IQ + IDX_HEADS * IDX_DIM // LANE) * LANE
IDX_TAIL_COLS = IDX_DIM + IDX_HEADS

PROJ_TM = 2048
PROJ_TN = 1024
PROJ_ALIGNED_TILES = IDX_TAIL_START // PROJ_TN
PROJ_ROWS = 512
PROJ_PREP_ROWS = 256
VMEM_LIMIT = 56 * 1024 * 1024

NT = (((1,), (1,)), ((), ()))
TN = (((0,), (0,)), ((), ()))


def _sigmoid(v):
    return 1.0 / (1.0 + jnp.exp(-v))


def _silu(v):
    return v * _sigmoid(v)


def _rmsnorm_kernel(x_ref, w_ref, o_ref):
    x = x_ref[...]
    ms = jnp.mean(x * x, axis=-1, keepdims=True)
    o_ref[...] = (x * lax.rsqrt(ms + EPS) * w_ref[...]).astype(o_ref.dtype)


def _rmsnorm(x2d, w, tm=512):
    m, d = x2d.shape
    return pl.pallas_call(
        _rmsnorm_kernel,
        grid=(m // tm,),
        in_specs=[pl.BlockSpec((tm, d), lambda i: (i, 0)),
                  pl.BlockSpec((1, d), lambda i: (0, 0))],
        out_specs=pl.BlockSpec((tm, d), lambda i: (i, 0)),
        out_shape=jax.ShapeDtypeStruct((m, d), BF16),
        compiler_params=pltpu.CompilerParams(dimension_semantics=("arbitrary",)),
        name="rmsnorm_bf16",
    )(x2d, w.reshape(1, d))


def _proj_kernel(h_ref, wa_ref, wb_ref, z_ref, wbf_ref):
    j = pl.program_id(0)
    i = pl.program_id(1)
    d = wa_ref.shape[0]

    @pl.when(jnp.logical_and(i == 0, j < PROJ_ALIGNED_TILES))
    def _():
        for r in range(0, d, PROJ_PREP_ROWS):
            wbf_ref[r:r + PROJ_PREP_ROWS, :] = wa_ref[r:r + PROJ_PREP_ROWS, :].astype(BF16)

    @pl.when(jnp.logical_and(i == 0, j >= PROJ_ALIGNED_TILES))
    def _():
        for r in range(0, d, PROJ_PREP_ROWS):
            cat = jnp.concatenate([wa_ref[r:r + PROJ_PREP_ROWS, :], wb_ref[r:r + PROJ_PREP_ROWS, :]],
                                  axis=1)
            wbf_ref[r:r + PROJ_PREP_ROWS, :] = (
                cat[:, IDX_TAIL_COLS:IDX_TAIL_COLS + PROJ_TN].astype(BF16))

    w = wbf_ref[...]
    for r in range(0, h_ref.shape[0], PROJ_ROWS):
        z_ref[r:r + PROJ_ROWS, :] = jnp.dot(
            h_ref[r:r + PROJ_ROWS, :], w, preferred_element_type=F32).astype(z_ref.dtype)


def _project(h, w_in):
    m, d = h.shape
    tm = min(PROJ_TM, m)
    lanes_per_tile = PROJ_TN // LANE
    return pl.pallas_call(
        _proj_kernel,
        grid=(Z_COLS // PROJ_TN, m // tm),
        in_specs=[pl.BlockSpec((tm, d), lambda j, i: (i, 0)),
                  pl.BlockSpec((d, PROJ_TN), lambda j, i: (0, j)),
                  pl.BlockSpec((d, LANE), lambda j, i: (0, (j + 1) * lanes_per_tile))],
        out_specs=pl.BlockSpec((tm, PROJ_TN), lambda j, i: (i, j)),
        out_shape=jax.ShapeDtypeStruct((m, Z_COLS), BF16),
        scratch_shapes=[pltpu.VMEM((d, PROJ_TN), BF16)],
        compiler_params=pltpu.CompilerParams(
            dimension_semantics=("arbitrary", "arbitrary"), vmem_limit_bytes=VMEM_LIMIT),
        name="in_proj",
    )(h, w_in, w_in)


def _idx_proj_kernel(h_ref, w_ref, o_ref):
    o_ref[...] = jnp.dot(h_ref[...], w_ref[...], preferred_element_type=F32).astype(o_ref.dtype)


def _idx_project(h, w_ikw, tm=1024):
    m, d = h.shape
    return pl.pallas_call(
        _idx_proj_kernel,
        grid=(m // tm,),
        in_specs=[pl.BlockSpec((tm, d), lambda i: (i, 0)),
                  pl.BlockSpec((d, LANE), lambda i: (0, 0))],
        out_specs=pl.BlockSpec((tm, LANE), lambda i: (i, 0)),
        out_shape=jax.ShapeDtypeStruct((m, LANE), BF16),
        compiler_params=pltpu.CompilerParams(dimension_semantics=("arbitrary",)),
        name="idx_proj",
    )(h, w_ikw)


RET_C = 256


def _retention_kernel(lg_ref, q_ref, k_ref, v_ref, g_ref, gw_ref, o_ref):
    h = pl.program_id(1)
    lg = lg_ref[h]
    c = RET_C
    seq = q_ref.shape[0]
    scale = HEAD_DIM ** -0.5
    ri = lax.broadcasted_iota(I32, (c, c), 0)
    ci = lax.broadcasted_iota(I32, (c, c), 1)
    diff = (ri - ci).astype(F32)
    decay = jnp.where(diff >= 0, jnp.exp(lg * jnp.maximum(diff, 0.0)), 0.0) * scale
    idx = lax.broadcasted_iota(I32, (c, 1), 0).astype(F32)
    q_dec = jnp.exp(lg * (idx + 1.0))
    k_dec = jnp.exp(lg * (c - 1.0 - idx)) * scale
    chunk_dec = jnp.exp(lg * jnp.full((1, HEAD_DIM), float(c), F32))
    gw = gw_ref[...]

    def body(i, state):
        r = pl.multiple_of(i * c, c)
        qi = q_ref[pl.ds(r, c), :]
        ki = k_ref[pl.ds(r, c), :]
        vi = v_ref[pl.ds(r, c), :]
        inner = lax.dot_general(qi, ki, NT, preferred_element_type=F32) * decay
        o = (jnp.dot(inner.astype(BF16), vi, preferred_element_type=F32)
             + jnp.dot(qi, state.astype(BF16), preferred_element_type=F32) * q_dec)
        vs = (vi.astype(F32) * k_dec).astype(BF16)
        state = state * chunk_dec + lax.dot_general(ki, vs, TN, preferred_element_type=F32)
        mu = jnp.mean(o, axis=-1, keepdims=True)
        oc = o - mu
        var = jnp.mean(oc * oc, axis=-1, keepdims=True)
        on = oc * lax.rsqrt(var + 1e-5) * gw
        g = g_ref[pl.ds(r, c), :].astype(F32)
        o_ref[pl.ds(r, c), :] = (_silu(g) * on).astype(o_ref.dtype)
        return state

    lax.fori_loop(0, seq // c, body, jnp.zeros((HEAD_DIM, HEAD_DIM), F32))


def _retention(z, log_g, gn_w, batch, seq):
    blk = lambda off: pl.BlockSpec((seq, HEAD_DIM), lambda b, h: (b, off + h))
    return pl.pallas_call(
        _retention_kernel,
        grid=(batch, RET_HEADS),
        in_specs=[pl.BlockSpec(memory_space=pltpu.SMEM),
                  blk(CB_RQ), blk(CB_RK), blk(CB_RV), blk(CB_RG),
                  pl.BlockSpec((1, HEAD_DIM), lambda b, h: (0, h))],
        out_specs=pl.BlockSpec((seq, HEAD_DIM), lambda b, h: (b, h)),
        out_shape=jax.ShapeDtypeStruct((batch * seq, RET_HEADS * HEAD_DIM), BF16),
        compiler_params=pltpu.CompilerParams(
            dimension_semantics=("arbitrary", "arbitrary"), vmem_limit_bytes=VMEM_LIMIT),
        name="retention",
    )(log_g, z, z, z, z, gn_w.reshape(1, -1))


DSA_T = 256


SUBLANES = 8


def _fold_rows(x, op):
    while x.shape[0] > SUBLANES:
        half = x.shape[0] // 2
        x = op(x[:half], x[half:])
    return x


def _dsa_kernel(iq_ref, ikw_ref, aq_ref, ak_ref, av_ref, ag_ref, o_ref,
                ika_ref, ikb_ref, vt_ref, wt_ref, keys_ref, bias_ref, logit_ref,
                m_ref, l_ref, acc_ref, *, top_k):
    t = DSA_T
    qb = pl.program_id(1)
    nk = qb + 1
    q0 = pl.multiple_of(qb * t, t)
    nq = vt_ref.shape[0]

    @pl.when(qb == 0)
    def _():
        ikw = ikw_ref[...].astype(F32)
        lane = lax.broadcasted_iota(I32, ikw.shape, 1)
        a = jnp.where(lane < IDX_DIM, ikw, 0.0)
        ika_ref[...] = a.astype(BF16)
        ikb_ref[...] = pltpu.roll(a, IDX_DIM, axis=1).astype(BF16)

        def vt_tile(kt, carry):
            k0 = pl.multiple_of(kt * t, t)
            for h in range(DSA_HEADS):
                hs = slice(h * HEAD_DIM, (h + 1) * HEAD_DIM)
                vt_ref[kt, hs, :] = av_ref[pl.ds(k0, t), hs].astype(F32).T.astype(BF16)
            return carry

        lax.fori_loop(0, nq, vt_tile, 0)

    wt_ref[...] = ikw_ref[pl.ds(q0, t), :].astype(F32).T

    key_row = lax.broadcasted_iota(I32, (t, t), 0)
    qry_col = lax.broadcasted_iota(I32, (t, t), 1)

    def score_tile(kt, carry):
        k0 = pl.multiple_of(kt * t, t)
        ka = ika_ref[pl.ds(k0, t), :]
        kb = ikb_ref[pl.ds(k0, t), :]
        acc = jnp.zeros((t, t), F32)
        for j in range(IDX_HEADS // 2):
            qj = iq_ref[:, j * LANE:(j + 1) * LANE]
            for half, kx in enumerate((ka, kb)):
                hh = IDX_DIM + 2 * j + half
                s = lax.dot_general(kx, qj, NT, preferred_element_type=F32)
                acc = acc + jnp.maximum(s, 0.0) * wt_ref[hh:hh + 1, :]
        bits = lax.bitcast_convert_type(acc, I32)
        key = bits ^ ((bits >> 31) & 0x7FFFFFFF)
        causal = (key_row + k0) <= (qry_col + q0)
        keys_ref[kt] = jnp.where(causal, key, INT_MIN)
        return carry

    lax.fori_loop(0, nk, score_tile, 0)

    def bit_step(i, thr):
        bit = lax.shift_left(jnp.int32(1), 31 - i)
        cand = thr ^ bit

        def count_tile(kt, cnt):
            return cnt + _fold_rows(jnp.where(keys_ref[kt] >= cand, 1, 0), jnp.add)

        cnt = lax.fori_loop(0, nk, count_tile, jnp.zeros((SUBLANES, t), I32))
        total = jnp.sum(cnt, axis=0, keepdims=True)
        return jnp.where(total >= top_k, cand, thr)

    thr = lax.fori_loop(0, 32, bit_step, jnp.full((1, t), INT_MIN, I32))
    thr = jnp.maximum(thr, INT_MIN + 1)

    def bias_tile(kt, carry):
        bias_ref[kt] = jnp.where(keys_ref[kt] >= thr, 0.0, NEG_INF)
        return carry

    lax.fori_loop(0, nk, bias_tile, 0)

    m_ref[...] = jnp.full(m_ref.shape, NEG_INF, F32)
    l_ref[...] = jnp.zeros(l_ref.shape, F32)
    acc_ref[...] = jnp.zeros(acc_ref.shape, F32)
    log2e = 1.4426950408889634
    scale = (HEAD_DIM ** -0.5) * log2e

    def logit_tile(kt, carry):
        k0 = pl.multiple_of(kt * t, t)
        bias = bias_ref[kt]
        kpos = (key_row + (k0 - q0)).astype(F32)
        for h in range(DSA_HEADS):
            slope = (2.0 ** (-8.0 * (h + 1) / DSA_HEADS)) * log2e
            hs = slice(h * HEAD_DIM, (h + 1) * HEAD_DIM)
            s = lax.dot_general(ak_ref[pl.ds(k0, t), hs], aq_ref[:, hs], NT,
                                preferred_element_type=F32)
            lg = s * scale + (kpos * slope + bias)
            logit_ref[h, kt] = lg
            m_ref[h] = jnp.maximum(m_ref[h], _fold_rows(lg, jnp.maximum))
        return carry

    lax.fori_loop(0, nk, logit_tile, 0)

    for h in range(DSA_HEADS):
        m_ref[h] = jnp.broadcast_to(jnp.max(m_ref[h], axis=0, keepdims=True), (SUBLANES, t))

    def pv_tile(kt, carry):
        for h in range(DSA_HEADS):
            hs = slice(h * HEAD_DIM, (h + 1) * HEAD_DIM)
            p = jnp.exp2(logit_ref[h, kt] - m_ref[h][:1])
            l_ref[h] = l_ref[h] + _fold_rows(p, jnp.add)
            acc_ref[h] = acc_ref[h] + jnp.dot(vt_ref[kt, hs, :], p.astype(BF16),
                                              preferred_element_type=F32)
        return carry

    lax.fori_loop(0, nk, pv_tile, 0)

    for h in range(DSA_HEADS):
        hs = slice(h * HEAD_DIM, (h + 1) * HEAD_DIM)
        l = jnp.sum(l_ref[h], axis=0, keepdims=True)
        o = (acc_ref[h] / l).T
        o_ref[:, hs] = (_silu(ag_ref[:, hs].astype(F32)) * o).astype(o_ref.dtype)


def _dsa(z, ikw, batch, seq, top_k):
    t = DSA_T
    nq = seq // t
    width = DSA_HEADS * HEAD_DIM
    wblk = width // LANE
    qspec = lambda off: pl.BlockSpec((t, width), lambda b, i: (b * nq + i, off // wblk))
    kvspec = lambda off: pl.BlockSpec((seq, width), lambda b, i: (b, off // wblk))
    return pl.pallas_call(
        functools.partial(_dsa_kernel, top_k=top_k),
        grid=(batch, nq),
        in_specs=[qspec(CB_IQ),
                  pl.BlockSpec((seq, LANE), lambda b, i: (b, 0)),
                  qspec(CB_AQ), kvspec(CB_AK), kvspec(CB_AV), qspec(CB_AG)],
        out_specs=pl.BlockSpec((t, width), lambda b, i: (b * nq + i, 0)),
        out_shape=jax.ShapeDtypeStruct((batch * seq, width), BF16),
        scratch_shapes=[pltpu.VMEM((seq, LANE), BF16),
                        pltpu.VMEM((seq, LANE), BF16),
                        pltpu.VMEM((nq, width, t), BF16),
                        pltpu.VMEM((LANE, t), F32),
                        pltpu.VMEM((nq, t, t), I32),
                        pltpu.VMEM((nq, t, t), F32),
                        pltpu.VMEM((DSA_HEADS, nq, t, t), F32),
                        pltpu.VMEM((DSA_HEADS, SUBLANES, t), F32),
                        pltpu.VMEM((DSA_HEADS, SUBLANES, t), F32),
                        pltpu.VMEM((DSA_HEADS, HEAD_DIM, t), F32)],
        compiler_params=pltpu.CompilerParams(
            dimension_semantics=("arbitrary", "arbitrary"), vmem_limit_bytes=VMEM_LIMIT),
        name="dsa",
    )(z, ikw, z, z, z, z)


def _memkv_kernel(mem_ref, nw_ref, w_ref, o_ref):
    x = mem_ref[...]
    ms = jnp.mean(x * x, axis=-1, keepdims=True)
    hn = (x * lax.rsqrt(ms + EPS) * nw_ref[...]).astype(BF16)
    o_ref[...] = jnp.dot(hn, w_ref[...], preferred_element_type=F32).astype(o_ref.dtype)


def _memkv(mem2d, norm_w, w_kv, tm=256, tn=1024):
    m, d = mem2d.shape
    n = w_kv.shape[1]
    return pl.pallas_call(
        _memkv_kernel,
        grid=(n // tn, m // tm),
        in_specs=[pl.BlockSpec((tm, d), lambda j, i: (i, 0)),
                  pl.BlockSpec((1, d), lambda j, i: (0, 0)),
                  pl.BlockSpec((d, tn), lambda j, i: (0, j))],
        out_specs=pl.BlockSpec((tm, tn), lambda j, i: (i, j)),
        out_shape=jax.ShapeDtypeStruct((m, n), BF16),
        compiler_params=pltpu.CompilerParams(
            dimension_semantics=("arbitrary", "arbitrary"), vmem_limit_bytes=VMEM_LIMIT),
        name="mem_kv",
    )(mem2d, norm_w.reshape(1, d), w_kv)


def _memattn_kernel(q_ref, g_ref, k_ref, v_ref, o_ref):
    scale = MEM_HEAD_DIM ** -0.5
    for h in range(MEM_HEADS):
        hs = slice(h * MEM_HEAD_DIM, (h + 1) * MEM_HEAD_DIM)
        s = lax.dot_general(q_ref[:, hs], k_ref[:, hs], NT, preferred_element_type=F32) * scale
        p = jnp.exp(s - jnp.max(s, axis=-1, keepdims=True))
        l = jnp.sum(p, axis=-1, keepdims=True)
        o = jnp.dot(p.astype(BF16), v_ref[:, hs], preferred_element_type=F32) / l
        o_ref[:, hs] = (_silu(g_ref[:, hs].astype(F32)) * o).astype(o_ref.dtype)


def _memattn(z, kv, batch, seq, mem_tokens, tl=512):
    width = MEM_HEADS * MEM_HEAD_DIM
    wblk = width // LANE
    nl = seq // tl
    return pl.pallas_call(
        _memattn_kernel,
        grid=(batch, nl),
        in_specs=[pl.BlockSpec((tl, width), lambda b, i: (b * nl + i, CB_MQ // wblk)),
                  pl.BlockSpec((tl, width), lambda b, i: (b * nl + i, CB_MG // wblk)),
                  pl.BlockSpec((mem_tokens, width), lambda b, i: (b, 0)),
                  pl.BlockSpec((mem_tokens, width), lambda b, i: (b, 1))],
        out_specs=pl.BlockSpec((tl, width), lambda b, i: (b * nl + i, 0)),
        out_shape=jax.ShapeDtypeStruct((batch * seq, width), BF16),
        compiler_params=pltpu.CompilerParams(
            dimension_semantics=("arbitrary", "arbitrary"), vmem_limit_bytes=VMEM_LIMIT),
        name="mem_attn",
    )(z, z, kv, kv)


def _merge_kernel(yr_ref, yd_ref, ym_ref, wr_ref, wd_ref, wm_ref, gr_ref, gd_ref, gm_ref, o_ref):
    def branch(y_ref, w_ref, g_ref):
        up = jnp.dot(y_ref[...], w_ref[...], preferred_element_type=F32)
        return _sigmoid(g_ref[...].astype(F32)) * up

    merged = (branch(yr_ref, wr_ref, gr_ref) + branch(yd_ref, wd_ref, gd_ref)
              + branch(ym_ref, wm_ref, gm_ref))
    o_ref[...] = merged.astype(o_ref.dtype)


def _merge(y_ret, y_dsa, y_mem, w_ret, w_dsa, w_mem, z, tm=512, tn=1024):
    m, kdim = y_ret.shape
    n = w_ret.shape[1]
    gblk = tn // LANE
    yspec = pl.BlockSpec((tm, kdim), lambda j, i: (i, 0))
    wspec = pl.BlockSpec((kdim, tn), lambda j, i: (0, j))
    gspec = lambda off: pl.BlockSpec((tm, tn), lambda j, i: (i, off // gblk + j))
    return pl.pallas_call(
        _merge_kernel,
        grid=(n // tn, m // tm),
        in_specs=[yspec, yspec, yspec, wspec, wspec, wspec,
                  gspec(CB_GRET), gspec(CB_GDSA), gspec(CB_GMEM)],
        out_specs=pl.BlockSpec((tm, tn), lambda j, i: (i, j)),
        out_shape=jax.ShapeDtypeStruct((m, n), BF16),
        compiler_params=pltpu.CompilerParams(
            dimension_semantics=("arbitrary", "arbitrary"), vmem_limit_bytes=VMEM_LIMIT),
        name="merge",
    )(y_ret, y_dsa, y_mem, w_ret, w_dsa, w_mem, z, z, z)


def _out_kernel(m_ref, w_ref, x_ref, nw_ref, o_ref):
    out = jnp.dot(m_ref[...], w_ref[...], preferred_element_type=F32)
    ms = jnp.mean(out * out, axis=-1, keepdims=True)
    o_ref[...] = x_ref[...] + out * lax.rsqrt(ms + EPS) * nw_ref[...]


def _out_proj(merged, w_o, x2d, post_w, tm=512):
    m, d = x2d.shape
    return pl.pallas_call(
        _out_kernel,
        grid=(m // tm,),
        in_specs=[pl.BlockSpec((tm, d), lambda i: (i, 0)),
                  pl.BlockSpec((d, d), lambda i: (0, 0)),
                  pl.BlockSpec((tm, d), lambda i: (i, 0)),
                  pl.BlockSpec((1, d), lambda i: (0, 0))],
        out_specs=pl.BlockSpec((tm, d), lambda i: (i, 0)),
        out_shape=jax.ShapeDtypeStruct((m, d), F32),
        compiler_params=pltpu.CompilerParams(
            dimension_semantics=("arbitrary",), vmem_limit_bytes=VMEM_LIMIT),
        name="out_proj_postnorm",
    )(merged, w_o, x2d, post_w.reshape(1, d))


def _layer(x, mem, pre_norm_w, w_in, ret_gn_w, mem_norm_w, w_mem_kv,
           w_up_ret, w_up_dsa, w_up_mem, w_o, post_norm_w):
    batch, seq, d = x.shape
    mem_tokens = mem.shape[1]
    top_k = min(IDX_TOPK_MAX, seq // 4)
    x2d = x.reshape(batch * seq, d)

    h = _rmsnorm(x2d, pre_norm_w)
    z = _project(h, w_in)
    w_ikw = jnp.pad(w_in[:, IDX_TAIL_START:IDX_TAIL_START + IDX_TAIL_COLS],
                    ((0, 0), (0, LANE - IDX_TAIL_COLS))).astype(BF16)
    ikw = _idx_project(h, w_ikw)

    log_g = jnp.log1p(-(2.0 ** (-5.0 - jnp.arange(RET_HEADS, dtype=F32))))
    y_ret = _retention(z, log_g, ret_gn_w, batch, seq)
    y_dsa = _dsa(z, ikw, batch, seq, top_k)
    kv = _memkv(mem.reshape(batch * mem_tokens, d), mem_norm_w, w_mem_kv.astype(BF16))
    y_mem = _memattn(z, kv, batch, seq, mem_tokens)

    merged = _merge(y_ret, y_dsa, y_mem, w_up_ret.astype(BF16), w_up_dsa.astype(BF16),
                    w_up_mem.astype(BF16), z)
    out = _out_proj(merged, w_o.astype(BF16), x2d, post_norm_w)
    return out.reshape(batch, seq, d)


def kernel(x, mem, pre_norm_w, w_in, ret_gn_w, mem_norm_w, w_mem_kv, w_up_ret, w_up_dsa,
           w_up_mem, w_o, post_norm_w):
    for layer in range(w_in.shape[0]):
        x = _layer(x, mem, pre_norm_w[layer], w_in[layer], ret_gn_w[layer], mem_norm_w[layer],
                   w_mem_kv[layer], w_up_ret[layer], w_up_dsa[layer], w_up_mem[layer],
                   w_o[layer], post_norm_w[layer])
    return x
```

```python
import functools

import jax
import jax.numpy as jnp
from jax import lax
from jax.experimental import pallas as pl
from jax.experimental.pallas import tpu as pltpu

F32 = jnp.float32
BF16 = jnp.bfloat16
I32 = jnp.int32

LANE = 128
D_MODEL = 2048
RET_HEADS = 8
HEAD_DIM = 128
DSA_HEADS = 8
IDX_HEADS = 16
IDX_DIM = 64
IDX_TOPK_MAX = 256
MEM_HEADS = 4
MEM_HEAD_DIM = 256
NEG_INF = -1e30
EPS = 1e-6
INT_MIN = -2**31

CB_RQ, CB_RK, CB_RV, CB_RG = 0, 8, 16, 24
CB_AQ, CB_AK, CB_AV, CB_AG = 32, 40, 48, 56
CB_IQ = 64
CB_MQ, CB_MG = 72, 80
CB_GRET, CB_GDSA, CB_GMEM = 88, 104, 120
N_COL_BLOCKS = 136
Z_COLS = N_COL_BLOCKS * LANE
IDX_TAIL_START = (CB_IQ + IDX_HEADS * IDX_DIM // LANE) * LANE
IDX_TAIL_COLS = IDX_DIM + IDX_HEADS

PROJ_TM = 2048
PROJ_TN = 1024
PROJ_ALIGNED_TILES = IDX_TAIL_START // PROJ_TN
PROJ_ROWS = 512
PROJ_PREP_ROWS = 256
VMEM_LIMIT = 56 * 1024 * 1024

NT = (((1,), (1,)), ((), ()))
TN = (((0,), (0,)), ((), ()))


def _sigmoid(v):
    return 1.0 / (1.0 + jnp.exp(-v))


def _silu(v):
    return v * _sigmoid(v)


def _rmsnorm_kernel(x_ref, w_ref, o_ref):
    x = x_ref[...]
    ms = jnp.mean(x * x, axis=-1, keepdims=True)
    o_ref[...] = (x * lax.rsqrt(ms + EPS) * w_ref[...]).astype(o_ref.dtype)


def _rmsnorm(x2d, w, tm=512):
    m, d = x2d.shape
    return pl.pallas_call(
        _rmsnorm_kernel,
        grid=(m // tm,),
        in_specs=[pl.BlockSpec((tm, d), lambda i: (i, 0)),
                  pl.BlockSpec((1, d), lambda i: (0, 0))],
        out_specs=pl.BlockSpec((tm, d), lambda i: (i, 0)),
        out_shape=jax.ShapeDtypeStruct((m, d), BF16),
        compiler_params=pltpu.CompilerParams(dimension_semantics=("arbitrary",)),
        name="rmsnorm_bf16",
    )(x2d, w.reshape(1, d))


def _proj_kernel(h_ref, w_ref, z_ref, wbf_ref):
    @pl.when(pl.program_id(1) == 0)
    def _():
        for r in range(0, w_ref.shape[0], PROJ_PREP_ROWS):
            wbf_ref[r:r + PROJ_PREP_ROWS, :] = w_ref[r:r + PROJ_PREP_ROWS, :].astype(BF16)

    w = wbf_ref[...]
    for r in range(0, h_ref.shape[0], PROJ_ROWS):
        z_ref[r:r + PROJ_ROWS, :] = lax.dot_general(
            h_ref[r:r + PROJ_ROWS, :], w, NT, preferred_element_type=F32).astype(z_ref.dtype)


def _project(h, w_in_t):
    m, d = h.shape
    tm = min(PROJ_TM, m)

    def w_rows(j, i):
        skip = jnp.where(j >= PROJ_ALIGNED_TILES, IDX_TAIL_COLS // SUBLANES, 0)
        return ((j * (PROJ_TN // SUBLANES) + skip) * SUBLANES, 0)

    return pl.pallas_call(
        _proj_kernel,
        grid=(Z_COLS // PROJ_TN, m // tm),
        in_specs=[pl.BlockSpec((tm, d), lambda j, i: (i, 0)),
                  pl.BlockSpec((pl.Element(PROJ_TN), pl.Element(d)), w_rows)],
        out_specs=pl.BlockSpec((tm, PROJ_TN), lambda j, i: (i, j)),
        out_shape=jax.ShapeDtypeStruct((m, Z_COLS), BF16),
        scratch_shapes=[pltpu.VMEM((PROJ_TN, d), BF16)],
        compiler_params=pltpu.CompilerParams(
            dimension_semantics=("arbitrary", "arbitrary"), vmem_limit_bytes=VMEM_LIMIT),
        name="in_proj",
    )(h, w_in_t)


def _idx_proj_kernel(h_ref, w_ref, o_ref):
    o_ref[...] = lax.dot_general(h_ref[...], w_ref[...], NT,
                                 preferred_element_type=F32).astype(o_ref.dtype)


def _idx_project(h, w_ikw, tm=1024):
    m, d = h.shape
    return pl.pallas_call(
        _idx_proj_kernel,
        grid=(m // tm,),
        in_specs=[pl.BlockSpec((tm, d), lambda i: (i, 0)),
                  pl.BlockSpec((LANE, d), lambda i: (0, 0))],
        out_specs=pl.BlockSpec((tm, LANE), lambda i: (i, 0)),
        out_shape=jax.ShapeDtypeStruct((m, LANE), BF16),
        compiler_params=pltpu.CompilerParams(dimension_semantics=("arbitrary",)),
        name="idx_proj",
    )(h, w_ikw)


RET_C = 256


def _retention_kernel(lg_ref, q_ref, k_ref, v_ref, g_ref, gw_ref, o_ref):
    h = pl.program_id(1)
    lg = lg_ref[h]
    c = RET_C
    seq = q_ref.shape[0]
    scale = HEAD_DIM ** -0.5
    ri = lax.broadcasted_iota(I32, (c, c), 0)
    ci = lax.broadcasted_iota(I32, (c, c), 1)
    diff = (ri - ci).astype(F32)
    decay = jnp.where(diff >= 0, jnp.exp(lg * jnp.maximum(diff, 0.0)), 0.0) * scale
    idx = lax.broadcasted_iota(I32, (c, 1), 0).astype(F32)
    q_dec = jnp.exp(lg * (idx + 1.0))
    k_dec = jnp.exp(lg * (c - 1.0 - idx)) * scale
    chunk_dec = jnp.exp(lg * jnp.full((1, HEAD_DIM), float(c), F32))
    gw = gw_ref[...]

    def body(i, state):
        r = pl.multiple_of(i * c, c)
        qi = q_ref[pl.ds(r, c), :]
        ki = k_ref[pl.ds(r, c), :]
        vi = v_ref[pl.ds(r, c), :]
        inner = lax.dot_general(qi, ki, NT, preferred_element_type=F32) * decay
        o = (jnp.dot(inner.astype(BF16), vi, preferred_element_type=F32)
             + jnp.dot(qi, state.astype(BF16), preferred_element_type=F32) * q_dec)
        vs = (vi.astype(F32) * k_dec).astype(BF16)
        state = state * chunk_dec + lax.dot_general(ki, vs, TN, preferred_element_type=F32)
        mu = jnp.mean(o, axis=-1, keepdims=True)
        oc = o - mu
        var = jnp.mean(oc * oc, axis=-1, keepdims=True)
        on = oc * lax.rsqrt(var + 1e-5) * gw
        g = g_ref[pl.ds(r, c), :].astype(F32)
        o_ref[pl.ds(r, c), :] = (_silu(g) * on).astype(o_ref.dtype)
        return state

    lax.fori_loop(0, seq // c, body, jnp.zeros((HEAD_DIM, HEAD_DIM), F32))


def _retention(z, log_g, gn_w, batch, seq):
    blk = lambda off: pl.BlockSpec((seq, HEAD_DIM), lambda b, h: (b, off + h))
    return pl.pallas_call(
        _retention_kernel,
        grid=(batch, RET_HEADS),
        in_specs=[pl.BlockSpec(memory_space=pltpu.SMEM),
                  blk(CB_RQ), blk(CB_RK), blk(CB_RV), blk(CB_RG),
                  pl.BlockSpec((1, HEAD_DIM), lambda b, h: (0, h))],
        out_specs=pl.BlockSpec((seq, HEAD_DIM), lambda b, h: (b, h)),
        out_shape=jax.ShapeDtypeStruct((batch * seq, RET_HEADS * HEAD_DIM), BF16),
        compiler_params=pltpu.CompilerParams(
            dimension_semantics=("arbitrary", "arbitrary"), vmem_limit_bytes=VMEM_LIMIT),
        name="retention",
    )(log_g, z, z, z, z, gn_w.reshape(1, -1))


DSA_T = 256


SUBLANES = 8


def _fold_rows(x, op):
    while x.shape[0] > SUBLANES:
        half = x.shape[0] // 2
        x = op(x[:half], x[half:])
    return x


def _dsa_kernel(iq_ref, ikw_ref, aq_ref, ak_ref, av_ref, ag_ref, o_ref,
                ika_ref, ikb_ref, vt_ref, wt_ref, keys_ref, bias_ref, logit_ref,
                m_ref, l_ref, acc_ref, *, top_k):
    t = DSA_T
    qb = pl.program_id(1)
    nk = qb + 1
    q0 = pl.multiple_of(qb * t, t)
    nq = vt_ref.shape[0]

    @pl.when(qb == 0)
    def _():
        ikw = ikw_ref[...].astype(F32)
        lane = lax.broadcasted_iota(I32, ikw.shape, 1)
        a = jnp.where(lane < IDX_DIM, ikw, 0.0)
        ika_ref[...] = a.astype(BF16)
        ikb_ref[...] = pltpu.roll(a, IDX_DIM, axis=1).astype(BF16)

        def vt_tile(kt, carry):
            k0 = pl.multiple_of(kt * t, t)
            for h in range(DSA_HEADS):
                hs = slice(h * HEAD_DIM, (h + 1) * HEAD_DIM)
                vt_ref[kt, hs, :] = av_ref[pl.ds(k0, t), hs].astype(F32).T.astype(BF16)
            return carry

        lax.fori_loop(0, nq, vt_tile, 0)

    wt_ref[...] = ikw_ref[pl.ds(q0, t), :].astype(F32).T

    key_row = lax.broadcasted_iota(I32, (t, t), 0)
    qry_col = lax.broadcasted_iota(I32, (t, t), 1)

    def score_tile(kt, carry):
        k0 = pl.multiple_of(kt * t, t)
        ka = ika_ref[pl.ds(k0, t), :]
        kb = ikb_ref[pl.ds(k0, t), :]
        acc = jnp.zeros((t, t), F32)
        for j in range(IDX_HEADS // 2):
            qj = iq_ref[:, j * LANE:(j + 1) * LANE]
            for half, kx in enumerate((ka, kb)):
                hh = IDX_DIM + 2 * j + half
                s = lax.dot_general(kx, qj, NT, preferred_element_type=F32)
                acc = acc + jnp.maximum(s, 0.0) * wt_ref[hh:hh + 1, :]
        bits = lax.bitcast_convert_type(acc, I32)
        key = bits ^ ((bits >> 31) & 0x7FFFFFFF)
        causal = (key_row + k0) <= (qry_col + q0)
        keys_ref[kt] = jnp.where(causal, key, INT_MIN)
        return carry

    lax.fori_loop(0, nk, score_tile, 0)

    def bit_step(i, thr):
        bit = lax.shift_left(jnp.int32(1), 31 - i)
        cand = thr ^ bit

        def count_tile(kt, cnt):
            return cnt + _fold_rows(jnp.where(keys_ref[kt] >= cand, 1, 0), jnp.add)

        cnt = lax.fori_loop(0, nk, count_tile, jnp.zeros((SUBLANES, t), I32))
        total = jnp.sum(cnt, axis=0, keepdims=True)
        return jnp.where(total >= top_k, cand, thr)

    thr = lax.fori_loop(0, 32, bit_step, jnp.full((1, t), INT_MIN, I32))
    thr = jnp.maximum(thr, INT_MIN + 1)

    def bias_tile(kt, carry):
        bias_ref[kt] = jnp.where(keys_ref[kt] >= thr, 0.0, NEG_INF)
        return carry

    lax.fori_loop(0, nk, bias_tile, 0)

    m_ref[...] = jnp.full(m_ref.shape, NEG_INF, F32)
    l_ref[...] = jnp.zeros(l_ref.shape, F32)
    acc_ref[...] = jnp.zeros(acc_ref.shape, F32)
    log2e = 1.4426950408889634
    scale = (HEAD_DIM ** -0.5) * log2e

    def logit_tile(kt, carry):
        k0 = pl.multiple_of(kt * t, t)
        bias = bias_ref[kt]
        kpos = (key_row + (k0 - q0)).astype(F32)
        for h in range(DSA_HEADS):
            slope = (2.0 ** (-8.0 * (h + 1) / DSA_HEADS)) * log2e
            hs = slice(h * HEAD_DIM, (h + 1) * HEAD_DIM)
            s = lax.dot_general(ak_ref[pl.ds(k0, t), hs], aq_ref[:, hs], NT,
                                preferred_element_type=F32)
            lg = s * scale + (kpos * slope + bias)
            logit_ref[h, kt] = lg
            m_ref[h] = jnp.maximum(m_ref[h], _fold_rows(lg, jnp.maximum))
        return carry

    lax.fori_loop(0, nk, logit_tile, 0)

    for h in range(DSA_HEADS):
        m_ref[h] = jnp.broadcast_to(jnp.max(m_ref[h], axis=0, keepdims=True), (SUBLANES, t))

    def pv_tile(kt, carry):
        for h in range(DSA_HEADS):
            hs = slice(h * HEAD_DIM, (h + 1) * HEAD_DIM)
            p = jnp.exp2(logit_ref[h, kt] - m_ref[h][:1])
            l_ref[h] = l_ref[h] + _fold_rows(p, jnp.add)
            acc_ref[h] = acc_ref[h] + jnp.dot(vt_ref[kt, hs, :], p.astype(BF16),
                                              preferred_element_type=F32)
        return carry

    lax.fori_loop(0, nk, pv_tile, 0)

    for h in range(DSA_HEADS):
        hs = slice(h * HEAD_DIM, (h + 1) * HEAD_DIM)
        l = jnp.sum(l_ref[h], axis=0, keepdims=True)
        o = (acc_ref[h] / l).T
        o_ref[:, hs] = (_silu(ag_ref[:, hs].astype(F32)) * o).astype(o_ref.dtype)


def _dsa(z, ikw, batch, seq, top_k):
    t = DSA_T
    nq = seq // t
    width = DSA_HEADS * HEAD_DIM
    wblk = width // LANE
    qspec = lambda off: pl.BlockSpec((t, width), lambda b, i: (b * nq + i, off // wblk))
    kvspec = lambda off: pl.BlockSpec((seq, width), lambda b, i: (b, off // wblk))
    return pl.pallas_call(
        functools.partial(_dsa_kernel, top_k=top_k),
        grid=(batch, nq),
        in_specs=[qspec(CB_IQ),
                  pl.BlockSpec((seq, LANE), lambda b, i: (b, 0)),
                  qspec(CB_AQ), kvspec(CB_AK), kvspec(CB_AV), qspec(CB_AG)],
        out_specs=pl.BlockSpec((t, width), lambda b, i: (b * nq + i, 0)),
        out_shape=jax.ShapeDtypeStruct((batch * seq, width), BF16),
        scratch_shapes=[pltpu.VMEM((seq, LANE), BF16),
                        pltpu.VMEM((seq, LANE), BF16),
                        pltpu.VMEM((nq, width, t), BF16),
                        pltpu.VMEM((LANE, t), F32),
                        pltpu.VMEM((nq, t, t), I32),
                        pltpu.VMEM((nq, t, t), F32),
                        pltpu.VMEM((DSA_HEADS, nq, t, t), F32),
                        pltpu.VMEM((DSA_HEADS, SUBLANES, t), F32),
                        pltpu.VMEM((DSA_HEADS, SUBLANES, t), F32),
                        pltpu.VMEM((DSA_HEADS, HEAD_DIM, t), F32)],
        compiler_params=pltpu.CompilerParams(
            dimension_semantics=("arbitrary", "arbitrary"), vmem_limit_bytes=VMEM_LIMIT),
        name="dsa",
    )(z, ikw, z, z, z, z)


def _memkv_kernel(mem_ref, nw_ref, w_ref, o_ref):
    x = mem_ref[...]
    ms = jnp.mean(x * x, axis=-1, keepdims=True)
    hn = (x * lax.rsqrt(ms + EPS) * nw_ref[...]).astype(BF16)
    o_ref[...] = jnp.dot(hn, w_ref[...], preferred_element_type=F32).astype(o_ref.dtype)


def _memkv(mem2d, norm_w, w_kv, tm=256, tn=1024):
    m, d = mem2d.shape
    n = w_kv.shape[1]
    return pl.pallas_call(
        _memkv_kernel,
        grid=(n // tn, m // tm),
        in_specs=[pl.BlockSpec((tm, d), lambda j, i: (i, 0)),
                  pl.BlockSpec((1, d), lambda j, i: (0, 0)),
                  pl.BlockSpec((d, tn), lambda j, i: (0, j))],
        out_specs=pl.BlockSpec((tm, tn), lambda j, i: (i, j)),
        out_shape=jax.ShapeDtypeStruct((m, n), BF16),
        compiler_params=pltpu.CompilerParams(
            dimension_semantics=("arbitrary", "arbitrary"), vmem_limit_bytes=VMEM_LIMIT),
        name="mem_kv",
    )(mem2d, norm_w.reshape(1, d), w_kv)


def _memattn_kernel(q_ref, g_ref, k_ref, v_ref, o_ref):
    scale = MEM_HEAD_DIM ** -0.5
    for h in range(MEM_HEADS):
        hs = slice(h * MEM_HEAD_DIM, (h + 1) * MEM_HEAD_DIM)
        s = lax.dot_general(q_ref[:, hs], k_ref[:, hs], NT, preferred_element_type=F32) * scale
        p = jnp.exp(s - jnp.max(s, axis=-1, keepdims=True))
        l = jnp.sum(p, axis=-1, keepdims=True)
        o = jnp.dot(p.astype(BF16), v_ref[:, hs], preferred_element_type=F32) / l
        o_ref[:, hs] = (_silu(g_ref[:, hs].astype(F32)) * o).astype(o_ref.dtype)


def _memattn(z, kv, batch, seq, mem_tokens, tl=512):
    width = MEM_HEADS * MEM_HEAD_DIM
    wblk = width // LANE
    nl = seq // tl
    return pl.pallas_call(
        _memattn_kernel,
        grid=(batch, nl),
        in_specs=[pl.BlockSpec((tl, width), lambda b, i: (b * nl + i, CB_MQ // wblk)),
                  pl.BlockSpec((tl, width), lambda b, i: (b * nl + i, CB_MG // wblk)),
                  pl.BlockSpec((mem_tokens, width), lambda b, i: (b, 0)),
                  pl.BlockSpec((mem_tokens, width), lambda b, i: (b, 1))],
        out_specs=pl.BlockSpec((tl, width), lambda b, i: (b * nl + i, 0)),
        out_shape=jax.ShapeDtypeStruct((batch * seq, width), BF16),
        compiler_params=pltpu.CompilerParams(
            dimension_semantics=("arbitrary", "arbitrary"), vmem_limit_bytes=VMEM_LIMIT),
        name="mem_attn",
    )(z, z, kv, kv)


def _merge_kernel(yr_ref, yd_ref, ym_ref, wr_ref, wd_ref, wm_ref, gr_ref, gd_ref, gm_ref, o_ref):
    def branch(y_ref, w_ref, g_ref):
        up = jnp.dot(y_ref[...], w_ref[...], preferred_element_type=F32)
        return _sigmoid(g_ref[...].astype(F32)) * up

    merged = (branch(yr_ref, wr_ref, gr_ref) + branch(yd_ref, wd_ref, gd_ref)
              + branch(ym_ref, wm_ref, gm_ref))
    o_ref[...] = merged.astype(o_ref.dtype)


def _merge(y_ret, y_dsa, y_mem, w_ret, w_dsa, w_mem, z, tm=512, tn=1024):
    m, kdim = y_ret.shape
    n = w_ret.shape[1]
    gblk = tn // LANE
    yspec = pl.BlockSpec((tm, kdim), lambda j, i: (i, 0))
    wspec = pl.BlockSpec((kdim, tn), lambda j, i: (0, j))
    gspec = lambda off: pl.BlockSpec((tm, tn), lambda j, i: (i, off // gblk + j))
    return pl.pallas_call(
        _merge_kernel,
        grid=(n // tn, m // tm),
        in_specs=[yspec, yspec, yspec, wspec, wspec, wspec,
                  gspec(CB_GRET), gspec(CB_GDSA), gspec(CB_GMEM)],
        out_specs=pl.BlockSpec((tm, tn), lambda j, i: (i, j)),
        out_shape=jax.ShapeDtypeStruct((m, n), BF16),
        compiler_params=pltpu.CompilerParams(
            dimension_semantics=("arbitrary", "arbitrary"), vmem_limit_bytes=VMEM_LIMIT),
        name="merge",
    )(y_ret, y_dsa, y_mem, w_ret, w_dsa, w_mem, z, z, z)


def _out_kernel(m_ref, w_ref, x_ref, nw_ref, o_ref):
    out = jnp.dot(m_ref[...], w_ref[...], preferred_element_type=F32)
    ms = jnp.mean(out * out, axis=-1, keepdims=True)
    o_ref[...] = x_ref[...] + out * lax.rsqrt(ms + EPS) * nw_ref[...]


def _out_proj(merged, w_o, x2d, post_w, tm=512):
    m, d = x2d.shape
    return pl.pallas_call(
        _out_kernel,
        grid=(m // tm,),
        in_specs=[pl.BlockSpec((tm, d), lambda i: (i, 0)),
                  pl.BlockSpec((d, d), lambda i: (0, 0)),
                  pl.BlockSpec((tm, d), lambda i: (i, 0)),
                  pl.BlockSpec((1, d), lambda i: (0, 0))],
        out_specs=pl.BlockSpec((tm, d), lambda i: (i, 0)),
        out_shape=jax.ShapeDtypeStruct((m, d), F32),
        compiler_params=pltpu.CompilerParams(
            dimension_semantics=("arbitrary",), vmem_limit_bytes=VMEM_LIMIT),
        name="out_proj_postnorm",
    )(merged, w_o, x2d, post_w.reshape(1, d))


def _layer(x, mem, pre_norm_w, w_in, ret_gn_w, mem_norm_w, w_mem_kv,
           w_up_ret, w_up_dsa, w_up_mem, w_o, post_norm_w):
    batch, seq, d = x.shape
    mem_tokens = mem.shape[1]
    top_k = min(IDX_TOPK_MAX, seq // 4)
    x2d = x.reshape(batch * seq, d)

    h = _rmsnorm(x2d, pre_norm_w)
    w_in_t = w_in.T
    z = _project(h, w_in_t)
    w_ikw = jnp.pad(w_in_t[IDX_TAIL_START:IDX_TAIL_START + IDX_TAIL_COLS],
                    ((0, LANE - IDX_TAIL_COLS), (0, 0))).astype(BF16)
    ikw = _idx_project(h, w_ikw)

    log_g = jnp.log1p(-(2.0 ** (-5.0 - jnp.arange(RET_HEADS, dtype=F32))))
    y_ret = _retention(z, log_g, ret_gn_w, batch, seq)
    y_dsa = _dsa(z, ikw, batch, seq, top_k)
    kv = _memkv(mem.reshape(batch * mem_tokens, d), mem_norm_w, w_mem_kv.astype(BF16))
    y_mem = _memattn(z, kv, batch, seq, mem_tokens)

    merged = _merge(y_ret, y_dsa, y_mem, w_up_ret.astype(BF16), w_up_dsa.astype(BF16),
                    w_up_mem.astype(BF16), z)
    out = _out_proj(merged, w_o.astype(BF16), x2d, post_norm_w)
    return out.reshape(batch, seq, d)


def kernel(x, mem, pre_norm_w, w_in, ret_gn_w, mem_norm_w, w_mem_kv, w_up_ret, w_up_dsa,
           w_up_mem, w_o, post_norm_w):
    for layer in range(w_in.shape[0]):
        x = _layer(x, mem, pre_norm_w[layer], w_in[layer], ret_gn_w[layer], mem_norm_w[layer],
                   w_mem_kv[layer], w_up_ret[layer], w_up_dsa[layer], w_up_mem[layer],
                   w_o[layer], post_norm_w[layer])
    return x
```

```python
import functools

import jax
import jax.numpy as jnp
from jax import lax
from jax.experimental import pallas as pl
from jax.experimental.pallas import tpu as pltpu

F32 = jnp.float32
BF16 = jnp.bfloat16
I32 = jnp.int32

LANE = 128
SUBLANES = 8
N_BRANCHES = 3
D_MODEL = 2048
RET_HEADS = 8
HEAD_DIM = 128
DSA_HEADS = 8
IDX_HEADS = 16
IDX_DIM = 64
IDX_TOPK_MAX = 256
MEM_HEADS = 4
MEM_HEAD_DIM = 256
NEG_INF = -1e30
EPS = 1e-6
INT_MIN = -2**31

CB_RQ, CB_RK, CB_RV, CB_RG = 0, 8, 16, 24
CB_AQ, CB_AK, CB_AV, CB_AG = 32, 40, 48, 56
CB_IQ = 64
CB_MQ, CB_MG = 72, 80
CB_GRET, CB_GDSA, CB_GMEM = 88, 104, 120
N_COL_BLOCKS = 136
Z_COLS = N_COL_BLOCKS * LANE
IDX_TAIL_START = (CB_IQ + IDX_HEADS * IDX_DIM // LANE) * LANE
IDX_TAIL_COLS = IDX_DIM + IDX_HEADS

PROJ_TM = 2048
PROJ_TN = 1024
PROJ_ALIGNED_TILES = IDX_TAIL_START // PROJ_TN
PROJ_ROWS = 512
PROJ_PREP_ROWS = 256
VMEM_LIMIT = 56 * 1024 * 1024

NT = (((1,), (1,)), ((), ()))
TN = (((0,), (0,)), ((), ()))


def _sigmoid(v):
    return 1.0 / (1.0 + jnp.exp(-v))


def _silu(v):
    return v * _sigmoid(v)


def _prenorm_kernel(x_ref, w_ref, wikw_ref, h_ref, ikw_ref):
    x = x_ref[...]
    ms = jnp.mean(x * x, axis=-1, keepdims=True)
    h = (x * lax.rsqrt(ms + EPS) * w_ref[...]).astype(h_ref.dtype)
    h_ref[...] = h
    ikw_ref[...] = lax.dot_general(h, wikw_ref[...], NT,
                                   preferred_element_type=F32).astype(ikw_ref.dtype)


def _prenorm(x2d, w, w_ikw, tm=512):
    m, d = x2d.shape
    return pl.pallas_call(
        _prenorm_kernel,
        grid=(m // tm,),
        in_specs=[pl.BlockSpec((tm, d), lambda i: (i, 0)),
                  pl.BlockSpec((1, d), lambda i: (0, 0)),
                  pl.BlockSpec((LANE, d), lambda i: (0, 0))],
        out_specs=[pl.BlockSpec((tm, d), lambda i: (i, 0)),
                   pl.BlockSpec((tm, LANE), lambda i: (i, 0))],
        out_shape=[jax.ShapeDtypeStruct((m, d), BF16),
                   jax.ShapeDtypeStruct((m, LANE), BF16)],
        compiler_params=pltpu.CompilerParams(dimension_semantics=("arbitrary",)),
        name="prenorm_idx",
    )(x2d, w.reshape(1, d), w_ikw)


def _proj_kernel(h_ref, w_ref, z_ref, wbf_ref):
    @pl.when(pl.program_id(1) == 0)
    def _():
        for r in range(0, w_ref.shape[0], PROJ_PREP_ROWS):
            wbf_ref[r:r + PROJ_PREP_ROWS, :] = w_ref[r:r + PROJ_PREP_ROWS, :].astype(BF16)

    w = wbf_ref[...]
    for r in range(0, h_ref.shape[0], PROJ_ROWS):
        z_ref[r:r + PROJ_ROWS, :] = lax.dot_general(
            h_ref[r:r + PROJ_ROWS, :], w, NT, preferred_element_type=F32).astype(z_ref.dtype)


def _project(h, w_in_t):
    m, d = h.shape
    tm = min(PROJ_TM, m)

    def w_rows(j, i):
        skip = jnp.where(j >= PROJ_ALIGNED_TILES, IDX_TAIL_COLS // SUBLANES, 0)
        return ((j * (PROJ_TN // SUBLANES) + skip) * SUBLANES, 0)

    return pl.pallas_call(
        _proj_kernel,
        grid=(Z_COLS // PROJ_TN, m // tm),
        in_specs=[pl.BlockSpec((tm, d), lambda j, i: (i, 0)),
                  pl.BlockSpec((pl.Element(PROJ_TN), pl.Element(d)), w_rows)],
        out_specs=pl.BlockSpec((tm, PROJ_TN), lambda j, i: (i, j)),
        out_shape=jax.ShapeDtypeStruct((m, Z_COLS), BF16),
        scratch_shapes=[pltpu.VMEM((PROJ_TN, d), BF16)],
        compiler_params=pltpu.CompilerParams(
            dimension_semantics=("arbitrary", "arbitrary"), vmem_limit_bytes=VMEM_LIMIT),
        name="in_proj",
    )(h, w_in_t)


RET_C = 256


def _retention_kernel(lg_ref, q_ref, k_ref, v_ref, g_ref, gw_ref, o_ref):
    h = pl.program_id(1)
    lg = lg_ref[h]
    c = RET_C
    seq = q_ref.shape[0]
    scale = HEAD_DIM ** -0.5
    ri = lax.broadcasted_iota(I32, (c, c), 0)
    ci = lax.broadcasted_iota(I32, (c, c), 1)
    diff = (ri - ci).astype(F32)
    decay = jnp.where(diff >= 0, jnp.exp(lg * jnp.maximum(diff, 0.0)), 0.0) * scale
    idx = lax.broadcasted_iota(I32, (c, 1), 0).astype(F32)
    q_dec = jnp.exp(lg * (idx + 1.0))
    k_dec = jnp.exp(lg * (c - 1.0 - idx)) * scale
    chunk_dec = jnp.exp(lg * jnp.full((1, HEAD_DIM), float(c), F32))
    gw = gw_ref[...]

    state = jnp.zeros((HEAD_DIM, HEAD_DIM), F32)
    for i in range(seq // c):
        r = i * c
        qi = q_ref[pl.ds(r, c), :]
        ki = k_ref[pl.ds(r, c), :]
        vi = v_ref[pl.ds(r, c), :]
        inner = lax.dot_general(qi, ki, NT, preferred_element_type=F32) * decay
        o = (jnp.dot(inner.astype(BF16), vi, preferred_element_type=F32)
             + jnp.dot(qi, state.astype(BF16), preferred_element_type=F32) * q_dec)
        vs = (vi.astype(F32) * k_dec).astype(BF16)
        state = state * chunk_dec + lax.dot_general(ki, vs, TN, preferred_element_type=F32)
        mu = jnp.mean(o, axis=-1, keepdims=True)
        oc = o - mu
        var = jnp.mean(oc * oc, axis=-1, keepdims=True)
        on = oc * lax.rsqrt(var + 1e-5) * gw
        g = g_ref[pl.ds(r, c), :].astype(F32)
        o_ref[pl.ds(r, c), :] = (_silu(g) * on).astype(o_ref.dtype)


def _retention(z, log_g, gn_w, batch, seq):
    blk = lambda off: pl.BlockSpec((seq, HEAD_DIM), lambda b, h: (b, off + h))
    return pl.pallas_call(
        _retention_kernel,
        grid=(batch, RET_HEADS),
        in_specs=[pl.BlockSpec(memory_space=pltpu.SMEM),
                  blk(CB_RQ), blk(CB_RK), blk(CB_RV), blk(CB_RG),
                  pl.BlockSpec((1, HEAD_DIM), lambda b, h: (0, h))],
        out_specs=pl.BlockSpec((seq, HEAD_DIM), lambda b, h: (b, h)),
        out_shape=jax.ShapeDtypeStruct((batch * seq, RET_HEADS * HEAD_DIM), BF16),
        compiler_params=pltpu.CompilerParams(
            dimension_semantics=("arbitrary", "arbitrary"), vmem_limit_bytes=VMEM_LIMIT),
        name="retention",
    )(log_g, z, z, z, z, gn_w.reshape(1, -1))


DSA_T = 256
VT_ROWS = HEAD_DIM + 16


def _fold_rows(x, op):
    while x.shape[0] > SUBLANES:
        half = x.shape[0] // 2
        x = op(x[:half], x[half:])
    return x


def _dsa_kernel(iq_ref, ikw_ref, aq_ref, ak_ref, av_ref, ag_ref, o_ref,
                ika_ref, ikb_ref, vt_ref, wt_ref, keys_ref, bias_ref, logit_ref,
                m_ref, acc_ref, *, top_k):
    t = DSA_T
    qb = pl.program_id(1)
    nk = qb + 1
    q0 = pl.multiple_of(qb * t, t)
    nq = vt_ref.shape[0]

    @pl.when(qb == 0)
    def _():
        ikw = ikw_ref[...].astype(F32)
        lane = lax.broadcasted_iota(I32, ikw.shape, 1)
        a = jnp.where(lane < IDX_DIM, ikw, 0.0)
        ika_ref[...] = a.astype(BF16)
        ikb_ref[...] = pltpu.roll(a, IDX_DIM, axis=1).astype(BF16)

        def vt_tile(kt, carry):
            k0 = pl.multiple_of(kt * t, t)
            for h in range(DSA_HEADS):
                hs = slice(h * HEAD_DIM, (h + 1) * HEAD_DIM)
                vt_ref[kt, h, :HEAD_DIM, :] = av_ref[pl.ds(k0, t), hs].astype(F32).T.astype(BF16)
                vt_ref[kt, h, HEAD_DIM:, :] = jnp.ones((VT_ROWS - HEAD_DIM, t), BF16)
            return carry

        lax.fori_loop(0, nq, vt_tile, 0)

    wt_ref[...] = ikw_ref[pl.ds(q0, t), :].astype(F32).T

    key_row = lax.broadcasted_iota(I32, (t, t), 0)
    qry_col = lax.broadcasted_iota(I32, (t, t), 1)

    def score_tile(kt, carry):
        k0 = pl.multiple_of(kt * t, t)
        ka = ika_ref[pl.ds(k0, t), :]
        kb = ikb_ref[pl.ds(k0, t), :]
        acc = jnp.zeros((t, t), F32)
        for j in range(IDX_HEADS // 2):
            qj = iq_ref[:, j * LANE:(j + 1) * LANE]
            for half, kx in enumerate((ka, kb)):
                hh = IDX_DIM + 2 * j + half
                s = lax.dot_general(kx, qj, NT, preferred_element_type=F32)
                acc = acc + jnp.maximum(s, 0.0) * wt_ref[hh:hh + 1, :]
        bits = lax.bitcast_convert_type(acc, I32)
        key = bits ^ ((bits >> 31) & 0x7FFFFFFF)
        causal = (key_row + k0) <= (qry_col + q0)
        keys_ref[kt] = jnp.where(causal, key, INT_MIN)
        return carry

    lax.fori_loop(0, nk, score_tile, 0)

    def bit_step(i, thr):
        bit = lax.shift_left(jnp.int32(1), 31 - i)
        cand = thr ^ bit

        def count_tile(kt, cnt):
            return cnt + _fold_rows(jnp.where(keys_ref[kt] >= cand, 1, 0), jnp.add)

        cnt = lax.fori_loop(0, nk, count_tile, jnp.zeros((SUBLANES, t), I32))
        total = jnp.sum(cnt, axis=0, keepdims=True)
        return jnp.where(total >= top_k, cand, thr)

    thr = lax.fori_loop(0, 32, bit_step, jnp.full((1, t), INT_MIN, I32))
    thr = jnp.maximum(thr, INT_MIN + 1)

    def bias_tile(kt, carry):
        bias_ref[kt] = jnp.where(keys_ref[kt] >= thr, 0.0, NEG_INF)
        return carry

    lax.fori_loop(0, nk, bias_tile, 0)

    m_ref[...] = jnp.full(m_ref.shape, NEG_INF, F32)
    acc_ref[...] = jnp.zeros(acc_ref.shape, F32)
    log2e = 1.4426950408889634
    scale = (HEAD_DIM ** -0.5) * log2e

    def logit_tile(kt, carry):
        k0 = pl.multiple_of(kt * t, t)
        bias = bias_ref[kt]
        kpos = (key_row + (k0 - q0)).astype(F32)
        for h in range(DSA_HEADS):
            slope = (2.0 ** (-8.0 * (h + 1) / DSA_HEADS)) * log2e
            hs = slice(h * HEAD_DIM, (h + 1) * HEAD_DIM)
            s = lax.dot_general(ak_ref[pl.ds(k0, t), hs], aq_ref[:, hs], NT,
                                preferred_element_type=F32)
            lg = s * scale + (kpos * slope + bias)
            logit_ref[h, kt] = lg
            m_ref[h] = jnp.maximum(m_ref[h], _fold_rows(lg, jnp.maximum))
        return carry

    lax.fori_loop(0, nk, logit_tile, 0)

    for h in range(DSA_HEADS):
        m_ref[h] = jnp.broadcast_to(jnp.max(m_ref[h], axis=0, keepdims=True), (SUBLANES, t))

    def pv_tile(kt, carry):
        for h in range(DSA_HEADS):
            p = jnp.exp2(logit_ref[h, kt] - m_ref[h][:1])
            acc_ref[h] = acc_ref[h] + jnp.dot(vt_ref[kt, h], p.astype(BF16),
                                              preferred_element_type=F32)
        return carry

    lax.fori_loop(0, nk, pv_tile, 0)

    for h in range(DSA_HEADS):
        hs = slice(h * HEAD_DIM, (h + 1) * HEAD_DIM)
        acc = acc_ref[h]
        o = (acc[:HEAD_DIM] / acc[HEAD_DIM:HEAD_DIM + 1]).T
        o_ref[:, hs] = (_silu(ag_ref[:, hs].astype(F32)) * o).astype(o_ref.dtype)


def _dsa(z, ikw, batch, seq, top_k):
    t = DSA_T
    nq = seq // t
    width = DSA_HEADS * HEAD_DIM
    wblk = width // LANE
    qspec = lambda off: pl.BlockSpec((t, width), lambda b, i: (b * nq + i, off // wblk))
    kvspec = lambda off: pl.BlockSpec((seq, width), lambda b, i: (b, off // wblk))
    return pl.pallas_call(
        functools.partial(_dsa_kernel, top_k=top_k),
        grid=(batch, nq),
        in_specs=[qspec(CB_IQ),
                  pl.BlockSpec((seq, LANE), lambda b, i: (b, 0)),
                  qspec(CB_AQ), kvspec(CB_AK), kvspec(CB_AV), qspec(CB_AG)],
        out_specs=pl.BlockSpec((t, width), lambda b, i: (b * nq + i, 0)),
        out_shape=jax.ShapeDtypeStruct((batch * seq, width), BF16),
        scratch_shapes=[pltpu.VMEM((seq, LANE), BF16),
                        pltpu.VMEM((seq, LANE), BF16),
                        pltpu.VMEM((nq, DSA_HEADS, VT_ROWS, t), BF16),
                        pltpu.VMEM((LANE, t), F32),
                        pltpu.VMEM((nq, t, t), I32),
                        pltpu.VMEM((nq, t, t), F32),
                        pltpu.VMEM((DSA_HEADS, nq, t, t), F32),
                        pltpu.VMEM((DSA_HEADS, SUBLANES, t), F32),
                        pltpu.VMEM((DSA_HEADS, VT_ROWS, t), F32)],
        compiler_params=pltpu.CompilerParams(
            dimension_semantics=("arbitrary", "arbitrary"), vmem_limit_bytes=VMEM_LIMIT),
        name="dsa",
    )(z, ikw, z, z, z, z)


def _memkv_kernel(mem_ref, nw_ref, w_ref, o_ref, wbf_ref):
    @pl.when(pl.program_id(1) == 0)
    def _():
        wbf_ref[...] = w_ref[...].astype(BF16)

    x = mem_ref[...]
    ms = jnp.mean(x * x, axis=-1, keepdims=True)
    hn = (x * lax.rsqrt(ms + EPS) * nw_ref[...]).astype(BF16)
    o_ref[...] = jnp.dot(hn, wbf_ref[...], preferred_element_type=F32).astype(o_ref.dtype)


def _memkv(mem2d, norm_w, w_kv, tm=256, tn=1024):
    m, d = mem2d.shape
    n = w_kv.shape[1]
    return pl.pallas_call(
        _memkv_kernel,
        grid=(n // tn, m // tm),
        in_specs=[pl.BlockSpec((tm, d), lambda j, i: (i, 0)),
                  pl.BlockSpec((1, d), lambda j, i: (0, 0)),
                  pl.BlockSpec((d, tn), lambda j, i: (0, j))],
        out_specs=pl.BlockSpec((tm, tn), lambda j, i: (i, j)),
        out_shape=jax.ShapeDtypeStruct((m, n), BF16),
        scratch_shapes=[pltpu.VMEM((d, tn), BF16)],
        compiler_params=pltpu.CompilerParams(
            dimension_semantics=("arbitrary", "arbitrary"), vmem_limit_bytes=VMEM_LIMIT),
        name="mem_kv",
    )(mem2d, norm_w.reshape(1, d), w_kv)


def _memattn_kernel(q_ref, g_ref, k_ref, v_ref, o_ref):
    scale = MEM_HEAD_DIM ** -0.5
    for h in range(MEM_HEADS):
        hs = slice(h * MEM_HEAD_DIM, (h + 1) * MEM_HEAD_DIM)
        s = lax.dot_general(q_ref[:, hs], k_ref[:, hs], NT, preferred_element_type=F32) * scale
        p = jnp.exp(s - jnp.max(s, axis=-1, keepdims=True))
        l = jnp.sum(p, axis=-1, keepdims=True)
        o = jnp.dot(p.astype(BF16), v_ref[:, hs], preferred_element_type=F32) / l
        o_ref[:, hs] = (_silu(g_ref[:, hs].astype(F32)) * o).astype(o_ref.dtype)


def _memattn(z, kv, batch, seq, mem_tokens, tl=512):
    width = MEM_HEADS * MEM_HEAD_DIM
    wblk = width // LANE
    nl = seq // tl
    return pl.pallas_call(
        _memattn_kernel,
        grid=(batch, nl),
        in_specs=[pl.BlockSpec((tl, width), lambda b, i: (b * nl + i, CB_MQ // wblk)),
                  pl.BlockSpec((tl, width), lambda b, i: (b * nl + i, CB_MG // wblk)),
                  pl.BlockSpec((mem_tokens, width), lambda b, i: (b, 0)),
                  pl.BlockSpec((mem_tokens, width), lambda b, i: (b, 1))],
        out_specs=pl.BlockSpec((tl, width), lambda b, i: (b * nl + i, 0)),
        out_shape=jax.ShapeDtypeStruct((batch * seq, width), BF16),
        compiler_params=pltpu.CompilerParams(
            dimension_semantics=("arbitrary", "arbitrary"), vmem_limit_bytes=VMEM_LIMIT),
        name="mem_attn",
    )(z, z, kv, kv)


def _merge_kernel(yr_ref, yd_ref, ym_ref, wr_ref, wd_ref, wm_ref, gr_ref, gd_ref, gm_ref, o_ref,
                  wbf_ref):
    @pl.when(pl.program_id(1) == 0)
    def _():
        for b, w_ref in enumerate((wr_ref, wd_ref, wm_ref)):
            wbf_ref[b] = w_ref[...].astype(BF16)

    def branch(b, y_ref, g_ref):
        up = jnp.dot(y_ref[...], wbf_ref[b], preferred_element_type=F32)
        return _sigmoid(g_ref[...].astype(F32)) * up

    merged = branch(0, yr_ref, gr_ref) + branch(1, yd_ref, gd_ref) + branch(2, ym_ref, gm_ref)
    o_ref[...] = merged.astype(o_ref.dtype)


def _merge(y_ret, y_dsa, y_mem, w_ret, w_dsa, w_mem, z, tm=512, tn=1024):
    m, kdim = y_ret.shape
    n = w_ret.shape[1]
    gblk = tn // LANE
    yspec = pl.BlockSpec((tm, kdim), lambda j, i: (i, 0))
    wspec = pl.BlockSpec((kdim, tn), lambda j, i: (0, j))
    gspec = lambda off: pl.BlockSpec((tm, tn), lambda j, i: (i, off // gblk + j))
    return pl.pallas_call(
        _merge_kernel,
        grid=(n // tn, m // tm),
        in_specs=[yspec, yspec, yspec, wspec, wspec, wspec,
                  gspec(CB_GRET), gspec(CB_GDSA), gspec(CB_GMEM)],
        out_specs=pl.BlockSpec((tm, tn), lambda j, i: (i, j)),
        out_shape=jax.ShapeDtypeStruct((m, n), BF16),
        scratch_shapes=[pltpu.VMEM((N_BRANCHES, kdim, tn), BF16)],
        compiler_params=pltpu.CompilerParams(
            dimension_semantics=("arbitrary", "arbitrary"), vmem_limit_bytes=VMEM_LIMIT),
        name="merge",
    )(y_ret, y_dsa, y_mem, w_ret, w_dsa, w_mem, z, z, z)


def _out_kernel(m_ref, w_ref, x_ref, nw_ref, o_ref):
    out = jnp.dot(m_ref[...], w_ref[...], preferred_element_type=F32)
    ms = jnp.mean(out * out, axis=-1, keepdims=True)
    o_ref[...] = x_ref[...] + out * lax.rsqrt(ms + EPS) * nw_ref[...]


def _out_proj(merged, w_o, x2d, post_w, tm=512):
    m, d = x2d.shape
    return pl.pallas_call(
        _out_kernel,
        grid=(m // tm,),
        in_specs=[pl.BlockSpec((tm, d), lambda i: (i, 0)),
                  pl.BlockSpec((d, d), lambda i: (0, 0)),
                  pl.BlockSpec((tm, d), lambda i: (i, 0)),
                  pl.BlockSpec((1, d), lambda i: (0, 0))],
        out_specs=pl.BlockSpec((tm, d), lambda i: (i, 0)),
        out_shape=jax.ShapeDtypeStruct((m, d), F32),
        compiler_params=pltpu.CompilerParams(
            dimension_semantics=("arbitrary",), vmem_limit_bytes=VMEM_LIMIT),
        name="out_proj_postnorm",
    )(merged, w_o, x2d, post_w.reshape(1, d))


def _layer(x, mem, pre_norm_w, w_in, ret_gn_w, mem_norm_w, w_mem_kv,
           w_up_ret, w_up_dsa, w_up_mem, w_o, post_norm_w):
    batch, seq, d = x.shape
    mem_tokens = mem.shape[1]
    top_k = min(IDX_TOPK_MAX, seq // 4)
    x2d = x.reshape(batch * seq, d)

    w_in_t = w_in.T
    w_ikw = jnp.pad(w_in_t[IDX_TAIL_START:IDX_TAIL_START + IDX_TAIL_COLS],
                    ((0, LANE - IDX_TAIL_COLS), (0, 0))).astype(BF16)
    h, ikw = _prenorm(x2d, pre_norm_w, w_ikw)
    z = _project(h, w_in_t)

    log_g = jnp.log1p(-(2.0 ** (-5.0 - jnp.arange(RET_HEADS, dtype=F32))))
    y_ret = _retention(z, log_g, ret_gn_w, batch, seq)
    y_dsa = _dsa(z, ikw, batch, seq, top_k)
    kv = _memkv(mem.reshape(batch * mem_tokens, d), mem_norm_w, w_mem_kv)
    y_mem = _memattn(z, kv, batch, seq, mem_tokens)

    merged = _merge(y_ret, y_dsa, y_mem, w_up_ret, w_up_dsa, w_up_mem, z)
    out = _out_proj(merged, w_o.astype(BF16), x2d, post_norm_w)
    return out.reshape(batch, seq, d)


def kernel(x, mem, pre_norm_w, w_in, ret_gn_w, mem_norm_w, w_mem_kv, w_up_ret, w_up_dsa,
           w_up_mem, w_o, post_norm_w):
    for layer in range(w_in.shape[0]):
        x = _layer(x, mem, pre_norm_w[layer], w_in[layer], ret_gn_w[layer], mem_norm_w[layer],
                   w_mem_kv[layer], w_up_ret[layer], w_up_dsa[layer], w_up_mem[layer],
                   w_o[layer], post_norm_w[layer])
    return x
```

```python
import functools

import jax
import jax.numpy as jnp
from jax import lax
from jax.experimental import pallas as pl
from jax.experimental.pallas import tpu as pltpu

F32 = jnp.float32
BF16 = jnp.bfloat16
I32 = jnp.int32

LANE = 128
SUBLANES = 8
N_BRANCHES = 3
D_MODEL = 2048
RET_HEADS = 8
HEAD_DIM = 128
DSA_HEADS = 8
IDX_HEADS = 16
IDX_DIM = 64
IDX_TOPK_MAX = 256
MEM_HEADS = 4
MEM_HEAD_DIM = 256
NEG_INF = -1e30
EPS = 1e-6
INT_MIN = -2**31

CB_RQ, CB_RK, CB_RV, CB_RG = 0, 8, 16, 24
CB_AQ, CB_AK, CB_AV, CB_AG = 32, 40, 48, 56
CB_IQ = 64
CB_MQ, CB_MG = 72, 80
CB_GRET, CB_GDSA, CB_GMEM = 88, 104, 120
N_COL_BLOCKS = 136
Z_COLS = N_COL_BLOCKS * LANE
IDX_TAIL_START = (CB_IQ + IDX_HEADS * IDX_DIM // LANE) * LANE
IDX_TAIL_COLS = IDX_DIM + IDX_HEADS

PROJ_TM = 2048
PROJ_TN = 1024
PROJ_ALIGNED_TILES = IDX_TAIL_START // PROJ_TN
PROJ_ROWS = 512
PROJ_PREP_ROWS = 256
VMEM_LIMIT = 56 * 1024 * 1024

NT = (((1,), (1,)), ((), ()))
TN = (((0,), (0,)), ((), ()))


def _sigmoid(v):
    return 1.0 / (1.0 + jnp.exp(-v))


def _silu(v):
    return v * _sigmoid(v)


def _prenorm_kernel(x_ref, w_ref, wikw_ref, h_ref, ikw_ref):
    x = x_ref[...]
    ms = jnp.mean(x * x, axis=-1, keepdims=True)
    h = (x * lax.rsqrt(ms + EPS) * w_ref[...]).astype(h_ref.dtype)
    h_ref[...] = h
    ikw_ref[...] = lax.dot_general(h, wikw_ref[...], NT,
                                   preferred_element_type=F32).astype(ikw_ref.dtype)


def _prenorm(x2d, w, w_ikw, tm=512):
    m, d = x2d.shape
    return pl.pallas_call(
        _prenorm_kernel,
        grid=(m // tm,),
        in_specs=[pl.BlockSpec((tm, d), lambda i: (i, 0)),
                  pl.BlockSpec((1, d), lambda i: (0, 0)),
                  pl.BlockSpec((LANE, d), lambda i: (0, 0))],
        out_specs=[pl.BlockSpec((tm, d), lambda i: (i, 0)),
                   pl.BlockSpec((tm, LANE), lambda i: (i, 0))],
        out_shape=[jax.ShapeDtypeStruct((m, d), BF16),
                   jax.ShapeDtypeStruct((m, LANE), BF16)],
        compiler_params=pltpu.CompilerParams(dimension_semantics=("arbitrary",)),
        name="prenorm_idx",
    )(x2d, w.reshape(1, d), w_ikw)


def _proj_kernel(h_ref, w_ref, z_ref, wbf_ref):
    @pl.when(pl.program_id(1) == 0)
    def _():
        for r in range(0, w_ref.shape[0], PROJ_PREP_ROWS):
            wbf_ref[r:r + PROJ_PREP_ROWS, :] = w_ref[r:r + PROJ_PREP_ROWS, :].astype(BF16)

    w = wbf_ref[...]
    for r in range(0, h_ref.shape[0], PROJ_ROWS):
        z_ref[r:r + PROJ_ROWS, :] = lax.dot_general(
            h_ref[r:r + PROJ_ROWS, :], w, NT, preferred_element_type=F32).astype(z_ref.dtype)


def _project(h, w_in_t):
    m, d = h.shape
    tm = min(PROJ_TM, m)

    def w_rows(j, i):
        skip = jnp.where(j >= PROJ_ALIGNED_TILES, IDX_TAIL_COLS // SUBLANES, 0)
        return ((j * (PROJ_TN // SUBLANES) + skip) * SUBLANES, 0)

    return pl.pallas_call(
        _proj_kernel,
        grid=(Z_COLS // PROJ_TN, m // tm),
        in_specs=[pl.BlockSpec((tm, d), lambda j, i: (i, 0)),
                  pl.BlockSpec((pl.Element(PROJ_TN), pl.Element(d)), w_rows)],
        out_specs=pl.BlockSpec((tm, PROJ_TN), lambda j, i: (i, j)),
        out_shape=jax.ShapeDtypeStruct((m, Z_COLS), BF16),
        scratch_shapes=[pltpu.VMEM((PROJ_TN, d), BF16)],
        compiler_params=pltpu.CompilerParams(
            dimension_semantics=("arbitrary", "arbitrary"), vmem_limit_bytes=VMEM_LIMIT),
        name="in_proj",
    )(h, w_in_t)


RET_C = 256


def _retention_kernel(lg_ref, q_ref, k_ref, v_ref, g_ref, gw_ref, o_ref):
    h = pl.program_id(1)
    lg = lg_ref[h]
    c = RET_C
    seq = q_ref.shape[0]
    scale = HEAD_DIM ** -0.5
    ri = lax.broadcasted_iota(I32, (c, c), 0)
    ci = lax.broadcasted_iota(I32, (c, c), 1)
    diff = (ri - ci).astype(F32)
    decay = jnp.where(diff >= 0, jnp.exp(lg * jnp.maximum(diff, 0.0)), 0.0) * scale
    idx = lax.broadcasted_iota(I32, (c, 1), 0).astype(F32)
    q_dec = jnp.exp(lg * (idx + 1.0))
    k_dec = jnp.exp(lg * (c - 1.0 - idx)) * scale
    chunk_dec = jnp.exp(lg * jnp.full((1, HEAD_DIM), float(c), F32))
    gw = gw_ref[...]

    state = jnp.zeros((HEAD_DIM, HEAD_DIM), F32)
    for i in range(seq // c):
        r = i * c
        qi = q_ref[pl.ds(r, c), :]
        ki = k_ref[pl.ds(r, c), :]
        vi = v_ref[pl.ds(r, c), :]
        inner = lax.dot_general(qi, ki, NT, preferred_element_type=F32) * decay
        o = (jnp.dot(inner.astype(BF16), vi, preferred_element_type=F32)
             + jnp.dot(qi, state.astype(BF16), preferred_element_type=F32) * q_dec)
        vs = (vi.astype(F32) * k_dec).astype(BF16)
        state = state * chunk_dec + lax.dot_general(ki, vs, TN, preferred_element_type=F32)
        mu = jnp.mean(o, axis=-1, keepdims=True)
        oc = o - mu
        var = jnp.mean(oc * oc, axis=-1, keepdims=True)
        on = oc * lax.rsqrt(var + 1e-5) * gw
        g = g_ref[pl.ds(r, c), :].astype(F32)
        o_ref[pl.ds(r, c), :] = (_silu(g) * on).astype(o_ref.dtype)


def _retention(z, log_g, gn_w, batch, seq):
    blk = lambda off: pl.BlockSpec((seq, HEAD_DIM), lambda b, h: (b, off + h))
    return pl.pallas_call(
        _retention_kernel,
        grid=(batch, RET_HEADS),
        in_specs=[pl.BlockSpec(memory_space=pltpu.SMEM),
                  blk(CB_RQ), blk(CB_RK), blk(CB_RV), blk(CB_RG),
                  pl.BlockSpec((1, HEAD_DIM), lambda b, h: (0, h))],
        out_specs=pl.BlockSpec((seq, HEAD_DIM), lambda b, h: (b, h)),
        out_shape=jax.ShapeDtypeStruct((batch * seq, RET_HEADS * HEAD_DIM), BF16),
        compiler_params=pltpu.CompilerParams(
            dimension_semantics=("arbitrary", "arbitrary"), vmem_limit_bytes=VMEM_LIMIT),
        name="retention",
    )(log_g, z, z, z, z, gn_w.reshape(1, -1))


DSA_T = 256
VT_ROWS = HEAD_DIM + 16


def _fold_rows(x, op, rows=SUBLANES):
    while x.shape[0] > rows:
        half = x.shape[0] // 2
        x = op(x[:half], x[half:])
    return x


ORD_NEG_INF = -2**31 + 0x7FFFFF


def _ordinal_to_f32(o):
    o = jnp.maximum(o, ORD_NEG_INF)
    return lax.bitcast_convert_type(o ^ ((o >> 31) & 0x7FFFFFFF), F32)


def _dsa_kernel(iq_ref, ikw_ref, aq_ref, ak_ref, av_ref, ag_ref, o_ref,
                ika_ref, ikb_ref, vt_ref, wt_ref, score_ref, bias_ref, logit_ref,
                m_ref, acc_ref, *, top_k):
    t = DSA_T
    qb = pl.program_id(1)
    nk = qb + 1
    q0 = pl.multiple_of(qb * t, t)
    nq = vt_ref.shape[0]

    @pl.when(qb == 0)
    def _():
        ikw = ikw_ref[...].astype(F32)
        lane = lax.broadcasted_iota(I32, ikw.shape, 1)
        a = jnp.where(lane < IDX_DIM, ikw, 0.0)
        ika_ref[...] = a.astype(BF16)
        ikb_ref[...] = pltpu.roll(a, IDX_DIM, axis=1).astype(BF16)

        def vt_tile(kt, carry):
            k0 = pl.multiple_of(kt * t, t)
            for h in range(DSA_HEADS):
                hs = slice(h * HEAD_DIM, (h + 1) * HEAD_DIM)
                vt_ref[kt, h, :HEAD_DIM, :] = av_ref[pl.ds(k0, t), hs].astype(F32).T.astype(BF16)
                vt_ref[kt, h, HEAD_DIM:, :] = jnp.ones((VT_ROWS - HEAD_DIM, t), BF16)
            return carry

        lax.fori_loop(0, nq, vt_tile, 0)

    wt_ref[...] = ikw_ref[pl.ds(q0, t), :].astype(F32).T

    key_row = lax.broadcasted_iota(I32, (t, t), 0)
    qry_col = lax.broadcasted_iota(I32, (t, t), 1)

    def score_tile(kt, carry):
        k0 = pl.multiple_of(kt * t, t)
        ka = ika_ref[pl.ds(k0, t), :]
        kb = ikb_ref[pl.ds(k0, t), :]
        acc = jnp.zeros((t, t), F32)
        for j in range(IDX_HEADS // 2):
            qj = iq_ref[:, j * LANE:(j + 1) * LANE]
            for half, kx in enumerate((ka, kb)):
                hh = IDX_DIM + 2 * j + half
                s = lax.dot_general(kx, qj, NT, preferred_element_type=F32)
                acc = acc + jnp.maximum(s, 0.0) * wt_ref[hh:hh + 1, :]
        causal = (key_row + k0) <= (qry_col + q0)
        score_ref[kt] = jnp.where(causal, acc, -jnp.inf)
        return carry

    lax.fori_loop(0, nk, score_tile, 0)

    def bit_step(i, thr_ord):
        cand_ord = thr_ord ^ lax.shift_left(jnp.int32(1), 31 - i)
        cand = _ordinal_to_f32(cand_ord)

        def count_tile(kt, cnt):
            return cnt + _fold_rows(jnp.where(score_ref[kt] >= cand, 1, 0), jnp.add)

        cnt = lax.fori_loop(0, nk, count_tile, jnp.zeros((SUBLANES, t), I32))
        total = jnp.sum(cnt, axis=0, keepdims=True)
        return jnp.where(total >= top_k, cand_ord, thr_ord)

    thr = _ordinal_to_f32(lax.fori_loop(0, 32, bit_step, jnp.full((1, t), INT_MIN, I32)))

    def bias_tile(kt, carry):
        k0 = pl.multiple_of(kt * t, t)
        keep = jnp.logical_and(score_ref[kt] >= thr, (key_row + k0) <= (qry_col + q0))
        bias_ref[kt] = jnp.where(keep, 0.0, NEG_INF)
        return carry

    lax.fori_loop(0, nk, bias_tile, 0)

    m_ref[...] = jnp.full(m_ref.shape, NEG_INF, F32)
    acc_ref[...] = jnp.zeros(acc_ref.shape, F32)
    log2e = 1.4426950408889634
    scale = (HEAD_DIM ** -0.5) * log2e

    def logit_tile(kt, carry):
        k0 = pl.multiple_of(kt * t, t)
        bias = bias_ref[kt]
        kpos = (key_row + (k0 - q0)).astype(F32)
        for h in range(DSA_HEADS):
            slope = (2.0 ** (-8.0 * (h + 1) / DSA_HEADS)) * log2e
            hs = slice(h * HEAD_DIM, (h + 1) * HEAD_DIM)
            s = lax.dot_general(ak_ref[pl.ds(k0, t), hs], aq_ref[:, hs], NT,
                                preferred_element_type=F32)
            lg = s * scale + (kpos * slope + bias)
            logit_ref[h, kt] = lg
            m_ref[h] = jnp.maximum(m_ref[h], _fold_rows(lg, jnp.maximum))
        return carry

    lax.fori_loop(0, nk, logit_tile, 0)

    for h in range(DSA_HEADS):
        m_ref[h] = jnp.broadcast_to(jnp.max(m_ref[h], axis=0, keepdims=True), (SUBLANES, t))

    def pv_tile(kt, carry):
        for h in range(DSA_HEADS):
            p = jnp.exp2(logit_ref[h, kt] - m_ref[h][:1])
            acc_ref[h] = acc_ref[h] + jnp.dot(vt_ref[kt, h], p.astype(BF16),
                                              preferred_element_type=F32)
        return carry

    lax.fori_loop(0, nk, pv_tile, 0)

    for h in range(DSA_HEADS):
        hs = slice(h * HEAD_DIM, (h + 1) * HEAD_DIM)
        acc = acc_ref[h]
        o = (acc[:HEAD_DIM] / acc[HEAD_DIM:HEAD_DIM + 1]).T
        o_ref[:, hs] = (_silu(ag_ref[:, hs].astype(F32)) * o).astype(o_ref.dtype)


def _dsa(z, ikw, batch, seq, top_k):
    t = DSA_T
    nq = seq // t
    width = DSA_HEADS * HEAD_DIM
    wblk = width // LANE
    qspec = lambda off: pl.BlockSpec((t, width), lambda b, i: (b * nq + i, off // wblk))
    kvspec = lambda off: pl.BlockSpec((seq, width), lambda b, i: (b, off // wblk))
    return pl.pallas_call(
        functools.partial(_dsa_kernel, top_k=top_k),
        grid=(batch, nq),
        in_specs=[qspec(CB_IQ),
                  pl.BlockSpec((seq, LANE), lambda b, i: (b, 0)),
                  qspec(CB_AQ), kvspec(CB_AK), kvspec(CB_AV), qspec(CB_AG)],
        out_specs=pl.BlockSpec((t, width), lambda b, i: (b * nq + i, 0)),
        out_shape=jax.ShapeDtypeStruct((batch * seq, width), BF16),
        scratch_shapes=[pltpu.VMEM((seq, LANE), BF16),
                        pltpu.VMEM((seq, LANE), BF16),
                        pltpu.VMEM((nq, DSA_HEADS, VT_ROWS, t), BF16),
                        pltpu.VMEM((LANE, t), F32),
                        pltpu.VMEM((nq, t, t), F32),
                        pltpu.VMEM((nq, t, t), F32),
                        pltpu.VMEM((DSA_HEADS, nq, t, t), F32),
                        pltpu.VMEM((DSA_HEADS, SUBLANES, t), F32),
                        pltpu.VMEM((DSA_HEADS, VT_ROWS, t), F32)],
        compiler_params=pltpu.CompilerParams(
            dimension_semantics=("arbitrary", "arbitrary"), vmem_limit_bytes=VMEM_LIMIT),
        name="dsa",
    )(z, ikw, z, z, z, z)


def _memkv_kernel(mem_ref, nw_ref, w_ref, o_ref, wbf_ref):
    @pl.when(pl.program_id(1) == 0)
    def _():
        wbf_ref[...] = w_ref[...].astype(BF16)

    x = mem_ref[...]
    ms = jnp.mean(x * x, axis=-1, keepdims=True)
    hn = (x * lax.rsqrt(ms + EPS) * nw_ref[...]).astype(BF16)
    o_ref[...] = jnp.dot(hn, wbf_ref[...], preferred_element_type=F32).astype(o_ref.dtype)


def _memkv(mem2d, norm_w, w_kv, tm=256, tn=1024):
    m, d = mem2d.shape
    n = w_kv.shape[1]
    return pl.pallas_call(
        _memkv_kernel,
        grid=(n // tn, m // tm),
        in_specs=[pl.BlockSpec((tm, d), lambda j, i: (i, 0)),
                  pl.BlockSpec((1, d), lambda j, i: (0, 0)),
                  pl.BlockSpec((d, tn), lambda j, i: (0, j))],
        out_specs=pl.BlockSpec((tm, tn), lambda j, i: (i, j)),
        out_shape=jax.ShapeDtypeStruct((m, n), BF16),
        scratch_shapes=[pltpu.VMEM((d, tn), BF16)],
        compiler_params=pltpu.CompilerParams(
            dimension_semantics=("arbitrary", "arbitrary"), vmem_limit_bytes=VMEM_LIMIT),
        name="mem_kv",
    )(mem2d, norm_w.reshape(1, d), w_kv)


def _memattn_kernel(q_ref, g_ref, k_ref, v_ref, o_ref):
    scale = MEM_HEAD_DIM ** -0.5
    for h in range(MEM_HEADS):
        hs = slice(h * MEM_HEAD_DIM, (h + 1) * MEM_HEAD_DIM)
        s = lax.dot_general(q_ref[:, hs], k_ref[:, hs], NT, preferred_element_type=F32) * scale
        p = jnp.exp(s - jnp.max(s, axis=-1, keepdims=True))
        l = jnp.sum(p, axis=-1, keepdims=True)
        o = jnp.dot(p.astype(BF16), v_ref[:, hs], preferred_element_type=F32) / l
        o_ref[:, hs] = (_silu(g_ref[:, hs].astype(F32)) * o).astype(o_ref.dtype)


def _memattn(z, kv, batch, seq, mem_tokens, tl=512):
    width = MEM_HEADS * MEM_HEAD_DIM
    wblk = width // LANE
    nl = seq // tl
    return pl.pallas_call(
        _memattn_kernel,
        grid=(batch, nl),
        in_specs=[pl.BlockSpec((tl, width), lambda b, i: (b * nl + i, CB_MQ // wblk)),
                  pl.BlockSpec((tl, width), lambda b, i: (b * nl + i, CB_MG // wblk)),
                  pl.BlockSpec((mem_tokens, width), lambda b, i: (b, 0)),
                  pl.BlockSpec((mem_tokens, width), lambda b, i: (b, 1))],
        out_specs=pl.BlockSpec((tl, width), lambda b, i: (b * nl + i, 0)),
        out_shape=jax.ShapeDtypeStruct((batch * seq, width), BF16),
        compiler_params=pltpu.CompilerParams(
            dimension_semantics=("arbitrary", "arbitrary"), vmem_limit_bytes=VMEM_LIMIT),
        name="mem_attn",
    )(z, z, kv, kv)


def _merge_kernel(yr_ref, yd_ref, ym_ref, wr_ref, wd_ref, wm_ref, gr_ref, gd_ref, gm_ref, o_ref,
                  wbf_ref):
    @pl.when(pl.program_id(1) == 0)
    def _():
        for b, w_ref in enumerate((wr_ref, wd_ref, wm_ref)):
            wbf_ref[b] = w_ref[...].astype(BF16)

    def branch(b, y_ref, g_ref):
        up = jnp.dot(y_ref[...], wbf_ref[b], preferred_element_type=F32)
        return _sigmoid(g_ref[...].astype(F32)) * up

    merged = branch(0, yr_ref, gr_ref) + branch(1, yd_ref, gd_ref) + branch(2, ym_ref, gm_ref)
    o_ref[...] = merged.astype(o_ref.dtype)


def _merge(y_ret, y_dsa, y_mem, w_ret, w_dsa, w_mem, z, tm=512, tn=1024):
    m, kdim = y_ret.shape
    n = w_ret.shape[1]
    gblk = tn // LANE
    yspec = pl.BlockSpec((tm, kdim), lambda j, i: (i, 0))
    wspec = pl.BlockSpec((kdim, tn), lambda j, i: (0, j))
    gspec = lambda off: pl.BlockSpec((tm, tn), lambda j, i: (i, off // gblk + j))
    return pl.pallas_call(
        _merge_kernel,
        grid=(n // tn, m // tm),
        in_specs=[yspec, yspec, yspec, wspec, wspec, wspec,
                  gspec(CB_GRET), gspec(CB_GDSA), gspec(CB_GMEM)],
        out_specs=pl.BlockSpec((tm, tn), lambda j, i: (i, j)),
        out_shape=jax.ShapeDtypeStruct((m, n), BF16),
        scratch_shapes=[pltpu.VMEM((N_BRANCHES, kdim, tn), BF16)],
        compiler_params=pltpu.CompilerParams(
            dimension_semantics=("arbitrary", "arbitrary"), vmem_limit_bytes=VMEM_LIMIT),
        name="merge",
    )(y_ret, y_dsa, y_mem, w_ret, w_dsa, w_mem, z, z, z)


def _out_kernel(m_ref, w_ref, x_ref, nw_ref, o_ref):
    out = jnp.dot(m_ref[...], w_ref[...], preferred_element_type=F32)
    ms = jnp.mean(out * out, axis=-1, keepdims=True)
    o_ref[...] = x_ref[...] + out * lax.rsqrt(ms + EPS) * nw_ref[...]


def _out_proj(merged, w_o, x2d, post_w, tm=512):
    m, d = x2d.shape
    return pl.pallas_call(
        _out_kernel,
        grid=(m // tm,),
        in_specs=[pl.BlockSpec((tm, d), lambda i: (i, 0)),
                  pl.BlockSpec((d, d), lambda i: (0, 0)),
                  pl.BlockSpec((tm, d), lambda i: (i, 0)),
                  pl.BlockSpec((1, d), lambda i: (0, 0))],
        out_specs=pl.BlockSpec((tm, d), lambda i: (i, 0)),
        out_shape=jax.ShapeDtypeStruct((m, d), F32),
        compiler_params=pltpu.CompilerParams(
            dimension_semantics=("arbitrary",), vmem_limit_bytes=VMEM_LIMIT),
        name="out_proj_postnorm",
    )(merged, w_o, x2d, post_w.reshape(1, d))


def _layer(x, mem, pre_norm_w, w_in, ret_gn_w, mem_norm_w, w_mem_kv,
           w_up_ret, w_up_dsa, w_up_mem, w_o, post_norm_w):
    batch, seq, d = x.shape
    mem_tokens = mem.shape[1]
    top_k = min(IDX_TOPK_MAX, seq // 4)
    x2d = x.reshape(batch * seq, d)

    w_in_t = w_in.T
    w_ikw = jnp.pad(w_in_t[IDX_TAIL_START:IDX_TAIL_START + IDX_TAIL_COLS],
                    ((0, LANE - IDX_TAIL_COLS), (0, 0))).astype(BF16)
    h, ikw = _prenorm(x2d, pre_norm_w, w_ikw)
    z = _project(h, w_in_t)

    log_g = jnp.log1p(-(2.0 ** (-5.0 - jnp.arange(RET_HEADS, dtype=F32))))
    y_ret = _retention(z, log_g, ret_gn_w, batch, seq)
    y_dsa = _dsa(z, ikw, batch, seq, top_k)
    kv = _memkv(mem.reshape(batch * mem_tokens, d), mem_norm_w, w_mem_kv)
    y_mem = _memattn(z, kv, batch, seq, mem_tokens)

    merged = _merge(y_ret, y_dsa, y_mem, w_up_ret, w_up_dsa, w_up_mem, z)
    out = _out_proj(merged, w_o.astype(BF16), x2d, post_norm_w)
    return out.reshape(batch, seq, d)


def kernel(x, mem, pre_norm_w, w_in, ret_gn_w, mem_norm_w, w_mem_kv, w_up_ret, w_up_dsa,
           w_up_mem, w_o, post_norm_w):
    for layer in range(w_in.shape[0]):
        x = _layer(x, mem, pre_norm_w[layer], w_in[layer], ret_gn_w[layer], mem_norm_w[layer],
                   w_mem_kv[layer], w_up_ret[layer], w_up_dsa[layer], w_up_mem[layer],
                   w_o[layer], post_norm_w[layer])
    return x
```

```python
import functools

import jax
import jax.numpy as jnp
from jax import lax
from jax.experimental import pallas as pl
from jax.experimental.pallas import tpu as pltpu

F32 = jnp.float32
BF16 = jnp.bfloat16
I32 = jnp.int32

LANE = 128
SUBLANES = 8
N_BRANCHES = 3
D_MODEL = 2048
RET_HEADS = 8
HEAD_DIM = 128
DSA_HEADS = 8
IDX_HEADS = 16
IDX_DIM = 64
IDX_TOPK_MAX = 256
MEM_HEADS = 4
MEM_HEAD_DIM = 256
NEG_INF = -1e30
EPS = 1e-6
INT_MIN = -2**31

CB_RQ, CB_RK, CB_RV, CB_RG = 0, 8, 16, 24
CB_AQ, CB_AK, CB_AV, CB_AG = 32, 40, 48, 56
CB_IQ = 64
CB_MQ, CB_MG = 72, 80
CB_GRET, CB_GDSA, CB_GMEM = 88, 104, 120
N_COL_BLOCKS = 136
Z_COLS = N_COL_BLOCKS * LANE
IDX_TAIL_START = (CB_IQ + IDX_HEADS * IDX_DIM // LANE) * LANE
IDX_TAIL_COLS = IDX_DIM + IDX_HEADS

PROJ_TM = 2048
PROJ_TN = 1024
PROJ_ALIGNED_TILES = IDX_TAIL_START // PROJ_TN
PROJ_ROWS = 512
PROJ_PREP_ROWS = 256
VMEM_LIMIT = 56 * 1024 * 1024

NT = (((1,), (1,)), ((), ()))
TN = (((0,), (0,)), ((), ()))


def _sigmoid(v):
    return 1.0 / (1.0 + jnp.exp(-v))


def _silu(v):
    return v * _sigmoid(v)


def _prenorm_kernel(x_ref, w_ref, wikw_ref, h_ref, ikw_ref):
    x = x_ref[...]
    ms = jnp.mean(x * x, axis=-1, keepdims=True)
    h = (x * lax.rsqrt(ms + EPS) * w_ref[...]).astype(h_ref.dtype)
    h_ref[...] = h
    ikw_ref[...] = lax.dot_general(h, wikw_ref[...], NT,
                                   preferred_element_type=F32).astype(ikw_ref.dtype)


def _prenorm(x2d, w, w_ikw, tm=512):
    m, d = x2d.shape
    return pl.pallas_call(
        _prenorm_kernel,
        grid=(m // tm,),
        in_specs=[pl.BlockSpec((tm, d), lambda i: (i, 0)),
                  pl.BlockSpec((1, d), lambda i: (0, 0)),
                  pl.BlockSpec((LANE, d), lambda i: (0, 0))],
        out_specs=[pl.BlockSpec((tm, d), lambda i: (i, 0)),
                   pl.BlockSpec((tm, LANE), lambda i: (i, 0))],
        out_shape=[jax.ShapeDtypeStruct((m, d), BF16),
                   jax.ShapeDtypeStruct((m, LANE), BF16)],
        compiler_params=pltpu.CompilerParams(dimension_semantics=("arbitrary",)),
        name="prenorm_idx",
    )(x2d, w.reshape(1, d), w_ikw)


def _proj_kernel(h_ref, w_ref, z_ref, wbf_ref):
    @pl.when(pl.program_id(1) == 0)
    def _():
        for r in range(0, w_ref.shape[0], PROJ_PREP_ROWS):
            wbf_ref[r:r + PROJ_PREP_ROWS, :] = w_ref[r:r + PROJ_PREP_ROWS, :].astype(BF16)

    w = wbf_ref[...]
    for r in range(0, h_ref.shape[0], PROJ_ROWS):
        z_ref[r:r + PROJ_ROWS, :] = lax.dot_general(
            h_ref[r:r + PROJ_ROWS, :], w, NT, preferred_element_type=F32).astype(z_ref.dtype)


def _project(h, w_in_t):
    m, d = h.shape
    tm = min(PROJ_TM, m)

    def w_rows(j, i):
        skip = jnp.where(j >= PROJ_ALIGNED_TILES, IDX_TAIL_COLS // SUBLANES, 0)
        return ((j * (PROJ_TN // SUBLANES) + skip) * SUBLANES, 0)

    return pl.pallas_call(
        _proj_kernel,
        grid=(Z_COLS // PROJ_TN, m // tm),
        in_specs=[pl.BlockSpec((tm, d), lambda j, i: (i, 0)),
                  pl.BlockSpec((pl.Element(PROJ_TN), pl.Element(d)), w_rows)],
        out_specs=pl.BlockSpec((tm, PROJ_TN), lambda j, i: (i, j)),
        out_shape=jax.ShapeDtypeStruct((m, Z_COLS), BF16),
        scratch_shapes=[pltpu.VMEM((PROJ_TN, d), BF16)],
        compiler_params=pltpu.CompilerParams(
            dimension_semantics=("arbitrary", "arbitrary"), vmem_limit_bytes=VMEM_LIMIT),
        name="in_proj",
    )(h, w_in_t)


RET_C = 256


def _retention_kernel(lg_ref, q_ref, k_ref, v_ref, g_ref, gw_ref, o_ref):
    h = pl.program_id(1)
    lg = lg_ref[h]
    c = RET_C
    seq = q_ref.shape[0]
    scale = HEAD_DIM ** -0.5
    ri = lax.broadcasted_iota(I32, (c, c), 0)
    ci = lax.broadcasted_iota(I32, (c, c), 1)
    diff = (ri - ci).astype(F32)
    decay = jnp.where(diff >= 0, jnp.exp(lg * jnp.maximum(diff, 0.0)), 0.0) * scale
    idx = lax.broadcasted_iota(I32, (c, 1), 0).astype(F32)
    q_dec = jnp.exp(lg * (idx + 1.0))
    k_dec = jnp.exp(lg * (c - 1.0 - idx)) * scale
    chunk_dec = jnp.exp(lg * jnp.full((1, HEAD_DIM), float(c), F32))
    gw = gw_ref[...]

    state = jnp.zeros((HEAD_DIM, HEAD_DIM), F32)
    for i in range(seq // c):
        r = i * c
        qi = q_ref[pl.ds(r, c), :]
        ki = k_ref[pl.ds(r, c), :]
        vi = v_ref[pl.ds(r, c), :]
        inner = lax.dot_general(qi, ki, NT, preferred_element_type=F32) * decay
        o = (jnp.dot(inner.astype(BF16), vi, preferred_element_type=F32)
             + jnp.dot(qi, state.astype(BF16), preferred_element_type=F32) * q_dec)
        vs = (vi.astype(F32) * k_dec).astype(BF16)
        state = state * chunk_dec + lax.dot_general(ki, vs, TN, preferred_element_type=F32)
        mu = jnp.mean(o, axis=-1, keepdims=True)
        oc = o - mu
        var = jnp.mean(oc * oc, axis=-1, keepdims=True)
        on = oc * lax.rsqrt(var + 1e-5) * gw
        g = g_ref[pl.ds(r, c), :].astype(F32)
        o_ref[pl.ds(r, c), :] = (_silu(g) * on).astype(o_ref.dtype)


def _retention(z, log_g, gn_w, batch, seq):
    blk = lambda off: pl.BlockSpec((seq, HEAD_DIM), lambda b, h: (b, off + h))
    return pl.pallas_call(
        _retention_kernel,
        grid=(batch, RET_HEADS),
        in_specs=[pl.BlockSpec(memory_space=pltpu.SMEM),
                  blk(CB_RQ), blk(CB_RK), blk(CB_RV), blk(CB_RG),
                  pl.BlockSpec((1, HEAD_DIM), lambda b, h: (0, h))],
        out_specs=pl.BlockSpec((seq, HEAD_DIM), lambda b, h: (b, h)),
        out_shape=jax.ShapeDtypeStruct((batch * seq, RET_HEADS * HEAD_DIM), BF16),
        compiler_params=pltpu.CompilerParams(
            dimension_semantics=("arbitrary", "arbitrary"), vmem_limit_bytes=VMEM_LIMIT),
        name="retention",
    )(log_g, z, z, z, z, gn_w.reshape(1, -1))


DSA_T = 256
VT_ROWS = HEAD_DIM + 16


def _fold_rows(x, op, rows=SUBLANES):
    while x.shape[0] > rows:
        half = x.shape[0] // 2
        x = op(x[:half], x[half:])
    return x


I16 = jnp.int16
PACKED_ROWS = 2 * SUBLANES
ORD_NEG_INF = -2**31 + 0x7FFFFF


def _flip(v):
    return v ^ ((v >> 31) & 0x7FFFFFFF)


def _ordinal_to_f32(o):
    return lax.bitcast_convert_type(_flip(jnp.maximum(o, ORD_NEG_INF)), F32)


def _prefix_to_bf16_bits(p):
    return _flip(jnp.maximum(p, ORD_NEG_INF >> 16) << 16) & -65536


def _dsa_kernel(iq_ref, ikw_ref, aq_ref, ak_ref, av_ref, ag_ref, o_ref,
                ika_ref, ikb_ref, vt_ref, wt_ref, score_ref, score16_ref, bias_ref, logit_ref,
                m_ref, acc_ref, *, top_k):
    t = DSA_T
    qb = pl.program_id(1)
    nk = qb + 1
    q0 = pl.multiple_of(qb * t, t)
    nq = vt_ref.shape[0]

    @pl.when(qb == 0)
    def _():
        ikw = ikw_ref[...].astype(F32)
        lane = lax.broadcasted_iota(I32, ikw.shape, 1)
        a = jnp.where(lane < IDX_DIM, ikw, 0.0)
        ika_ref[...] = a.astype(BF16)
        ikb_ref[...] = pltpu.roll(a, IDX_DIM, axis=1).astype(BF16)

        def vt_tile(kt, carry):
            k0 = pl.multiple_of(kt * t, t)
            for h in range(DSA_HEADS):
                hs = slice(h * HEAD_DIM, (h + 1) * HEAD_DIM)
                vt_ref[kt, h, :HEAD_DIM, :] = av_ref[pl.ds(k0, t), hs].astype(F32).T.astype(BF16)
                vt_ref[kt, h, HEAD_DIM:, :] = jnp.ones((VT_ROWS - HEAD_DIM, t), BF16)
            return carry

        lax.fori_loop(0, nq, vt_tile, 0)

    wt_ref[...] = ikw_ref[pl.ds(q0, t), :].astype(F32).T

    key_row = lax.broadcasted_iota(I32, (t, t), 0)
    qry_col = lax.broadcasted_iota(I32, (t, t), 1)

    def score_tile(kt, carry):
        k0 = pl.multiple_of(kt * t, t)
        ka = ika_ref[pl.ds(k0, t), :]
        kb = ikb_ref[pl.ds(k0, t), :]
        acc = jnp.zeros((t, t), F32)
        for j in range(IDX_HEADS // 2):
            qj = iq_ref[:, j * LANE:(j + 1) * LANE]
            for half, kx in enumerate((ka, kb)):
                hh = IDX_DIM + 2 * j + half
                s = lax.dot_general(kx, qj, NT, preferred_element_type=F32)
                acc = acc + jnp.maximum(s, 0.0) * wt_ref[hh:hh + 1, :]
        causal = (key_row + k0) <= (qry_col + q0)
        score = jnp.where(causal, acc, -jnp.inf)
        score_ref[kt] = score
        score16_ref[kt] = score.astype(BF16)
        return carry

    lax.fori_loop(0, nk, score_tile, 0)

    one16, zero16 = jnp.ones((), I16), jnp.zeros((), I16)

    def prefix_step(i, pfx):
        cand_pfx = pfx + lax.shift_left(jnp.int32(1), 15 - i)
        cand = lax.bitcast_convert_type(_prefix_to_bf16_bits(cand_pfx), F32)
        cand16 = jnp.broadcast_to(cand, (PACKED_ROWS, t)).astype(BF16)[:1]

        def count_tile(kt, cnt):
            ge = jnp.where(score16_ref[kt] >= cand16, one16, zero16)
            return cnt + _fold_rows(ge, jnp.add, PACKED_ROWS)

        cnt = lax.fori_loop(0, nk, count_tile, jnp.zeros((PACKED_ROWS, t), I16))
        total = jnp.sum(cnt.astype(I32), axis=0, keepdims=True)
        return jnp.where(total >= top_k, cand_pfx, pfx)

    pfx = lax.fori_loop(0, 16, prefix_step, jnp.full((1, t), -2**15, I32))

    def bisect_step(i, bounds):
        lo, hi = bounds
        mid = lo + ((hi - lo) >> 1)
        cand = _ordinal_to_f32(mid)

        def count_tile(kt, cnt):
            return cnt + _fold_rows(jnp.where(score_ref[kt] >= cand, 1, 0), jnp.add)

        cnt = lax.fori_loop(0, nk, count_tile, jnp.zeros((SUBLANES, t), I32))
        enough = jnp.sum(cnt, axis=0, keepdims=True) >= top_k
        return jnp.where(enough, mid, lo), jnp.where(enough, hi, mid)

    lo, _ = lax.fori_loop(0, 17, bisect_step, (_flip(_prefix_to_bf16_bits(pfx - 1)),
                                               _flip(_prefix_to_bf16_bits(pfx + 1))))
    thr = _ordinal_to_f32(lo)

    def bias_tile(kt, carry):
        k0 = pl.multiple_of(kt * t, t)
        keep = jnp.logical_and(score_ref[kt] >= thr, (key_row + k0) <= (qry_col + q0))
        bias_ref[kt] = jnp.where(keep, 0.0, NEG_INF)
        return carry

    lax.fori_loop(0, nk, bias_tile, 0)

    m_ref[...] = jnp.full(m_ref.shape, NEG_INF, F32)
    acc_ref[...] = jnp.zeros(acc_ref.shape, F32)
    log2e = 1.4426950408889634
    scale = (HEAD_DIM ** -0.5) * log2e

    def logit_tile(kt, carry):
        k0 = pl.multiple_of(kt * t, t)
        bias = bias_ref[kt]
        kpos = (key_row + (k0 - q0)).astype(F32)
        for h in range(DSA_HEADS):
            slope = (2.0 ** (-8.0 * (h + 1) / DSA_HEADS)) * log2e
            hs = slice(h * HEAD_DIM, (h + 1) * HEAD_DIM)
            s = lax.dot_general(ak_ref[pl.ds(k0, t), hs], aq_ref[:, hs], NT,
                                preferred_element_type=F32)
            lg = s * scale + (kpos * slope + bias)
            logit_ref[h, kt] = lg
            m_ref[h] = jnp.maximum(m_ref[h], _fold_rows(lg, jnp.maximum))
        return carry

    lax.fori_loop(0, nk, logit_tile, 0)

    for h in range(DSA_HEADS):
        m_ref[h] = jnp.broadcast_to(jnp.max(m_ref[h], axis=0, keepdims=True), (SUBLANES, t))

    def pv_tile(kt, carry):
        for h in range(DSA_HEADS):
            p = jnp.exp2(logit_ref[h, kt] - m_ref[h][:1])
            acc_ref[h] = acc_ref[h] + jnp.dot(vt_ref[kt, h], p.astype(BF16),
                                              preferred_element_type=F32)
        return carry

    lax.fori_loop(0, nk, pv_tile, 0)

    for h in range(DSA_HEADS):
        hs = slice(h * HEAD_DIM, (h + 1) * HEAD_DIM)
        acc = acc_ref[h]
        o = (acc[:HEAD_DIM] / acc[HEAD_DIM:HEAD_DIM + 1]).T
        o_ref[:, hs] = (_silu(ag_ref[:, hs].astype(F32)) * o).astype(o_ref.dtype)


def _dsa(z, ikw, batch, seq, top_k):
    t = DSA_T
    nq = seq // t
    width = DSA_HEADS * HEAD_DIM
    wblk = width // LANE
    qspec = lambda off: pl.BlockSpec((t, width), lambda b, i: (b * nq + i, off // wblk))
    kvspec = lambda off: pl.BlockSpec((seq, width), lambda b, i: (b, off // wblk))
    return pl.pallas_call(
        functools.partial(_dsa_kernel, top_k=top_k),
        grid=(batch, nq),
        in_specs=[qspec(CB_IQ),
                  pl.BlockSpec((seq, LANE), lambda b, i: (b, 0)),
                  qspec(CB_AQ), kvspec(CB_AK), kvspec(CB_AV), qspec(CB_AG)],
        out_specs=pl.BlockSpec((t, width), lambda b, i: (b * nq + i, 0)),
        out_shape=jax.ShapeDtypeStruct((batch * seq, width), BF16),
        scratch_shapes=[pltpu.VMEM((seq, LANE), BF16),
                        pltpu.VMEM((seq, LANE), BF16),
                        pltpu.VMEM((nq, DSA_HEADS, VT_ROWS, t), BF16),
                        pltpu.VMEM((LANE, t), F32),
                        pltpu.VMEM((nq, t, t), F32),
                        pltpu.VMEM((nq, t, t), BF16),
                        pltpu.VMEM((nq, t, t), F32),
                        pltpu.VMEM((DSA_HEADS, nq, t, t), F32),
                        pltpu.VMEM((DSA_HEADS, SUBLANES, t), F32),
                        pltpu.VMEM((DSA_HEADS, VT_ROWS, t), F32)],
        compiler_params=pltpu.CompilerParams(
            dimension_semantics=("arbitrary", "arbitrary"), vmem_limit_bytes=VMEM_LIMIT),
        name="dsa",
    )(z, ikw, z, z, z, z)


def _memkv_kernel(mem_ref, nw_ref, w_ref, o_ref, wbf_ref):
    @pl.when(pl.program_id(1) == 0)
    def _():
        wbf_ref[...] = w_ref[...].astype(BF16)

    x = mem_ref[...]
    ms = jnp.mean(x * x, axis=-1, keepdims=True)
    hn = (x * lax.rsqrt(ms + EPS) * nw_ref[...]).astype(BF16)
    o_ref[...] = jnp.dot(hn, wbf_ref[...], preferred_element_type=F32).astype(o_ref.dtype)


def _memkv(mem2d, norm_w, w_kv, tm=256, tn=1024):
    m, d = mem2d.shape
    n = w_kv.shape[1]
    return pl.pallas_call(
        _memkv_kernel,
        grid=(n // tn, m // tm),
        in_specs=[pl.BlockSpec((tm, d), lambda j, i: (i, 0)),
                  pl.BlockSpec((1, d), lambda j, i: (0, 0)),
                  pl.BlockSpec((d, tn), lambda j, i: (0, j))],
        out_specs=pl.BlockSpec((tm, tn), lambda j, i: (i, j)),
        out_shape=jax.ShapeDtypeStruct((m, n), BF16),
        scratch_shapes=[pltpu.VMEM((d, tn), BF16)],
        compiler_params=pltpu.CompilerParams(
            dimension_semantics=("arbitrary", "arbitrary"), vmem_limit_bytes=VMEM_LIMIT),
        name="mem_kv",
    )(mem2d, norm_w.reshape(1, d), w_kv)


def _memattn_kernel(q_ref, g_ref, k_ref, v_ref, o_ref):
    scale = MEM_HEAD_DIM ** -0.5
    for h in range(MEM_HEADS):
        hs = slice(h * MEM_HEAD_DIM, (h + 1) * MEM_HEAD_DIM)
        s = lax.dot_general(q_ref[:, hs], k_ref[:, hs], NT, preferred_element_type=F32) * scale
        p = jnp.exp(s - jnp.max(s, axis=-1, keepdims=True))
        l = jnp.sum(p, axis=-1, keepdims=True)
        o = jnp.dot(p.astype(BF16), v_ref[:, hs], preferred_element_type=F32) / l
        o_ref[:, hs] = (_silu(g_ref[:, hs].astype(F32)) * o).astype(o_ref.dtype)


def _memattn(z, kv, batch, seq, mem_tokens, tl=512):
    width = MEM_HEADS * MEM_HEAD_DIM
    wblk = width // LANE
    nl = seq // tl
    return pl.pallas_call(
        _memattn_kernel,
        grid=(batch, nl),
        in_specs=[pl.BlockSpec((tl, width), lambda b, i: (b * nl + i, CB_MQ // wblk)),
                  pl.BlockSpec((tl, width), lambda b, i: (b * nl + i, CB_MG // wblk)),
                  pl.BlockSpec((mem_tokens, width), lambda b, i: (b, 0)),
                  pl.BlockSpec((mem_tokens, width), lambda b, i: (b, 1))],
        out_specs=pl.BlockSpec((tl, width), lambda b, i: (b * nl + i, 0)),
        out_shape=jax.ShapeDtypeStruct((batch * seq, width), BF16),
        compiler_params=pltpu.CompilerParams(
            dimension_semantics=("arbitrary", "arbitrary"), vmem_limit_bytes=VMEM_LIMIT),
        name="mem_attn",
    )(z, z, kv, kv)


def _merge_kernel(yr_ref, yd_ref, ym_ref, wr_ref, wd_ref, wm_ref, gr_ref, gd_ref, gm_ref, o_ref,
                  wbf_ref):
    @pl.when(pl.program_id(1) == 0)
    def _():
        for b, w_ref in enumerate((wr_ref, wd_ref, wm_ref)):
            wbf_ref[b] = w_ref[...].astype(BF16)

    def branch(b, y_ref, g_ref):
        up = jnp.dot(y_ref[...], wbf_ref[b], preferred_element_type=F32)
        return _sigmoid(g_ref[...].astype(F32)) * up

    merged = branch(0, yr_ref, gr_ref) + branch(1, yd_ref, gd_ref) + branch(2, ym_ref, gm_ref)
    o_ref[...] = merged.astype(o_ref.dtype)


def _merge(y_ret, y_dsa, y_mem, w_ret, w_dsa, w_mem, z, tm=512, tn=1024):
    m, kdim = y_ret.shape
    n = w_ret.shape[1]
    gblk = tn // LANE
    yspec = pl.BlockSpec((tm, kdim), lambda j, i: (i, 0))
    wspec = pl.BlockSpec((kdim, tn), lambda j, i: (0, j))
    gspec = lambda off: pl.BlockSpec((tm, tn), lambda j, i: (i, off // gblk + j))
    return pl.pallas_call(
        _merge_kernel,
        grid=(n // tn, m // tm),
        in_specs=[yspec, yspec, yspec, wspec, wspec, wspec,
                  gspec(CB_GRET), gspec(CB_GDSA), gspec(CB_GMEM)],
        out_specs=pl.BlockSpec((tm, tn), lambda j, i: (i, j)),
        out_shape=jax.ShapeDtypeStruct((m, n), BF16),
        scratch_shapes=[pltpu.VMEM((N_BRANCHES, kdim, tn), BF16)],
        compiler_params=pltpu.CompilerParams(
            dimension_semantics=("arbitrary", "arbitrary"), vmem_limit_bytes=VMEM_LIMIT),
        name="merge",
    )(y_ret, y_dsa, y_mem, w_ret, w_dsa, w_mem, z, z, z)


def _out_kernel(m_ref, w_ref, x_ref, nw_ref, o_ref):
    out = jnp.dot(m_ref[...], w_ref[...], preferred_element_type=F32)
    ms = jnp.mean(out * out, axis=-1, keepdims=True)
    o_ref[...] = x_ref[...] + out * lax.rsqrt(ms + EPS) * nw_ref[...]


def _out_proj(merged, w_o, x2d, post_w, tm=512):
    m, d = x2d.shape
    return pl.pallas_call(
        _out_kernel,
        grid=(m // tm,),
        in_specs=[pl.BlockSpec((tm, d), lambda i: (i, 0)),
                  pl.BlockSpec((d, d), lambda i: (0, 0)),
                  pl.BlockSpec((tm, d), lambda i: (i, 0)),
                  pl.BlockSpec((1, d), lambda i: (0, 0))],
        out_specs=pl.BlockSpec((tm, d), lambda i: (i, 0)),
        out_shape=jax.ShapeDtypeStruct((m, d), F32),
        compiler_params=pltpu.CompilerParams(
            dimension_semantics=("arbitrary",), vmem_limit_bytes=VMEM_LIMIT),
        name="out_proj_postnorm",
    )(merged, w_o, x2d, post_w.reshape(1, d))


def _layer(x, mem, pre_norm_w, w_in, ret_gn_w, mem_norm_w, w_mem_kv,
           w_up_ret, w_up_dsa, w_up_mem, w_o, post_norm_w):
    batch, seq, d = x.shape
    mem_tokens = mem.shape[1]
    top_k = min(IDX_TOPK_MAX, seq // 4)
    x2d = x.reshape(batch * seq, d)

    w_in_t = w_in.T
    w_ikw = jnp.pad(w_in_t[IDX_TAIL_START:IDX_TAIL_START + IDX_TAIL_COLS],
                    ((0, LANE - IDX_TAIL_COLS), (0, 0))).astype(BF16)
    h, ikw = _prenorm(x2d, pre_norm_w, w_ikw)
    z = _project(h, w_in_t)

    log_g = jnp.log1p(-(2.0 ** (-5.0 - jnp.arange(RET_HEADS, dtype=F32))))
    y_ret = _retention(z, log_g, ret_gn_w, batch, seq)
    y_dsa = _dsa(z, ikw, batch, seq, top_k)
    kv = _memkv(mem.reshape(batch * mem_tokens, d), mem_norm_w, w_mem_kv)
    y_mem = _memattn(z, kv, batch, seq, mem_tokens)

    merged = _merge(y_ret, y_dsa, y_mem, w_up_ret, w_up_dsa, w_up_mem, z)
    out = _out_proj(merged, w_o.astype(BF16), x2d, post_norm_w)
    return out.reshape(batch, seq, d)


def kernel(x, mem, pre_norm_w, w_in, ret_gn_w, mem_norm_w, w_mem_kv, w_up_ret, w_up_dsa,
           w_up_mem, w_o, post_norm_w):
    for layer in range(w_in.shape[0]):
        x = _layer(x, mem, pre_norm_w[layer], w_in[layer], ret_gn_w[layer], mem_norm_w[layer],
                   w_mem_kv[layer], w_up_ret[layer], w_up_dsa[layer], w_up_mem[layer],
                   w_o[layer], post_norm_w[layer])
    return x
```

```python
import functools

import jax
import jax.numpy as jnp
from jax import lax
from jax.experimental import pallas as pl
from jax.experimental.pallas import tpu as pltpu

F32 = jnp.float32
BF16 = jnp.bfloat16
I32 = jnp.int32

LANE = 128
SUBLANES = 8
N_BRANCHES = 3
D_MODEL = 2048
RET_HEADS = 8
HEAD_DIM = 128
DSA_HEADS = 8
IDX_HEADS = 16
IDX_DIM = 64
IDX_TOPK_MAX = 256
MEM_HEADS = 4
MEM_HEAD_DIM = 256
NEG_INF = -1e30
EPS = 1e-6
INT_MIN = -2**31

CB_RQ, CB_RK, CB_RV, CB_RG = 0, 8, 16, 24
CB_AQ, CB_AK, CB_AV, CB_AG = 32, 40, 48, 56
CB_IQ = 64
CB_MQ, CB_MG = 72, 80
CB_GRET, CB_GDSA, CB_GMEM = 88, 104, 120
N_COL_BLOCKS = 136
Z_COLS = N_COL_BLOCKS * LANE
IDX_TAIL_START = (CB_IQ + IDX_HEADS * IDX_DIM // LANE) * LANE
IDX_TAIL_COLS = IDX_DIM + IDX_HEADS

PROJ_TM = 2048
PROJ_TN = 1024
PROJ_ALIGNED_TILES = IDX_TAIL_START // PROJ_TN
PROJ_ROWS = 512
PROJ_PREP_ROWS = 256
VMEM_LIMIT = 56 * 1024 * 1024
SINGLE_BUFFER = pl.Buffered(1)

NT = (((1,), (1,)), ((), ()))
TN = (((0,), (0,)), ((), ()))


def _sigmoid(v):
    return 1.0 / (1.0 + jnp.exp(-v))


def _silu(v):
    return v * _sigmoid(v)


def _prenorm_kernel(x_ref, w_ref, wikw_ref, h_ref, ikw_ref):
    x = x_ref[...]
    ms = jnp.mean(x * x, axis=-1, keepdims=True)
    h = (x * lax.rsqrt(ms + EPS) * w_ref[...]).astype(h_ref.dtype)
    h_ref[...] = h
    ikw_ref[...] = lax.dot_general(h, wikw_ref[...], NT,
                                   preferred_element_type=F32).astype(ikw_ref.dtype)


def _prenorm(x2d, w, w_ikw, tm=512):
    m, d = x2d.shape
    return pl.pallas_call(
        _prenorm_kernel,
        grid=(m // tm,),
        in_specs=[pl.BlockSpec((tm, d), lambda i: (i, 0)),
                  pl.BlockSpec((1, d), lambda i: (0, 0)),
                  pl.BlockSpec((LANE, d), lambda i: (0, 0))],
        out_specs=[pl.BlockSpec((tm, d), lambda i: (i, 0)),
                   pl.BlockSpec((tm, LANE), lambda i: (i, 0))],
        out_shape=[jax.ShapeDtypeStruct((m, d), BF16),
                   jax.ShapeDtypeStruct((m, LANE), BF16)],
        compiler_params=pltpu.CompilerParams(dimension_semantics=("arbitrary",)),
        name="prenorm_idx",
    )(x2d, w.reshape(1, d), w_ikw)


def _proj_kernel(h_ref, w_ref, z_ref, wbf_ref):
    @pl.when(pl.program_id(1) == 0)
    def _():
        for r in range(0, w_ref.shape[0], PROJ_PREP_ROWS):
            wbf_ref[r:r + PROJ_PREP_ROWS, :] = w_ref[r:r + PROJ_PREP_ROWS, :].astype(BF16)

    w = wbf_ref[...]
    for r in range(0, h_ref.shape[0], PROJ_ROWS):
        z_ref[r:r + PROJ_ROWS, :] = lax.dot_general(
            h_ref[r:r + PROJ_ROWS, :], w, NT, preferred_element_type=F32).astype(z_ref.dtype)


def _project(h, w_in_t):
    m, d = h.shape
    tm = min(PROJ_TM, m)

    def w_rows(j, i):
        skip = jnp.where(j >= PROJ_ALIGNED_TILES, IDX_TAIL_COLS // SUBLANES, 0)
        return ((j * (PROJ_TN // SUBLANES) + skip) * SUBLANES, 0)

    return pl.pallas_call(
        _proj_kernel,
        grid=(Z_COLS // PROJ_TN, m // tm),
        in_specs=[pl.BlockSpec((tm, d), lambda j, i: (i, 0)),
                  pl.BlockSpec((pl.Element(PROJ_TN), pl.Element(d)), w_rows)],
        out_specs=pl.BlockSpec((tm, PROJ_TN), lambda j, i: (i, j)),
        out_shape=jax.ShapeDtypeStruct((m, Z_COLS), BF16),
        scratch_shapes=[pltpu.VMEM((PROJ_TN, d), BF16)],
        compiler_params=pltpu.CompilerParams(
            dimension_semantics=("arbitrary", "arbitrary"), vmem_limit_bytes=VMEM_LIMIT),
        name="in_proj",
    )(h, w_in_t)


RET_C = 256


def _retention_kernel(lg_ref, q_ref, k_ref, v_ref, g_ref, gw_ref, o_ref):
    h = pl.program_id(1)
    lg = lg_ref[h]
    c = RET_C
    seq = q_ref.shape[0]
    scale = HEAD_DIM ** -0.5
    ri = lax.broadcasted_iota(I32, (c, c), 0)
    ci = lax.broadcasted_iota(I32, (c, c), 1)
    diff = (ri - ci).astype(F32)
    decay = jnp.where(diff >= 0, jnp.exp(lg * jnp.maximum(diff, 0.0)), 0.0) * scale
    idx = lax.broadcasted_iota(I32, (c, 1), 0).astype(F32)
    q_dec = jnp.exp(lg * (idx + 1.0))
    k_dec = jnp.exp(lg * (c - 1.0 - idx)) * scale
    chunk_dec = jnp.exp(lg * jnp.full((1, HEAD_DIM), float(c), F32))
    gw = gw_ref[...]

    state = jnp.zeros((HEAD_DIM, HEAD_DIM), F32)
    for i in range(seq // c):
        r = i * c
        qi = q_ref[pl.ds(r, c), :]
        ki = k_ref[pl.ds(r, c), :]
        vi = v_ref[pl.ds(r, c), :]
        inner = lax.dot_general(qi, ki, NT, preferred_element_type=F32) * decay
        o = (jnp.dot(inner.astype(BF16), vi, preferred_element_type=F32)
             + jnp.dot(qi, state.astype(BF16), preferred_element_type=F32) * q_dec)
        vs = (vi.astype(F32) * k_dec).astype(BF16)
        state = state * chunk_dec + lax.dot_general(ki, vs, TN, preferred_element_type=F32)
        mu = jnp.mean(o, axis=-1, keepdims=True)
        oc = o - mu
        var = jnp.mean(oc * oc, axis=-1, keepdims=True)
        on = oc * lax.rsqrt(var + 1e-5) * gw
        g = g_ref[pl.ds(r, c), :].astype(F32)
        o_ref[pl.ds(r, c), :] = (_silu(g) * on).astype(o_ref.dtype)


def _retention(z, log_g, gn_w, batch, seq):
    blk = lambda off: pl.BlockSpec((seq, HEAD_DIM), lambda b, h: (b, off + h))
    return pl.pallas_call(
        _retention_kernel,
        grid=(batch, RET_HEADS),
        in_specs=[pl.BlockSpec(memory_space=pltpu.SMEM),
                  blk(CB_RQ), blk(CB_RK), blk(CB_RV), blk(CB_RG),
                  pl.BlockSpec((1, HEAD_DIM), lambda b, h: (0, h))],
        out_specs=pl.BlockSpec((seq, HEAD_DIM), lambda b, h: (b, h)),
        out_shape=jax.ShapeDtypeStruct((batch * seq, RET_HEADS * HEAD_DIM), BF16),
        compiler_params=pltpu.CompilerParams(
            dimension_semantics=("arbitrary", "arbitrary"), vmem_limit_bytes=VMEM_LIMIT),
        name="retention",
    )(log_g, z, z, z, z, gn_w.reshape(1, -1))


DSA_T = 256
VT_ROWS = HEAD_DIM + 16

def _fold_rows(x, op, rows=SUBLANES):
    while x.shape[0] > rows:
        half = x.shape[0] // 2
        x = op(x[:half], x[half:])
    return x


I16 = jnp.int16
PACKED_ROWS = 2 * SUBLANES
ORD_NEG_INF = -2**31 + 0x7FFFFF


def _flip(v):
    return v ^ ((v >> 31) & 0x7FFFFFFF)


def _ordinal_to_f32(o):
    return lax.bitcast_convert_type(_flip(jnp.maximum(o, ORD_NEG_INF)), F32)


def _prefix_to_bf16_bits(p):
    return _flip(jnp.maximum(p, ORD_NEG_INF >> 16) << 16) & -65536


def _dsa_kernel(iq_ref, ikw_ref, aq_ref, ak_ref, av_ref, ag_ref, o_ref,
                ika_ref, ikb_ref, vt_ref, wt_ref, score_ref, score16_ref, bias_ref, logit_ref,
                m_ref, acc_ref, *, top_k):
    t = DSA_T
    qb = pl.program_id(1)
    nk = qb + 1
    q0 = pl.multiple_of(qb * t, t)
    nq = vt_ref.shape[0]

    @pl.when(qb == 0)
    def _():
        ikw = ikw_ref[...].astype(F32)
        lane = lax.broadcasted_iota(I32, ikw.shape, 1)
        a = jnp.where(lane < IDX_DIM, ikw, 0.0)
        ika_ref[...] = a.astype(BF16)
        ikb_ref[...] = pltpu.roll(a, IDX_DIM, axis=1).astype(BF16)

        def vt_tile(kt, carry):
            k0 = pl.multiple_of(kt * t, t)
            for h in range(DSA_HEADS):
                hs = slice(h * HEAD_DIM, (h + 1) * HEAD_DIM)
                vt_ref[kt, h, :HEAD_DIM, :] = av_ref[pl.ds(k0, t), hs].astype(F32).T.astype(BF16)
                vt_ref[kt, h, HEAD_DIM:, :] = jnp.ones((VT_ROWS - HEAD_DIM, t), BF16)
            return carry

        lax.fori_loop(0, nq, vt_tile, 0)

    wt_ref[...] = ikw_ref[pl.ds(q0, t), :].astype(F32).T

    key_row = lax.broadcasted_iota(I32, (t, t), 0)
    qry_col = lax.broadcasted_iota(I32, (t, t), 1)

    def score_tile(kt, carry):
        k0 = pl.multiple_of(kt * t, t)
        ka = ika_ref[pl.ds(k0, t), :]
        kb = ikb_ref[pl.ds(k0, t), :]
        acc = jnp.zeros((t, t), F32)
        for j in range(IDX_HEADS // 2):
            qj = iq_ref[:, j * LANE:(j + 1) * LANE]
            for half, kx in enumerate((ka, kb)):
                hh = IDX_DIM + 2 * j + half
                s = lax.dot_general(kx, qj, NT, preferred_element_type=F32)
                acc = acc + jnp.maximum(s, 0.0) * wt_ref[hh:hh + 1, :]
        causal = (key_row + k0) <= (qry_col + q0)
        score = jnp.where(causal, acc, -jnp.inf)
        score_ref[kt] = score
        score16_ref[kt] = score.astype(BF16)
        return carry

    lax.fori_loop(0, nk, score_tile, 0)

    one16, zero16 = jnp.ones((), I16), jnp.zeros((), I16)

    def prefix_step(i, pfx):
        cand_pfx = pfx + lax.shift_left(jnp.int32(1), 15 - i)
        cand = lax.bitcast_convert_type(_prefix_to_bf16_bits(cand_pfx), F32)
        cand16 = jnp.broadcast_to(cand, (PACKED_ROWS, t)).astype(BF16)[:1]

        def count_tile(kt, cnt):
            ge = jnp.where(score16_ref[kt] >= cand16, one16, zero16)
            return cnt + _fold_rows(ge, jnp.add, PACKED_ROWS)

        cnt = lax.fori_loop(0, nk, count_tile, jnp.zeros((PACKED_ROWS, t), I16))
        total = jnp.sum(cnt.astype(I32), axis=0, keepdims=True)
        return jnp.where(total >= top_k, cand_pfx, pfx)

    pfx = lax.fori_loop(0, 16, prefix_step, jnp.full((1, t), -2**15, I32))

    def bisect_step(i, bounds):
        lo, hi = bounds
        mid = lo + ((hi - lo) >> 1)
        cand = _ordinal_to_f32(mid)

        def count_tile(kt, cnt):
            return cnt + _fold_rows(jnp.where(score_ref[kt] >= cand, 1, 0), jnp.add)

        cnt = lax.fori_loop(0, nk, count_tile, jnp.zeros((SUBLANES, t), I32))
        enough = jnp.sum(cnt, axis=0, keepdims=True) >= top_k
        return jnp.where(enough, mid, lo), jnp.where(enough, hi, mid)

    lo, _ = lax.fori_loop(0, 17, bisect_step, (_flip(_prefix_to_bf16_bits(pfx - 1)),
                                               _flip(_prefix_to_bf16_bits(pfx + 1))))
    thr = _ordinal_to_f32(lo)

    def bias_tile(kt, carry):
        k0 = pl.multiple_of(kt * t, t)
        keep = jnp.logical_and(score_ref[kt] >= thr, (key_row + k0) <= (qry_col + q0))
        bias_ref[kt] = jnp.where(keep, 0.0, NEG_INF)
        return carry

    lax.fori_loop(0, nk, bias_tile, 0)

    m_ref[...] = jnp.full(m_ref.shape, NEG_INF, F32)
    acc_ref[...] = jnp.zeros(acc_ref.shape, F32)
    log2e = 1.4426950408889634
    scale = (HEAD_DIM ** -0.5) * log2e

    def logit_tile(kt, carry):
        k0 = pl.multiple_of(kt * t, t)
        bias = bias_ref[kt]
        kpos = (key_row + (k0 - q0)).astype(F32)
        for h in range(DSA_HEADS):
            slope = (2.0 ** (-8.0 * (h + 1) / DSA_HEADS)) * log2e
            hs = slice(h * HEAD_DIM, (h + 1) * HEAD_DIM)
            s = lax.dot_general(ak_ref[pl.ds(k0, t), hs], aq_ref[:, hs], NT,
                                preferred_element_type=F32)
            lg = s * scale + (kpos * slope + bias)
            logit_ref[h, kt] = lg
            m_ref[h] = jnp.maximum(m_ref[h], _fold_rows(lg, jnp.maximum))
        return carry

    lax.fori_loop(0, nk, logit_tile, 0)

    for h in range(DSA_HEADS):
        m_ref[h] = jnp.broadcast_to(jnp.max(m_ref[h], axis=0, keepdims=True), (SUBLANES, t))

    def pv_tile(kt, carry):
        for h in range(DSA_HEADS):
            p = jnp.exp2(logit_ref[h, kt] - m_ref[h][:1])
            acc_ref[h] = acc_ref[h] + jnp.dot(vt_ref[kt, h], p.astype(BF16),
                                              preferred_element_type=F32)
        return carry

    lax.fori_loop(0, nk, pv_tile, 0)

    for h in range(DSA_HEADS):
        hs = slice(h * HEAD_DIM, (h + 1) * HEAD_DIM)
        acc = acc_ref[h]
        o = (acc[:HEAD_DIM] / acc[HEAD_DIM:HEAD_DIM + 1]).T
        o_ref[:, hs] = (_silu(ag_ref[:, hs].astype(F32)) * o).astype(o_ref.dtype)


def _dsa(z, ikw, batch, seq, top_k):
    t = DSA_T
    nq = seq // t
    width = DSA_HEADS * HEAD_DIM
    wblk = width // LANE
    qspec = lambda off: pl.BlockSpec((t, width), lambda b, i: (b * nq + i, off // wblk))
    kvspec = lambda off: pl.BlockSpec((seq, width), lambda b, i: (b, off // wblk))
    return pl.pallas_call(
        functools.partial(_dsa_kernel, top_k=top_k),
        grid=(batch, nq),
        in_specs=[qspec(CB_IQ),
                  pl.BlockSpec((seq, LANE), lambda b, i: (b, 0)),
                  qspec(CB_AQ), kvspec(CB_AK), kvspec(CB_AV), qspec(CB_AG)],
        out_specs=pl.BlockSpec((t, width), lambda b, i: (b * nq + i, 0)),
        out_shape=jax.ShapeDtypeStruct((batch * seq, width), BF16),
        scratch_shapes=[pltpu.VMEM((seq, LANE), BF16),
                        pltpu.VMEM((seq, LANE), BF16),
                        pltpu.VMEM((nq, DSA_HEADS, VT_ROWS, t), BF16),
                        pltpu.VMEM((LANE, t), F32),
                        pltpu.VMEM((nq, t, t), F32),
                        pltpu.VMEM((nq, t, t), BF16),
                        pltpu.VMEM((nq, t, t), F32),
                        pltpu.VMEM((DSA_HEADS, nq, t, t), F32),
                        pltpu.VMEM((DSA_HEADS, SUBLANES, t), F32),
                        pltpu.VMEM((DSA_HEADS, VT_ROWS, t), F32)],
        compiler_params=pltpu.CompilerParams(
            dimension_semantics=("arbitrary", "arbitrary"), vmem_limit_bytes=VMEM_LIMIT),
        name="dsa",
    )(z, ikw, z, z, z, z)


def _memkv_kernel(mem_ref, nw_ref, w_ref, o_ref, wbf_ref):
    @pl.when(pl.program_id(1) == 0)
    def _():
        wbf_ref[...] = w_ref[...].astype(BF16)

    x = mem_ref[...]
    ms = jnp.mean(x * x, axis=-1, keepdims=True)
    hn = (x * lax.rsqrt(ms + EPS) * nw_ref[...]).astype(BF16)
    o_ref[...] = jnp.dot(hn, wbf_ref[...], preferred_element_type=F32).astype(o_ref.dtype)


def _memkv(mem2d, norm_w, w_kv, tm=256, tn=1024):
    m, d = mem2d.shape
    n = w_kv.shape[1]
    return pl.pallas_call(
        _memkv_kernel,
        grid=(n // tn, m // tm),
        in_specs=[pl.BlockSpec((tm, d), lambda j, i: (i, 0)),
                  pl.BlockSpec((1, d), lambda j, i: (0, 0)),
                  pl.BlockSpec((d, tn), lambda j, i: (0, j))],
        out_specs=pl.BlockSpec((tm, tn), lambda j, i: (i, j)),
        out_shape=jax.ShapeDtypeStruct((m, n), BF16),
        scratch_shapes=[pltpu.VMEM((d, tn), BF16)],
        compiler_params=pltpu.CompilerParams(
            dimension_semantics=("arbitrary", "arbitrary"), vmem_limit_bytes=VMEM_LIMIT),
        name="mem_kv",
    )(mem2d, norm_w.reshape(1, d), w_kv)


def _memattn_kernel(q_ref, g_ref, k_ref, v_ref, o_ref):
    scale = MEM_HEAD_DIM ** -0.5
    for h in range(MEM_HEADS):
        hs = slice(h * MEM_HEAD_DIM, (h + 1) * MEM_HEAD_DIM)
        s = lax.dot_general(q_ref[:, hs], k_ref[:, hs], NT, preferred_element_type=F32) * scale
        p = jnp.exp(s - jnp.max(s, axis=-1, keepdims=True))
        l = jnp.sum(p, axis=-1, keepdims=True)
        o = jnp.dot(p.astype(BF16), v_ref[:, hs], preferred_element_type=F32) / l
        o_ref[:, hs] = (_silu(g_ref[:, hs].astype(F32)) * o).astype(o_ref.dtype)


def _memattn(z, kv, batch, seq, mem_tokens, tl=512):
    width = MEM_HEADS * MEM_HEAD_DIM
    wblk = width // LANE
    nl = seq // tl
    return pl.pallas_call(
        _memattn_kernel,
        grid=(batch, nl),
        in_specs=[pl.BlockSpec((tl, width), lambda b, i: (b * nl + i, CB_MQ // wblk)),
                  pl.BlockSpec((tl, width), lambda b, i: (b * nl + i, CB_MG // wblk)),
                  pl.BlockSpec((mem_tokens, width), lambda b, i: (b, 0)),
                  pl.BlockSpec((mem_tokens, width), lambda b, i: (b, 1))],
        out_specs=pl.BlockSpec((tl, width), lambda b, i: (b * nl + i, 0)),
        out_shape=jax.ShapeDtypeStruct((batch * seq, width), BF16),
        compiler_params=pltpu.CompilerParams(
            dimension_semantics=("arbitrary", "arbitrary"), vmem_limit_bytes=VMEM_LIMIT),
        name="mem_attn",
    )(z, z, kv, kv)


def _merge_kernel(yr_ref, yd_ref, ym_ref, wr_ref, wd_ref, wm_ref, gr_ref, gd_ref, gm_ref, o_ref,
                  wbf_ref):
    @pl.when(pl.program_id(1) == 0)
    def _():
        for b, w_ref in enumerate((wr_ref, wd_ref, wm_ref)):
            wbf_ref[b] = w_ref[...].astype(BF16)

    def branch(b, y_ref, g_ref):
        up = jnp.dot(y_ref[...], wbf_ref[b], preferred_element_type=F32)
        return _sigmoid(g_ref[...].astype(F32)) * up

    merged = branch(0, yr_ref, gr_ref) + branch(1, yd_ref, gd_ref) + branch(2, ym_ref, gm_ref)
    o_ref[...] = merged.astype(o_ref.dtype)


def _merge(y_ret, y_dsa, y_mem, w_ret, w_dsa, w_mem, z, tm=1024, tn=1024):
    m, kdim = y_ret.shape
    n = w_ret.shape[1]
    gblk = tn // LANE
    yspec = pl.BlockSpec((tm, kdim), lambda j, i: (i, 0))
    wspec = pl.BlockSpec((kdim, tn), lambda j, i: (0, j), pipeline_mode=SINGLE_BUFFER)
    gspec = lambda off: pl.BlockSpec((tm, tn), lambda j, i: (i, off // gblk + j))
    return pl.pallas_call(
        _merge_kernel,
        grid=(n // tn, m // tm),
        in_specs=[yspec, yspec, yspec, wspec, wspec, wspec,
                  gspec(CB_GRET), gspec(CB_GDSA), gspec(CB_GMEM)],
        out_specs=pl.BlockSpec((tm, tn), lambda j, i: (i, j)),
        out_shape=jax.ShapeDtypeStruct((m, n), BF16),
        scratch_shapes=[pltpu.VMEM((N_BRANCHES, kdim, tn), BF16)],
        compiler_params=pltpu.CompilerParams(
            dimension_semantics=("arbitrary", "arbitrary"), vmem_limit_bytes=VMEM_LIMIT),
        name="merge",
    )(y_ret, y_dsa, y_mem, w_ret, w_dsa, w_mem, z, z, z)


def _out_kernel(m_ref, w_ref, x_ref, nw_ref, o_ref, wbf_ref):
    @pl.when(pl.program_id(0) == 0)
    def _():
        for r in range(0, w_ref.shape[0], PROJ_PREP_ROWS):
            wbf_ref[r:r + PROJ_PREP_ROWS, :] = w_ref[r:r + PROJ_PREP_ROWS, :].astype(BF16)

    out = jnp.dot(m_ref[...], wbf_ref[...], preferred_element_type=F32)
    ms = jnp.mean(out * out, axis=-1, keepdims=True)
    o_ref[...] = x_ref[...] + out * lax.rsqrt(ms + EPS) * nw_ref[...]


def _out_proj(merged, w_o, x2d, post_w, tm=512):
    m, d = x2d.shape
    return pl.pallas_call(
        _out_kernel,
        grid=(m // tm,),
        in_specs=[pl.BlockSpec((tm, d), lambda i: (i, 0)),
                  pl.BlockSpec((d, d), lambda i: (0, 0), pipeline_mode=SINGLE_BUFFER),
                  pl.BlockSpec((tm, d), lambda i: (i, 0)),
                  pl.BlockSpec((1, d), lambda i: (0, 0))],
        out_specs=pl.BlockSpec((tm, d), lambda i: (i, 0)),
        out_shape=jax.ShapeDtypeStruct((m, d), F32),
        scratch_shapes=[pltpu.VMEM((d, d), BF16)],
        compiler_params=pltpu.CompilerParams(
            dimension_semantics=("arbitrary",), vmem_limit_bytes=VMEM_LIMIT),
        name="out_proj_postnorm",
    )(merged, w_o, x2d, post_w.reshape(1, d))


def _layer(x, mem, pre_norm_w, w_in, ret_gn_w, mem_norm_w, w_mem_kv,
           w_up_ret, w_up_dsa, w_up_mem, w_o, post_norm_w):
    batch, seq, d = x.shape
    mem_tokens = mem.shape[1]
    top_k = min(IDX_TOPK_MAX, seq // 4)
    x2d = x.reshape(batch * seq, d)

    w_in_t = w_in.T
    w_ikw = jnp.pad(w_in_t[IDX_TAIL_START:IDX_TAIL_START + IDX_TAIL_COLS],
                    ((0, LANE - IDX_TAIL_COLS), (0, 0))).astype(BF16)
    h, ikw = _prenorm(x2d, pre_norm_w, w_ikw)
    z = _project(h, w_in_t)

    log_g = jnp.log1p(-(2.0 ** (-5.0 - jnp.arange(RET_HEADS, dtype=F32))))
    y_ret = _retention(z, log_g, ret_gn_w, batch, seq)
    y_dsa = _dsa(z, ikw, batch, seq, top_k)
    kv = _memkv(mem.reshape(batch * mem_tokens, d), mem_norm_w, w_mem_kv)
    y_mem = _memattn(z, kv, batch, seq, mem_tokens)

    merged = _merge(y_ret, y_dsa, y_mem, w_up_ret, w_up_dsa, w_up_mem, z)
    out = _out_proj(merged, w_o, x2d, post_norm_w)
    return out.reshape(batch, seq, d)


def kernel(x, mem, pre_norm_w, w_in, ret_gn_w, mem_norm_w, w_mem_kv, w_up_ret, w_up_dsa,
           w_up_mem, w_o, post_norm_w):
    for layer in range(w_in.shape[0]):
        x = _layer(x, mem, pre_norm_w[layer], w_in[layer], ret_gn_w[layer], mem_norm_w[layer],
                   w_mem_kv[layer], w_up_ret[layer], w_up_dsa[layer], w_up_mem[layer],
                   w_o[layer], post_norm_w[layer])
    return x
```

```python
import functools

import jax
import jax.numpy as jnp
from jax import lax
from jax.experimental import pallas as pl
from jax.experimental.pallas import tpu as pltpu

F32 = jnp.float32
BF16 = jnp.bfloat16
I32 = jnp.int32

LANE = 128
SUBLANES = 8
N_BRANCHES = 3
D_MODEL = 2048
RET_HEADS = 8
HEAD_DIM = 128
DSA_HEADS = 8
IDX_HEADS = 16
IDX_DIM = 64
IDX_TOPK_MAX = 256
MEM_HEADS = 4
MEM_HEAD_DIM = 256
NEG_INF = -1e30
EPS = 1e-6
INT_MIN = -2**31

CB_RQ, CB_RK, CB_RV, CB_RG = 0, 8, 16, 24
CB_AQ, CB_AK, CB_AV, CB_AG = 32, 40, 48, 56
CB_IQ = 64
CB_MQ, CB_MG = 72, 80
CB_GRET, CB_GDSA, CB_GMEM = 88, 104, 120
N_COL_BLOCKS = 136
Z_COLS = N_COL_BLOCKS * LANE
IDX_TAIL_START = (CB_IQ + IDX_HEADS * IDX_DIM // LANE) * LANE
IDX_TAIL_COLS = IDX_DIM + IDX_HEADS

PROJ_TM = 2048
PROJ_TN = 1024
PROJ_ALIGNED_TILES = IDX_TAIL_START // PROJ_TN
PROJ_ROWS = 512
PROJ_PREP_ROWS = 256
VMEM_LIMIT = 56 * 1024 * 1024
SINGLE_BUFFER = pl.Buffered(1)

NT = (((1,), (1,)), ((), ()))
TN = (((0,), (0,)), ((), ()))


def _sigmoid(v):
    return 1.0 / (1.0 + jnp.exp(-v))


def _silu(v):
    return v * _sigmoid(v)


def _prenorm_kernel(x_ref, w_ref, wikw_ref, h_ref, ikw_ref):
    x = x_ref[...]
    ms = jnp.mean(x * x, axis=-1, keepdims=True)
    h = (x * lax.rsqrt(ms + EPS) * w_ref[...]).astype(h_ref.dtype)
    h_ref[...] = h
    ikw_ref[...] = lax.dot_general(h, wikw_ref[...], NT,
                                   preferred_element_type=F32).astype(ikw_ref.dtype)


def _prenorm(x2d, w, w_ikw, tm=512):
    m, d = x2d.shape
    return pl.pallas_call(
        _prenorm_kernel,
        grid=(m // tm,),
        in_specs=[pl.BlockSpec((tm, d), lambda i: (i, 0)),
                  pl.BlockSpec((1, d), lambda i: (0, 0)),
                  pl.BlockSpec((LANE, d), lambda i: (0, 0))],
        out_specs=[pl.BlockSpec((tm, d), lambda i: (i, 0)),
                   pl.BlockSpec((tm, LANE), lambda i: (i, 0))],
        out_shape=[jax.ShapeDtypeStruct((m, d), BF16),
                   jax.ShapeDtypeStruct((m, LANE), BF16)],
        compiler_params=pltpu.CompilerParams(dimension_semantics=("arbitrary",)),
        name="prenorm_idx",
    )(x2d, w.reshape(1, d), w_ikw)


def _proj_kernel(h_ref, w_ref, z_ref, wbf_ref):
    @pl.when(pl.program_id(1) == 0)
    def _():
        for r in range(0, w_ref.shape[0], PROJ_PREP_ROWS):
            wbf_ref[r:r + PROJ_PREP_ROWS, :] = w_ref[r:r + PROJ_PREP_ROWS, :].astype(BF16)

    w = wbf_ref[...]
    for r in range(0, h_ref.shape[0], PROJ_ROWS):
        z_ref[r:r + PROJ_ROWS, :] = lax.dot_general(
            h_ref[r:r + PROJ_ROWS, :], w, NT, preferred_element_type=F32).astype(z_ref.dtype)


def _project(h, w_in_t):
    m, d = h.shape
    tm = min(PROJ_TM, m)

    def w_rows(j, i):
        skip = jnp.where(j >= PROJ_ALIGNED_TILES, IDX_TAIL_COLS // SUBLANES, 0)
        return ((j * (PROJ_TN // SUBLANES) + skip) * SUBLANES, 0)

    return pl.pallas_call(
        _proj_kernel,
        grid=(Z_COLS // PROJ_TN, m // tm),
        in_specs=[pl.BlockSpec((tm, d), lambda j, i: (i, 0)),
                  pl.BlockSpec((pl.Element(PROJ_TN), pl.Element(d)), w_rows)],
        out_specs=pl.BlockSpec((tm, PROJ_TN), lambda j, i: (i, j)),
        out_shape=jax.ShapeDtypeStruct((m, Z_COLS), BF16),
        scratch_shapes=[pltpu.VMEM((PROJ_TN, d), BF16)],
        compiler_params=pltpu.CompilerParams(
            dimension_semantics=("arbitrary", "arbitrary"), vmem_limit_bytes=VMEM_LIMIT),
        name="in_proj",
    )(h, w_in_t)


RET_C = 256


def _retention_kernel(lg_ref, q_ref, k_ref, v_ref, g_ref, gw_ref, o_ref):
    h = pl.program_id(1)
    lg = lg_ref[h]
    c = RET_C
    seq = q_ref.shape[0]
    scale = HEAD_DIM ** -0.5
    ri = lax.broadcasted_iota(I32, (c, c), 0)
    ci = lax.broadcasted_iota(I32, (c, c), 1)
    diff = (ri - ci).astype(F32)
    decay = jnp.where(diff >= 0, jnp.exp(lg * jnp.maximum(diff, 0.0)), 0.0) * scale
    idx = lax.broadcasted_iota(I32, (c, 1), 0).astype(F32)
    q_dec = jnp.exp(lg * (idx + 1.0))
    k_dec = jnp.exp(lg * (c - 1.0 - idx)) * scale
    chunk_dec = jnp.exp(lg * jnp.full((1, HEAD_DIM), float(c), F32))
    gw = gw_ref[...]

    state = jnp.zeros((HEAD_DIM, HEAD_DIM), F32)
    for i in range(seq // c):
        r = i * c
        qi = q_ref[pl.ds(r, c), :]
        ki = k_ref[pl.ds(r, c), :]
        vi = v_ref[pl.ds(r, c), :]
        inner = lax.dot_general(qi, ki, NT, preferred_element_type=F32) * decay
        o = (jnp.dot(inner.astype(BF16), vi, preferred_element_type=F32)
             + jnp.dot(qi, state.astype(BF16), preferred_element_type=F32) * q_dec)
        vs = (vi.astype(F32) * k_dec).astype(BF16)
        state = state * chunk_dec + lax.dot_general(ki, vs, TN, preferred_element_type=F32)
        mu = jnp.mean(o, axis=-1, keepdims=True)
        oc = o - mu
        var = jnp.mean(oc * oc, axis=-1, keepdims=True)
        on = oc * lax.rsqrt(var + 1e-5) * gw
        g = g_ref[pl.ds(r, c), :].astype(F32)
        o_ref[pl.ds(r, c), :] = (_silu(g) * on).astype(o_ref.dtype)


def _retention(z, log_g, gn_w, batch, seq):
    blk = lambda off: pl.BlockSpec((seq, HEAD_DIM), lambda b, h: (b, off + h))
    return pl.pallas_call(
        _retention_kernel,
        grid=(batch, RET_HEADS),
        in_specs=[pl.BlockSpec(memory_space=pltpu.SMEM),
                  blk(CB_RQ), blk(CB_RK), blk(CB_RV), blk(CB_RG),
                  pl.BlockSpec((1, HEAD_DIM), lambda b, h: (0, h))],
        out_specs=pl.BlockSpec((seq, HEAD_DIM), lambda b, h: (b, h)),
        out_shape=jax.ShapeDtypeStruct((batch * seq, RET_HEADS * HEAD_DIM), BF16),
        compiler_params=pltpu.CompilerParams(
            dimension_semantics=("arbitrary", "arbitrary"), vmem_limit_bytes=VMEM_LIMIT),
        name="retention",
    )(log_g, z, z, z, z, gn_w.reshape(1, -1))


DSA_T = 256
VT_ROWS = HEAD_DIM + 16

def _fold_rows(x, op, rows=SUBLANES):
    while x.shape[0] > rows:
        half = x.shape[0] // 2
        x = op(x[:half], x[half:])
    return x


I16 = jnp.int16
PACKED_ROWS = 2 * SUBLANES
ORD_NEG_INF = -2**31 + 0x7FFFFF


def _flip(v):
    return v ^ ((v >> 31) & 0x7FFFFFFF)


def _ordinal_to_f32(o):
    return lax.bitcast_convert_type(_flip(jnp.maximum(o, ORD_NEG_INF)), F32)


def _prefix_to_bf16_bits(p):
    return _flip(jnp.maximum(p, ORD_NEG_INF >> 16) << 16) & -65536


def _dsa_kernel(iq_ref, ikw_ref, aq_ref, ak_ref, av_ref, ag_ref, o_ref,
                ika_ref, ikb_ref, vt_ref, wt_ref, score_ref, score16_ref, bias_ref, logit_ref,
                m_ref, acc_ref, *, top_k):
    t = DSA_T
    qb = pl.program_id(1)
    nk = qb + 1
    q0 = pl.multiple_of(qb * t, t)
    nq = vt_ref.shape[0]

    @pl.when(qb == 0)
    def _():
        ikw = ikw_ref[...].astype(F32)
        lane = lax.broadcasted_iota(I32, ikw.shape, 1)
        a = jnp.where(lane < IDX_DIM, ikw, 0.0)
        ika_ref[...] = a.astype(BF16)
        ikb_ref[...] = pltpu.roll(a, IDX_DIM, axis=1).astype(BF16)

        def vt_tile(kt, carry):
            k0 = pl.multiple_of(kt * t, t)
            for h in range(DSA_HEADS):
                hs = slice(h * HEAD_DIM, (h + 1) * HEAD_DIM)
                vt_ref[kt, h, :HEAD_DIM, :] = av_ref[pl.ds(k0, t), hs].astype(F32).T.astype(BF16)
                vt_ref[kt, h, HEAD_DIM:, :] = jnp.ones((VT_ROWS - HEAD_DIM, t), BF16)
            return carry

        lax.fori_loop(0, nq, vt_tile, 0)

    wt_ref[...] = ikw_ref[pl.ds(q0, t), :].astype(F32).T

    key_row = lax.broadcasted_iota(I32, (t, t), 0)
    qry_col = lax.broadcasted_iota(I32, (t, t), 1)

    def score_tile(kt, carry):
        k0 = pl.multiple_of(kt * t, t)
        ka = ika_ref[pl.ds(k0, t), :]
        kb = ikb_ref[pl.ds(k0, t), :]
        acc = jnp.zeros((t, t), F32)
        for j in range(IDX_HEADS // 2):
            qj = iq_ref[:, j * LANE:(j + 1) * LANE]
            for half, kx in enumerate((ka, kb)):
                hh = IDX_DIM + 2 * j + half
                s = lax.dot_general(kx, qj, NT, preferred_element_type=F32)
                acc = acc + jnp.maximum(s, 0.0) * wt_ref[hh:hh + 1, :]
        causal = (key_row + k0) <= (qry_col + q0)
        score = jnp.where(causal, acc, -jnp.inf)
        score_ref[kt] = score
        score16_ref[kt] = score.astype(BF16)
        return carry

    lax.fori_loop(0, nk, score_tile, 0)

    one16, zero16 = jnp.ones((), I16), jnp.zeros((), I16)

    def prefix_step(i, pfx):
        cand_pfx = pfx + lax.shift_left(jnp.int32(1), 15 - i)
        cand = lax.bitcast_convert_type(_prefix_to_bf16_bits(cand_pfx), F32)
        cand16 = jnp.broadcast_to(cand, (PACKED_ROWS, t)).astype(BF16)[:1]

        def count_tile(kt, cnt):
            ge = jnp.where(score16_ref[kt] >= cand16, one16, zero16)
            return cnt + _fold_rows(ge, jnp.add, PACKED_ROWS)

        cnt = lax.fori_loop(0, nk, count_tile, jnp.zeros((PACKED_ROWS, t), I16))
        total = jnp.sum(cnt.astype(I32), axis=0, keepdims=True)
        return jnp.where(total >= top_k, cand_pfx, pfx)

    pfx = lax.fori_loop(0, 16, prefix_step, jnp.full((1, t), -2**15, I32))

    def bisect_step(i, bounds):
        lo, hi = bounds
        mid = lo + ((hi - lo) >> 1)
        cand = _ordinal_to_f32(mid)

        def count_tile(kt, cnt):
            return cnt + _fold_rows(jnp.where(score_ref[kt] >= cand, 1, 0), jnp.add)

        cnt = lax.fori_loop(0, nk, count_tile, jnp.zeros((SUBLANES, t), I32))
        enough = jnp.sum(cnt, axis=0, keepdims=True) >= top_k
        return jnp.where(enough, mid, lo), jnp.where(enough, hi, mid)

    lo, _ = lax.fori_loop(0, 17, bisect_step, (_flip(_prefix_to_bf16_bits(pfx - 1)),
                                               _flip(_prefix_to_bf16_bits(pfx + 1))))
    thr = _ordinal_to_f32(lo)

    def bias_tile(kt, kept):
        k0 = pl.multiple_of(kt * t, t)
        keep = jnp.logical_and(score_ref[kt] >= thr, (key_row + k0) <= (qry_col + q0))
        bias_ref[kt] = jnp.where(keep, 0.0, NEG_INF)
        return kept + _fold_rows(jnp.where(keep, 1, 0), jnp.add)

    kept = lax.fori_loop(0, nk, bias_tile, jnp.zeros((SUBLANES, t), I32))
    kept = jnp.sum(kept, axis=0, keepdims=True)

    @pl.when(jnp.max(kept) > top_k)
    def _():
        strictly_lower = (qry_col < key_row).astype(BF16)

        def tie_tile(kt, carry):
            above, tied_before = carry
            k0 = pl.multiple_of(kt * t, t)
            causal = (key_row + k0) <= (qry_col + q0)
            score = score_ref[kt]
            gt = jnp.logical_and(score > thr, causal)
            eq = jnp.logical_and(score == thr, causal)
            eq01 = jnp.where(eq, 1.0, 0.0)
            tied_here = tied_before + jnp.dot(strictly_lower, eq01.astype(BF16),
                                              preferred_element_type=F32)
            logit_ref[0, kt] = jnp.where(eq, tied_here, -1.0)
            above = above + _fold_rows(jnp.where(gt, 1.0, 0.0), jnp.add)
            return above, tied_before + jnp.sum(eq01, axis=0, keepdims=True)

        above, _ = lax.fori_loop(0, nk, tie_tile,
                                 (jnp.zeros((SUBLANES, t), F32), jnp.zeros((1, t), F32)))
        room = top_k - jnp.sum(above, axis=0, keepdims=True)

        def rebias_tile(kt, carry):
            k0 = pl.multiple_of(kt * t, t)
            gt = jnp.logical_and(score_ref[kt] > thr, (key_row + k0) <= (qry_col + q0))
            rank = logit_ref[0, kt]
            keep = jnp.logical_or(gt, jnp.logical_and(rank >= 0.0, rank < room))
            bias_ref[kt] = jnp.where(keep, 0.0, NEG_INF)
            return carry

        lax.fori_loop(0, nk, rebias_tile, 0)

    m_ref[...] = jnp.full(m_ref.shape, NEG_INF, F32)
    acc_ref[...] = jnp.zeros(acc_ref.shape, F32)
    log2e = 1.4426950408889634
    scale = (HEAD_DIM ** -0.5) * log2e

    def logit_tile(kt, carry):
        k0 = pl.multiple_of(kt * t, t)
        bias = bias_ref[kt]
        kpos = (key_row + (k0 - q0)).astype(F32)
        for h in range(DSA_HEADS):
            slope = (2.0 ** (-8.0 * (h + 1) / DSA_HEADS)) * log2e
            hs = slice(h * HEAD_DIM, (h + 1) * HEAD_DIM)
            s = lax.dot_general(ak_ref[pl.ds(k0, t), hs], aq_ref[:, hs], NT,
                                preferred_element_type=F32)
            lg = s * scale + (kpos * slope + bias)
            logit_ref[h, kt] = lg
            m_ref[h] = jnp.maximum(m_ref[h], _fold_rows(lg, jnp.maximum))
        return carry

    lax.fori_loop(0, nk, logit_tile, 0)

    for h in range(DSA_HEADS):
        m_ref[h] = jnp.broadcast_to(jnp.max(m_ref[h], axis=0, keepdims=True), (SUBLANES, t))

    def pv_tile(kt, carry):
        for h in range(DSA_HEADS):
            p = jnp.exp2(logit_ref[h, kt] - m_ref[h][:1])
            acc_ref[h] = acc_ref[h] + jnp.dot(vt_ref[kt, h], p.astype(BF16),
                                              preferred_element_type=F32)
        return carry

    lax.fori_loop(0, nk, pv_tile, 0)

    for h in range(DSA_HEADS):
        hs = slice(h * HEAD_DIM, (h + 1) * HEAD_DIM)
        acc = acc_ref[h]
        o = (acc[:HEAD_DIM] / acc[HEAD_DIM:HEAD_DIM + 1]).T
        o_ref[:, hs] = (_silu(ag_ref[:, hs].astype(F32)) * o).astype(o_ref.dtype)


def _dsa(z, ikw, batch, seq, top_k):
    t = DSA_T
    nq = seq // t
    width = DSA_HEADS * HEAD_DIM
    wblk = width // LANE
    qspec = lambda off: pl.BlockSpec((t, width), lambda b, i: (b * nq + i, off // wblk))
    kvspec = lambda off: pl.BlockSpec((seq, width), lambda b, i: (b, off // wblk))
    return pl.pallas_call(
        functools.partial(_dsa_kernel, top_k=top_k),
        grid=(batch, nq),
        in_specs=[qspec(CB_IQ),
                  pl.BlockSpec((seq, LANE), lambda b, i: (b, 0)),
                  qspec(CB_AQ), kvspec(CB_AK), kvspec(CB_AV), qspec(CB_AG)],
        out_specs=pl.BlockSpec((t, width), lambda b, i: (b * nq + i, 0)),
        out_shape=jax.ShapeDtypeStruct((batch * seq, width), BF16),
        scratch_shapes=[pltpu.VMEM((seq, LANE), BF16),
                        pltpu.VMEM((seq, LANE), BF16),
                        pltpu.VMEM((nq, DSA_HEADS, VT_ROWS, t), BF16),
                        pltpu.VMEM((LANE, t), F32),
                        pltpu.VMEM((nq, t, t), F32),
                        pltpu.VMEM((nq, t, t), BF16),
                        pltpu.VMEM((nq, t, t), F32),
                        pltpu.VMEM((DSA_HEADS, nq, t, t), F32),
                        pltpu.VMEM((DSA_HEADS, SUBLANES, t), F32),
                        pltpu.VMEM((DSA_HEADS, VT_ROWS, t), F32)],
        compiler_params=pltpu.CompilerParams(
            dimension_semantics=("arbitrary", "arbitrary"), vmem_limit_bytes=VMEM_LIMIT),
        name="dsa",
    )(z, ikw, z, z, z, z)


def _memkv_kernel(mem_ref, nw_ref, w_ref, o_ref, wbf_ref):
    @pl.when(pl.program_id(1) == 0)
    def _():
        wbf_ref[...] = w_ref[...].astype(BF16)

    x = mem_ref[...]
    ms = jnp.mean(x * x, axis=-1, keepdims=True)
    hn = (x * lax.rsqrt(ms + EPS) * nw_ref[...]).astype(BF16)
    o_ref[...] = jnp.dot(hn, wbf_ref[...], preferred_element_type=F32).astype(o_ref.dtype)


def _memkv(mem2d, norm_w, w_kv, tm=256, tn=1024):
    m, d = mem2d.shape
    n = w_kv.shape[1]
    return pl.pallas_call(
        _memkv_kernel,
        grid=(n // tn, m // tm),
        in_specs=[pl.BlockSpec((tm, d), lambda j, i: (i, 0)),
                  pl.BlockSpec((1, d), lambda j, i: (0, 0)),
                  pl.BlockSpec((d, tn), lambda j, i: (0, j))],
        out_specs=pl.BlockSpec((tm, tn), lambda j, i: (i, j)),
        out_shape=jax.ShapeDtypeStruct((m, n), BF16),
        scratch_shapes=[pltpu.VMEM((d, tn), BF16)],
        compiler_params=pltpu.CompilerParams(
            dimension_semantics=("arbitrary", "arbitrary"), vmem_limit_bytes=VMEM_LIMIT),
        name="mem_kv",
    )(mem2d, norm_w.reshape(1, d), w_kv)


def _memattn_kernel(q_ref, g_ref, k_ref, v_ref, o_ref):
    scale = MEM_HEAD_DIM ** -0.5
    for h in range(MEM_HEADS):
        hs = slice(h * MEM_HEAD_DIM, (h + 1) * MEM_HEAD_DIM)
        s = lax.dot_general(q_ref[:, hs], k_ref[:, hs], NT, preferred_element_type=F32) * scale
        p = jnp.exp(s - jnp.max(s, axis=-1, keepdims=True))
        l = jnp.sum(p, axis=-1, keepdims=True)
        o = jnp.dot(p.astype(BF16), v_ref[:, hs], preferred_element_type=F32) / l
        o_ref[:, hs] = (_silu(g_ref[:, hs].astype(F32)) * o).astype(o_ref.dtype)


def _memattn(z, kv, batch, seq, mem_tokens, tl=512):
    width = MEM_HEADS * MEM_HEAD_DIM
    wblk = width // LANE
    nl = seq // tl
    return pl.pallas_call(
        _memattn_kernel,
        grid=(batch, nl),
        in_specs=[pl.BlockSpec((tl, width), lambda b, i: (b * nl + i, CB_MQ // wblk)),
                  pl.BlockSpec((tl, width), lambda b, i: (b * nl + i, CB_MG // wblk)),
                  pl.BlockSpec((mem_tokens, width), lambda b, i: (b, 0)),
                  pl.BlockSpec((mem_tokens, width), lambda b, i: (b, 1))],
        out_specs=pl.BlockSpec((tl, width), lambda b, i: (b * nl + i, 0)),
        out_shape=jax.ShapeDtypeStruct((batch * seq, width), BF16),
        compiler_params=pltpu.CompilerParams(
            dimension_semantics=("arbitrary", "arbitrary"), vmem_limit_bytes=VMEM_LIMIT),
        name="mem_attn",
    )(z, z, kv, kv)


def _merge_kernel(yr_ref, yd_ref, ym_ref, wr_ref, wd_ref, wm_ref, gr_ref, gd_ref, gm_ref, o_ref,
                  wbf_ref):
    @pl.when(pl.program_id(1) == 0)
    def _():
        for b, w_ref in enumerate((wr_ref, wd_ref, wm_ref)):
            wbf_ref[b] = w_ref[...].astype(BF16)

    def branch(b, y_ref, g_ref):
        up = jnp.dot(y_ref[...], wbf_ref[b], preferred_element_type=F32)
        return _sigmoid(g_ref[...].astype(F32)) * up

    merged = branch(0, yr_ref, gr_ref) + branch(1, yd_ref, gd_ref) + branch(2, ym_ref, gm_ref)
    o_ref[...] = merged.astype(o_ref.dtype)


def _merge(y_ret, y_dsa, y_mem, w_ret, w_dsa, w_mem, z, tm=512, tn=1024):
    m, kdim = y_ret.shape
    n = w_ret.shape[1]
    gblk = tn // LANE
    yspec = pl.BlockSpec((tm, kdim), lambda j, i: (i, 0))
    wspec = pl.BlockSpec((kdim, tn), lambda j, i: (0, j))
    gspec = lambda off: pl.BlockSpec((tm, tn), lambda j, i: (i, off // gblk + j))
    return pl.pallas_call(
        _merge_kernel,
        grid=(n // tn, m // tm),
        in_specs=[yspec, yspec, yspec, wspec, wspec, wspec,
                  gspec(CB_GRET), gspec(CB_GDSA), gspec(CB_GMEM)],
        out_specs=pl.BlockSpec((tm, tn), lambda j, i: (i, j)),
        out_shape=jax.ShapeDtypeStruct((m, n), BF16),
        scratch_shapes=[pltpu.VMEM((N_BRANCHES, kdim, tn), BF16)],
        compiler_params=pltpu.CompilerParams(
            dimension_semantics=("arbitrary", "arbitrary"), vmem_limit_bytes=VMEM_LIMIT),
        name="merge",
    )(y_ret, y_dsa, y_mem, w_ret, w_dsa, w_mem, z, z, z)


def _out_kernel(m_ref, w_ref, x_ref, nw_ref, o_ref, wbf_ref):
    @pl.when(pl.program_id(0) == 0)
    def _():
        for r in range(0, w_ref.shape[0], PROJ_PREP_ROWS):
            wbf_ref[r:r + PROJ_PREP_ROWS, :] = w_ref[r:r + PROJ_PREP_ROWS, :].astype(BF16)

    out = jnp.dot(m_ref[...], wbf_ref[...], preferred_element_type=F32)
    ms = jnp.mean(out * out, axis=-1, keepdims=True)
    o_ref[...] = x_ref[...] + out * lax.rsqrt(ms + EPS) * nw_ref[...]


def _out_proj(merged, w_o, x2d, post_w, tm=512):
    m, d = x2d.shape
    return pl.pallas_call(
        _out_kernel,
        grid=(m // tm,),
        in_specs=[pl.BlockSpec((tm, d), lambda i: (i, 0)),
                  pl.BlockSpec((d, d), lambda i: (0, 0), pipeline_mode=SINGLE_BUFFER),
                  pl.BlockSpec((tm, d), lambda i: (i, 0)),
                  pl.BlockSpec((1, d), lambda i: (0, 0))],
        out_specs=pl.BlockSpec((tm, d), lambda i: (i, 0)),
        out_shape=jax.ShapeDtypeStruct((m, d), F32),
        scratch_shapes=[pltpu.VMEM((d, d), BF16)],
        compiler_params=pltpu.CompilerParams(
            dimension_semantics=("arbitrary",), vmem_limit_bytes=VMEM_LIMIT),
        name="out_proj_postnorm",
    )(merged, w_o, x2d, post_w.reshape(1, d))


def _layer(x, mem, pre_norm_w, w_in, ret_gn_w, mem_norm_w, w_mem_kv,
           w_up_ret, w_up_dsa, w_up_mem, w_o, post_norm_w):
    batch, seq, d = x.shape
    mem_tokens = mem.shape[1]
    top_k = min(IDX_TOPK_MAX, seq // 4)
    x2d = x.reshape(batch * seq, d)

    w_in_t = w_in.T
    w_ikw = jnp.pad(w_in_t[IDX_TAIL_START:IDX_TAIL_START + IDX_TAIL_COLS],
                    ((0, LANE - IDX_TAIL_COLS), (0, 0))).astype(BF16)
    h, ikw = _prenorm(x2d, pre_norm_w, w_ikw)
    z = _project(h, w_in_t)

    log_g = jnp.log1p(-(2.0 ** (-5.0 - jnp.arange(RET_HEADS, dtype=F32))))
    y_ret = _retention(z, log_g, ret_gn_w, batch, seq)
    y_dsa = _dsa(z, ikw, batch, seq, top_k)
    kv = _memkv(mem.reshape(batch * mem_tokens, d), mem_norm_w, w_mem_kv)
    y_mem = _memattn(z, kv, batch, seq, mem_tokens)

    merged = _merge(y_ret, y_dsa, y_mem, w_up_ret, w_up_dsa, w_up_mem, z)
    out = _out_proj(merged, w_o, x2d, post_norm_w)
    return out.reshape(batch, seq, d)


def kernel(x, mem, pre_norm_w, w_in, ret_gn_w, mem_norm_w, w_mem_kv, w_up_ret, w_up_dsa,
           w_up_mem, w_o, post_norm_w):
    for layer in range(w_in.shape[0]):
        x = _layer(x, mem, pre_norm_w[layer], w_in[layer], ret_gn_w[layer], mem_norm_w[layer],
                   w_mem_kv[layer], w_up_ret[layer], w_up_dsa[layer], w_up_mem[layer],
                   w_o[layer], post_norm_w[layer])
    return x
```

```python
import functools

import jax
import jax.numpy as jnp
from jax import lax
from jax.experimental import pallas as pl
from jax.experimental.pallas import tpu as pltpu

F32 = jnp.float32
BF16 = jnp.bfloat16
I32 = jnp.int32

LANE = 128
SUBLANES = 8
N_BRANCHES = 3
RET_HEADS = 8
HEAD_DIM = 128
DSA_HEADS = 8
IDX_HEADS = 16
IDX_DIM = 64
IDX_TOPK_MAX = 256
MEM_HEADS = 4
MEM_HEAD_DIM = 256
NEG_INF = -1e30
EPS = 1e-6
INT_MIN = -2**31

CB_RQ, CB_RK, CB_RV, CB_RG = 0, 8, 16, 24
CB_AQ, CB_AK, CB_AV, CB_AG = 32, 40, 48, 56
CB_IQ = 64
CB_MQ, CB_MG = 72, 80
CB_GRET, CB_GDSA, CB_GMEM = 88, 104, 120
N_COL_BLOCKS = 136
Z_COLS = N_COL_BLOCKS * LANE
IDX_TAIL_START = (CB_IQ + IDX_HEADS * IDX_DIM // LANE) * LANE
IDX_TAIL_COLS = IDX_DIM + IDX_HEADS

PROJ_TM = 2048
PROJ_TN = 1024
PROJ_ALIGNED_TILES = IDX_TAIL_START // PROJ_TN
PROJ_ROWS = 512
PROJ_PREP_ROWS = 256
PRENORM_TM = 1024
MEM_TL = 512
MERGE_TM, MERGE_TN = 512, 1024
OUT_TM = 512
VMEM_LIMIT = 56 * 1024 * 1024
SINGLE_BUFFER = pl.Buffered(1)

NT = (((1,), (1,)), ((), ()))
TN = (((0,), (0,)), ((), ()))


def _sigmoid(v):
    return 1.0 / (1.0 + jnp.exp(-v))


def _silu(v):
    return v * _sigmoid(v)


def _prenorm_kernel(x_ref, w_ref, wikw_ref, h_ref, ikw_ref):
    x = x_ref[...]
    ms = jnp.mean(x * x, axis=-1, keepdims=True)
    h = (x * lax.rsqrt(ms + EPS) * w_ref[...]).astype(h_ref.dtype)
    h_ref[...] = h
    ikw_ref[...] = lax.dot_general(h, wikw_ref[...], NT,
                                   preferred_element_type=F32).astype(ikw_ref.dtype)


def _prenorm(x2d, w, w_ikw):
    m, d = x2d.shape
    tm = min(PRENORM_TM, m)
    return pl.pallas_call(
        _prenorm_kernel,
        grid=(m // tm,),
        in_specs=[pl.BlockSpec((tm, d), lambda i: (i, 0)),
                  pl.BlockSpec((1, d), lambda i: (0, 0)),
                  pl.BlockSpec((LANE, d), lambda i: (0, 0))],
        out_specs=[pl.BlockSpec((tm, d), lambda i: (i, 0)),
                   pl.BlockSpec((tm, LANE), lambda i: (i, 0))],
        out_shape=[jax.ShapeDtypeStruct((m, d), BF16),
                   jax.ShapeDtypeStruct((m, LANE), BF16)],
        compiler_params=pltpu.CompilerParams(dimension_semantics=("arbitrary",)),
        name="prenorm_idx",
    )(x2d, w.reshape(1, d), w_ikw)


def _proj_kernel(h_ref, w_ref, z_ref, wbf_ref):
    @pl.when(pl.program_id(1) == 0)
    def _():
        for r in range(0, w_ref.shape[0], PROJ_PREP_ROWS):
            wbf_ref[r:r + PROJ_PREP_ROWS, :] = w_ref[r:r + PROJ_PREP_ROWS, :].astype(BF16)

    w = wbf_ref[...]
    for r in range(0, h_ref.shape[0], PROJ_ROWS):
        z_ref[r:r + PROJ_ROWS, :] = lax.dot_general(
            h_ref[r:r + PROJ_ROWS, :], w, NT, preferred_element_type=F32).astype(z_ref.dtype)


def _project(h, w_in_t):
    m, d = h.shape
    tm = min(PROJ_TM, m)

    def w_rows(j, i):
        skip = jnp.where(j >= PROJ_ALIGNED_TILES, IDX_TAIL_COLS // SUBLANES, 0)
        return ((j * (PROJ_TN // SUBLANES) + skip) * SUBLANES, 0)

    return pl.pallas_call(
        _proj_kernel,
        grid=(Z_COLS // PROJ_TN, m // tm),
        in_specs=[pl.BlockSpec((tm, d), lambda j, i: (i, 0)),
                  pl.BlockSpec((pl.Element(PROJ_TN), pl.Element(d)), w_rows)],
        out_specs=pl.BlockSpec((tm, PROJ_TN), lambda j, i: (i, j)),
        out_shape=jax.ShapeDtypeStruct((m, Z_COLS), BF16),
        scratch_shapes=[pltpu.VMEM((PROJ_TN, d), BF16)],
        compiler_params=pltpu.CompilerParams(
            dimension_semantics=("arbitrary", "arbitrary"), vmem_limit_bytes=VMEM_LIMIT),
        name="in_proj",
    )(h, w_in_t)


RET_C = 256


def _retention_kernel(lg_ref, q_ref, k_ref, v_ref, g_ref, gw_ref, o_ref):
    h = pl.program_id(1)
    lg = lg_ref[h]
    c = RET_C
    seq = q_ref.shape[0]
    scale = HEAD_DIM ** -0.5
    ri = lax.broadcasted_iota(I32, (c, c), 0)
    ci = lax.broadcasted_iota(I32, (c, c), 1)
    diff = (ri - ci).astype(F32)
    decay = jnp.where(diff >= 0, jnp.exp(lg * jnp.maximum(diff, 0.0)), 0.0) * scale
    idx = lax.broadcasted_iota(I32, (c, 1), 0).astype(F32)
    q_dec = jnp.exp(lg * (idx + 1.0))
    k_dec = jnp.exp(lg * (c - 1.0 - idx)) * scale
    chunk_dec = jnp.exp(lg * jnp.full((1, HEAD_DIM), float(c), F32))
    gw = gw_ref[...]

    state = jnp.zeros((HEAD_DIM, HEAD_DIM), F32)
    for i in range(seq // c):
        r = i * c
        qi = q_ref[pl.ds(r, c), :]
        ki = k_ref[pl.ds(r, c), :]
        vi = v_ref[pl.ds(r, c), :]
        inner = lax.dot_general(qi, ki, NT, preferred_element_type=F32) * decay
        o = (jnp.dot(inner.astype(BF16), vi, preferred_element_type=F32)
             + jnp.dot(qi, state.astype(BF16), preferred_element_type=F32) * q_dec)
        vs = (vi.astype(F32) * k_dec).astype(BF16)
        state = state * chunk_dec + lax.dot_general(ki, vs, TN, preferred_element_type=F32)
        mu = jnp.mean(o, axis=-1, keepdims=True)
        oc = o - mu
        var = jnp.mean(oc * oc, axis=-1, keepdims=True)
        on = oc * lax.rsqrt(var + 1e-5) * gw
        g = g_ref[pl.ds(r, c), :].astype(F32)
        o_ref[pl.ds(r, c), :] = (_silu(g) * on).astype(o_ref.dtype)


def _retention(z, log_g, gn_w, batch, seq):
    blk = lambda off: pl.BlockSpec((seq, HEAD_DIM), lambda b, h: (b, off + h))
    return pl.pallas_call(
        _retention_kernel,
        grid=(batch, RET_HEADS),
        in_specs=[pl.BlockSpec(memory_space=pltpu.SMEM),
                  blk(CB_RQ), blk(CB_RK), blk(CB_RV), blk(CB_RG),
                  pl.BlockSpec((1, HEAD_DIM), lambda b, h: (0, h))],
        out_specs=pl.BlockSpec((seq, HEAD_DIM), lambda b, h: (b, h)),
        out_shape=jax.ShapeDtypeStruct((batch * seq, RET_HEADS * HEAD_DIM), BF16),
        compiler_params=pltpu.CompilerParams(
            dimension_semantics=("arbitrary", "arbitrary"), vmem_limit_bytes=VMEM_LIMIT),
        name="retention",
    )(log_g, z, z, z, z, gn_w.reshape(1, -1))


DSA_T = 256
VT_ROWS = HEAD_DIM + 16

def _fold_rows(x, op, rows=SUBLANES):
    while x.shape[0] > rows:
        half = x.shape[0] // 2
        x = op(x[:half], x[half:])
    return x


I16 = jnp.int16
PACKED_ROWS = 2 * SUBLANES
ORD_NEG_INF = -2**31 + 0x7FFFFF


def _flip(v):
    return v ^ ((v >> 31) & 0x7FFFFFFF)


def _ordinal_to_f32(o):
    return lax.bitcast_convert_type(_flip(jnp.maximum(o, ORD_NEG_INF)), F32)


def _prefix_to_bf16_bits(p):
    return _flip(jnp.maximum(p, ORD_NEG_INF >> 16) << 16) & -65536


def _dsa_kernel(iq_ref, ikw_ref, aq_ref, ak_ref, av_ref, ag_ref, o_ref,
                ika_ref, ikb_ref, vt_ref, wt_ref, score_ref, score16_ref, bias_ref, logit_ref,
                m_ref, acc_ref, *, top_k):
    t = DSA_T
    qb = pl.program_id(1)
    nk = qb + 1
    q0 = pl.multiple_of(qb * t, t)
    nq = vt_ref.shape[0]

    @pl.when(qb == 0)
    def _():
        ikw = ikw_ref[...].astype(F32)
        lane = lax.broadcasted_iota(I32, ikw.shape, 1)
        a = jnp.where(lane < IDX_DIM, ikw, 0.0)
        ika_ref[...] = a.astype(BF16)
        ikb_ref[...] = pltpu.roll(a, IDX_DIM, axis=1).astype(BF16)

        def vt_tile(kt, carry):
            k0 = pl.multiple_of(kt * t, t)
            for h in range(DSA_HEADS):
                hs = slice(h * HEAD_DIM, (h + 1) * HEAD_DIM)
                vt_ref[kt, h, :HEAD_DIM, :] = av_ref[pl.ds(k0, t), hs].astype(F32).T.astype(BF16)
                vt_ref[kt, h, HEAD_DIM:, :] = jnp.ones((VT_ROWS - HEAD_DIM, t), BF16)
            return carry

        lax.fori_loop(0, nq, vt_tile, 0)

    wt_ref[...] = ikw_ref[pl.ds(q0, t), :].astype(F32).T

    key_row = lax.broadcasted_iota(I32, (t, t), 0)
    qry_col = lax.broadcasted_iota(I32, (t, t), 1)

    def score_tile(kt, carry):
        k0 = pl.multiple_of(kt * t, t)
        ka = ika_ref[pl.ds(k0, t), :]
        kb = ikb_ref[pl.ds(k0, t), :]
        acc = jnp.zeros((t, t), F32)
        for j in range(IDX_HEADS // 2):
            qj = iq_ref[:, j * LANE:(j + 1) * LANE]
            for half, kx in enumerate((ka, kb)):
                hh = IDX_DIM + 2 * j + half
                s = lax.dot_general(kx, qj, NT, preferred_element_type=F32)
                acc = acc + jnp.maximum(s, 0.0) * wt_ref[hh:hh + 1, :]
        causal = (key_row + k0) <= (qry_col + q0)
        score = jnp.where(causal, acc, -jnp.inf)
        score_ref[kt] = score
        score16_ref[kt] = score.astype(BF16)
        return carry

    lax.fori_loop(0, nk, score_tile, 0)

    one16, zero16 = jnp.ones((), I16), jnp.zeros((), I16)

    def prefix_step(i, pfx):
        cand_pfx = pfx + lax.shift_left(jnp.int32(1), 15 - i)
        cand = lax.bitcast_convert_type(_prefix_to_bf16_bits(cand_pfx), F32)
        cand16 = jnp.broadcast_to(cand, (PACKED_ROWS, t)).astype(BF16)[:1]

        def count_tile(kt, cnt):
            ge = jnp.where(score16_ref[kt] >= cand16, one16, zero16)
            return cnt + _fold_rows(ge, jnp.add, PACKED_ROWS)

        cnt = lax.fori_loop(0, nk, count_tile, jnp.zeros((PACKED_ROWS, t), I16))
        total = jnp.sum(cnt.astype(I32), axis=0, keepdims=True)
        return jnp.where(total >= top_k, cand_pfx, pfx)

    pfx = lax.fori_loop(0, 16, prefix_step, jnp.full((1, t), -2**15, I32))

    def bisect_step(i, bounds):
        lo, hi = bounds
        mid = lo + ((hi - lo) >> 1)
        cand = _ordinal_to_f32(mid)

        def count_tile(kt, cnt):
            return cnt + _fold_rows(jnp.where(score_ref[kt] >= cand, 1, 0), jnp.add)

        cnt = lax.fori_loop(0, nk, count_tile, jnp.zeros((SUBLANES, t), I32))
        enough = jnp.sum(cnt, axis=0, keepdims=True) >= top_k
        return jnp.where(enough, mid, lo), jnp.where(enough, hi, mid)

    lo, _ = lax.fori_loop(0, 17, bisect_step, (_flip(_prefix_to_bf16_bits(pfx - 1)),
                                               _flip(_prefix_to_bf16_bits(pfx + 1))))
    thr = _ordinal_to_f32(lo)

    def bias_tile(kt, kept):
        k0 = pl.multiple_of(kt * t, t)
        keep = jnp.logical_and(score_ref[kt] >= thr, (key_row + k0) <= (qry_col + q0))
        bias_ref[kt] = jnp.where(keep, 0.0, NEG_INF)
        return kept + _fold_rows(jnp.where(keep, 1, 0), jnp.add)

    kept = lax.fori_loop(0, nk, bias_tile, jnp.zeros((SUBLANES, t), I32))
    kept = jnp.sum(kept, axis=0, keepdims=True)

    @pl.when(jnp.max(kept) > top_k)
    def _():
        strictly_lower = (qry_col < key_row).astype(BF16)

        def tie_tile(kt, carry):
            above, tied_before = carry
            k0 = pl.multiple_of(kt * t, t)
            causal = (key_row + k0) <= (qry_col + q0)
            score = score_ref[kt]
            gt = jnp.logical_and(score > thr, causal)
            eq = jnp.logical_and(score == thr, causal)
            eq01 = jnp.where(eq, 1.0, 0.0)
            tied_here = tied_before + jnp.dot(strictly_lower, eq01.astype(BF16),
                                              preferred_element_type=F32)
            logit_ref[0, kt] = jnp.where(eq, tied_here, -1.0)
            above = above + _fold_rows(jnp.where(gt, 1.0, 0.0), jnp.add)
            return above, tied_before + jnp.sum(eq01, axis=0, keepdims=True)

        above, _ = lax.fori_loop(0, nk, tie_tile,
                                 (jnp.zeros((SUBLANES, t), F32), jnp.zeros((1, t), F32)))
        room = top_k - jnp.sum(above, axis=0, keepdims=True)

        def rebias_tile(kt, carry):
            k0 = pl.multiple_of(kt * t, t)
            gt = jnp.logical_and(score_ref[kt] > thr, (key_row + k0) <= (qry_col + q0))
            rank = logit_ref[0, kt]
            keep = jnp.logical_or(gt, jnp.logical_and(rank >= 0.0, rank < room))
            bias_ref[kt] = jnp.where(keep, 0.0, NEG_INF)
            return carry

        lax.fori_loop(0, nk, rebias_tile, 0)

    m_ref[...] = jnp.full(m_ref.shape, NEG_INF, F32)
    acc_ref[...] = jnp.zeros(acc_ref.shape, F32)
    log2e = 1.4426950408889634
    scale = (HEAD_DIM ** -0.5) * log2e

    def logit_tile(kt, carry):
        k0 = pl.multiple_of(kt * t, t)
        bias = bias_ref[kt]
        kpos = (key_row + (k0 - q0)).astype(F32)
        for h in range(DSA_HEADS):
            slope = (2.0 ** (-8.0 * (h + 1) / DSA_HEADS)) * log2e
            hs = slice(h * HEAD_DIM, (h + 1) * HEAD_DIM)
            s = lax.dot_general(ak_ref[pl.ds(k0, t), hs], aq_ref[:, hs], NT,
                                preferred_element_type=F32)
            lg = s * scale + (kpos * slope + bias)
            logit_ref[h, kt] = lg
            m_ref[h] = jnp.maximum(m_ref[h], _fold_rows(lg, jnp.maximum))
        return carry

    lax.fori_loop(0, nk, logit_tile, 0)

    for h in range(DSA_HEADS):
        m_ref[h] = jnp.broadcast_to(jnp.max(m_ref[h], axis=0, keepdims=True), (SUBLANES, t))

    def pv_tile(kt, carry):
        for h in range(DSA_HEADS):
            p = jnp.exp2(logit_ref[h, kt] - m_ref[h][:1])
            acc_ref[h] = acc_ref[h] + jnp.dot(vt_ref[kt, h], p.astype(BF16),
                                              preferred_element_type=F32)
        return carry

    lax.fori_loop(0, nk, pv_tile, 0)

    for h in range(DSA_HEADS):
        hs = slice(h * HEAD_DIM, (h + 1) * HEAD_DIM)
        acc = acc_ref[h]
        o = (acc[:HEAD_DIM] / acc[HEAD_DIM:HEAD_DIM + 1]).T
        o_ref[:, hs] = (_silu(ag_ref[:, hs].astype(F32)) * o).astype(o_ref.dtype)


def _dsa(z, ikw, batch, seq, top_k):
    t = DSA_T
    nq = seq // t
    width = DSA_HEADS * HEAD_DIM
    wblk = width // LANE
    qspec = lambda off: pl.BlockSpec((t, width), lambda b, i: (b * nq + i, off // wblk))
    kvspec = lambda off: pl.BlockSpec((seq, width), lambda b, i: (b, off // wblk))
    return pl.pallas_call(
        functools.partial(_dsa_kernel, top_k=top_k),
        grid=(batch, nq),
        in_specs=[qspec(CB_IQ),
                  pl.BlockSpec((seq, LANE), lambda b, i: (b, 0)),
                  qspec(CB_AQ), kvspec(CB_AK), kvspec(CB_AV), qspec(CB_AG)],
        out_specs=pl.BlockSpec((t, width), lambda b, i: (b * nq + i, 0)),
        out_shape=jax.ShapeDtypeStruct((batch * seq, width), BF16),
        scratch_shapes=[pltpu.VMEM((seq, LANE), BF16),
                        pltpu.VMEM((seq, LANE), BF16),
                        pltpu.VMEM((nq, DSA_HEADS, VT_ROWS, t), BF16),
                        pltpu.VMEM((LANE, t), F32),
                        pltpu.VMEM((nq, t, t), F32),
                        pltpu.VMEM((nq, t, t), BF16),
                        pltpu.VMEM((nq, t, t), F32),
                        pltpu.VMEM((DSA_HEADS, nq, t, t), F32),
                        pltpu.VMEM((DSA_HEADS, SUBLANES, t), F32),
                        pltpu.VMEM((DSA_HEADS, VT_ROWS, t), F32)],
        compiler_params=pltpu.CompilerParams(
            dimension_semantics=("arbitrary", "arbitrary"), vmem_limit_bytes=VMEM_LIMIT),
        name="dsa",
    )(z, ikw, z, z, z, z)


def _memattn_kernel(q_ref, g_ref, mem_ref, nw_ref, w_ref, o_ref, wbf_ref, kv_ref):
    b = pl.program_id(0)
    i = pl.program_id(1)
    width = MEM_HEADS * MEM_HEAD_DIM

    @pl.when(jnp.logical_and(b == 0, i == 0))
    def _():
        for r in range(0, w_ref.shape[0], PROJ_PREP_ROWS):
            wbf_ref[r:r + PROJ_PREP_ROWS, :] = w_ref[r:r + PROJ_PREP_ROWS, :].astype(BF16)

    @pl.when(i == 0)
    def _():
        x = mem_ref[...]
        ms = jnp.mean(x * x, axis=-1, keepdims=True)
        hn = (x * lax.rsqrt(ms + EPS) * nw_ref[...]).astype(BF16)
        kv_ref[...] = jnp.dot(hn, wbf_ref[...], preferred_element_type=F32).astype(BF16)

    scale = MEM_HEAD_DIM ** -0.5
    for h in range(MEM_HEADS):
        hs = slice(h * MEM_HEAD_DIM, (h + 1) * MEM_HEAD_DIM)
        vs = slice(width + h * MEM_HEAD_DIM, width + (h + 1) * MEM_HEAD_DIM)
        s = lax.dot_general(q_ref[:, hs], kv_ref[:, hs], NT, preferred_element_type=F32) * scale
        p = jnp.exp(s - jnp.max(s, axis=-1, keepdims=True))
        l = jnp.sum(p, axis=-1, keepdims=True)
        o = jnp.dot(p.astype(BF16), kv_ref[:, vs], preferred_element_type=F32) / l
        o_ref[:, hs] = (_silu(g_ref[:, hs].astype(F32)) * o).astype(o_ref.dtype)


def _memattn(z, mem2d, norm_w, w_kv, batch, seq, mem_tokens):
    tl = MEM_TL
    d = mem2d.shape[1]
    width = MEM_HEADS * MEM_HEAD_DIM
    wblk = width // LANE
    nl = seq // tl
    return pl.pallas_call(
        _memattn_kernel,
        grid=(batch, nl),
        in_specs=[pl.BlockSpec((tl, width), lambda b, i: (b * nl + i, CB_MQ // wblk)),
                  pl.BlockSpec((tl, width), lambda b, i: (b * nl + i, CB_MG // wblk)),
                  pl.BlockSpec((mem_tokens, d), lambda b, i: (b, 0)),
                  pl.BlockSpec((1, d), lambda b, i: (0, 0)),
                  pl.BlockSpec((d, 2 * width), lambda b, i: (0, 0), pipeline_mode=SINGLE_BUFFER)],
        out_specs=pl.BlockSpec((tl, width), lambda b, i: (b * nl + i, 0)),
        out_shape=jax.ShapeDtypeStruct((batch * seq, width), BF16),
        scratch_shapes=[pltpu.VMEM((d, 2 * width), BF16),
                        pltpu.VMEM((mem_tokens, 2 * width), BF16)],
        compiler_params=pltpu.CompilerParams(
            dimension_semantics=("arbitrary", "arbitrary"), vmem_limit_bytes=VMEM_LIMIT),
        name="mem_attn",
    )(z, z, mem2d, norm_w.reshape(1, d), w_kv)


def _merge_kernel(yr_ref, yd_ref, ym_ref, wr_ref, wd_ref, wm_ref, gr_ref, gd_ref, gm_ref, o_ref,
                  wbf_ref):
    @pl.when(pl.program_id(1) == 0)
    def _():
        for b, w_ref in enumerate((wr_ref, wd_ref, wm_ref)):
            wbf_ref[b] = w_ref[...].astype(BF16)

    def branch(b, y_ref, g_ref):
        up = jnp.dot(y_ref[...], wbf_ref[b], preferred_element_type=F32)
        return _sigmoid(g_ref[...].astype(F32)) * up

    merged = branch(0, yr_ref, gr_ref) + branch(1, yd_ref, gd_ref) + branch(2, ym_ref, gm_ref)
    o_ref[...] = merged.astype(o_ref.dtype)


def _merge(y_ret, y_dsa, y_mem, w_ret, w_dsa, w_mem, z):
    tm, tn = MERGE_TM, MERGE_TN
    m, kdim = y_ret.shape
    n = w_ret.shape[1]
    gblk = tn // LANE
    yspec = pl.BlockSpec((tm, kdim), lambda j, i: (i, 0))
    wspec = pl.BlockSpec((kdim, tn), lambda j, i: (0, j))
    gspec = lambda off: pl.BlockSpec((tm, tn), lambda j, i: (i, off // gblk + j))
    return pl.pallas_call(
        _merge_kernel,
        grid=(n // tn, m // tm),
        in_specs=[yspec, yspec, yspec, wspec, wspec, wspec,
                  gspec(CB_GRET), gspec(CB_GDSA), gspec(CB_GMEM)],
        out_specs=pl.BlockSpec((tm, tn), lambda j, i: (i, j)),
        out_shape=jax.ShapeDtypeStruct((m, n), BF16),
        scratch_shapes=[pltpu.VMEM((N_BRANCHES, kdim, tn), BF16)],
        compiler_params=pltpu.CompilerParams(
            dimension_semantics=("arbitrary", "arbitrary"), vmem_limit_bytes=VMEM_LIMIT),
        name="merge",
    )(y_ret, y_dsa, y_mem, w_ret, w_dsa, w_mem, z, z, z)


def _out_kernel(m_ref, w_ref, x_ref, nw_ref, o_ref, wbf_ref):
    @pl.when(pl.program_id(0) == 0)
    def _():
        for r in range(0, w_ref.shape[0], PROJ_PREP_ROWS):
            wbf_ref[r:r + PROJ_PREP_ROWS, :] = w_ref[r:r + PROJ_PREP_ROWS, :].astype(BF16)

    out = jnp.dot(m_ref[...], wbf_ref[...], preferred_element_type=F32)
    ms = jnp.mean(out * out, axis=-1, keepdims=True)
    o_ref[...] = x_ref[...] + out * lax.rsqrt(ms + EPS) * nw_ref[...]


def _out_proj(merged, w_o, x2d, post_w):
    tm = OUT_TM
    m, d = x2d.shape
    return pl.pallas_call(
        _out_kernel,
        grid=(m // tm,),
        in_specs=[pl.BlockSpec((tm, d), lambda i: (i, 0)),
                  pl.BlockSpec((d, d), lambda i: (0, 0), pipeline_mode=SINGLE_BUFFER),
                  pl.BlockSpec((tm, d), lambda i: (i, 0)),
                  pl.BlockSpec((1, d), lambda i: (0, 0))],
        out_specs=pl.BlockSpec((tm, d), lambda i: (i, 0)),
        out_shape=jax.ShapeDtypeStruct((m, d), F32),
        scratch_shapes=[pltpu.VMEM((d, d), BF16)],
        compiler_params=pltpu.CompilerParams(
            dimension_semantics=("arbitrary",), vmem_limit_bytes=VMEM_LIMIT),
        name="out_proj_postnorm",
    )(merged, w_o, x2d, post_w.reshape(1, d))


def _layer(x, mem, pre_norm_w, w_in, ret_gn_w, mem_norm_w, w_mem_kv,
           w_up_ret, w_up_dsa, w_up_mem, w_o, post_norm_w):
    batch, seq, d = x.shape
    mem_tokens = mem.shape[1]
    top_k = min(IDX_TOPK_MAX, seq // 4)
    x2d = x.reshape(batch * seq, d)

    w_in_t = w_in.T
    w_ikw = jnp.pad(w_in_t[IDX_TAIL_START:IDX_TAIL_START + IDX_TAIL_COLS],
                    ((0, LANE - IDX_TAIL_COLS), (0, 0))).astype(BF16)
    h, ikw = _prenorm(x2d, pre_norm_w, w_ikw)
    z = _project(h, w_in_t)

    log_g = jnp.log1p(-(2.0 ** (-5.0 - jnp.arange(RET_HEADS, dtype=F32))))
    y_ret = _retention(z, log_g, ret_gn_w, batch, seq)
    y_dsa = _dsa(z, ikw, batch, seq, top_k)
    y_mem = _memattn(z, mem.reshape(batch * mem_tokens, d), mem_norm_w, w_mem_kv, batch, seq,
                     mem_tokens)

    merged = _merge(y_ret, y_dsa, y_mem, w_up_ret, w_up_dsa, w_up_mem, z)
    out = _out_proj(merged, w_o, x2d, post_norm_w)
    return out.reshape(batch, seq, d)


def kernel(x, mem, pre_norm_w, w_in, ret_gn_w, mem_norm_w, w_mem_kv, w_up_ret, w_up_dsa,
           w_up_mem, w_o, post_norm_w):
    for layer in range(w_in.shape[0]):
        x = _layer(x, mem, pre_norm_w[layer], w_in[layer], ret_gn_w[layer], mem_norm_w[layer],
                   w_mem_kv[layer], w_up_ret[layer], w_up_dsa[layer], w_up_mem[layer],
                   w_o[layer], post_norm_w[layer])
    return x
```

```python
import functools

import jax
import jax.numpy as jnp
from jax import lax
from jax.experimental import pallas as pl
from jax.experimental.pallas import tpu as pltpu

F32 = jnp.float32
BF16 = jnp.bfloat16
I32 = jnp.int32

LANE = 128
SUBLANES = 8
N_BRANCHES = 3
RET_HEADS = 8
HEAD_DIM = 128
DSA_HEADS = 8
IDX_HEADS = 16
IDX_DIM = 64
IDX_TOPK_MAX = 256
MEM_HEADS = 4
MEM_HEAD_DIM = 256
NEG_INF = -1e30
EPS = 1e-6
INT_MIN = -2**31

CB_RQ, CB_RK, CB_RV, CB_RG = 0, 8, 16, 24
CB_AQ, CB_AK, CB_AV, CB_AG = 32, 40, 48, 56
CB_IQ = 64
CB_MQ, CB_MG = 72, 80
CB_GRET, CB_GDSA, CB_GMEM = 88, 104, 120
N_COL_BLOCKS = 136
Z_COLS = N_COL_BLOCKS * LANE
IDX_TAIL_START = (CB_IQ + IDX_HEADS * IDX_DIM // LANE) * LANE
IDX_TAIL_COLS = IDX_DIM + IDX_HEADS

PROJ_TM = 2048
PROJ_TN = 1024
PROJ_ALIGNED_TILES = IDX_TAIL_START // PROJ_TN
PROJ_ROWS = 512
PROJ_PREP_ROWS = 256
PRENORM_TM = 1024
MEM_TL = 512
MERGE_TM, MERGE_TN = 512, 1024
OUT_TM = 512
VMEM_LIMIT = 56 * 1024 * 1024
SINGLE_BUFFER = pl.Buffered(1)

NT = (((1,), (1,)), ((), ()))
TN = (((0,), (0,)), ((), ()))


def _sigmoid(v):
    return 1.0 / (1.0 + jnp.exp(-v))


def _silu(v):
    return v * _sigmoid(v)


def _prenorm_kernel(x_ref, w_ref, wikw_ref, h_ref, ikw_ref):
    x = x_ref[...]
    ms = jnp.mean(x * x, axis=-1, keepdims=True)
    h = (x * lax.rsqrt(ms + EPS) * w_ref[...]).astype(h_ref.dtype)
    h_ref[...] = h
    ikw_ref[...] = lax.dot_general(h, wikw_ref[...], NT,
                                   preferred_element_type=F32).astype(ikw_ref.dtype)


def _prenorm(x2d, w, w_ikw):
    m, d = x2d.shape
    tm = min(PRENORM_TM, m)
    return pl.pallas_call(
        _prenorm_kernel,
        grid=(m // tm,),
        in_specs=[pl.BlockSpec((tm, d), lambda i: (i, 0)),
                  pl.BlockSpec((1, d), lambda i: (0, 0)),
                  pl.BlockSpec((LANE, d), lambda i: (0, 0))],
        out_specs=[pl.BlockSpec((tm, d), lambda i: (i, 0)),
                   pl.BlockSpec((tm, LANE), lambda i: (i, 0))],
        out_shape=[jax.ShapeDtypeStruct((m, d), BF16),
                   jax.ShapeDtypeStruct((m, LANE), BF16)],
        compiler_params=pltpu.CompilerParams(dimension_semantics=("arbitrary",)),
        name="prenorm_idx",
    )(x2d, w.reshape(1, d), w_ikw)


def _proj_kernel(h_ref, w_ref, z_ref, wbf_ref):
    @pl.when(pl.program_id(1) == 0)
    def _():
        for r in range(0, w_ref.shape[0], PROJ_PREP_ROWS):
            wbf_ref[r:r + PROJ_PREP_ROWS, :] = w_ref[r:r + PROJ_PREP_ROWS, :].astype(BF16)

    w = wbf_ref[...]
    for r in range(0, h_ref.shape[0], PROJ_ROWS):
        z_ref[r:r + PROJ_ROWS, :] = lax.dot_general(
            h_ref[r:r + PROJ_ROWS, :], w, NT, preferred_element_type=F32).astype(z_ref.dtype)


def _project(h, w_in_t):
    m, d = h.shape
    tm = min(PROJ_TM, m)

    def w_rows(j, i):
        skip = jnp.where(j >= PROJ_ALIGNED_TILES, IDX_TAIL_COLS // SUBLANES, 0)
        return ((j * (PROJ_TN // SUBLANES) + skip) * SUBLANES, 0)

    return pl.pallas_call(
        _proj_kernel,
        grid=(Z_COLS // PROJ_TN, m // tm),
        in_specs=[pl.BlockSpec((tm, d), lambda j, i: (i, 0)),
                  pl.BlockSpec((pl.Element(PROJ_TN), pl.Element(d)), w_rows)],
        out_specs=pl.BlockSpec((tm, PROJ_TN), lambda j, i: (i, j)),
        out_shape=jax.ShapeDtypeStruct((m, Z_COLS), BF16),
        scratch_shapes=[pltpu.VMEM((PROJ_TN, d), BF16)],
        compiler_params=pltpu.CompilerParams(
            dimension_semantics=("arbitrary", "arbitrary"), vmem_limit_bytes=VMEM_LIMIT),
        name="in_proj",
    )(h, w_in_t)


RET_C = 256
RET_HEADS_PER_STEP = 2


def _retention_kernel(lg_ref, q_ref, k_ref, v_ref, g_ref, gw_ref, o_ref):
    c = RET_C
    seq = q_ref.shape[0]
    scale = HEAD_DIM ** -0.5
    ri = lax.broadcasted_iota(I32, (c, c), 0)
    ci = lax.broadcasted_iota(I32, (c, c), 1)
    diff = (ri - ci).astype(F32)
    idx = lax.broadcasted_iota(I32, (c, 1), 0).astype(F32)

    for hh in range(RET_HEADS_PER_STEP):
        hs = slice(hh * HEAD_DIM, (hh + 1) * HEAD_DIM)
        lg = lg_ref[pl.program_id(1) * RET_HEADS_PER_STEP + hh]
        decay = jnp.where(diff >= 0, jnp.exp(lg * jnp.maximum(diff, 0.0)), 0.0) * scale
        q_dec = jnp.exp(lg * (idx + 1.0))
        k_dec = jnp.exp(lg * (c - 1.0 - idx)) * scale
        chunk_dec = jnp.exp(lg * jnp.full((1, HEAD_DIM), float(c), F32))
        gw = gw_ref[:, hs]
        state = jnp.zeros((HEAD_DIM, HEAD_DIM), F32)
        for i in range(seq // c):
            r = i * c
            qi = q_ref[pl.ds(r, c), hs]
            ki = k_ref[pl.ds(r, c), hs]
            vi = v_ref[pl.ds(r, c), hs]
            inner = lax.dot_general(qi, ki, NT, preferred_element_type=F32) * decay
            o = (jnp.dot(inner.astype(BF16), vi, preferred_element_type=F32)
                 + jnp.dot(qi, state.astype(BF16), preferred_element_type=F32) * q_dec)
            vs = (vi.astype(F32) * k_dec).astype(BF16)
            state = state * chunk_dec + lax.dot_general(ki, vs, TN, preferred_element_type=F32)
            mu = jnp.mean(o, axis=-1, keepdims=True)
            oc = o - mu
            var = jnp.mean(oc * oc, axis=-1, keepdims=True)
            on = oc * lax.rsqrt(var + 1e-5) * gw
            g = g_ref[pl.ds(r, c), hs].astype(F32)
            o_ref[pl.ds(r, c), hs] = (_silu(g) * on).astype(o_ref.dtype)


def _retention(z, log_g, gn_w, batch, seq):
    hp = RET_HEADS_PER_STEP
    width = hp * HEAD_DIM
    blk = lambda off: pl.BlockSpec((seq, width), lambda b, h: (b, off // hp + h))
    return pl.pallas_call(
        _retention_kernel,
        grid=(batch, RET_HEADS // hp),
        in_specs=[pl.BlockSpec(memory_space=pltpu.SMEM),
                  blk(CB_RQ), blk(CB_RK), blk(CB_RV), blk(CB_RG),
                  pl.BlockSpec((1, width), lambda b, h: (0, h))],
        out_specs=pl.BlockSpec((seq, width), lambda b, h: (b, h)),
        out_shape=jax.ShapeDtypeStruct((batch * seq, RET_HEADS * HEAD_DIM), BF16),
        compiler_params=pltpu.CompilerParams(
            dimension_semantics=("arbitrary", "arbitrary"), vmem_limit_bytes=VMEM_LIMIT),
        name="retention",
    )(log_g, z, z, z, z, gn_w.reshape(1, -1))


DSA_T = 256
VT_ROWS = HEAD_DIM + 16

def _fold_rows(x, op, rows=SUBLANES):
    while x.shape[0] > rows:
        half = x.shape[0] // 2
        x = op(x[:half], x[half:])
    return x


I16 = jnp.int16
PACKED_ROWS = 2 * SUBLANES
ORD_NEG_INF = -2**31 + 0x7FFFFF


def _flip(v):
    return v ^ ((v >> 31) & 0x7FFFFFFF)


def _ordinal_to_f32(o):
    return lax.bitcast_convert_type(_flip(jnp.maximum(o, ORD_NEG_INF)), F32)


def _prefix_to_bf16_bits(p):
    return _flip(jnp.maximum(p, ORD_NEG_INF >> 16) << 16) & -65536


def _dsa_kernel(iq_ref, ikw_ref, aq_ref, ak_ref, av_ref, ag_ref, o_ref,
                ika_ref, ikb_ref, vt_ref, wt_ref, score_ref, score16_ref, bias_ref, logit_ref,
                m_ref, acc_ref, *, top_k):
    t = DSA_T
    qb = pl.program_id(1)
    nk = qb + 1
    q0 = pl.multiple_of(qb * t, t)
    nq = vt_ref.shape[0]

    @pl.when(qb == 0)
    def _():
        ikw = ikw_ref[...].astype(F32)
        lane = lax.broadcasted_iota(I32, ikw.shape, 1)
        a = jnp.where(lane < IDX_DIM, ikw, 0.0)
        ika_ref[...] = a.astype(BF16)
        ikb_ref[...] = pltpu.roll(a, IDX_DIM, axis=1).astype(BF16)

        def vt_tile(kt, carry):
            k0 = pl.multiple_of(kt * t, t)
            for h in range(DSA_HEADS):
                hs = slice(h * HEAD_DIM, (h + 1) * HEAD_DIM)
                vt_ref[kt, h, :HEAD_DIM, :] = av_ref[pl.ds(k0, t), hs].astype(F32).T.astype(BF16)
                vt_ref[kt, h, HEAD_DIM:, :] = jnp.ones((VT_ROWS - HEAD_DIM, t), BF16)
            return carry

        lax.fori_loop(0, nq, vt_tile, 0)

    wt_ref[...] = ikw_ref[pl.ds(q0, t), :].astype(F32).T

    key_row = lax.broadcasted_iota(I32, (t, t), 0)
    qry_col = lax.broadcasted_iota(I32, (t, t), 1)

    def score_tile(kt, carry):
        k0 = pl.multiple_of(kt * t, t)
        ka = ika_ref[pl.ds(k0, t), :]
        kb = ikb_ref[pl.ds(k0, t), :]
        acc = jnp.zeros((t, t), F32)
        for j in range(IDX_HEADS // 2):
            qj = iq_ref[:, j * LANE:(j + 1) * LANE]
            for half, kx in enumerate((ka, kb)):
                hh = IDX_DIM + 2 * j + half
                s = lax.dot_general(kx, qj, NT, preferred_element_type=F32)
                acc = acc + jnp.maximum(s, 0.0) * wt_ref[hh:hh + 1, :]
        causal = (key_row + k0) <= (qry_col + q0)
        score = jnp.where(causal, acc, -jnp.inf)
        score_ref[kt] = score
        score16_ref[kt] = score.astype(BF16)
        return carry

    lax.fori_loop(0, nk, score_tile, 0)

    one16, zero16 = jnp.ones((), I16), jnp.zeros((), I16)

    def prefix_step(i, pfx):
        cand_pfx = pfx + lax.shift_left(jnp.int32(1), 15 - i)
        cand = lax.bitcast_convert_type(_prefix_to_bf16_bits(cand_pfx), F32)
        cand16 = jnp.broadcast_to(cand, (PACKED_ROWS, t)).astype(BF16)[:1]

        def count_tile(kt, cnt):
            ge = jnp.where(score16_ref[kt] >= cand16, one16, zero16)
            return cnt + _fold_rows(ge, jnp.add, PACKED_ROWS)

        cnt = lax.fori_loop(0, nk, count_tile, jnp.zeros((PACKED_ROWS, t), I16))
        total = jnp.sum(cnt.astype(I32), axis=0, keepdims=True)
        return jnp.where(total >= top_k, cand_pfx, pfx)

    pfx = lax.fori_loop(0, 16, prefix_step, jnp.full((1, t), -2**15, I32))

    def bisect_step(i, bounds):
        lo, hi = bounds
        mid = lo + ((hi - lo) >> 1)
        cand = _ordinal_to_f32(mid)

        def count_tile(kt, cnt):
            return cnt + _fold_rows(jnp.where(score_ref[kt] >= cand, 1, 0), jnp.add)

        cnt = lax.fori_loop(0, nk, count_tile, jnp.zeros((SUBLANES, t), I32))
        enough = jnp.sum(cnt, axis=0, keepdims=True) >= top_k
        return jnp.where(enough, mid, lo), jnp.where(enough, hi, mid)

    lo, _ = lax.fori_loop(0, 17, bisect_step, (_flip(_prefix_to_bf16_bits(pfx - 1)),
                                               _flip(_prefix_to_bf16_bits(pfx + 1))))
    thr = _ordinal_to_f32(lo)

    def bias_tile(kt, kept):
        k0 = pl.multiple_of(kt * t, t)
        keep = jnp.logical_and(score_ref[kt] >= thr, (key_row + k0) <= (qry_col + q0))
        bias_ref[kt] = jnp.where(keep, 0.0, NEG_INF)
        return kept + _fold_rows(jnp.where(keep, 1, 0), jnp.add)

    kept = lax.fori_loop(0, nk, bias_tile, jnp.zeros((SUBLANES, t), I32))
    kept = jnp.sum(kept, axis=0, keepdims=True)

    @pl.when(jnp.max(kept) > top_k)
    def _():
        strictly_lower = (qry_col < key_row).astype(BF16)

        def tie_tile(kt, carry):
            above, tied_before = carry
            k0 = pl.multiple_of(kt * t, t)
            causal = (key_row + k0) <= (qry_col + q0)
            score = score_ref[kt]
            gt = jnp.logical_and(score > thr, causal)
            eq = jnp.logical_and(score == thr, causal)
            eq01 = jnp.where(eq, 1.0, 0.0)
            tied_here = tied_before + jnp.dot(strictly_lower, eq01.astype(BF16),
                                              preferred_element_type=F32)
            logit_ref[0, kt] = jnp.where(eq, tied_here, -1.0)
            above = above + _fold_rows(jnp.where(gt, 1.0, 0.0), jnp.add)
            return above, tied_before + jnp.sum(eq01, axis=0, keepdims=True)

        above, _ = lax.fori_loop(0, nk, tie_tile,
                                 (jnp.zeros((SUBLANES, t), F32), jnp.zeros((1, t), F32)))
        room = top_k - jnp.sum(above, axis=0, keepdims=True)

        def rebias_tile(kt, carry):
            k0 = pl.multiple_of(kt * t, t)
            gt = jnp.logical_and(score_ref[kt] > thr, (key_row + k0) <= (qry_col + q0))
            rank = logit_ref[0, kt]
            keep = jnp.logical_or(gt, jnp.logical_and(rank >= 0.0, rank < room))
            bias_ref[kt] = jnp.where(keep, 0.0, NEG_INF)
            return carry

        lax.fori_loop(0, nk, rebias_tile, 0)

    m_ref[...] = jnp.full(m_ref.shape, NEG_INF, F32)
    acc_ref[...] = jnp.zeros(acc_ref.shape, F32)
    log2e = 1.4426950408889634
    scale = (HEAD_DIM ** -0.5) * log2e

    def logit_tile(kt, carry):
        k0 = pl.multiple_of(kt * t, t)
        bias = bias_ref[kt]
        kpos = (key_row + (k0 - q0)).astype(F32)
        for h in range(DSA_HEADS):
            slope = (2.0 ** (-8.0 * (h + 1) / DSA_HEADS)) * log2e
            hs = slice(h * HEAD_DIM, (h + 1) * HEAD_DIM)
            s = lax.dot_general(ak_ref[pl.ds(k0, t), hs], aq_ref[:, hs], NT,
                                preferred_element_type=F32)
            lg = s * scale + (kpos * slope + bias)
            logit_ref[h, kt] = lg
            m_ref[h] = jnp.maximum(m_ref[h], _fold_rows(lg, jnp.maximum))
        return carry

    lax.fori_loop(0, nk, logit_tile, 0)

    for h in range(DSA_HEADS):
        m_ref[h] = jnp.broadcast_to(jnp.max(m_ref[h], axis=0, keepdims=True), (SUBLANES, t))

    def pv_tile(kt, carry):
        for h in range(DSA_HEADS):
            p = jnp.exp2(logit_ref[h, kt] - m_ref[h][:1])
            acc_ref[h] = acc_ref[h] + jnp.dot(vt_ref[kt, h], p.astype(BF16),
                                              preferred_element_type=F32)
        return carry

    lax.fori_loop(0, nk, pv_tile, 0)

    for h in range(DSA_HEADS):
        hs = slice(h * HEAD_DIM, (h + 1) * HEAD_DIM)
        acc = acc_ref[h]
        o = (acc[:HEAD_DIM] / acc[HEAD_DIM:HEAD_DIM + 1]).T
        o_ref[:, hs] = (_silu(ag_ref[:, hs].astype(F32)) * o).astype(o_ref.dtype)


def _dsa(z, ikw, batch, seq, top_k):
    t = DSA_T
    nq = seq // t
    width = DSA_HEADS * HEAD_DIM
    wblk = width // LANE
    qspec = lambda off: pl.BlockSpec((t, width), lambda b, i: (b * nq + i, off // wblk))
    kvspec = lambda off: pl.BlockSpec((seq, width), lambda b, i: (b, off // wblk))
    return pl.pallas_call(
        functools.partial(_dsa_kernel, top_k=top_k),
        grid=(batch, nq),
        in_specs=[qspec(CB_IQ),
                  pl.BlockSpec((seq, LANE), lambda b, i: (b, 0)),
                  qspec(CB_AQ), kvspec(CB_AK), kvspec(CB_AV), qspec(CB_AG)],
        out_specs=pl.BlockSpec((t, width), lambda b, i: (b * nq + i, 0)),
        out_shape=jax.ShapeDtypeStruct((batch * seq, width), BF16),
        scratch_shapes=[pltpu.VMEM((seq, LANE), BF16),
                        pltpu.VMEM((seq, LANE), BF16),
                        pltpu.VMEM((nq, DSA_HEADS, VT_ROWS, t), BF16),
                        pltpu.VMEM((LANE, t), F32),
                        pltpu.VMEM((nq, t, t), F32),
                        pltpu.VMEM((nq, t, t), BF16),
                        pltpu.VMEM((nq, t, t), F32),
                        pltpu.VMEM((DSA_HEADS, nq, t, t), F32),
                        pltpu.VMEM((DSA_HEADS, SUBLANES, t), F32),
                        pltpu.VMEM((DSA_HEADS, VT_ROWS, t), F32)],
        compiler_params=pltpu.CompilerParams(
            dimension_semantics=("arbitrary", "arbitrary"), vmem_limit_bytes=VMEM_LIMIT),
        name="dsa",
    )(z, ikw, z, z, z, z)


def _memattn_kernel(q_ref, g_ref, mem_ref, nw_ref, w_ref, o_ref, wbf_ref, kv_ref):
    b = pl.program_id(0)
    i = pl.program_id(1)
    width = MEM_HEADS * MEM_HEAD_DIM

    @pl.when(jnp.logical_and(b == 0, i == 0))
    def _():
        for r in range(0, w_ref.shape[0], PROJ_PREP_ROWS):
            wbf_ref[r:r + PROJ_PREP_ROWS, :] = w_ref[r:r + PROJ_PREP_ROWS, :].astype(BF16)

    @pl.when(i == 0)
    def _():
        x = mem_ref[...]
        ms = jnp.mean(x * x, axis=-1, keepdims=True)
        hn = (x * lax.rsqrt(ms + EPS) * nw_ref[...]).astype(BF16)
        kv_ref[...] = jnp.dot(hn, wbf_ref[...], preferred_element_type=F32).astype(BF16)

    scale = MEM_HEAD_DIM ** -0.5
    for h in range(MEM_HEADS):
        hs = slice(h * MEM_HEAD_DIM, (h + 1) * MEM_HEAD_DIM)
        vs = slice(width + h * MEM_HEAD_DIM, width + (h + 1) * MEM_HEAD_DIM)
        s = lax.dot_general(q_ref[:, hs], kv_ref[:, hs], NT, preferred_element_type=F32) * scale
        p = jnp.exp(s - jnp.max(s, axis=-1, keepdims=True))
        l = jnp.sum(p, axis=-1, keepdims=True)
        o = jnp.dot(p.astype(BF16), kv_ref[:, vs], preferred_element_type=F32) / l
        o_ref[:, hs] = (_silu(g_ref[:, hs].astype(F32)) * o).astype(o_ref.dtype)


def _memattn(z, mem2d, norm_w, w_kv, batch, seq, mem_tokens):
    tl = MEM_TL
    d = mem2d.shape[1]
    width = MEM_HEADS * MEM_HEAD_DIM
    wblk = width // LANE
    nl = seq // tl
    return pl.pallas_call(
        _memattn_kernel,
        grid=(batch, nl),
        in_specs=[pl.BlockSpec((tl, width), lambda b, i: (b * nl + i, CB_MQ // wblk)),
                  pl.BlockSpec((tl, width), lambda b, i: (b * nl + i, CB_MG // wblk)),
                  pl.BlockSpec((mem_tokens, d), lambda b, i: (b, 0)),
                  pl.BlockSpec((1, d), lambda b, i: (0, 0)),
                  pl.BlockSpec((d, 2 * width), lambda b, i: (0, 0), pipeline_mode=SINGLE_BUFFER)],
        out_specs=pl.BlockSpec((tl, width), lambda b, i: (b * nl + i, 0)),
        out_shape=jax.ShapeDtypeStruct((batch * seq, width), BF16),
        scratch_shapes=[pltpu.VMEM((d, 2 * width), BF16),
                        pltpu.VMEM((mem_tokens, 2 * width), BF16)],
        compiler_params=pltpu.CompilerParams(
            dimension_semantics=("arbitrary", "arbitrary"), vmem_limit_bytes=VMEM_LIMIT),
        name="mem_attn",
    )(z, z, mem2d, norm_w.reshape(1, d), w_kv)


def _merge_kernel(yr_ref, yd_ref, ym_ref, wr_ref, wd_ref, wm_ref, gr_ref, gd_ref, gm_ref, o_ref,
                  wbf_ref):
    @pl.when(pl.program_id(1) == 0)
    def _():
        for b, w_ref in enumerate((wr_ref, wd_ref, wm_ref)):
            wbf_ref[b] = w_ref[...].astype(BF16)

    def branch(b, y_ref, g_ref):
        up = jnp.dot(y_ref[...], wbf_ref[b], preferred_element_type=F32)
        return _sigmoid(g_ref[...].astype(F32)) * up

    merged = branch(0, yr_ref, gr_ref) + branch(1, yd_ref, gd_ref) + branch(2, ym_ref, gm_ref)
    o_ref[...] = merged.astype(o_ref.dtype)


def _merge(y_ret, y_dsa, y_mem, w_ret, w_dsa, w_mem, z):
    tm, tn = MERGE_TM, MERGE_TN
    m, kdim = y_ret.shape
    n = w_ret.shape[1]
    gblk = tn // LANE
    yspec = pl.BlockSpec((tm, kdim), lambda j, i: (i, 0))
    wspec = pl.BlockSpec((kdim, tn), lambda j, i: (0, j))
    gspec = lambda off: pl.BlockSpec((tm, tn), lambda j, i: (i, off // gblk + j))
    return pl.pallas_call(
        _merge_kernel,
        grid=(n // tn, m // tm),
        in_specs=[yspec, yspec, yspec, wspec, wspec, wspec,
                  gspec(CB_GRET), gspec(CB_GDSA), gspec(CB_GMEM)],
        out_specs=pl.BlockSpec((tm, tn), lambda j, i: (i, j)),
        out_shape=jax.ShapeDtypeStruct((m, n), BF16),
        scratch_shapes=[pltpu.VMEM((N_BRANCHES, kdim, tn), BF16)],
        compiler_params=pltpu.CompilerParams(
            dimension_semantics=("arbitrary", "arbitrary"), vmem_limit_bytes=VMEM_LIMIT),
        name="merge",
    )(y_ret, y_dsa, y_mem, w_ret, w_dsa, w_mem, z, z, z)


def _out_kernel(m_ref, w_ref, x_ref, nw_ref, o_ref, wbf_ref):
    @pl.when(pl.program_id(0) == 0)
    def _():
        for r in range(0, w_ref.shape[0], PROJ_PREP_ROWS):
            wbf_ref[r:r + PROJ_PREP_ROWS, :] = w_ref[r:r + PROJ_PREP_ROWS, :].astype(BF16)

    out = jnp.dot(m_ref[...], wbf_ref[...], preferred_element_type=F32)
    ms = jnp.mean(out * out, axis=-1, keepdims=True)
    o_ref[...] = x_ref[...] + out * lax.rsqrt(ms + EPS) * nw_ref[...]


def _out_proj(merged, w_o, x2d, post_w):
    tm = OUT_TM
    m, d = x2d.shape
    return pl.pallas_call(
        _out_kernel,
        grid=(m // tm,),
        in_specs=[pl.BlockSpec((tm, d), lambda i: (i, 0)),
                  pl.BlockSpec((d, d), lambda i: (0, 0), pipeline_mode=SINGLE_BUFFER),
                  pl.BlockSpec((tm, d), lambda i: (i, 0)),
                  pl.BlockSpec((1, d), lambda i: (0, 0))],
        out_specs=pl.BlockSpec((tm, d), lambda i: (i, 0)),
        out_shape=jax.ShapeDtypeStruct((m, d), F32),
        scratch_shapes=[pltpu.VMEM((d, d), BF16)],
        compiler_params=pltpu.CompilerParams(
            dimension_semantics=("arbitrary",), vmem_limit_bytes=VMEM_LIMIT),
        name="out_proj_postnorm",
    )(merged, w_o, x2d, post_w.reshape(1, d))


def _layer(x, mem, pre_norm_w, w_in, ret_gn_w, mem_norm_w, w_mem_kv,
           w_up_ret, w_up_dsa, w_up_mem, w_o, post_norm_w):
    batch, seq, d = x.shape
    mem_tokens = mem.shape[1]
    top_k = min(IDX_TOPK_MAX, seq // 4)
    assert w_in.shape == (d, Z_COLS + IDX_TAIL_COLS), w_in.shape
    assert seq % max(DSA_T, RET_C, MEM_TL) == 0 and seq < 2**15, seq
    assert (batch * seq) % max(MERGE_TM, OUT_TM) == 0 and top_k <= DSA_T, (batch, seq)
    x2d = x.reshape(batch * seq, d)

    w_in_t = w_in.T
    w_ikw = jnp.pad(w_in_t[IDX_TAIL_START:IDX_TAIL_START + IDX_TAIL_COLS],
                    ((0, LANE - IDX_TAIL_COLS), (0, 0))).astype(BF16)
    h, ikw = _prenorm(x2d, pre_norm_w, w_ikw)
    z = _project(h, w_in_t)

    log_g = jnp.log1p(-(2.0 ** (-5.0 - jnp.arange(RET_HEADS, dtype=F32))))
    y_ret = _retention(z, log_g, ret_gn_w, batch, seq)
    y_dsa = _dsa(z, ikw, batch, seq, top_k)
    y_mem = _memattn(z, mem.reshape(batch * mem_tokens, d), mem_norm_w, w_mem_kv, batch, seq,
                     mem_tokens)

    merged = _merge(y_ret, y_dsa, y_mem, w_up_ret, w_up_dsa, w_up_mem, z)
    out = _out_proj(merged, w_o, x2d, post_norm_w)
    return out.reshape(batch, seq, d)


def kernel(x, mem, pre_norm_w, w_in, ret_gn_w, mem_norm_w, w_mem_kv, w_up_ret, w_up_dsa,
           w_up_mem, w_o, post_norm_w):
    for layer in range(w_in.shape[0]):
        x = _layer(x, mem, pre_norm_w[layer], w_in[layer], ret_gn_w[layer], mem_norm_w[layer],
                   w_mem_kv[layer], w_up_ret[layer], w_up_dsa[layer], w_up_mem[layer],
                   w_o[layer], post_norm_w[layer])
    return x
```

```python
import functools

import jax
import jax.numpy as jnp
from jax import lax
from jax.experimental import pallas as pl
from jax.experimental.pallas import tpu as pltpu

F32 = jnp.float32
BF16 = jnp.bfloat16
I32 = jnp.int32

LANE = 128
SUBLANES = 8
N_BRANCHES = 3
RET_HEADS = 8
HEAD_DIM = 128
DSA_HEADS = 8
IDX_HEADS = 16
IDX_DIM = 64
IDX_TOPK_MAX = 256
MEM_HEADS = 4
MEM_HEAD_DIM = 256
NEG_INF = -1e30
EPS = 1e-6
INT_MIN = -2**31

CB_RQ, CB_RK, CB_RV, CB_RG = 0, 8, 16, 24
CB_AQ, CB_AK, CB_AV, CB_AG = 32, 40, 48, 56
CB_IQ = 64
CB_MQ, CB_MG = 72, 80
CB_GRET, CB_GDSA, CB_GMEM = 88, 104, 120
N_COL_BLOCKS = 136
Z_COLS = N_COL_BLOCKS * LANE
IDX_TAIL_START = (CB_IQ + IDX_HEADS * IDX_DIM // LANE) * LANE
IDX_TAIL_COLS = IDX_DIM + IDX_HEADS

PROJ_TM = 2048
PROJ_TN = 1024
PROJ_ALIGNED_TILES = IDX_TAIL_START // PROJ_TN
PROJ_ROWS = 512
PROJ_PREP_ROWS = 256
PRENORM_TM = 1024
MEM_TL = 1024
MERGE_TM, MERGE_TN = 512, 1024
OUT_TM = 512
VMEM_LIMIT = 56 * 1024 * 1024
SINGLE_BUFFER = pl.Buffered(1)

NT = (((1,), (1,)), ((), ()))
TN = (((0,), (0,)), ((), ()))


def _sigmoid(v):
    return 1.0 / (1.0 + jnp.exp(-v))


def _silu(v):
    return v * _sigmoid(v)


def _prenorm_kernel(x_ref, w_ref, wikw_ref, h_ref, ikw_ref):
    x = x_ref[...]
    ms = jnp.mean(x * x, axis=-1, keepdims=True)
    h = (x * lax.rsqrt(ms + EPS) * w_ref[...]).astype(h_ref.dtype)
    h_ref[...] = h
    ikw_ref[...] = lax.dot_general(h, wikw_ref[...], NT,
                                   preferred_element_type=F32).astype(ikw_ref.dtype)


def _prenorm(x2d, w, w_ikw):
    m, d = x2d.shape
    tm = min(PRENORM_TM, m)
    return pl.pallas_call(
        _prenorm_kernel,
        grid=(m // tm,),
        in_specs=[pl.BlockSpec((tm, d), lambda i: (i, 0)),
                  pl.BlockSpec((1, d), lambda i: (0, 0)),
                  pl.BlockSpec((LANE, d), lambda i: (0, 0))],
        out_specs=[pl.BlockSpec((tm, d), lambda i: (i, 0)),
                   pl.BlockSpec((tm, LANE), lambda i: (i, 0))],
        out_shape=[jax.ShapeDtypeStruct((m, d), BF16),
                   jax.ShapeDtypeStruct((m, LANE), BF16)],
        compiler_params=pltpu.CompilerParams(dimension_semantics=("arbitrary",)),
        name="prenorm_idx",
    )(x2d, w.reshape(1, d), w_ikw)


def _proj_kernel(h_ref, w_ref, z_ref, wbf_ref):
    @pl.when(pl.program_id(1) == 0)
    def _():
        for r in range(0, w_ref.shape[0], PROJ_PREP_ROWS):
            wbf_ref[r:r + PROJ_PREP_ROWS, :] = w_ref[r:r + PROJ_PREP_ROWS, :].astype(BF16)

    w = wbf_ref[...]
    for r in range(0, h_ref.shape[0], PROJ_ROWS):
        z_ref[r:r + PROJ_ROWS, :] = lax.dot_general(
            h_ref[r:r + PROJ_ROWS, :], w, NT, preferred_element_type=F32).astype(z_ref.dtype)


def _project(h, w_in_t):
    m, d = h.shape
    tm = min(PROJ_TM, m)

    def w_rows(j, i):
        skip = jnp.where(j >= PROJ_ALIGNED_TILES, IDX_TAIL_COLS // SUBLANES, 0)
        return ((j * (PROJ_TN // SUBLANES) + skip) * SUBLANES, 0)

    return pl.pallas_call(
        _proj_kernel,
        grid=(Z_COLS // PROJ_TN, m // tm),
        in_specs=[pl.BlockSpec((tm, d), lambda j, i: (i, 0)),
                  pl.BlockSpec((pl.Element(PROJ_TN), pl.Element(d)), w_rows)],
        out_specs=pl.BlockSpec((tm, PROJ_TN), lambda j, i: (i, j)),
        out_shape=jax.ShapeDtypeStruct((m, Z_COLS), BF16),
        scratch_shapes=[pltpu.VMEM((PROJ_TN, d), BF16)],
        compiler_params=pltpu.CompilerParams(
            dimension_semantics=("arbitrary", "arbitrary"), vmem_limit_bytes=VMEM_LIMIT),
        name="in_proj",
    )(h, w_in_t)


RET_C = 256
RET_HEADS_PER_STEP = 2


def _retention_kernel(lg_ref, q_ref, k_ref, v_ref, g_ref, gw_ref, o_ref):
    c = RET_C
    seq = q_ref.shape[0]
    scale = HEAD_DIM ** -0.5
    ri = lax.broadcasted_iota(I32, (c, c), 0)
    ci = lax.broadcasted_iota(I32, (c, c), 1)
    diff = (ri - ci).astype(F32)
    idx = lax.broadcasted_iota(I32, (c, 1), 0).astype(F32)

    for hh in range(RET_HEADS_PER_STEP):
        hs = slice(hh * HEAD_DIM, (hh + 1) * HEAD_DIM)
        lg = lg_ref[pl.program_id(1) * RET_HEADS_PER_STEP + hh]
        decay = jnp.where(diff >= 0, jnp.exp(lg * jnp.maximum(diff, 0.0)), 0.0) * scale
        q_dec = jnp.exp(lg * (idx + 1.0))
        k_dec = jnp.exp(lg * (c - 1.0 - idx)) * scale
        chunk_dec = jnp.exp(lg * jnp.full((1, HEAD_DIM), float(c), F32))
        gw = gw_ref[:, hs]
        state = jnp.zeros((HEAD_DIM, HEAD_DIM), F32)
        for i in range(seq // c):
            r = i * c
            qi = q_ref[pl.ds(r, c), hs]
            ki = k_ref[pl.ds(r, c), hs]
            vi = v_ref[pl.ds(r, c), hs]
            inner = lax.dot_general(qi, ki, NT, preferred_element_type=F32) * decay
            o = (jnp.dot(inner.astype(BF16), vi, preferred_element_type=F32)
                 + jnp.dot(qi, state.astype(BF16), preferred_element_type=F32) * q_dec)
            vs = (vi.astype(F32) * k_dec).astype(BF16)
            state = state * chunk_dec + lax.dot_general(ki, vs, TN, preferred_element_type=F32)
            mu = jnp.mean(o, axis=-1, keepdims=True)
            oc = o - mu
            var = jnp.mean(oc * oc, axis=-1, keepdims=True)
            on = oc * lax.rsqrt(var + 1e-5) * gw
            g = g_ref[pl.ds(r, c), hs].astype(F32)
            o_ref[pl.ds(r, c), hs] = (_silu(g) * on).astype(o_ref.dtype)


def _retention(z, log_g, gn_w, batch, seq):
    hp = RET_HEADS_PER_STEP
    width = hp * HEAD_DIM
    blk = lambda off: pl.BlockSpec((seq, width), lambda b, h: (b, off // hp + h))
    return pl.pallas_call(
        _retention_kernel,
        grid=(batch, RET_HEADS // hp),
        in_specs=[pl.BlockSpec(memory_space=pltpu.SMEM),
                  blk(CB_RQ), blk(CB_RK), blk(CB_RV), blk(CB_RG),
                  pl.BlockSpec((1, width), lambda b, h: (0, h))],
        out_specs=pl.BlockSpec((seq, width), lambda b, h: (b, h)),
        out_shape=jax.ShapeDtypeStruct((batch * seq, RET_HEADS * HEAD_DIM), BF16),
        compiler_params=pltpu.CompilerParams(
            dimension_semantics=("arbitrary", "arbitrary"), vmem_limit_bytes=VMEM_LIMIT),
        name="retention",
    )(log_g, z, z, z, z, gn_w.reshape(1, -1))


DSA_T = 256
VT_ROWS = HEAD_DIM + 16

def _fold_rows(x, op, rows=SUBLANES):
    while x.shape[0] > rows:
        half = x.shape[0] // 2
        x = op(x[:half], x[half:])
    return x


def _for_tiles_paired(nk, step):
    def pair(i, carry):
        step(2 * i)
        step(2 * i + 1)
        return carry

    lax.fori_loop(0, nk >> 1, pair, 0)

    @pl.when((nk & 1) == 1)
    def _():
        step(nk - 1)


I16 = jnp.int16
PACKED_ROWS = 2 * SUBLANES
ORD_NEG_INF = -2**31 + 0x7FFFFF


def _flip(v):
    return v ^ ((v >> 31) & 0x7FFFFFFF)


def _ordinal_to_f32(o):
    return lax.bitcast_convert_type(_flip(jnp.maximum(o, ORD_NEG_INF)), F32)


def _prefix_to_bf16_bits(p):
    return _flip(jnp.maximum(p, ORD_NEG_INF >> 16) << 16) & -65536


def _dsa_kernel(iq_ref, ikw_ref, aq_ref, ak_ref, av_ref, ag_ref, o_ref,
                ika_ref, ikb_ref, vt_ref, wt_ref, score_ref, score16_ref, bias_ref, logit_ref,
                m_ref, acc_ref, *, top_k):
    t = DSA_T
    qb = pl.program_id(1)
    nk = qb + 1
    q0 = pl.multiple_of(qb * t, t)
    nq = vt_ref.shape[0]

    @pl.when(qb == 0)
    def _():
        ikw = ikw_ref[...].astype(F32)
        lane = lax.broadcasted_iota(I32, ikw.shape, 1)
        a = jnp.where(lane < IDX_DIM, ikw, 0.0)
        ika_ref[...] = a.astype(BF16)
        ikb_ref[...] = pltpu.roll(a, IDX_DIM, axis=1).astype(BF16)

        def vt_tile(kt, carry):
            k0 = pl.multiple_of(kt * t, t)
            for h in range(DSA_HEADS):
                hs = slice(h * HEAD_DIM, (h + 1) * HEAD_DIM)
                vt_ref[kt, h, :HEAD_DIM, :] = av_ref[pl.ds(k0, t), hs].astype(F32).T.astype(BF16)
                vt_ref[kt, h, HEAD_DIM:, :] = jnp.ones((VT_ROWS - HEAD_DIM, t), BF16)
            return carry

        lax.fori_loop(0, nq, vt_tile, 0)

    wt_ref[...] = ikw_ref[pl.ds(q0, t), :].astype(F32).T

    key_row = lax.broadcasted_iota(I32, (t, t), 0)
    qry_col = lax.broadcasted_iota(I32, (t, t), 1)

    def score_tile(kt):
        k0 = pl.multiple_of(kt * t, t)
        ka = ika_ref[pl.ds(k0, t), :]
        kb = ikb_ref[pl.ds(k0, t), :]
        acc = jnp.zeros((t, t), F32)
        for j in range(IDX_HEADS // 2):
            qj = iq_ref[:, j * LANE:(j + 1) * LANE]
            for half, kx in enumerate((ka, kb)):
                hh = IDX_DIM + 2 * j + half
                s = lax.dot_general(kx, qj, NT, preferred_element_type=F32)
                acc = acc + jnp.maximum(s, 0.0) * wt_ref[hh:hh + 1, :]
        causal = (key_row + k0) <= (qry_col + q0)
        score = jnp.where(causal, acc, -jnp.inf)
        score_ref[kt] = score
        score16_ref[kt] = score.astype(BF16)

    _for_tiles_paired(nk, score_tile)

    one16, zero16 = jnp.ones((), I16), jnp.zeros((), I16)

    def prefix_step(i, pfx):
        cand_pfx = pfx + lax.shift_left(jnp.int32(1), 15 - i)
        cand = lax.bitcast_convert_type(_prefix_to_bf16_bits(cand_pfx), F32)
        cand16 = jnp.broadcast_to(cand, (PACKED_ROWS, t)).astype(BF16)[:1]

        def count_tile(kt, cnt):
            ge = jnp.where(score16_ref[kt] >= cand16, one16, zero16)
            return cnt + _fold_rows(ge, jnp.add, PACKED_ROWS)

        cnt = lax.fori_loop(0, nk, count_tile, jnp.zeros((PACKED_ROWS, t), I16))
        total = jnp.sum(cnt.astype(I32), axis=0, keepdims=True)
        return jnp.where(total >= top_k, cand_pfx, pfx)

    pfx = lax.fori_loop(0, 16, prefix_step, jnp.full((1, t), -2**15, I32))

    def bisect_step(i, bounds):
        lo, hi = bounds
        mid = lo + ((hi - lo) >> 1)
        cand = _ordinal_to_f32(mid)

        def count_tile(kt, cnt):
            return cnt + _fold_rows(jnp.where(score_ref[kt] >= cand, 1, 0), jnp.add)

        cnt = lax.fori_loop(0, nk, count_tile, jnp.zeros((SUBLANES, t), I32))
        enough = jnp.sum(cnt, axis=0, keepdims=True) >= top_k
        return jnp.where(enough, mid, lo), jnp.where(enough, hi, mid)

    lo, _ = lax.fori_loop(0, 17, bisect_step, (_flip(_prefix_to_bf16_bits(pfx - 1)),
                                               _flip(_prefix_to_bf16_bits(pfx + 1))))
    thr = _ordinal_to_f32(lo)

    def bias_tile(kt, kept):
        k0 = pl.multiple_of(kt * t, t)
        keep = jnp.logical_and(score_ref[kt] >= thr, (key_row + k0) <= (qry_col + q0))
        bias_ref[kt] = jnp.where(keep, 0.0, NEG_INF)
        return kept + _fold_rows(jnp.where(keep, 1, 0), jnp.add)

    kept = lax.fori_loop(0, nk, bias_tile, jnp.zeros((SUBLANES, t), I32))
    kept = jnp.sum(kept, axis=0, keepdims=True)

    @pl.when(jnp.max(kept) > top_k)
    def _():
        strictly_lower = (qry_col < key_row).astype(BF16)

        def tie_tile(kt, carry):
            above, tied_before = carry
            k0 = pl.multiple_of(kt * t, t)
            causal = (key_row + k0) <= (qry_col + q0)
            score = score_ref[kt]
            gt = jnp.logical_and(score > thr, causal)
            eq = jnp.logical_and(score == thr, causal)
            eq01 = jnp.where(eq, 1.0, 0.0)
            tied_here = tied_before + jnp.dot(strictly_lower, eq01.astype(BF16),
                                              preferred_element_type=F32)
            logit_ref[0, kt] = jnp.where(eq, tied_here, -1.0)
            above = above + _fold_rows(jnp.where(gt, 1.0, 0.0), jnp.add)
            return above, tied_before + jnp.sum(eq01, axis=0, keepdims=True)

        above, _ = lax.fori_loop(0, nk, tie_tile,
                                 (jnp.zeros((SUBLANES, t), F32), jnp.zeros((1, t), F32)))
        room = top_k - jnp.sum(above, axis=0, keepdims=True)

        def rebias_tile(kt, carry):
            k0 = pl.multiple_of(kt * t, t)
            gt = jnp.logical_and(score_ref[kt] > thr, (key_row + k0) <= (qry_col + q0))
            rank = logit_ref[0, kt]
            keep = jnp.logical_or(gt, jnp.logical_and(rank >= 0.0, rank < room))
            bias_ref[kt] = jnp.where(keep, 0.0, NEG_INF)
            return carry

        lax.fori_loop(0, nk, rebias_tile, 0)

    m_ref[...] = jnp.full(m_ref.shape, NEG_INF, F32)
    acc_ref[...] = jnp.zeros(acc_ref.shape, F32)
    log2e = 1.4426950408889634
    scale = (HEAD_DIM ** -0.5) * log2e

    def logit_tile(kt):
        k0 = pl.multiple_of(kt * t, t)
        bias = bias_ref[kt]
        kpos = (key_row + (k0 - q0)).astype(F32)
        for h in range(DSA_HEADS):
            slope = (2.0 ** (-8.0 * (h + 1) / DSA_HEADS)) * log2e
            hs = slice(h * HEAD_DIM, (h + 1) * HEAD_DIM)
            s = lax.dot_general(ak_ref[pl.ds(k0, t), hs], aq_ref[:, hs], NT,
                                preferred_element_type=F32)
            lg = s * scale + (kpos * slope + bias)
            logit_ref[h, kt] = lg
            m_ref[h] = jnp.maximum(m_ref[h], _fold_rows(lg, jnp.maximum))

    _for_tiles_paired(nk, logit_tile)

    for h in range(DSA_HEADS):
        m_ref[h] = jnp.broadcast_to(jnp.max(m_ref[h], axis=0, keepdims=True), (SUBLANES, t))

    def pv_tile(kt):
        for h in range(DSA_HEADS):
            p = jnp.exp2(logit_ref[h, kt] - m_ref[h][:1])
            acc_ref[h] = acc_ref[h] + jnp.dot(vt_ref[kt, h], p.astype(BF16),
                                              preferred_element_type=F32)

    _for_tiles_paired(nk, pv_tile)

    for h in range(DSA_HEADS):
        hs = slice(h * HEAD_DIM, (h + 1) * HEAD_DIM)
        acc = acc_ref[h]
        o = (acc[:HEAD_DIM] / acc[HEAD_DIM:HEAD_DIM + 1]).T
        o_ref[:, hs] = (_silu(ag_ref[:, hs].astype(F32)) * o).astype(o_ref.dtype)


def _dsa(z, ikw, batch, seq, top_k):
    t = DSA_T
    nq = seq // t
    width = DSA_HEADS * HEAD_DIM
    wblk = width // LANE
    qspec = lambda off: pl.BlockSpec((t, width), lambda b, i: (b * nq + i, off // wblk))
    kvspec = lambda off: pl.BlockSpec((seq, width), lambda b, i: (b, off // wblk))
    return pl.pallas_call(
        functools.partial(_dsa_kernel, top_k=top_k),
        grid=(batch, nq),
        in_specs=[qspec(CB_IQ),
                  pl.BlockSpec((seq, LANE), lambda b, i: (b, 0)),
                  qspec(CB_AQ), kvspec(CB_AK), kvspec(CB_AV), qspec(CB_AG)],
        out_specs=pl.BlockSpec((t, width), lambda b, i: (b * nq + i, 0)),
        out_shape=jax.ShapeDtypeStruct((batch * seq, width), BF16),
        scratch_shapes=[pltpu.VMEM((seq, LANE), BF16),
                        pltpu.VMEM((seq, LANE), BF16),
                        pltpu.VMEM((nq, DSA_HEADS, VT_ROWS, t), BF16),
                        pltpu.VMEM((LANE, t), F32),
                        pltpu.VMEM((nq, t, t), F32),
                        pltpu.VMEM((nq, t, t), BF16),
                        pltpu.VMEM((nq, t, t), F32),
                        pltpu.VMEM((DSA_HEADS, nq, t, t), F32),
                        pltpu.VMEM((DSA_HEADS, SUBLANES, t), F32),
                        pltpu.VMEM((DSA_HEADS, VT_ROWS, t), F32)],
        compiler_params=pltpu.CompilerParams(
            dimension_semantics=("arbitrary", "arbitrary"), vmem_limit_bytes=VMEM_LIMIT),
        name="dsa",
    )(z, ikw, z, z, z, z)


def _memattn_kernel(q_ref, g_ref, mem_ref, nw_ref, w_ref, o_ref, wbf_ref, kv_ref):
    b = pl.program_id(0)
    i = pl.program_id(1)
    width = MEM_HEADS * MEM_HEAD_DIM

    @pl.when(jnp.logical_and(b == 0, i == 0))
    def _():
        for r in range(0, w_ref.shape[0], PROJ_PREP_ROWS):
            wbf_ref[r:r + PROJ_PREP_ROWS, :] = w_ref[r:r + PROJ_PREP_ROWS, :].astype(BF16)

    @pl.when(i == 0)
    def _():
        x = mem_ref[...]
        ms = jnp.mean(x * x, axis=-1, keepdims=True)
        hn = (x * lax.rsqrt(ms + EPS) * nw_ref[...]).astype(BF16)
        kv_ref[...] = jnp.dot(hn, wbf_ref[...], preferred_element_type=F32).astype(BF16)

    scale = MEM_HEAD_DIM ** -0.5
    for h in range(MEM_HEADS):
        hs = slice(h * MEM_HEAD_DIM, (h + 1) * MEM_HEAD_DIM)
        vs = slice(width + h * MEM_HEAD_DIM, width + (h + 1) * MEM_HEAD_DIM)
        s = lax.dot_general(q_ref[:, hs], kv_ref[:, hs], NT, preferred_element_type=F32) * scale
        p = jnp.exp(s - jnp.max(s, axis=-1, keepdims=True))
        l = jnp.sum(p, axis=-1, keepdims=True)
        o = jnp.dot(p.astype(BF16), kv_ref[:, vs], preferred_element_type=F32) / l
        o_ref[:, hs] = (_silu(g_ref[:, hs].astype(F32)) * o).astype(o_ref.dtype)


def _memattn(z, mem2d, norm_w, w_kv, batch, seq, mem_tokens):
    tl = MEM_TL
    d = mem2d.shape[1]
    width = MEM_HEADS * MEM_HEAD_DIM
    wblk = width // LANE
    nl = seq // tl
    return pl.pallas_call(
        _memattn_kernel,
        grid=(batch, nl),
        in_specs=[pl.BlockSpec((tl, width), lambda b, i: (b * nl + i, CB_MQ // wblk)),
                  pl.BlockSpec((tl, width), lambda b, i: (b * nl + i, CB_MG // wblk)),
                  pl.BlockSpec((mem_tokens, d), lambda b, i: (b, 0)),
                  pl.BlockSpec((1, d), lambda b, i: (0, 0)),
                  pl.BlockSpec((d, 2 * width), lambda b, i: (0, 0), pipeline_mode=SINGLE_BUFFER)],
        out_specs=pl.BlockSpec((tl, width), lambda b, i: (b * nl + i, 0)),
        out_shape=jax.ShapeDtypeStruct((batch * seq, width), BF16),
        scratch_shapes=[pltpu.VMEM((d, 2 * width), BF16),
                        pltpu.VMEM((mem_tokens, 2 * width), BF16)],
        compiler_params=pltpu.CompilerParams(
            dimension_semantics=("arbitrary", "arbitrary"), vmem_limit_bytes=VMEM_LIMIT),
        name="mem_attn",
    )(z, z, mem2d, norm_w.reshape(1, d), w_kv)


def _merge_kernel(yr_ref, yd_ref, ym_ref, wr_ref, wd_ref, wm_ref, gr_ref, gd_ref, gm_ref, o_ref,
                  wbf_ref):
    @pl.when(pl.program_id(1) == 0)
    def _():
        for b, w_ref in enumerate((wr_ref, wd_ref, wm_ref)):
            wbf_ref[b] = w_ref[...].astype(BF16)

    def branch(b, y_ref, g_ref):
        up = jnp.dot(y_ref[...], wbf_ref[b], preferred_element_type=F32)
        return _sigmoid(g_ref[...].astype(F32)) * up

    merged = branch(0, yr_ref, gr_ref) + branch(1, yd_ref, gd_ref) + branch(2, ym_ref, gm_ref)
    o_ref[...] = merged.astype(o_ref.dtype)


def _merge(y_ret, y_dsa, y_mem, w_ret, w_dsa, w_mem, z):
    tm, tn = MERGE_TM, MERGE_TN
    m, kdim = y_ret.shape
    n = w_ret.shape[1]
    gblk = tn // LANE
    yspec = pl.BlockSpec((tm, kdim), lambda j, i: (i, 0))
    wspec = pl.BlockSpec((kdim, tn), lambda j, i: (0, j))
    gspec = lambda off: pl.BlockSpec((tm, tn), lambda j, i: (i, off // gblk + j))
    return pl.pallas_call(
        _merge_kernel,
        grid=(n // tn, m // tm),
        in_specs=[yspec, yspec, yspec, wspec, wspec, wspec,
                  gspec(CB_GRET), gspec(CB_GDSA), gspec(CB_GMEM)],
        out_specs=pl.BlockSpec((tm, tn), lambda j, i: (i, j)),
        out_shape=jax.ShapeDtypeStruct((m, n), BF16),
        scratch_shapes=[pltpu.VMEM((N_BRANCHES, kdim, tn), BF16)],
        compiler_params=pltpu.CompilerParams(
            dimension_semantics=("arbitrary", "arbitrary"), vmem_limit_bytes=VMEM_LIMIT),
        name="merge",
    )(y_ret, y_dsa, y_mem, w_ret, w_dsa, w_mem, z, z, z)


def _out_kernel(m_ref, w_ref, x_ref, nw_ref, o_ref, wbf_ref):
    @pl.when(pl.program_id(0) == 0)
    def _():
        for r in range(0, w_ref.shape[0], PROJ_PREP_ROWS):
            wbf_ref[r:r + PROJ_PREP_ROWS, :] = w_ref[r:r + PROJ_PREP_ROWS, :].astype(BF16)

    out = jnp.dot(m_ref[...], wbf_ref[...], preferred_element_type=F32)
    ms = jnp.mean(out * out, axis=-1, keepdims=True)
    o_ref[...] = x_ref[...] + out * lax.rsqrt(ms + EPS) * nw_ref[...]


def _out_proj(merged, w_o, x2d, post_w):
    tm = OUT_TM
    m, d = x2d.shape
    return pl.pallas_call(
        _out_kernel,
        grid=(m // tm,),
        in_specs=[pl.BlockSpec((tm, d), lambda i: (i, 0)),
                  pl.BlockSpec((d, d), lambda i: (0, 0), pipeline_mode=SINGLE_BUFFER),
                  pl.BlockSpec((tm, d), lambda i: (i, 0)),
                  pl.BlockSpec((1, d), lambda i: (0, 0))],
        out_specs=pl.BlockSpec((tm, d), lambda i: (i, 0)),
        out_shape=jax.ShapeDtypeStruct((m, d), F32),
        scratch_shapes=[pltpu.VMEM((d, d), BF16)],
        compiler_params=pltpu.CompilerParams(
            dimension_semantics=("arbitrary",), vmem_limit_bytes=VMEM_LIMIT),
        name="out_proj_postnorm",
    )(merged, w_o, x2d, post_w.reshape(1, d))


def _layer(x, mem, pre_norm_w, w_in, ret_gn_w, mem_norm_w, w_mem_kv,
           w_up_ret, w_up_dsa, w_up_mem, w_o, post_norm_w):
    batch, seq, d = x.shape
    mem_tokens = mem.shape[1]
    top_k = min(IDX_TOPK_MAX, seq // 4)
    assert w_in.shape == (d, Z_COLS + IDX_TAIL_COLS), w_in.shape
    assert seq % max(DSA_T, RET_C, MEM_TL) == 0 and seq < 2**15, seq
    assert (batch * seq) % max(MERGE_TM, OUT_TM) == 0 and top_k <= DSA_T, (batch, seq)
    x2d = x.reshape(batch * seq, d)

    w_in_t = w_in.T
    w_ikw = jnp.pad(w_in_t[IDX_TAIL_START:IDX_TAIL_START + IDX_TAIL_COLS],
                    ((0, LANE - IDX_TAIL_COLS), (0, 0))).astype(BF16)
    h, ikw = _prenorm(x2d, pre_norm_w, w_ikw)
    z = _project(h, w_in_t)

    log_g = jnp.log1p(-(2.0 ** (-5.0 - jnp.arange(RET_HEADS, dtype=F32))))
    y_ret = _retention(z, log_g, ret_gn_w, batch, seq)
    y_dsa = _dsa(z, ikw, batch, seq, top_k)
    y_mem = _memattn(z, mem.reshape(batch * mem_tokens, d), mem_norm_w, w_mem_kv, batch, seq,
                     mem_tokens)

    merged = _merge(y_ret, y_dsa, y_mem, w_up_ret, w_up_dsa, w_up_mem, z)
    out = _out_proj(merged, w_o, x2d, post_norm_w)
    return out.reshape(batch, seq, d)


def kernel(x, mem, pre_norm_w, w_in, ret_gn_w, mem_norm_w, w_mem_kv, w_up_ret, w_up_dsa,
           w_up_mem, w_o, post_norm_w):
    for layer in range(w_in.shape[0]):
        x = _layer(x, mem, pre_norm_w[layer], w_in[layer], ret_gn_w[layer], mem_norm_w[layer],
                   w_mem_kv[layer], w_up_ret[layer], w_up_dsa[layer], w_up_mem[layer],
                   w_o[layer], post_norm_w[layer])
    return x
```

```python
import functools

import jax
import jax.numpy as jnp
from jax import lax
from jax.experimental import pallas as pl
from jax.experimental.pallas import tpu as pltpu

F32 = jnp.float32
BF16 = jnp.bfloat16
I32 = jnp.int32

LANE = 128
SUBLANES = 8
N_BRANCHES = 3
RET_HEADS = 8
HEAD_DIM = 128
DSA_HEADS = 8
IDX_HEADS = 16
IDX_DIM = 64
IDX_TOPK_MAX = 256
MEM_HEADS = 4
MEM_HEAD_DIM = 256
NEG_INF = -1e30
EPS = 1e-6
INT_MIN = -2**31

CB_RQ, CB_RK, CB_RV, CB_RG = 0, 8, 16, 24
CB_AQ, CB_AK, CB_AV, CB_AG = 32, 40, 48, 56
CB_IQ = 64
CB_MQ, CB_MG = 72, 80
CB_GRET, CB_GDSA, CB_GMEM = 88, 104, 120
N_COL_BLOCKS = 136
Z_COLS = N_COL_BLOCKS * LANE
IDX_TAIL_START = (CB_IQ + IDX_HEADS * IDX_DIM // LANE) * LANE
IDX_TAIL_COLS = IDX_DIM + IDX_HEADS

PROJ_TM = 2048
PROJ_TN = 1024
PROJ_ALIGNED_TILES = IDX_TAIL_START // PROJ_TN
PROJ_ROWS = 512
PROJ_PREP_ROWS = 256
PRENORM_TM = 1024
MEM_TL = 1024
MERGE_TM, MERGE_TN = 512, 1024
OUT_TM = 512
VMEM_LIMIT = 56 * 1024 * 1024
SINGLE_BUFFER = pl.Buffered(1)

NT = (((1,), (1,)), ((), ()))
TN = (((0,), (0,)), ((), ()))


def _sigmoid(v):
    return 1.0 / (1.0 + jnp.exp(-v))


def _silu(v):
    return v * _sigmoid(v)


def _prenorm_kernel(x_ref, w_ref, wikw_ref, h_ref, ikw_ref):
    x = x_ref[...]
    ms = jnp.mean(x * x, axis=-1, keepdims=True)
    h = (x * lax.rsqrt(ms + EPS) * w_ref[...]).astype(h_ref.dtype)
    h_ref[...] = h
    ikw_ref[...] = lax.dot_general(h, wikw_ref[...], NT,
                                   preferred_element_type=F32).astype(ikw_ref.dtype)


def _prenorm(x2d, w, w_ikw):
    m, d = x2d.shape
    tm = min(PRENORM_TM, m)
    return pl.pallas_call(
        _prenorm_kernel,
        grid=(m // tm,),
        in_specs=[pl.BlockSpec((tm, d), lambda i: (i, 0)),
                  pl.BlockSpec((1, d), lambda i: (0, 0)),
                  pl.BlockSpec((LANE, d), lambda i: (0, 0))],
        out_specs=[pl.BlockSpec((tm, d), lambda i: (i, 0)),
                   pl.BlockSpec((tm, LANE), lambda i: (i, 0))],
        out_shape=[jax.ShapeDtypeStruct((m, d), BF16),
                   jax.ShapeDtypeStruct((m, LANE), BF16)],
        compiler_params=pltpu.CompilerParams(dimension_semantics=("arbitrary",)),
        name="prenorm_idx",
    )(x2d, w.reshape(1, d), w_ikw)


def _proj_kernel(h_ref, w_ref, z_ref, wbf_ref):
    @pl.when(pl.program_id(1) == 0)
    def _():
        for r in range(0, w_ref.shape[0], PROJ_PREP_ROWS):
            wbf_ref[r:r + PROJ_PREP_ROWS, :] = w_ref[r:r + PROJ_PREP_ROWS, :].astype(BF16)

    w = wbf_ref[...]
    for r in range(0, h_ref.shape[0], PROJ_ROWS):
        z_ref[r:r + PROJ_ROWS, :] = lax.dot_general(
            h_ref[r:r + PROJ_ROWS, :], w, NT, preferred_element_type=F32).astype(z_ref.dtype)


def _project(h, w_in_t):
    m, d = h.shape
    tm = min(PROJ_TM, m)

    def w_rows(j, i):
        skip = jnp.where(j >= PROJ_ALIGNED_TILES, IDX_TAIL_COLS // SUBLANES, 0)
        return ((j * (PROJ_TN // SUBLANES) + skip) * SUBLANES, 0)

    n_row_tiles = m // tm

    def row_tile(j, i):
        return jnp.where((j & 1) == 0, i, n_row_tiles - 1 - i)

    return pl.pallas_call(
        _proj_kernel,
        grid=(Z_COLS // PROJ_TN, n_row_tiles),
        in_specs=[pl.BlockSpec((tm, d), lambda j, i: (row_tile(j, i), 0)),
                  pl.BlockSpec((pl.Element(PROJ_TN), pl.Element(d)), w_rows)],
        out_specs=pl.BlockSpec((tm, PROJ_TN), lambda j, i: (row_tile(j, i), j)),
        out_shape=jax.ShapeDtypeStruct((m, Z_COLS), BF16),
        scratch_shapes=[pltpu.VMEM((PROJ_TN, d), BF16)],
        compiler_params=pltpu.CompilerParams(
            dimension_semantics=("arbitrary", "arbitrary"), vmem_limit_bytes=VMEM_LIMIT),
        name="in_proj",
    )(h, w_in_t)


RET_C = 256
RET_HEADS_PER_STEP = 2


def _retention_kernel(lg_ref, q_ref, k_ref, v_ref, g_ref, gw_ref, o_ref):
    c = RET_C
    seq = q_ref.shape[0]
    scale = HEAD_DIM ** -0.5
    ri = lax.broadcasted_iota(I32, (c, c), 0)
    ci = lax.broadcasted_iota(I32, (c, c), 1)
    diff = (ri - ci).astype(F32)
    idx = lax.broadcasted_iota(I32, (c, 1), 0).astype(F32)

    for hh in range(RET_HEADS_PER_STEP):
        hs = slice(hh * HEAD_DIM, (hh + 1) * HEAD_DIM)
        lg = lg_ref[pl.program_id(1) * RET_HEADS_PER_STEP + hh]
        decay = jnp.where(diff >= 0, jnp.exp(lg * jnp.maximum(diff, 0.0)), 0.0) * scale
        q_dec = jnp.exp(lg * (idx + 1.0))
        k_dec = jnp.exp(lg * (c - 1.0 - idx)) * scale
        chunk_dec = jnp.exp(lg * jnp.full((1, HEAD_DIM), float(c), F32))
        gw = gw_ref[:, hs]
        state = jnp.zeros((HEAD_DIM, HEAD_DIM), F32)
        for i in range(seq // c):
            r = i * c
            qi = q_ref[pl.ds(r, c), hs]
            ki = k_ref[pl.ds(r, c), hs]
            vi = v_ref[pl.ds(r, c), hs]
            inner = lax.dot_general(qi, ki, NT, preferred_element_type=F32) * decay
            o = (jnp.dot(inner.astype(BF16), vi, preferred_element_type=F32)
                 + jnp.dot(qi, state.astype(BF16), preferred_element_type=F32) * q_dec)
            vs = (vi.astype(F32) * k_dec).astype(BF16)
            state = state * chunk_dec + lax.dot_general(ki, vs, TN, preferred_element_type=F32)
            mu = jnp.mean(o, axis=-1, keepdims=True)
            oc = o - mu
            var = jnp.mean(oc * oc, axis=-1, keepdims=True)
            on = oc * lax.rsqrt(var + 1e-5) * gw
            g = g_ref[pl.ds(r, c), hs].astype(F32)
            o_ref[pl.ds(r, c), hs] = (_silu(g) * on).astype(o_ref.dtype)


def _retention(z, log_g, gn_w, batch, seq):
    hp = RET_HEADS_PER_STEP
    width = hp * HEAD_DIM
    blk = lambda off: pl.BlockSpec((seq, width), lambda b, h: (b, off // hp + h))
    return pl.pallas_call(
        _retention_kernel,
        grid=(batch, RET_HEADS // hp),
        in_specs=[pl.BlockSpec(memory_space=pltpu.SMEM),
                  blk(CB_RQ), blk(CB_RK), blk(CB_RV), blk(CB_RG),
                  pl.BlockSpec((1, width), lambda b, h: (0, h))],
        out_specs=pl.BlockSpec((seq, width), lambda b, h: (b, h)),
        out_shape=jax.ShapeDtypeStruct((batch * seq, RET_HEADS * HEAD_DIM), BF16),
        compiler_params=pltpu.CompilerParams(
            dimension_semantics=("arbitrary", "arbitrary"), vmem_limit_bytes=VMEM_LIMIT),
        name="retention",
    )(log_g, z, z, z, z, gn_w.reshape(1, -1))


DSA_T = 256
I16 = jnp.int16
PACKED_ROWS = 2 * SUBLANES
VT_ROWS = HEAD_DIM + PACKED_ROWS


def _fold_rows(x, op, rows=SUBLANES):
    while x.shape[0] > rows:
        half = x.shape[0] // 2
        x = op(x[:half], x[half:])
    return x


def _for_tiles_paired(nk, step):
    def pair(i, carry):
        step(2 * i)
        step(2 * i + 1)
        return carry

    lax.fori_loop(0, nk >> 1, pair, 0)

    @pl.when((nk & 1) == 1)
    def _():
        step(nk - 1)


ORD_NEG_INF =-2**31 + 0x7FFFFF


def _flip(v):
    return v ^ ((v >> 31) & 0x7FFFFFFF)


def _ordinal_to_f32(o):
    return lax.bitcast_convert_type(_flip(jnp.maximum(o, ORD_NEG_INF)), F32)


def _prefix_to_bf16_bits(p):
    return _flip(jnp.maximum(p, ORD_NEG_INF >> 16) << 16) & -65536


def _dsa_kernel(iq_ref, ikw_ref, aq_ref, ak_ref, av_ref, ag_ref, o_ref,
                ika_ref, ikb_ref, vt_ref, wt_ref, score_ref, score16_ref, bias_ref, logit_ref,
                m_ref, acc_ref, *, top_k):
    t = DSA_T
    qb = pl.program_id(1)
    nk = qb + 1
    q0 = pl.multiple_of(qb * t, t)
    nq = vt_ref.shape[0]

    @pl.when(qb == 0)
    def _():
        ikw = ikw_ref[...].astype(F32)
        lane = lax.broadcasted_iota(I32, ikw.shape, 1)
        a = jnp.where(lane < IDX_DIM, ikw, 0.0)
        ika_ref[...] = a.astype(BF16)
        ikb_ref[...] = pltpu.roll(a, IDX_DIM, axis=1).astype(BF16)

        def vt_tile(kt, carry):
            k0 = pl.multiple_of(kt * t, t)
            for h in range(DSA_HEADS):
                hs = slice(h * HEAD_DIM, (h + 1) * HEAD_DIM)
                vt_ref[kt, h, :HEAD_DIM, :] = av_ref[pl.ds(k0, t), hs].astype(F32).T.astype(BF16)
                vt_ref[kt, h, HEAD_DIM:, :] = jnp.ones((VT_ROWS - HEAD_DIM, t), BF16)
            return carry

        lax.fori_loop(0, nq, vt_tile, 0)

    wt_ref[...] = ikw_ref[pl.ds(q0, t), :].astype(F32).T

    key_row = lax.broadcasted_iota(I32, (t, t), 0)
    qry_col = lax.broadcasted_iota(I32, (t, t), 1)

    def score_tile(kt):
        k0 = pl.multiple_of(kt * t, t)
        ka = ika_ref[pl.ds(k0, t), :]
        kb = ikb_ref[pl.ds(k0, t), :]
        acc = jnp.zeros((t, t), F32)
        for j in range(IDX_HEADS // 2):
            qj = iq_ref[:, j * LANE:(j + 1) * LANE]
            for half, kx in enumerate((ka, kb)):
                hh = IDX_DIM + 2 * j + half
                s = lax.dot_general(kx, qj, NT, preferred_element_type=F32)
                acc = acc + jnp.maximum(s, 0.0) * wt_ref[hh:hh + 1, :]
        causal = (key_row + k0) <= (qry_col + q0)
        score = jnp.where(causal, acc, -jnp.inf)
        score_ref[kt] = score
        score16_ref[kt] = score.astype(BF16)

    _for_tiles_paired(nk, score_tile)

    one16, zero16 = jnp.ones((), I16), jnp.zeros((), I16)

    def prefix_step(i, pfx):
        cand_pfx = pfx + lax.shift_left(jnp.int32(1), 15 - i)
        cand = lax.bitcast_convert_type(_prefix_to_bf16_bits(cand_pfx), F32)
        cand16 = jnp.broadcast_to(cand, (PACKED_ROWS, t)).astype(BF16)[:1]

        def count_tile(kt, cnt):
            ge = jnp.where(score16_ref[kt] >= cand16, one16, zero16)
            return cnt + _fold_rows(ge, jnp.add, PACKED_ROWS)

        cnt = lax.fori_loop(0, nk, count_tile, jnp.zeros((PACKED_ROWS, t), I16))
        total = jnp.sum(cnt.astype(I32), axis=0, keepdims=True)
        return jnp.where(total >= top_k, cand_pfx, pfx)

    pfx = lax.fori_loop(0, 16, prefix_step, jnp.full((1, t), -2**15, I32))

    def bisect_step(i, bounds):
        lo, hi = bounds
        mid = lo + ((hi - lo) >> 1)
        cand = _ordinal_to_f32(mid)

        def count_tile(kt, cnt):
            return cnt + _fold_rows(jnp.where(score_ref[kt] >= cand, 1, 0), jnp.add)

        cnt = lax.fori_loop(0, nk, count_tile, jnp.zeros((SUBLANES, t), I32))
        enough = jnp.sum(cnt, axis=0, keepdims=True) >= top_k
        return jnp.where(enough, mid, lo), jnp.where(enough, hi, mid)

    lo, _ = lax.fori_loop(0, 17, bisect_step, (_flip(_prefix_to_bf16_bits(pfx - 1)),
                                               _flip(_prefix_to_bf16_bits(pfx + 1))))
    thr = _ordinal_to_f32(lo)

    def bias_tile(kt, kept):
        k0 = pl.multiple_of(kt * t, t)
        keep = jnp.logical_and(score_ref[kt] >= thr, (key_row + k0) <= (qry_col + q0))
        bias_ref[kt] = jnp.where(keep, 0.0, NEG_INF)
        return kept + _fold_rows(jnp.where(keep, 1, 0), jnp.add)

    kept = lax.fori_loop(0, nk, bias_tile, jnp.zeros((SUBLANES, t), I32))
    kept = jnp.sum(kept, axis=0, keepdims=True)

    @pl.when(jnp.max(kept) > top_k)
    def _():
        strictly_lower = (qry_col < key_row).astype(BF16)

        def tie_tile(kt, carry):
            above, tied_before = carry
            k0 = pl.multiple_of(kt * t, t)
            causal = (key_row + k0) <= (qry_col + q0)
            score = score_ref[kt]
            gt = jnp.logical_and(score > thr, causal)
            eq = jnp.logical_and(score == thr, causal)
            eq01 = jnp.where(eq, 1.0, 0.0)
            tied_here = tied_before + jnp.dot(strictly_lower, eq01.astype(BF16),
                                              preferred_element_type=F32)
            logit_ref[0, kt] = jnp.where(eq, tied_here, -1.0)
            above = above + _fold_rows(jnp.where(gt, 1.0, 0.0), jnp.add)
            return above, tied_before + jnp.sum(eq01, axis=0, keepdims=True)

        above, _ = lax.fori_loop(0, nk, tie_tile,
                                 (jnp.zeros((SUBLANES, t), F32), jnp.zeros((1, t), F32)))
        room = top_k - jnp.sum(above, axis=0, keepdims=True)

        def rebias_tile(kt, carry):
            k0 = pl.multiple_of(kt * t, t)
            gt = jnp.logical_and(score_ref[kt] > thr, (key_row + k0) <= (qry_col + q0))
            rank = logit_ref[0, kt]
            keep = jnp.logical_or(gt, jnp.logical_and(rank >= 0.0, rank < room))
            bias_ref[kt] = jnp.where(keep, 0.0, NEG_INF)
            return carry

        lax.fori_loop(0, nk, rebias_tile, 0)

    m_ref[...] = jnp.full(m_ref.shape, NEG_INF, F32)
    acc_ref[...] = jnp.zeros(acc_ref.shape, F32)
    log2e = 1.4426950408889634
    scale = (HEAD_DIM ** -0.5) * log2e

    def logit_tile(kt):
        k0 = pl.multiple_of(kt * t, t)
        bias = bias_ref[kt]
        kpos = (key_row + (k0 - q0)).astype(F32)
        for h in range(DSA_HEADS):
            slope = (2.0 ** (-8.0 * (h + 1) / DSA_HEADS)) * log2e
            hs = slice(h * HEAD_DIM, (h + 1) * HEAD_DIM)
            s = lax.dot_general(ak_ref[pl.ds(k0, t), hs], aq_ref[:, hs], NT,
                                preferred_element_type=F32)
            lg = s * scale + (kpos * slope + bias)
            logit_ref[h, kt] = lg
            m_ref[h] = jnp.maximum(m_ref[h], _fold_rows(lg, jnp.maximum))

    _for_tiles_paired(nk, logit_tile)

    for h in range(DSA_HEADS):
        m_ref[h] = jnp.broadcast_to(jnp.max(m_ref[h], axis=0, keepdims=True), (SUBLANES, t))

    def pv_tile(kt):
        for h in range(DSA_HEADS):
            p = jnp.exp2(logit_ref[h, kt] - m_ref[h][:1])
            acc_ref[h] = acc_ref[h] + jnp.dot(vt_ref[kt, h], p.astype(BF16),
                                              preferred_element_type=F32)

    _for_tiles_paired(nk, pv_tile)

    for h in range(DSA_HEADS):
        hs = slice(h * HEAD_DIM, (h + 1) * HEAD_DIM)
        acc = acc_ref[h]
        o = (acc[:HEAD_DIM] / acc[HEAD_DIM:HEAD_DIM + 1]).T
        o_ref[:, hs] = (_silu(ag_ref[:, hs].astype(F32)) * o).astype(o_ref.dtype)


def _dsa(z, ikw, batch, seq, top_k):
    t = DSA_T
    nq = seq // t
    width = DSA_HEADS * HEAD_DIM
    wblk = width // LANE
    qspec = lambda off: pl.BlockSpec((t, width), lambda b, i: (b * nq + i, off // wblk))
    kvspec = lambda off: pl.BlockSpec((seq, width), lambda b, i: (b, off // wblk))
    return pl.pallas_call(
        functools.partial(_dsa_kernel, top_k=top_k),
        grid=(batch, nq),
        in_specs=[qspec(CB_IQ),
                  pl.BlockSpec((seq, LANE), lambda b, i: (b, 0)),
                  qspec(CB_AQ), kvspec(CB_AK), kvspec(CB_AV), qspec(CB_AG)],
        out_specs=pl.BlockSpec((t, width), lambda b, i: (b * nq + i, 0)),
        out_shape=jax.ShapeDtypeStruct((batch * seq, width), BF16),
        scratch_shapes=[pltpu.VMEM((seq, LANE), BF16),
                        pltpu.VMEM((seq, LANE), BF16),
                        pltpu.VMEM((nq, DSA_HEADS, VT_ROWS, t), BF16),
                        pltpu.VMEM((LANE, t), F32),
                        pltpu.VMEM((nq, t, t), F32),
                        pltpu.VMEM((nq, t, t), BF16),
                        pltpu.VMEM((nq, t, t), F32),
                        pltpu.VMEM((DSA_HEADS, nq, t, t), F32),
                        pltpu.VMEM((DSA_HEADS, SUBLANES, t), F32),
                        pltpu.VMEM((DSA_HEADS, VT_ROWS, t), F32)],
        compiler_params=pltpu.CompilerParams(
            dimension_semantics=("arbitrary", "arbitrary"), vmem_limit_bytes=VMEM_LIMIT),
        name="dsa",
    )(z, ikw, z, z, z, z)


def _memattn_kernel(q_ref, g_ref, mem_ref, nw_ref, w_ref, o_ref, wbf_ref, kv_ref):
    b = pl.program_id(0)
    i = pl.program_id(1)
    width = MEM_HEADS * MEM_HEAD_DIM

    @pl.when(jnp.logical_and(b == 0, i == 0))
    def _():
        for r in range(0, w_ref.shape[0], PROJ_PREP_ROWS):
            wbf_ref[r:r + PROJ_PREP_ROWS, :] = w_ref[r:r + PROJ_PREP_ROWS, :].astype(BF16)

    @pl.when(i == 0)
    def _():
        x = mem_ref[...]
        ms = jnp.mean(x * x, axis=-1, keepdims=True)
        hn = (x * lax.rsqrt(ms + EPS) * nw_ref[...]).astype(BF16)
        kv_ref[...] = jnp.dot(hn, wbf_ref[...], preferred_element_type=F32).astype(BF16)

    scale = MEM_HEAD_DIM ** -0.5
    for h in range(MEM_HEADS):
        hs = slice(h * MEM_HEAD_DIM, (h + 1) * MEM_HEAD_DIM)
        vs = slice(width + h * MEM_HEAD_DIM, width + (h + 1) * MEM_HEAD_DIM)
        s = lax.dot_general(q_ref[:, hs], kv_ref[:, hs], NT, preferred_element_type=F32) * scale
        p = jnp.exp(s - jnp.max(s, axis=-1, keepdims=True))
        l = jnp.sum(p, axis=-1, keepdims=True)
        o = jnp.dot(p.astype(BF16), kv_ref[:, vs], preferred_element_type=F32) / l
        o_ref[:, hs] = (_silu(g_ref[:, hs].astype(F32)) * o).astype(o_ref.dtype)


def _memattn(z, mem2d, norm_w, w_kv, batch, seq, mem_tokens):
    tl = MEM_TL
    d = mem2d.shape[1]
    width = MEM_HEADS * MEM_HEAD_DIM
    wblk = width // LANE
    nl = seq // tl
    return pl.pallas_call(
        _memattn_kernel,
        grid=(batch, nl),
        in_specs=[pl.BlockSpec((tl, width), lambda b, i: (b * nl + i, CB_MQ // wblk)),
                  pl.BlockSpec((tl, width), lambda b, i: (b * nl + i, CB_MG // wblk)),
                  pl.BlockSpec((mem_tokens, d), lambda b, i: (b, 0)),
                  pl.BlockSpec((1, d), lambda b, i: (0, 0)),
                  pl.BlockSpec((d, 2 * width), lambda b, i: (0, 0), pipeline_mode=SINGLE_BUFFER)],
        out_specs=pl.BlockSpec((tl, width), lambda b, i: (b * nl + i, 0)),
        out_shape=jax.ShapeDtypeStruct((batch * seq, width), BF16),
        scratch_shapes=[pltpu.VMEM((d, 2 * width), BF16),
                        pltpu.VMEM((mem_tokens, 2 * width), BF16)],
        compiler_params=pltpu.CompilerParams(
            dimension_semantics=("arbitrary", "arbitrary"), vmem_limit_bytes=VMEM_LIMIT),
        name="mem_attn",
    )(z, z, mem2d, norm_w.reshape(1, d), w_kv)


def _merge_kernel(yr_ref, yd_ref, ym_ref, wr_ref, wd_ref, wm_ref, gr_ref, gd_ref, gm_ref, o_ref,
                  wbf_ref):
    @pl.when(pl.program_id(1) == 0)
    def _():
        for b, w_ref in enumerate((wr_ref, wd_ref, wm_ref)):
            wbf_ref[b] = w_ref[...].astype(BF16)

    def branch(b, y_ref, g_ref):
        up = jnp.dot(y_ref[...], wbf_ref[b], preferred_element_type=F32)
        return _sigmoid(g_ref[...].astype(F32)) * up

    merged = branch(0, yr_ref, gr_ref) + branch(1, yd_ref, gd_ref) + branch(2, ym_ref, gm_ref)
    o_ref[...] = merged.astype(o_ref.dtype)


def _merge(y_ret, y_dsa, y_mem, w_ret, w_dsa, w_mem, z):
    tm, tn = MERGE_TM, MERGE_TN
    m, kdim = y_ret.shape
    n = w_ret.shape[1]
    gblk = tn // LANE
    yspec = pl.BlockSpec((tm, kdim), lambda j, i: (i, 0))
    wspec = pl.BlockSpec((kdim, tn), lambda j, i: (0, j))
    gspec = lambda off: pl.BlockSpec((tm, tn), lambda j, i: (i, off // gblk + j))
    return pl.pallas_call(
        _merge_kernel,
        grid=(n // tn, m // tm),
        in_specs=[yspec, yspec, yspec, wspec, wspec, wspec,
                  gspec(CB_GRET), gspec(CB_GDSA), gspec(CB_GMEM)],
        out_specs=pl.BlockSpec((tm, tn), lambda j, i: (i, j)),
        out_shape=jax.ShapeDtypeStruct((m, n), BF16),
        scratch_shapes=[pltpu.VMEM((N_BRANCHES, kdim, tn), BF16)],
        compiler_params=pltpu.CompilerParams(
            dimension_semantics=("arbitrary", "arbitrary"), vmem_limit_bytes=VMEM_LIMIT),
        name="merge",
    )(y_ret, y_dsa, y_mem, w_ret, w_dsa, w_mem, z, z, z)


def _out_kernel(m_ref, w_ref, x_ref, nw_ref, o_ref, wbf_ref):
    @pl.when(pl.program_id(0) == 0)
    def _():
        for r in range(0, w_ref.shape[0], PROJ_PREP_ROWS):
            wbf_ref[r:r + PROJ_PREP_ROWS, :] = w_ref[r:r + PROJ_PREP_ROWS, :].astype(BF16)

    out = jnp.dot(m_ref[...], wbf_ref[...], preferred_element_type=F32)
    ms = jnp.mean(out * out, axis=-1, keepdims=True)
    o_ref[...] = x_ref[...] + out * lax.rsqrt(ms + EPS) * nw_ref[...]


def _out_proj(merged, w_o, x2d, post_w):
    tm = OUT_TM
    m, d = x2d.shape
    return pl.pallas_call(
        _out_kernel,
        grid=(m // tm,),
        in_specs=[pl.BlockSpec((tm, d), lambda i: (i, 0)),
                  pl.BlockSpec((d, d), lambda i: (0, 0), pipeline_mode=SINGLE_BUFFER),
                  pl.BlockSpec((tm, d), lambda i: (i, 0)),
                  pl.BlockSpec((1, d), lambda i: (0, 0))],
        out_specs=pl.BlockSpec((tm, d), lambda i: (i, 0)),
        out_shape=jax.ShapeDtypeStruct((m, d), F32),
        scratch_shapes=[pltpu.VMEM((d, d), BF16)],
        compiler_params=pltpu.CompilerParams(
            dimension_semantics=("arbitrary",), vmem_limit_bytes=VMEM_LIMIT),
        name="out_proj_postnorm",
    )(merged, w_o, x2d, post_w.reshape(1, d))


def _layer(x, mem, pre_norm_w, w_in, ret_gn_w, mem_norm_w, w_mem_kv,
           w_up_ret, w_up_dsa, w_up_mem, w_o, post_norm_w):
    batch, seq, d = x.shape
    mem_tokens = mem.shape[1]
    top_k = min(IDX_TOPK_MAX, seq // 4)
    assert w_in.shape == (d, Z_COLS + IDX_TAIL_COLS), w_in.shape
    assert seq % max(DSA_T, RET_C, MEM_TL) == 0 and seq < 2**15, seq
    assert (batch * seq) % max(MERGE_TM, OUT_TM) == 0 and top_k <= DSA_T, (batch, seq)
    x2d = x.reshape(batch * seq, d)

    w_in_t = w_in.T
    w_ikw = jnp.pad(w_in_t[IDX_TAIL_START:IDX_TAIL_START + IDX_TAIL_COLS],
                    ((0, LANE - IDX_TAIL_COLS), (0, 0))).astype(BF16)
    h, ikw = _prenorm(x2d, pre_norm_w, w_ikw)
    z = _project(h, w_in_t)

    log_g = jnp.log1p(-(2.0 ** (-5.0 - jnp.arange(RET_HEADS, dtype=F32))))
    y_ret = _retention(z, log_g, ret_gn_w, batch, seq)
    y_dsa = _dsa(z, ikw, batch, seq, top_k)
    y_mem = _memattn(z, mem.reshape(batch * mem_tokens, d), mem_norm_w, w_mem_kv, batch, seq,
                     mem_tokens)

    merged = _merge(y_ret, y_dsa, y_mem, w_up_ret, w_up_dsa, w_up_mem, z)
    out = _out_proj(merged, w_o, x2d, post_norm_w)
    return out.reshape(batch, seq, d)


def kernel(x, mem, pre_norm_w, w_in, ret_gn_w, mem_norm_w, w_mem_kv, w_up_ret, w_up_dsa,
           w_up_mem, w_o, post_norm_w):
    for layer in range(w_in.shape[0]):
        x = _layer(x, mem, pre_norm_w[layer], w_in[layer], ret_gn_w[layer], mem_norm_w[layer],
                   w_mem_kv[layer], w_up_ret[layer], w_up_dsa[layer], w_up_mem[layer],
                   w_o[layer], post_norm_w[layer])
    return x
```

```python
import functools

import jax
import jax.numpy as jnp
from jax import lax
from jax.experimental import pallas as pl
from jax.experimental.pallas import tpu as pltpu

F32 = jnp.float32
BF16 = jnp.bfloat16
I32 = jnp.int32

LANE = 128
SUBLANES = 8
N_BRANCHES = 3
RET_HEADS = 8
HEAD_DIM = 128
DSA_HEADS = 8
IDX_HEADS = 16
IDX_DIM = 64
IDX_TOPK_MAX = 256
MEM_HEADS = 4
MEM_HEAD_DIM = 256
NEG_INF = -1e30
EPS = 1e-6
INT_MIN = -2**31

CB_RQ, CB_RK, CB_RV, CB_RG = 0, 8, 16, 24
CB_AQ, CB_AK, CB_AV, CB_AG = 32, 40, 48, 56
CB_IQ = 64
CB_MQ, CB_MG = 72, 80
CB_GRET, CB_GDSA, CB_GMEM = 88, 104, 120
N_COL_BLOCKS = 136
Z_COLS = N_COL_BLOCKS * LANE
IDX_TAIL_START = (CB_IQ + IDX_HEADS * IDX_DIM // LANE) * LANE
IDX_TAIL_COLS = IDX_DIM + IDX_HEADS

PROJ_TM = 2048
PROJ_TN = 1024
PROJ_ALIGNED_TILES = IDX_TAIL_START // PROJ_TN
PROJ_ROWS = 512
PROJ_PREP_ROWS = 256
PRENORM_TM = 1024
MEM_TL = 1024
MERGE_TM, MERGE_TN = 512, 1024
OUT_TM = 512
VMEM_LIMIT = 56 * 1024 * 1024
SINGLE_BUFFER = pl.Buffered(1)

NT = (((1,), (1,)), ((), ()))
TN = (((0,), (0,)), ((), ()))


def _sigmoid(v):
    return 1.0 / (1.0 + jnp.exp(-v))


def _silu(v):
    return v * _sigmoid(v)


def _prenorm_kernel(x_ref, w_ref, wikw_ref, h_ref, ikw_ref):
    x = x_ref[...]
    ms = jnp.mean(x * x, axis=-1, keepdims=True)
    h = (x * lax.rsqrt(ms + EPS) * w_ref[...]).astype(h_ref.dtype)
    h_ref[...] = h
    ikw_ref[...] = lax.dot_general(h, wikw_ref[...], NT,
                                   preferred_element_type=F32).astype(ikw_ref.dtype)


def _prenorm(x2d, w, w_ikw):
    m, d = x2d.shape
    tm = min(PRENORM_TM, m)
    return pl.pallas_call(
        _prenorm_kernel,
        grid=(m // tm,),
        in_specs=[pl.BlockSpec((tm, d), lambda i: (i, 0)),
                  pl.BlockSpec((1, d), lambda i: (0, 0)),
                  pl.BlockSpec((LANE, d), lambda i: (0, 0))],
        out_specs=[pl.BlockSpec((tm, d), lambda i: (i, 0)),
                   pl.BlockSpec((tm, LANE), lambda i: (i, 0))],
        out_shape=[jax.ShapeDtypeStruct((m, d), BF16),
                   jax.ShapeDtypeStruct((m, LANE), BF16)],
        compiler_params=pltpu.CompilerParams(dimension_semantics=("arbitrary",)),
        name="prenorm_idx",
    )(x2d, w.reshape(1, d), w_ikw)


def _proj_kernel(h_ref, w_ref, z_ref, wbf_ref):
    @pl.when(pl.program_id(1) == 0)
    def _():
        for r in range(0, w_ref.shape[0], PROJ_PREP_ROWS):
            wbf_ref[r:r + PROJ_PREP_ROWS, :] = w_ref[r:r + PROJ_PREP_ROWS, :].astype(BF16)

    w = wbf_ref[...]
    for r in range(0, h_ref.shape[0], PROJ_ROWS):
        z_ref[r:r + PROJ_ROWS, :] = lax.dot_general(
            h_ref[r:r + PROJ_ROWS, :], w, NT, preferred_element_type=F32).astype(z_ref.dtype)


def _project(h, w_in_t):
    m, d = h.shape
    tm = min(PROJ_TM, m)

    def w_rows(j, i):
        skip = jnp.where(j >= PROJ_ALIGNED_TILES, IDX_TAIL_COLS // SUBLANES, 0)
        return ((j * (PROJ_TN // SUBLANES) + skip) * SUBLANES, 0)

    return pl.pallas_call(
        _proj_kernel,
        grid=(Z_COLS // PROJ_TN, m // tm),
        in_specs=[pl.BlockSpec((tm, d), lambda j, i: (i, 0)),
                  pl.BlockSpec((pl.Element(PROJ_TN), pl.Element(d)), w_rows)],
        out_specs=pl.BlockSpec((tm, PROJ_TN), lambda j, i: (i, j)),
        out_shape=jax.ShapeDtypeStruct((m, Z_COLS), BF16),
        scratch_shapes=[pltpu.VMEM((PROJ_TN, d), BF16)],
        compiler_params=pltpu.CompilerParams(
            dimension_semantics=("arbitrary", "arbitrary"), vmem_limit_bytes=VMEM_LIMIT),
        name="in_proj",
    )(h, w_in_t)


RET_C = 256
RET_HEADS_PER_STEP = 2


def _retention_kernel(lg_ref, q_ref, k_ref, v_ref, g_ref, gw_ref, o_ref):
    c = RET_C
    seq = q_ref.shape[0]
    scale = HEAD_DIM ** -0.5
    ri = lax.broadcasted_iota(I32, (c, c), 0)
    ci = lax.broadcasted_iota(I32, (c, c), 1)
    diff = (ri - ci).astype(F32)
    idx = lax.broadcasted_iota(I32, (c, 1), 0).astype(F32)

    for hh in range(RET_HEADS_PER_STEP):
        hs = slice(hh * HEAD_DIM, (hh + 1) * HEAD_DIM)
        lg = lg_ref[pl.program_id(1) * RET_HEADS_PER_STEP + hh]
        decay = jnp.where(diff >= 0, jnp.exp(lg * jnp.maximum(diff, 0.0)), 0.0) * scale
        q_dec = jnp.exp(lg * (idx + 1.0))
        k_dec = jnp.exp(lg * (c - 1.0 - idx)) * scale
        chunk_dec = jnp.exp(lg * jnp.full((1, HEAD_DIM), float(c), F32))
        gw = gw_ref[:, hs]
        state = jnp.zeros((HEAD_DIM, HEAD_DIM), F32)
        for i in range(seq // c):
            r = i * c
            qi = q_ref[pl.ds(r, c), hs]
            ki = k_ref[pl.ds(r, c), hs]
            vi = v_ref[pl.ds(r, c), hs]
            inner = lax.dot_general(qi, ki, NT, preferred_element_type=F32) * decay
            o = (jnp.dot(inner.astype(BF16), vi, preferred_element_type=F32)
                 + jnp.dot(qi, state.astype(BF16), preferred_element_type=F32) * q_dec)
            vs = (vi.astype(F32) * k_dec).astype(BF16)
            state = state * chunk_dec + lax.dot_general(ki, vs, TN, preferred_element_type=F32)
            mu = jnp.mean(o, axis=-1, keepdims=True)
            oc = o - mu
            var = jnp.mean(oc * oc, axis=-1, keepdims=True)
            on = oc * lax.rsqrt(var + 1e-5) * gw
            g = g_ref[pl.ds(r, c), hs].astype(F32)
            o_ref[pl.ds(r, c), hs] = (_silu(g) * on).astype(o_ref.dtype)


def _retention(z, log_g, gn_w, batch, seq):
    hp = RET_HEADS_PER_STEP
    width = hp * HEAD_DIM
    blk = lambda off: pl.BlockSpec((seq, width), lambda b, h: (b, off // hp + h))
    return pl.pallas_call(
        _retention_kernel,
        grid=(batch, RET_HEADS // hp),
        in_specs=[pl.BlockSpec(memory_space=pltpu.SMEM),
                  blk(CB_RQ), blk(CB_RK), blk(CB_RV), blk(CB_RG),
                  pl.BlockSpec((1, width), lambda b, h: (0, h))],
        out_specs=pl.BlockSpec((seq, width), lambda b, h: (b, h)),
        out_shape=jax.ShapeDtypeStruct((batch * seq, RET_HEADS * HEAD_DIM), BF16),
        compiler_params=pltpu.CompilerParams(
            dimension_semantics=("arbitrary", "arbitrary"), vmem_limit_bytes=VMEM_LIMIT),
        name="retention",
    )(log_g, z, z, z, z, gn_w.reshape(1, -1))


DSA_T = 256
I16 = jnp.int16
PACKED_ROWS = 2 * SUBLANES
VT_ROWS = HEAD_DIM + PACKED_ROWS


def _fold_rows(x, op, rows=SUBLANES):
    while x.shape[0] > rows:
        half = x.shape[0] // 2
        x = op(x[:half], x[half:])
    return x


def _for_tiles_grouped(nk, step):
    def quad(i, carry):
        for u in range(4):
            step(4 * i + u)
        return carry

    lax.fori_loop(0, nk >> 2, quad, 0)
    done = (nk >> 2) << 2

    @pl.when((nk & 2) == 2)
    def _():
        step(done)
        step(done + 1)

    @pl.when((nk & 1) == 1)
    def _():
        step(nk - 1)


ORD_NEG_INF =-2**31 + 0x7FFFFF


def _flip(v):
    return v ^ ((v >> 31) & 0x7FFFFFFF)


def _ordinal_to_f32(o):
    return lax.bitcast_convert_type(_flip(jnp.maximum(o, ORD_NEG_INF)), F32)


def _prefix_to_bf16_bits(p):
    return _flip(jnp.maximum(p, ORD_NEG_INF >> 16) << 16) & -65536


def _dsa_kernel(iq_ref, ikw_ref, aq_ref, ak_ref, av_ref, ag_ref, o_ref,
                ika_ref, ikb_ref, vt_ref, wt_ref, score_ref, score16_ref, bias_ref, logit_ref,
                m_ref, acc_ref, *, top_k):
    t = DSA_T
    qb = pl.program_id(1)
    nk = qb + 1
    q0 = pl.multiple_of(qb * t, t)
    nq = vt_ref.shape[0]

    @pl.when(qb == 0)
    def _():
        ikw = ikw_ref[...].astype(F32)
        lane = lax.broadcasted_iota(I32, ikw.shape, 1)
        a = jnp.where(lane < IDX_DIM, ikw, 0.0)
        ika_ref[...] = a.astype(BF16)
        ikb_ref[...] = pltpu.roll(a, IDX_DIM, axis=1).astype(BF16)

        def vt_tile(kt, carry):
            k0 = pl.multiple_of(kt * t, t)
            for h in range(DSA_HEADS):
                hs = slice(h * HEAD_DIM, (h + 1) * HEAD_DIM)
                vt_ref[kt, h, :HEAD_DIM, :] = av_ref[pl.ds(k0, t), hs].astype(F32).T.astype(BF16)
                vt_ref[kt, h, HEAD_DIM:, :] = jnp.ones((VT_ROWS - HEAD_DIM, t), BF16)
            return carry

        lax.fori_loop(0, nq, vt_tile, 0)

    wt_ref[...] = ikw_ref[pl.ds(q0, t), :].astype(F32).T

    key_row = lax.broadcasted_iota(I32, (t, t), 0)
    qry_col = lax.broadcasted_iota(I32, (t, t), 1)

    def score_tile(kt):
        k0 = pl.multiple_of(kt * t, t)
        ka = ika_ref[pl.ds(k0, t), :]
        kb = ikb_ref[pl.ds(k0, t), :]
        acc = jnp.zeros((t, t), F32)
        for j in range(IDX_HEADS // 2):
            qj = iq_ref[:, j * LANE:(j + 1) * LANE]
            for half, kx in enumerate((ka, kb)):
                hh = IDX_DIM + 2 * j + half
                s = lax.dot_general(kx, qj, NT, preferred_element_type=F32)
                acc = acc + jnp.maximum(s, 0.0) * wt_ref[hh:hh + 1, :]
        causal = (key_row + k0) <= (qry_col + q0)
        score = jnp.where(causal, acc, -jnp.inf)
        score_ref[kt] = score
        score16_ref[kt] = score.astype(BF16)

    _for_tiles_grouped(nk, score_tile)

    one16, zero16 = jnp.ones((), I16), jnp.zeros((), I16)

    def prefix_step(i, pfx):
        cand_pfx = pfx + lax.shift_left(jnp.int32(1), 15 - i)
        cand = lax.bitcast_convert_type(_prefix_to_bf16_bits(cand_pfx), F32)
        cand16 = jnp.broadcast_to(cand, (PACKED_ROWS, t)).astype(BF16)[:1]

        def count_tile(kt, cnt):
            ge = jnp.where(score16_ref[kt] >= cand16, one16, zero16)
            return cnt + _fold_rows(ge, jnp.add, PACKED_ROWS)

        cnt = lax.fori_loop(0, nk, count_tile, jnp.zeros((PACKED_ROWS, t), I16))
        total = jnp.sum(cnt.astype(I32), axis=0, keepdims=True)
        return jnp.where(total >= top_k, cand_pfx, pfx)

    pfx = lax.fori_loop(0, 16, prefix_step, jnp.full((1, t), -2**15, I32))

    def bisect_step(i, bounds):
        lo, hi = bounds
        mid = lo + ((hi - lo) >> 1)
        cand = _ordinal_to_f32(mid)

        def count_tile(kt, cnt):
            return cnt + _fold_rows(jnp.where(score_ref[kt] >= cand, 1, 0), jnp.add)

        cnt = lax.fori_loop(0, nk, count_tile, jnp.zeros((SUBLANES, t), I32))
        enough = jnp.sum(cnt, axis=0, keepdims=True) >= top_k
        return jnp.where(enough, mid, lo), jnp.where(enough, hi, mid)

    lo, _ = lax.fori_loop(0, 17, bisect_step, (_flip(_prefix_to_bf16_bits(pfx - 1)),
                                               _flip(_prefix_to_bf16_bits(pfx + 1))))
    thr = _ordinal_to_f32(lo)

    def bias_tile(kt, kept):
        k0 = pl.multiple_of(kt * t, t)
        keep = jnp.logical_and(score_ref[kt] >= thr, (key_row + k0) <= (qry_col + q0))
        bias_ref[kt] = jnp.where(keep, 0.0, NEG_INF)
        return kept + _fold_rows(jnp.where(keep, 1, 0), jnp.add)

    kept = lax.fori_loop(0, nk, bias_tile, jnp.zeros((SUBLANES, t), I32))
    kept = jnp.sum(kept, axis=0, keepdims=True)

    @pl.when(jnp.max(kept) > top_k)
    def _():
        strictly_lower = (qry_col < key_row).astype(BF16)

        def tie_tile(kt, carry):
            above, tied_before = carry
            k0 = pl.multiple_of(kt * t, t)
            causal = (key_row + k0) <= (qry_col + q0)
            score = score_ref[kt]
            gt = jnp.logical_and(score > thr, causal)
            eq = jnp.logical_and(score == thr, causal)
            eq01 = jnp.where(eq, 1.0, 0.0)
            tied_here = tied_before + jnp.dot(strictly_lower, eq01.astype(BF16),
                                              preferred_element_type=F32)
            logit_ref[0, kt] = jnp.where(eq, tied_here, -1.0)
            above = above + _fold_rows(jnp.where(gt, 1.0, 0.0), jnp.add)
            return above, tied_before + jnp.sum(eq01, axis=0, keepdims=True)

        above, _ = lax.fori_loop(0, nk, tie_tile,
                                 (jnp.zeros((SUBLANES, t), F32), jnp.zeros((1, t), F32)))
        room = top_k - jnp.sum(above, axis=0, keepdims=True)

        def rebias_tile(kt, carry):
            k0 = pl.multiple_of(kt * t, t)
            gt = jnp.logical_and(score_ref[kt] > thr, (key_row + k0) <= (qry_col + q0))
            rank = logit_ref[0, kt]
            keep = jnp.logical_or(gt, jnp.logical_and(rank >= 0.0, rank < room))
            bias_ref[kt] = jnp.where(keep, 0.0, NEG_INF)
            return carry

        lax.fori_loop(0, nk, rebias_tile, 0)

    m_ref[...] = jnp.full(m_ref.shape, NEG_INF, F32)
    acc_ref[...] = jnp.zeros(acc_ref.shape, F32)
    log2e = 1.4426950408889634
    scale = (HEAD_DIM ** -0.5) * log2e

    def logit_tile(kt):
        k0 = pl.multiple_of(kt * t, t)
        bias = bias_ref[kt]
        kpos = (key_row + (k0 - q0)).astype(F32)
        for h in range(DSA_HEADS):
            slope = (2.0 ** (-8.0 * (h + 1) / DSA_HEADS)) * log2e
            hs = slice(h * HEAD_DIM, (h + 1) * HEAD_DIM)
            s = lax.dot_general(ak_ref[pl.ds(k0, t), hs], aq_ref[:, hs], NT,
                                preferred_element_type=F32)
            lg = s * scale + (kpos * slope + bias)
            logit_ref[h, kt] = lg
            m_ref[h] = jnp.maximum(m_ref[h], _fold_rows(lg, jnp.maximum))

    _for_tiles_grouped(nk, logit_tile)

    for h in range(DSA_HEADS):
        m_ref[h] = jnp.broadcast_to(jnp.max(m_ref[h], axis=0, keepdims=True), (SUBLANES, t))

    def pv_tile(kt):
        for h in range(DSA_HEADS):
            p = jnp.exp2(logit_ref[h, kt] - m_ref[h][:1])
            acc_ref[h] = acc_ref[h] + jnp.dot(vt_ref[kt, h], p.astype(BF16),
                                              preferred_element_type=F32)

    _for_tiles_grouped(nk, pv_tile)

    for h in range(DSA_HEADS):
        hs = slice(h * HEAD_DIM, (h + 1) * HEAD_DIM)
        acc = acc_ref[h]
        o = (acc[:HEAD_DIM] / acc[HEAD_DIM:HEAD_DIM + 1]).T
        o_ref[:, hs] = (_silu(ag_ref[:, hs].astype(F32)) * o).astype(o_ref.dtype)


def _dsa(z, ikw, batch, seq, top_k):
    t = DSA_T
    nq = seq // t
    width = DSA_HEADS * HEAD_DIM
    wblk = width // LANE
    qspec = lambda off: pl.BlockSpec((t, width), lambda b, i: (b * nq + i, off // wblk))
    kvspec = lambda off: pl.BlockSpec((seq, width), lambda b, i: (b, off // wblk))
    return pl.pallas_call(
        functools.partial(_dsa_kernel, top_k=top_k),
        grid=(batch, nq),
        in_specs=[qspec(CB_IQ),
                  pl.BlockSpec((seq, LANE), lambda b, i: (b, 0)),
                  qspec(CB_AQ), kvspec(CB_AK), kvspec(CB_AV), qspec(CB_AG)],
        out_specs=pl.BlockSpec((t, width), lambda b, i: (b * nq + i, 0)),
        out_shape=jax.ShapeDtypeStruct((batch * seq, width), BF16),
        scratch_shapes=[pltpu.VMEM((seq, LANE), BF16),
                        pltpu.VMEM((seq, LANE), BF16),
                        pltpu.VMEM((nq, DSA_HEADS, VT_ROWS, t), BF16),
                        pltpu.VMEM((LANE, t), F32),
                        pltpu.VMEM((nq, t, t), F32),
                        pltpu.VMEM((nq, t, t), BF16),
                        pltpu.VMEM((nq, t, t), F32),
                        pltpu.VMEM((DSA_HEADS, nq, t, t), F32),
                        pltpu.VMEM((DSA_HEADS, SUBLANES, t), F32),
                        pltpu.VMEM((DSA_HEADS, VT_ROWS, t), F32)],
        compiler_params=pltpu.CompilerParams(
            dimension_semantics=("arbitrary", "arbitrary"), vmem_limit_bytes=VMEM_LIMIT),
        name="dsa",
    )(z, ikw, z, z, z, z)


def _memattn_kernel(q_ref, g_ref, mem_ref, nw_ref, w_ref, o_ref, wbf_ref, kv_ref):
    b = pl.program_id(0)
    i = pl.program_id(1)
    width = MEM_HEADS * MEM_HEAD_DIM

    @pl.when(jnp.logical_and(b == 0, i == 0))
    def _():
        for r in range(0, w_ref.shape[0], PROJ_PREP_ROWS):
            wbf_ref[r:r + PROJ_PREP_ROWS, :] = w_ref[r:r + PROJ_PREP_ROWS, :].astype(BF16)

    @pl.when(i == 0)
    def _():
        x = mem_ref[...]
        ms = jnp.mean(x * x, axis=-1, keepdims=True)
        hn = (x * lax.rsqrt(ms + EPS) * nw_ref[...]).astype(BF16)
        kv_ref[...] = jnp.dot(hn, wbf_ref[...], preferred_element_type=F32).astype(BF16)

    scale = MEM_HEAD_DIM ** -0.5
    for h in range(MEM_HEADS):
        hs = slice(h * MEM_HEAD_DIM, (h + 1) * MEM_HEAD_DIM)
        vs = slice(width + h * MEM_HEAD_DIM, width + (h + 1) * MEM_HEAD_DIM)
        s = lax.dot_general(q_ref[:, hs], kv_ref[:, hs], NT, preferred_element_type=F32) * scale
        p = jnp.exp(s - jnp.max(s, axis=-1, keepdims=True))
        l = jnp.sum(p, axis=-1, keepdims=True)
        o = jnp.dot(p.astype(BF16), kv_ref[:, vs], preferred_element_type=F32) / l
        o_ref[:, hs] = (_silu(g_ref[:, hs].astype(F32)) * o).astype(o_ref.dtype)


def _memattn(z, mem2d, norm_w, w_kv, batch, seq, mem_tokens):
    tl = MEM_TL
    d = mem2d.shape[1]
    width = MEM_HEADS * MEM_HEAD_DIM
    wblk = width // LANE
    nl = seq // tl
    return pl.pallas_call(
        _memattn_kernel,
        grid=(batch, nl),
        in_specs=[pl.BlockSpec((tl, width), lambda b, i: (b * nl + i, CB_MQ // wblk)),
                  pl.BlockSpec((tl, width), lambda b, i: (b * nl + i, CB_MG // wblk)),
                  pl.BlockSpec((mem_tokens, d), lambda b, i: (b, 0)),
                  pl.BlockSpec((1, d), lambda b, i: (0, 0)),
                  pl.BlockSpec((d, 2 * width), lambda b, i: (0, 0), pipeline_mode=SINGLE_BUFFER)],
        out_specs=pl.BlockSpec((tl, width), lambda b, i: (b * nl + i, 0)),
        out_shape=jax.ShapeDtypeStruct((batch * seq, width), BF16),
        scratch_shapes=[pltpu.VMEM((d, 2 * width), BF16),
                        pltpu.VMEM((mem_tokens, 2 * width), BF16)],
        compiler_params=pltpu.CompilerParams(
            dimension_semantics=("arbitrary", "arbitrary"), vmem_limit_bytes=VMEM_LIMIT),
        name="mem_attn",
    )(z, z, mem2d, norm_w.reshape(1, d), w_kv)


def _merge_kernel(yr_ref, yd_ref, ym_ref, wr_ref, wd_ref, wm_ref, gr_ref, gd_ref, gm_ref, o_ref,
                  wbf_ref):
    @pl.when(pl.program_id(1) == 0)
    def _():
        for b, w_ref in enumerate((wr_ref, wd_ref, wm_ref)):
            wbf_ref[b] = w_ref[...].astype(BF16)

    def branch(b, y_ref, g_ref):
        up = jnp.dot(y_ref[...], wbf_ref[b], preferred_element_type=F32)
        return _sigmoid(g_ref[...].astype(F32)) * up

    merged = branch(0, yr_ref, gr_ref) + branch(1, yd_ref, gd_ref) + branch(2, ym_ref, gm_ref)
    o_ref[...] = merged.astype(o_ref.dtype)


def _merge(y_ret, y_dsa, y_mem, w_ret, w_dsa, w_mem, z):
    tm, tn = MERGE_TM, MERGE_TN
    m, kdim = y_ret.shape
    n = w_ret.shape[1]
    gblk = tn // LANE
    yspec = pl.BlockSpec((tm, kdim), lambda j, i: (i, 0))
    wspec = pl.BlockSpec((kdim, tn), lambda j, i: (0, j))
    gspec = lambda off: pl.BlockSpec((tm, tn), lambda j, i: (i, off // gblk + j))
    return pl.pallas_call(
        _merge_kernel,
        grid=(n // tn, m // tm),
        in_specs=[yspec, yspec, yspec, wspec, wspec, wspec,
                  gspec(CB_GRET), gspec(CB_GDSA), gspec(CB_GMEM)],
        out_specs=pl.BlockSpec((tm, tn), lambda j, i: (i, j)),
        out_shape=jax.ShapeDtypeStruct((m, n), BF16),
        scratch_shapes=[pltpu.VMEM((N_BRANCHES, kdim, tn), BF16)],
        compiler_params=pltpu.CompilerParams(
            dimension_semantics=("arbitrary", "arbitrary"), vmem_limit_bytes=VMEM_LIMIT),
        name="merge",
    )(y_ret, y_dsa, y_mem, w_ret, w_dsa, w_mem, z, z, z)


def _out_kernel(m_ref, w_ref, x_ref, nw_ref, o_ref, wbf_ref):
    @pl.when(pl.program_id(0) == 0)
    def _():
        for r in range(0, w_ref.shape[0], PROJ_PREP_ROWS):
            wbf_ref[r:r + PROJ_PREP_ROWS, :] = w_ref[r:r + PROJ_PREP_ROWS, :].astype(BF16)

    out = jnp.dot(m_ref[...], wbf_ref[...], preferred_element_type=F32)
    ms = jnp.mean(out * out, axis=-1, keepdims=True)
    o_ref[...] = x_ref[...] + out * lax.rsqrt(ms + EPS) * nw_ref[...]


def _out_proj(merged, w_o, x2d, post_w):
    tm = OUT_TM
    m, d = x2d.shape
    return pl.pallas_call(
        _out_kernel,
        grid=(m // tm,),
        in_specs=[pl.BlockSpec((tm, d), lambda i: (i, 0)),
                  pl.BlockSpec((d, d), lambda i: (0, 0), pipeline_mode=SINGLE_BUFFER),
                  pl.BlockSpec((tm, d), lambda i: (i, 0)),
                  pl.BlockSpec((1, d), lambda i: (0, 0))],
        out_specs=pl.BlockSpec((tm, d), lambda i: (i, 0)),
        out_shape=jax.ShapeDtypeStruct((m, d), F32),
        scratch_shapes=[pltpu.VMEM((d, d), BF16)],
        compiler_params=pltpu.CompilerParams(
            dimension_semantics=("arbitrary",), vmem_limit_bytes=VMEM_LIMIT),
        name="out_proj_postnorm",
    )(merged, w_o, x2d, post_w.reshape(1, d))


def _layer(x, mem, pre_norm_w, w_in, ret_gn_w, mem_norm_w, w_mem_kv,
           w_up_ret, w_up_dsa, w_up_mem, w_o, post_norm_w):
    batch, seq, d = x.shape
    mem_tokens = mem.shape[1]
    top_k = min(IDX_TOPK_MAX, seq // 4)
    assert w_in.shape == (d, Z_COLS + IDX_TAIL_COLS), w_in.shape
    assert seq % max(DSA_T, RET_C, MEM_TL) == 0 and seq < 2**15, seq
    assert (batch * seq) % max(MERGE_TM, OUT_TM) == 0 and top_k <= DSA_T, (batch, seq)
    x2d = x.reshape(batch * seq, d)

    w_in_t = w_in.T
    w_ikw = jnp.pad(w_in_t[IDX_TAIL_START:IDX_TAIL_START + IDX_TAIL_COLS],
                    ((0, LANE - IDX_TAIL_COLS), (0, 0))).astype(BF16)
    h, ikw = _prenorm(x2d, pre_norm_w, w_ikw)
    z = _project(h, w_in_t)

    log_g = jnp.log1p(-(2.0 ** (-5.0 - jnp.arange(RET_HEADS, dtype=F32))))
    y_ret = _retention(z, log_g, ret_gn_w, batch, seq)
    y_dsa = _dsa(z, ikw, batch, seq, top_k)
    y_mem = _memattn(z, mem.reshape(batch * mem_tokens, d), mem_norm_w, w_mem_kv, batch, seq,
                     mem_tokens)

    merged = _merge(y_ret, y_dsa, y_mem, w_up_ret, w_up_dsa, w_up_mem, z)
    out = _out_proj(merged, w_o, x2d, post_norm_w)
    return out.reshape(batch, seq, d)


def kernel(x, mem, pre_norm_w, w_in, ret_gn_w, mem_norm_w, w_mem_kv, w_up_ret, w_up_dsa,
           w_up_mem, w_o, post_norm_w):
    for layer in range(w_in.shape[0]):
        x = _layer(x, mem, pre_norm_w[layer], w_in[layer], ret_gn_w[layer], mem_norm_w[layer],
                   w_mem_kv[layer], w_up_ret[layer], w_up_dsa[layer], w_up_mem[layer],
                   w_o[layer], post_norm_w[layer])
    return x
```

```python
import functools

import jax
import jax.numpy as jnp
from jax import lax
from jax.experimental import pallas as pl
from jax.experimental.pallas import tpu as pltpu

F32 = jnp.float32
BF16 = jnp.bfloat16
I32 = jnp.int32

LANE = 128
SUBLANES = 8
N_BRANCHES = 3
RET_HEADS = 8
HEAD_DIM = 128
DSA_HEADS = 8
IDX_HEADS = 16
IDX_DIM = 64
IDX_TOPK_MAX = 256
MEM_HEADS = 4
MEM_HEAD_DIM = 256
NEG_INF = -1e30
EPS = 1e-6
INT_MIN = -2**31

CB_RQ, CB_RK, CB_RV, CB_RG = 0, 8, 16, 24
CB_AQ, CB_AK, CB_AV, CB_AG = 32, 40, 48, 56
CB_IQ = 64
CB_MQ, CB_MG = 72, 80
CB_GRET, CB_GDSA, CB_GMEM = 88, 104, 120
N_COL_BLOCKS = 136
Z_COLS = N_COL_BLOCKS * LANE
IDX_TAIL_START = (CB_IQ + IDX_HEADS * IDX_DIM // LANE) * LANE
IDX_TAIL_COLS = IDX_DIM + IDX_HEADS

PROJ_TM = 2048
PROJ_TN = 1024
PROJ_ALIGNED_TILES = IDX_TAIL_START // PROJ_TN
PROJ_ROWS = 512
PROJ_PREP_ROWS = 256
PRENORM_TM = 1024
MEM_TL = 1024
MERGE_TM, MERGE_TN = 512, 1024
OUT_TM = 512
VMEM_LIMIT = 56 * 1024 * 1024
SINGLE_BUFFER = pl.Buffered(1)

NT = (((1,), (1,)), ((), ()))
TN = (((0,), (0,)), ((), ()))


def _sigmoid(v):
    return 1.0 / (1.0 + jnp.exp(-v))


def _silu(v):
    return v * _sigmoid(v)


def _prenorm_kernel(x_ref, w_ref, wikw_ref, h_ref, ikw_ref):
    x = x_ref[...]
    ms = jnp.mean(x * x, axis=-1, keepdims=True)
    h = (x * lax.rsqrt(ms + EPS) * w_ref[...]).astype(h_ref.dtype)
    h_ref[...] = h
    ikw_ref[...] = lax.dot_general(h, wikw_ref[...], NT,
                                   preferred_element_type=F32).astype(ikw_ref.dtype)


def _prenorm(x2d, w, w_ikw):
    m, d = x2d.shape
    tm = min(PRENORM_TM, m)
    return pl.pallas_call(
        _prenorm_kernel,
        grid=(m // tm,),
        in_specs=[pl.BlockSpec((tm, d), lambda i: (i, 0)),
                  pl.BlockSpec((1, d), lambda i: (0, 0)),
                  pl.BlockSpec((LANE, d), lambda i: (0, 0))],
        out_specs=[pl.BlockSpec((tm, d), lambda i: (i, 0)),
                   pl.BlockSpec((tm, LANE), lambda i: (i, 0))],
        out_shape=[jax.ShapeDtypeStruct((m, d), BF16),
                   jax.ShapeDtypeStruct((m, LANE), BF16)],
        compiler_params=pltpu.CompilerParams(dimension_semantics=("arbitrary",)),
        name="prenorm_idx",
    )(x2d, w.reshape(1, d), w_ikw)


def _proj_kernel(h_ref, w_ref, z_ref, wbf_ref):
    @pl.when(pl.program_id(1) == 0)
    def _():
        for r in range(0, w_ref.shape[0], PROJ_PREP_ROWS):
            wbf_ref[r:r + PROJ_PREP_ROWS, :] = w_ref[r:r + PROJ_PREP_ROWS, :].astype(BF16)

    w = wbf_ref[...]
    for r in range(0, h_ref.shape[0], PROJ_ROWS):
        z_ref[r:r + PROJ_ROWS, :] = lax.dot_general(
            h_ref[r:r + PROJ_ROWS, :], w, NT, preferred_element_type=F32).astype(z_ref.dtype)


def _project(h, w_in_t):
    m, d = h.shape
    tm = min(PROJ_TM, m)

    def w_rows(j, i):
        skip = jnp.where(j >= PROJ_ALIGNED_TILES, IDX_TAIL_COLS // SUBLANES, 0)
        return ((j * (PROJ_TN // SUBLANES) + skip) * SUBLANES, 0)

    return pl.pallas_call(
        _proj_kernel,
        grid=(Z_COLS // PROJ_TN, m // tm),
        in_specs=[pl.BlockSpec((tm, d), lambda j, i: (i, 0)),
                  pl.BlockSpec((pl.Element(PROJ_TN), pl.Element(d)), w_rows)],
        out_specs=pl.BlockSpec((tm, PROJ_TN), lambda j, i: (i, j)),
        out_shape=jax.ShapeDtypeStruct((m, Z_COLS), BF16),
        scratch_shapes=[pltpu.VMEM((PROJ_TN, d), BF16)],
        compiler_params=pltpu.CompilerParams(
            dimension_semantics=("arbitrary", "arbitrary"), vmem_limit_bytes=VMEM_LIMIT),
        name="in_proj",
    )(h, w_in_t)


RET_C = 256
RET_HEADS_PER_STEP = 2


def _retention_kernel(lg_ref, q_ref, k_ref, v_ref, g_ref, gw_ref, o_ref):
    c = RET_C
    seq = q_ref.shape[0]
    scale = HEAD_DIM ** -0.5
    ri = lax.broadcasted_iota(I32, (c, c), 0)
    ci = lax.broadcasted_iota(I32, (c, c), 1)
    diff = (ri - ci).astype(F32)
    idx = lax.broadcasted_iota(I32, (c, 1), 0).astype(F32)

    for hh in range(RET_HEADS_PER_STEP):
        hs = slice(hh * HEAD_DIM, (hh + 1) * HEAD_DIM)
        lg = lg_ref[pl.program_id(1) * RET_HEADS_PER_STEP + hh]
        decay = jnp.where(diff >= 0, jnp.exp(lg * jnp.maximum(diff, 0.0)), 0.0) * scale
        q_dec = jnp.exp(lg * (idx + 1.0))
        k_dec = jnp.exp(lg * (c - 1.0 - idx)) * scale
        chunk_dec = jnp.exp(lg * jnp.full((1, HEAD_DIM), float(c), F32))
        gw = gw_ref[:, hs]
        state = jnp.zeros((HEAD_DIM, HEAD_DIM), F32)
        for i in range(seq // c):
            r = i * c
            qi = q_ref[pl.ds(r, c), hs]
            ki = k_ref[pl.ds(r, c), hs]
            vi = v_ref[pl.ds(r, c), hs]
            inner = lax.dot_general(qi, ki, NT, preferred_element_type=F32) * decay
            o = (jnp.dot(inner.astype(BF16), vi, preferred_element_type=F32)
                 + jnp.dot(qi, state.astype(BF16), preferred_element_type=F32) * q_dec)
            vs = (vi.astype(F32) * k_dec).astype(BF16)
            state = state * chunk_dec + lax.dot_general(ki, vs, TN, preferred_element_type=F32)
            mu = jnp.mean(o, axis=-1, keepdims=True)
            oc = o - mu
            var = jnp.mean(oc * oc, axis=-1, keepdims=True)
            on = oc * lax.rsqrt(var + 1e-5) * gw
            g = g_ref[pl.ds(r, c), hs].astype(F32)
            o_ref[pl.ds(r, c), hs] = (_silu(g) * on).astype(o_ref.dtype)


def _retention(z, log_g, gn_w, batch, seq):
    hp = RET_HEADS_PER_STEP
    width = hp * HEAD_DIM
    blk = lambda off: pl.BlockSpec((seq, width), lambda b, h: (b, off // hp + h))
    return pl.pallas_call(
        _retention_kernel,
        grid=(batch, RET_HEADS // hp),
        in_specs=[pl.BlockSpec(memory_space=pltpu.SMEM),
                  blk(CB_RQ), blk(CB_RK), blk(CB_RV), blk(CB_RG),
                  pl.BlockSpec((1, width), lambda b, h: (0, h))],
        out_specs=pl.BlockSpec((seq, width), lambda b, h: (b, h)),
        out_shape=jax.ShapeDtypeStruct((batch * seq, RET_HEADS * HEAD_DIM), BF16),
        compiler_params=pltpu.CompilerParams(
            dimension_semantics=("arbitrary", "arbitrary"), vmem_limit_bytes=VMEM_LIMIT),
        name="retention",
    )(log_g, z, z, z, z, gn_w.reshape(1, -1))


DSA_T = 256
I16 = jnp.int16
PACKED_ROWS = 2 * SUBLANES
VT_ROWS = HEAD_DIM + PACKED_ROWS


def _fold_rows(x, op, rows=SUBLANES):
    while x.shape[0] > rows:
        half = x.shape[0] // 2
        x = op(x[:half], x[half:])
    return x


def _for_tiles_grouped(nk, step):
    def quad(i, carry):
        for u in range(4):
            step(4 * i + u)
        return carry

    lax.fori_loop(0, nk >> 2, quad, 0)
    done = (nk >> 2) << 2

    @pl.when((nk & 2) == 2)
    def _():
        step(done)
        step(done + 1)

    @pl.when((nk & 1) == 1)
    def _():
        step(nk - 1)


def _fold_tiles_grouped(nk, step, carry):
    def quad(i, c):
        for u in range(4):
            c = step(4 * i + u, c)
        return c

    carry = lax.fori_loop(0, nk >> 2, quad, carry)
    done = (nk >> 2) << 2
    carry = lax.cond((nk & 2) == 2, lambda c: step(done + 1, step(done, c)), lambda c: c, carry)
    return lax.cond((nk & 1) == 1, lambda c: step(nk - 1, c), lambda c: c, carry)


ORD_NEG_INF = -2**31 + 0x7FFFFF


def _flip(v):
    return v ^ ((v >> 31) & 0x7FFFFFFF)


def _ordinal_to_f32(o):
    return lax.bitcast_convert_type(_flip(jnp.maximum(o, ORD_NEG_INF)), F32)


def _prefix_to_bf16_bits(p):
    return _flip(jnp.maximum(p, ORD_NEG_INF >> 16) << 16) & -65536


def _dsa_kernel(iq_ref, ikw_ref, aq_ref, ak_ref, av_ref, ag_ref, o_ref,
                ika_ref, ikb_ref, vt_ref, wt_ref, qs_ref, score_ref, score16_ref, bias_ref,
                logit_ref,
                m_ref, acc_ref, *, top_k):
    t = DSA_T
    qb = pl.program_id(1)
    nk = qb + 1
    q0 = pl.multiple_of(qb * t, t)
    nq = vt_ref.shape[0]

    @pl.when(qb == 0)
    def _():
        ikw = ikw_ref[...].astype(F32)
        lane = lax.broadcasted_iota(I32, ikw.shape, 1)
        a = jnp.where(lane < IDX_DIM, ikw, 0.0)
        ika_ref[...] = a.astype(BF16)
        ikb_ref[...] = pltpu.roll(a, IDX_DIM, axis=1).astype(BF16)

        def vt_tile(kt, carry):
            k0 = pl.multiple_of(kt * t, t)
            for h in range(DSA_HEADS):
                hs = slice(h * HEAD_DIM, (h + 1) * HEAD_DIM)
                vt_ref[kt, h, :HEAD_DIM, :] = av_ref[pl.ds(k0, t), hs].astype(F32).T.astype(BF16)
                vt_ref[kt, h, HEAD_DIM:, :] = jnp.ones((VT_ROWS - HEAD_DIM, t), BF16)
            return carry

        lax.fori_loop(0, nq, vt_tile, 0)

    wt_ref[...] = ikw_ref[pl.ds(q0, t), :].astype(F32).T

    key_row = lax.broadcasted_iota(I32, (t, t), 0)
    qry_col = lax.broadcasted_iota(I32, (t, t), 1)

    def score_tile(kt):
        k0 = pl.multiple_of(kt * t, t)
        ka = ika_ref[pl.ds(k0, t), :]
        kb = ikb_ref[pl.ds(k0, t), :]
        acc = jnp.zeros((t, t), F32)
        for j in range(IDX_HEADS // 2):
            qj = iq_ref[:, j * LANE:(j + 1) * LANE]
            for half, kx in enumerate((ka, kb)):
                hh = IDX_DIM + 2 * j + half
                s = lax.dot_general(kx, qj, NT, preferred_element_type=F32)
                acc = acc + jnp.maximum(s, 0.0) * wt_ref[hh:hh + 1, :]
        causal = (key_row + k0) <= (qry_col + q0)
        score = jnp.where(causal, acc, -jnp.inf)
        score_ref[kt] = score
        score16_ref[kt] = score.astype(BF16)

    _for_tiles_grouped(nk, score_tile)

    one16, zero16 = jnp.ones((), I16), jnp.zeros((), I16)

    def prefix_step(i, pfx):
        cand_pfx = pfx + lax.shift_left(jnp.int32(1), 15 - i)
        cand = lax.bitcast_convert_type(_prefix_to_bf16_bits(cand_pfx), F32)
        cand16 = jnp.broadcast_to(cand, (PACKED_ROWS, t)).astype(BF16)[:1]

        def count_tile(kt, cnt):
            ge = jnp.where(score16_ref[kt] >= cand16, one16, zero16)
            return cnt + _fold_rows(ge, jnp.add, PACKED_ROWS)

        cnt = _fold_tiles_grouped(nk, count_tile, jnp.zeros((PACKED_ROWS, t), I16))
        total = jnp.sum(cnt.astype(I32), axis=0, keepdims=True)
        return jnp.where(total >= top_k, cand_pfx, pfx)

    pfx = lax.fori_loop(0, 16, prefix_step, jnp.full((1, t), -2**15, I32))

    def bisect_step(i, bounds):
        lo, hi = bounds
        mid = lo + ((hi - lo) >> 1)
        cand = _ordinal_to_f32(mid)

        def count_tile(kt, cnt):
            return cnt + _fold_rows(jnp.where(score_ref[kt] >= cand, 1, 0), jnp.add)

        cnt = _fold_tiles_grouped(nk, count_tile, jnp.zeros((SUBLANES, t), I32))
        enough = jnp.sum(cnt, axis=0, keepdims=True) >= top_k
        return jnp.where(enough, mid, lo), jnp.where(enough, hi, mid)

    lo, _ = lax.fori_loop(0, 17, bisect_step, (_flip(_prefix_to_bf16_bits(pfx - 1)),
                                               _flip(_prefix_to_bf16_bits(pfx + 1))))
    thr = _ordinal_to_f32(lo)

    def bias_tile(kt, kept):
        k0 = pl.multiple_of(kt * t, t)
        keep = jnp.logical_and(score_ref[kt] >= thr, (key_row + k0) <= (qry_col + q0))
        bias_ref[kt] = jnp.where(keep, 0.0, NEG_INF)
        return kept + _fold_rows(jnp.where(keep, 1, 0), jnp.add)

    kept = lax.fori_loop(0, nk, bias_tile, jnp.zeros((SUBLANES, t), I32))
    kept = jnp.sum(kept, axis=0, keepdims=True)

    @pl.when(jnp.max(kept) > top_k)
    def _():
        strictly_lower = (qry_col < key_row).astype(BF16)

        def tie_tile(kt, carry):
            above, tied_before = carry
            k0 = pl.multiple_of(kt * t, t)
            causal = (key_row + k0) <= (qry_col + q0)
            score = score_ref[kt]
            gt = jnp.logical_and(score > thr, causal)
            eq = jnp.logical_and(score == thr, causal)
            eq01 = jnp.where(eq, 1.0, 0.0)
            tied_here = tied_before + jnp.dot(strictly_lower, eq01.astype(BF16),
                                              preferred_element_type=F32)
            logit_ref[0, kt] = jnp.where(eq, tied_here, -1.0)
            above = above + _fold_rows(jnp.where(gt, 1.0, 0.0), jnp.add)
            return above, tied_before + jnp.sum(eq01, axis=0, keepdims=True)

        above, _ = lax.fori_loop(0, nk, tie_tile,
                                 (jnp.zeros((SUBLANES, t), F32), jnp.zeros((1, t), F32)))
        room = top_k - jnp.sum(above, axis=0, keepdims=True)

        def rebias_tile(kt, carry):
            k0 = pl.multiple_of(kt * t, t)
            gt = jnp.logical_and(score_ref[kt] > thr, (key_row + k0) <= (qry_col + q0))
            rank = logit_ref[0, kt]
            keep = jnp.logical_or(gt, jnp.logical_and(rank >= 0.0, rank < room))
            bias_ref[kt] = jnp.where(keep, 0.0, NEG_INF)
            return carry

        lax.fori_loop(0, nk, rebias_tile, 0)

    m_ref[...] = jnp.full(m_ref.shape, NEG_INF, F32)
    acc_ref[...] = jnp.zeros(acc_ref.shape, F32)
    log2e = 1.4426950408889634
    qs_ref[...] = (aq_ref[...].astype(F32) * ((HEAD_DIM ** -0.5) * log2e)).astype(BF16)
    key_row_1 = lax.broadcasted_iota(I32, (t, LANE), 0)

    def logit_tile(kt):
        k0 = pl.multiple_of(kt * t, t)
        bias = bias_ref[kt]
        kpos = (key_row_1 + (k0 - q0)).astype(F32)
        for h in range(DSA_HEADS):
            slope = (2.0 ** (-8.0 * (h + 1) / DSA_HEADS)) * log2e
            hs = slice(h * HEAD_DIM, (h + 1) * HEAD_DIM)
            s = lax.dot_general(ak_ref[pl.ds(k0, t), hs], qs_ref[:, hs], NT,
                                preferred_element_type=F32)
            alibi = kpos * slope
            lg = s + jnp.concatenate([alibi] * (t // LANE), axis=1) + bias
            logit_ref[h, kt] = lg
            m_ref[h] = jnp.maximum(m_ref[h], _fold_rows(lg, jnp.maximum))

    _for_tiles_grouped(nk, logit_tile)

    for h in range(DSA_HEADS):
        m_ref[h] = jnp.broadcast_to(jnp.max(m_ref[h], axis=0, keepdims=True), (SUBLANES, t))

    def pv_tile(kt):
        for h in range(DSA_HEADS):
            p = jnp.exp2(logit_ref[h, kt] - m_ref[h][:1])
            acc_ref[h] = acc_ref[h] + jnp.dot(vt_ref[kt, h], p.astype(BF16),
                                              preferred_element_type=F32)

    _for_tiles_grouped(nk, pv_tile)

    for h in range(DSA_HEADS):
        hs = slice(h * HEAD_DIM, (h + 1) * HEAD_DIM)
        acc = acc_ref[h]
        o = (acc[:HEAD_DIM] / acc[HEAD_DIM:HEAD_DIM + 1]).T
        o_ref[:, hs] = (_silu(ag_ref[:, hs].astype(F32)) * o).astype(o_ref.dtype)


def _dsa(z, ikw, batch, seq, top_k):
    t = DSA_T
    nq = seq // t
    width = DSA_HEADS * HEAD_DIM
    wblk = width // LANE
    qspec = lambda off: pl.BlockSpec((t, width), lambda b, i: (b * nq + i, off // wblk))
    kvspec = lambda off: pl.BlockSpec((seq, width), lambda b, i: (b, off // wblk))
    return pl.pallas_call(
        functools.partial(_dsa_kernel, top_k=top_k),
        grid=(batch, nq),
        in_specs=[qspec(CB_IQ),
                  pl.BlockSpec((seq, LANE), lambda b, i: (b, 0)),
                  qspec(CB_AQ), kvspec(CB_AK), kvspec(CB_AV), qspec(CB_AG)],
        out_specs=pl.BlockSpec((t, width), lambda b, i: (b * nq + i, 0)),
        out_shape=jax.ShapeDtypeStruct((batch * seq, width), BF16),
        scratch_shapes=[pltpu.VMEM((seq, LANE), BF16),
                        pltpu.VMEM((seq, LANE), BF16),
                        pltpu.VMEM((nq, DSA_HEADS, VT_ROWS, t), BF16),
                        pltpu.VMEM((LANE, t), F32),
                        pltpu.VMEM((t, width), BF16),
                        pltpu.VMEM((nq, t, t), F32),
                        pltpu.VMEM((nq, t, t), BF16),
                        pltpu.VMEM((nq, t, t), F32),
                        pltpu.VMEM((DSA_HEADS, nq, t, t), F32),
                        pltpu.VMEM((DSA_HEADS, SUBLANES, t), F32),
                        pltpu.VMEM((DSA_HEADS, VT_ROWS, t), F32)],
        compiler_params=pltpu.CompilerParams(
            dimension_semantics=("arbitrary", "arbitrary"), vmem_limit_bytes=VMEM_LIMIT),
        name="dsa",
    )(z, ikw, z, z, z, z)


def _memattn_kernel(q_ref, g_ref, mem_ref, nw_ref, w_ref, o_ref, wbf_ref, kv_ref):
    b = pl.program_id(0)
    i = pl.program_id(1)
    width = MEM_HEADS * MEM_HEAD_DIM

    @pl.when(jnp.logical_and(b == 0, i == 0))
    def _():
        for r in range(0, w_ref.shape[0], PROJ_PREP_ROWS):
            wbf_ref[r:r + PROJ_PREP_ROWS, :] = w_ref[r:r + PROJ_PREP_ROWS, :].astype(BF16)

    @pl.when(i == 0)
    def _():
        x = mem_ref[...]
        ms = jnp.mean(x * x, axis=-1, keepdims=True)
        hn = (x * lax.rsqrt(ms + EPS) * nw_ref[...]).astype(BF16)
        kv_ref[...] = jnp.dot(hn, wbf_ref[...], preferred_element_type=F32).astype(BF16)

    scale = MEM_HEAD_DIM ** -0.5
    for h in range(MEM_HEADS):
        hs = slice(h * MEM_HEAD_DIM, (h + 1) * MEM_HEAD_DIM)
        vs = slice(width + h * MEM_HEAD_DIM, width + (h + 1) * MEM_HEAD_DIM)
        s = lax.dot_general(q_ref[:, hs], kv_ref[:, hs], NT, preferred_element_type=F32) * scale
        p = jnp.exp(s - jnp.max(s, axis=-1, keepdims=True))
        l = jnp.sum(p, axis=-1, keepdims=True)
        o = jnp.dot(p.astype(BF16), kv_ref[:, vs], preferred_element_type=F32) / l
        o_ref[:, hs] = (_silu(g_ref[:, hs].astype(F32)) * o).astype(o_ref.dtype)


def _memattn(z, mem2d, norm_w, w_kv, batch, seq, mem_tokens):
    tl = MEM_TL
    d = mem2d.shape[1]
    width = MEM_HEADS * MEM_HEAD_DIM
    wblk = width // LANE
    nl = seq // tl
    return pl.pallas_call(
        _memattn_kernel,
        grid=(batch, nl),
        in_specs=[pl.BlockSpec((tl, width), lambda b, i: (b * nl + i, CB_MQ // wblk)),
                  pl.BlockSpec((tl, width), lambda b, i: (b * nl + i, CB_MG // wblk)),
                  pl.BlockSpec((mem_tokens, d), lambda b, i: (b, 0)),
                  pl.BlockSpec((1, d), lambda b, i: (0, 0)),
                  pl.BlockSpec((d, 2 * width), lambda b, i: (0, 0), pipeline_mode=SINGLE_BUFFER)],
        out_specs=pl.BlockSpec((tl, width), lambda b, i: (b * nl + i, 0)),
        out_shape=jax.ShapeDtypeStruct((batch * seq, width), BF16),
        scratch_shapes=[pltpu.VMEM((d, 2 * width), BF16),
                        pltpu.VMEM((mem_tokens, 2 * width), BF16)],
        compiler_params=pltpu.CompilerParams(
            dimension_semantics=("arbitrary", "arbitrary"), vmem_limit_bytes=VMEM_LIMIT),
        name="mem_attn",
    )(z, z, mem2d, norm_w.reshape(1, d), w_kv)


def _merge_kernel(yr_ref, yd_ref, ym_ref, wr_ref, wd_ref, wm_ref, gr_ref, gd_ref, gm_ref, o_ref,
                  wbf_ref):
    @pl.when(pl.program_id(1) == 0)
    def _():
        for b, w_ref in enumerate((wr_ref, wd_ref, wm_ref)):
            wbf_ref[b] = w_ref[...].astype(BF16)

    def branch(b, y_ref, g_ref):
        up = jnp.dot(y_ref[...], wbf_ref[b], preferred_element_type=F32)
        return _sigmoid(g_ref[...].astype(F32)) * up

    merged = branch(0, yr_ref, gr_ref) + branch(1, yd_ref, gd_ref) + branch(2, ym_ref, gm_ref)
    o_ref[...] = merged.astype(o_ref.dtype)


def _merge(y_ret, y_dsa, y_mem, w_ret, w_dsa, w_mem, z):
    tm, tn = MERGE_TM, MERGE_TN
    m, kdim = y_ret.shape
    n = w_ret.shape[1]
    gblk = tn // LANE
    yspec = pl.BlockSpec((tm, kdim), lambda j, i: (i, 0))
    wspec = pl.BlockSpec((kdim, tn), lambda j, i: (0, j))
    gspec = lambda off: pl.BlockSpec((tm, tn), lambda j, i: (i, off // gblk + j))
    return pl.pallas_call(
        _merge_kernel,
        grid=(n // tn, m // tm),
        in_specs=[yspec, yspec, yspec, wspec, wspec, wspec,
                  gspec(CB_GRET), gspec(CB_GDSA), gspec(CB_GMEM)],
        out_specs=pl.BlockSpec((tm, tn), lambda j, i: (i, j)),
        out_shape=jax.ShapeDtypeStruct((m, n), BF16),
        scratch_shapes=[pltpu.VMEM((N_BRANCHES, kdim, tn), BF16)],
        compiler_params=pltpu.CompilerParams(
            dimension_semantics=("arbitrary", "arbitrary"), vmem_limit_bytes=VMEM_LIMIT),
        name="merge",
    )(y_ret, y_dsa, y_mem, w_ret, w_dsa, w_mem, z, z, z)


def _out_kernel(m_ref, w_ref, x_ref, nw_ref, o_ref, wbf_ref):
    @pl.when(pl.program_id(0) == 0)
    def _():
        for r in range(0, w_ref.shape[0], PROJ_PREP_ROWS):
            wbf_ref[r:r + PROJ_PREP_ROWS, :] = w_ref[r:r + PROJ_PREP_ROWS, :].astype(BF16)

    out = jnp.dot(m_ref[...], wbf_ref[...], preferred_element_type=F32)
    ms = jnp.mean(out * out, axis=-1, keepdims=True)
    o_ref[...] = x_ref[...] + out * lax.rsqrt(ms + EPS) * nw_ref[...]


def _out_proj(merged, w_o, x2d, post_w):
    tm = OUT_TM
    m, d = x2d.shape
    return pl.pallas_call(
        _out_kernel,
        grid=(m // tm,),
        in_specs=[pl.BlockSpec((tm, d), lambda i: (i, 0)),
                  pl.BlockSpec((d, d), lambda i: (0, 0), pipeline_mode=SINGLE_BUFFER),
                  pl.BlockSpec((tm, d), lambda i: (i, 0)),
                  pl.BlockSpec((1, d), lambda i: (0, 0))],
        out_specs=pl.BlockSpec((tm, d), lambda i: (i, 0)),
        out_shape=jax.ShapeDtypeStruct((m, d), F32),
        scratch_shapes=[pltpu.VMEM((d, d), BF16)],
        compiler_params=pltpu.CompilerParams(
            dimension_semantics=("arbitrary",), vmem_limit_bytes=VMEM_LIMIT),
        name="out_proj_postnorm",
    )(merged, w_o, x2d, post_w.reshape(1, d))


def _layer(x, mem, pre_norm_w, w_in, ret_gn_w, mem_norm_w, w_mem_kv,
           w_up_ret, w_up_dsa, w_up_mem, w_o, post_norm_w):
    batch, seq, d = x.shape
    mem_tokens = mem.shape[1]
    top_k = min(IDX_TOPK_MAX, seq // 4)
    assert w_in.shape == (d, Z_COLS + IDX_TAIL_COLS), w_in.shape
    assert seq % max(DSA_T, RET_C, MEM_TL) == 0 and seq < 2**15, seq
    assert (batch * seq) % max(MERGE_TM, OUT_TM) == 0 and top_k <= DSA_T, (batch, seq)
    x2d = x.reshape(batch * seq, d)

    w_in_t = w_in.T
    w_ikw = jnp.pad(w_in_t[IDX_TAIL_START:IDX_TAIL_START + IDX_TAIL_COLS],
                    ((0, LANE - IDX_TAIL_COLS), (0, 0))).astype(BF16)
    h, ikw = _prenorm(x2d, pre_norm_w, w_ikw)
    z = _project(h, w_in_t)

    log_g = jnp.log1p(-(2.0 ** (-5.0 - jnp.arange(RET_HEADS, dtype=F32))))
    y_ret = _retention(z, log_g, ret_gn_w, batch, seq)
    y_dsa = _dsa(z, ikw, batch, seq, top_k)
    y_mem = _memattn(z, mem.reshape(batch * mem_tokens, d), mem_norm_w, w_mem_kv, batch, seq,
                     mem_tokens)

    merged = _merge(y_ret, y_dsa, y_mem, w_up_ret, w_up_dsa, w_up_mem, z)
    out = _out_proj(merged, w_o, x2d, post_norm_w)
    return out.reshape(batch, seq, d)


def kernel(x, mem, pre_norm_w, w_in, ret_gn_w, mem_norm_w, w_mem_kv, w_up_ret, w_up_dsa,
           w_up_mem, w_o, post_norm_w):
    for layer in range(w_in.shape[0]):
        x = _layer(x, mem, pre_norm_w[layer], w_in[layer], ret_gn_w[layer], mem_norm_w[layer],
                   w_mem_kv[layer], w_up_ret[layer], w_up_dsa[layer], w_up_mem[layer],
                   w_o[layer], post_norm_w[layer])
    return x
```

```python
import functools

import jax
import jax.numpy as jnp
from jax import lax
from jax.experimental import pallas as pl
from jax.experimental.pallas import tpu as pltpu

F32 = jnp.float32
BF16 = jnp.bfloat16
I32 = jnp.int32

LANE = 128
SUBLANES = 8
N_BRANCHES = 3
RET_HEADS = 8
HEAD_DIM = 128
DSA_HEADS = 8
IDX_HEADS = 16
IDX_DIM = 64
IDX_TOPK_MAX = 256
MEM_HEADS = 4
MEM_HEAD_DIM = 256
NEG_INF = -1e30
EPS = 1e-6
INT_MIN = -2**31

CB_RQ, CB_RK, CB_RV, CB_RG = 0, 8, 16, 24
CB_AQ, CB_AK, CB_AV, CB_AG = 32, 40, 48, 56
CB_IQ = 64
CB_MQ, CB_MG = 72, 80
CB_GRET, CB_GDSA, CB_GMEM = 88, 104, 120
N_COL_BLOCKS = 136
Z_COLS = N_COL_BLOCKS * LANE
IDX_TAIL_START = (CB_IQ + IDX_HEADS * IDX_DIM // LANE) * LANE
IDX_TAIL_COLS = IDX_DIM + IDX_HEADS

PROJ_TM = 2048
PROJ_TN = 1024
PROJ_ALIGNED_TILES = IDX_TAIL_START // PROJ_TN
PROJ_ROWS = 512
PROJ_PREP_ROWS = 256
PRENORM_TM = 1024
MEM_TL = 1024
MERGE_TM, MERGE_TN = 512, 1024
OUT_TM = 512
VMEM_LIMIT = 56 * 1024 * 1024
SINGLE_BUFFER = pl.Buffered(1)

NT = (((1,), (1,)), ((), ()))
TN = (((0,), (0,)), ((), ()))


def _sigmoid(v):
    return 1.0 / (1.0 + jnp.exp(-v))


def _silu(v):
    return v * _sigmoid(v)


def _prenorm_kernel(x_ref, w_ref, wikw_ref, h_ref, ikw_ref):
    x = x_ref[...]
    ms = jnp.mean(x * x, axis=-1, keepdims=True)
    h = (x * lax.rsqrt(ms + EPS) * w_ref[...]).astype(h_ref.dtype)
    h_ref[...] = h
    ikw_ref[...] = lax.dot_general(h, wikw_ref[...], NT,
                                   preferred_element_type=F32).astype(ikw_ref.dtype)


def _prenorm(x2d, w, w_ikw):
    m, d = x2d.shape
    tm = min(PRENORM_TM, m)
    return pl.pallas_call(
        _prenorm_kernel,
        grid=(m // tm,),
        in_specs=[pl.BlockSpec((tm, d), lambda i: (i, 0)),
                  pl.BlockSpec((1, d), lambda i: (0, 0)),
                  pl.BlockSpec((LANE, d), lambda i: (0, 0))],
        out_specs=[pl.BlockSpec((tm, d), lambda i: (i, 0)),
                   pl.BlockSpec((tm, LANE), lambda i: (i, 0))],
        out_shape=[jax.ShapeDtypeStruct((m, d), BF16),
                   jax.ShapeDtypeStruct((m, LANE), BF16)],
        compiler_params=pltpu.CompilerParams(dimension_semantics=("arbitrary",)),
        name="prenorm_idx",
    )(x2d, w.reshape(1, d), w_ikw)


def _proj_kernel(h_ref, w_hbm, z_ref, wf32_ref, wbf_ref, sem, *, n_row_tiles):
    j = pl.program_id(0)
    i = pl.program_id(1)
    n_tiles = pl.num_programs(0)

    def tile_copy(tile, slot):
        skip = jnp.where(tile >= PROJ_ALIGNED_TILES, IDX_TAIL_COLS // SUBLANES, 0)
        start = pl.multiple_of((tile * (PROJ_TN // SUBLANES) + skip) * SUBLANES, SUBLANES)
        return pltpu.make_async_copy(w_hbm.at[pl.ds(start, PROJ_TN), :], wf32_ref.at[slot],
                                     sem.at[slot])

    @pl.when(jnp.logical_and(j == 0, i == 0))
    def _():
        tile_copy(0, 0).start()
        tile_copy(1, 1).start()
        tile_copy(0, 0).wait()
        for r in range(0, PROJ_TN, PROJ_PREP_ROWS):
            wbf_ref[0, r:r + PROJ_PREP_ROWS, :] = wf32_ref[0, r:r + PROJ_PREP_ROWS, :].astype(BF16)
        tile_copy(1, 1).wait()

    @pl.when(jnp.logical_and(i == 0, jnp.logical_and(j >= 1, j + 1 < n_tiles)))
    def _():
        tile_copy(j + 1, (j + 1) % 2).wait()

    nxt = (j + 1) % 2
    part = PROJ_TN // n_row_tiles
    rows = pl.ds(pl.multiple_of(i * part, part), part)
    wbf_ref[nxt, rows, :] = wf32_ref[nxt, rows, :].astype(BF16)

    w = wbf_ref[j % 2]
    for r in range(0, h_ref.shape[0], PROJ_ROWS):
        z_ref[r:r + PROJ_ROWS, :] = lax.dot_general(
            h_ref[r:r + PROJ_ROWS, :], w, NT, preferred_element_type=F32).astype(z_ref.dtype)

    @pl.when(jnp.logical_and(i == n_row_tiles - 1, j + 2 < n_tiles))
    def _():
        tile_copy(j + 2, j % 2).start()


def _project(h, w_in_t):
    m, d = h.shape
    tm = min(PROJ_TM, m)
    n_row_tiles = m // tm
    assert Z_COLS // PROJ_TN >= 2 and PROJ_TN % (n_row_tiles * PACKED_ROWS) == 0
    return pl.pallas_call(
        functools.partial(_proj_kernel, n_row_tiles=n_row_tiles),
        grid=(Z_COLS // PROJ_TN, n_row_tiles),
        in_specs=[pl.BlockSpec((tm, d), lambda j, i: (i, 0)),
                  pl.BlockSpec(memory_space=pl.ANY)],
        out_specs=pl.BlockSpec((tm, PROJ_TN), lambda j, i: (i, j)),
        out_shape=jax.ShapeDtypeStruct((m, Z_COLS), BF16),
        scratch_shapes=[pltpu.VMEM((2, PROJ_TN, d), F32),
                        pltpu.VMEM((2, PROJ_TN, d), BF16),
                        pltpu.SemaphoreType.DMA((2,))],
        compiler_params=pltpu.CompilerParams(
            dimension_semantics=("arbitrary", "arbitrary"), vmem_limit_bytes=VMEM_LIMIT),
        name="in_proj",
    )(h, w_in_t)


RET_C = 256
RET_HEADS_PER_STEP = 2


def _retention_kernel(lg_ref, q_ref, k_ref, v_ref, g_ref, gw_ref, o_ref):
    c = RET_C
    seq = q_ref.shape[0]
    scale = HEAD_DIM ** -0.5
    ri = lax.broadcasted_iota(I32, (c, c), 0)
    ci = lax.broadcasted_iota(I32, (c, c), 1)
    diff = (ri - ci).astype(F32)
    idx = lax.broadcasted_iota(I32, (c, 1), 0).astype(F32)

    for hh in range(RET_HEADS_PER_STEP):
        hs = slice(hh * HEAD_DIM, (hh + 1) * HEAD_DIM)
        lg = lg_ref[pl.program_id(1) * RET_HEADS_PER_STEP + hh]
        decay = jnp.where(diff >= 0, jnp.exp(lg * jnp.maximum(diff, 0.0)), 0.0) * scale
        q_dec = jnp.exp(lg * (idx + 1.0))
        k_dec = jnp.exp(lg * (c - 1.0 - idx)) * scale
        chunk_dec = jnp.exp(lg * jnp.full((1, HEAD_DIM), float(c), F32))
        gw = gw_ref[:, hs]
        state = jnp.zeros((HEAD_DIM, HEAD_DIM), F32)
        for i in range(seq // c):
            r = i * c
            qi = q_ref[pl.ds(r, c), hs]
            ki = k_ref[pl.ds(r, c), hs]
            vi = v_ref[pl.ds(r, c), hs]
            inner = lax.dot_general(qi, ki, NT, preferred_element_type=F32) * decay
            o = (jnp.dot(inner.astype(BF16), vi, preferred_element_type=F32)
                 + jnp.dot(qi, state.astype(BF16), preferred_element_type=F32) * q_dec)
            vs = (vi.astype(F32) * k_dec).astype(BF16)
            state = state * chunk_dec + lax.dot_general(ki, vs, TN, preferred_element_type=F32)
            mu = jnp.mean(o, axis=-1, keepdims=True)
            oc = o - mu
            var = jnp.mean(oc * oc, axis=-1, keepdims=True)
            on = oc * lax.rsqrt(var + 1e-5) * gw
            g = g_ref[pl.ds(r, c), hs].astype(F32)
            o_ref[pl.ds(r, c), hs] = (_silu(g) * on).astype(o_ref.dtype)


def _retention(z, log_g, gn_w, batch, seq):
    hp = RET_HEADS_PER_STEP
    width = hp * HEAD_DIM
    blk = lambda off: pl.BlockSpec((seq, width), lambda b, h: (b, off // hp + h))
    return pl.pallas_call(
        _retention_kernel,
        grid=(batch, RET_HEADS // hp),
        in_specs=[pl.BlockSpec(memory_space=pltpu.SMEM),
                  blk(CB_RQ), blk(CB_RK), blk(CB_RV), blk(CB_RG),
                  pl.BlockSpec((1, width), lambda b, h: (0, h))],
        out_specs=pl.BlockSpec((seq, width), lambda b, h: (b, h)),
        out_shape=jax.ShapeDtypeStruct((batch * seq, RET_HEADS * HEAD_DIM), BF16),
        compiler_params=pltpu.CompilerParams(
            dimension_semantics=("arbitrary", "arbitrary"), vmem_limit_bytes=VMEM_LIMIT),
        name="retention",
    )(log_g, z, z, z, z, gn_w.reshape(1, -1))


DSA_T = 256
I16 = jnp.int16
PACKED_ROWS = 2 * SUBLANES
VT_ROWS = HEAD_DIM + PACKED_ROWS


def _fold_rows(x, op, rows=SUBLANES):
    while x.shape[0] > rows:
        half = x.shape[0] // 2
        x = op(x[:half], x[half:])
    return x


def _for_tiles_grouped(nk, step):
    def quad(i, carry):
        for u in range(4):
            step(4 * i + u)
        return carry

    lax.fori_loop(0, nk >> 2, quad, 0)
    done = (nk >> 2) << 2

    @pl.when((nk & 2) == 2)
    def _():
        step(done)
        step(done + 1)

    @pl.when((nk & 1) == 1)
    def _():
        step(nk - 1)


def _fold_tiles_grouped(nk, step, carry):
    def quad(i, c):
        for u in range(4):
            c = step(4 * i + u, c)
        return c

    carry = lax.fori_loop(0, nk >> 2, quad, carry)
    done = (nk >> 2) << 2
    carry = lax.cond((nk & 2) == 2, lambda c: step(done + 1, step(done, c)), lambda c: c, carry)
    return lax.cond((nk & 1) == 1, lambda c: step(nk - 1, c), lambda c: c, carry)


ORD_NEG_INF = -2**31 + 0x7FFFFF


def _flip(v):
    return v ^ ((v >> 31) & 0x7FFFFFFF)


def _ordinal_to_f32(o):
    return lax.bitcast_convert_type(_flip(jnp.maximum(o, ORD_NEG_INF)), F32)


def _prefix_to_bf16_bits(p):
    return _flip(jnp.maximum(p, ORD_NEG_INF >> 16) << 16) & -65536


def _dsa_kernel(iq_ref, ikw_ref, aq_ref, ak_ref, av_ref, ag_ref, o_ref,
                ika_ref, ikb_ref, vt_ref, wt_ref, qs_ref, score_ref, score16_ref, bias_ref,
                logit_ref,
                m_ref, acc_ref, *, top_k):
    t = DSA_T
    qb = pl.program_id(1)
    nk = qb + 1
    q0 = pl.multiple_of(qb * t, t)
    nq = vt_ref.shape[0]

    @pl.when(qb == 0)
    def _():
        ikw = ikw_ref[...].astype(F32)
        lane = lax.broadcasted_iota(I32, ikw.shape, 1)
        a = jnp.where(lane < IDX_DIM, ikw, 0.0)
        ika_ref[...] = a.astype(BF16)
        ikb_ref[...] = pltpu.roll(a, IDX_DIM, axis=1).astype(BF16)

        def vt_tile(kt, carry):
            k0 = pl.multiple_of(kt * t, t)
            for h in range(DSA_HEADS):
                hs = slice(h * HEAD_DIM, (h + 1) * HEAD_DIM)
                vt_ref[kt, h, :HEAD_DIM, :] = av_ref[pl.ds(k0, t), hs].astype(F32).T.astype(BF16)
                vt_ref[kt, h, HEAD_DIM:, :] = jnp.ones((VT_ROWS - HEAD_DIM, t), BF16)
            return carry

        lax.fori_loop(0, nq, vt_tile, 0)

    wt_ref[...] = ikw_ref[pl.ds(q0, t), :].astype(F32).T

    key_row = lax.broadcasted_iota(I32, (t, t), 0)
    qry_col = lax.broadcasted_iota(I32, (t, t), 1)

    def score_tile(kt):
        k0 = pl.multiple_of(kt * t, t)
        ka = ika_ref[pl.ds(k0, t), :]
        kb = ikb_ref[pl.ds(k0, t), :]
        acc = jnp.zeros((t, t), F32)
        for j in range(IDX_HEADS // 2):
            qj = iq_ref[:, j * LANE:(j + 1) * LANE]
            for half, kx in enumerate((ka, kb)):
                hh = IDX_DIM + 2 * j + half
                s = lax.dot_general(kx, qj, NT, preferred_element_type=F32)
                acc = acc + jnp.maximum(s, 0.0) * wt_ref[hh:hh + 1, :]
        causal = (key_row + k0) <= (qry_col + q0)
        score = jnp.where(causal, acc, -jnp.inf)
        score_ref[kt] = score
        score16_ref[kt] = score.astype(BF16)

    _for_tiles_grouped(nk, score_tile)

    one16, zero16 = jnp.ones((), I16), jnp.zeros((), I16)

    def prefix_step(i, pfx):
        cand_pfx = pfx + lax.shift_left(jnp.int32(1), 15 - i)
        cand = lax.bitcast_convert_type(_prefix_to_bf16_bits(cand_pfx), F32)
        cand16 = jnp.broadcast_to(cand, (PACKED_ROWS, t)).astype(BF16)[:1]

        def count_tile(kt, cnt):
            ge = jnp.where(score16_ref[kt] >= cand16, one16, zero16)
            return cnt + _fold_rows(ge, jnp.add, PACKED_ROWS)

        cnt = _fold_tiles_grouped(nk, count_tile, jnp.zeros((PACKED_ROWS, t), I16))
        total = jnp.sum(cnt.astype(I32), axis=0, keepdims=True)
        return jnp.where(total >= top_k, cand_pfx, pfx)

    pfx = lax.fori_loop(0, 16, prefix_step, jnp.full((1, t), -2**15, I32))

    def bisect_step(i, bounds):
        lo, hi = bounds
        mid = lo + ((hi - lo) >> 1)
        cand = _ordinal_to_f32(mid)

        def count_tile(kt, cnt):
            return cnt + _fold_rows(jnp.where(score_ref[kt] >= cand, 1, 0), jnp.add)

        cnt = _fold_tiles_grouped(nk, count_tile, jnp.zeros((SUBLANES, t), I32))
        enough = jnp.sum(cnt, axis=0, keepdims=True) >= top_k
        return jnp.where(enough, mid, lo), jnp.where(enough, hi, mid)

    lo, _ = lax.fori_loop(0, 17, bisect_step, (_flip(_prefix_to_bf16_bits(pfx - 1)),
                                               _flip(_prefix_to_bf16_bits(pfx + 1))))
    thr = _ordinal_to_f32(lo)

    def bias_tile(kt, kept):
        k0 = pl.multiple_of(kt * t, t)
        keep = jnp.logical_and(score_ref[kt] >= thr, (key_row + k0) <= (qry_col + q0))
        bias_ref[kt] = jnp.where(keep, 0.0, NEG_INF)
        return kept + _fold_rows(jnp.where(keep, 1, 0), jnp.add)

    kept = lax.fori_loop(0, nk, bias_tile, jnp.zeros((SUBLANES, t), I32))
    kept = jnp.sum(kept, axis=0, keepdims=True)

    @pl.when(jnp.max(kept) > top_k)
    def _():
        strictly_lower = (qry_col < key_row).astype(BF16)

        def tie_tile(kt, carry):
            above, tied_before = carry
            k0 = pl.multiple_of(kt * t, t)
            causal = (key_row + k0) <= (qry_col + q0)
            score = score_ref[kt]
            gt = jnp.logical_and(score > thr, causal)
            eq = jnp.logical_and(score == thr, causal)
            eq01 = jnp.where(eq, 1.0, 0.0)
            tied_here = tied_before + jnp.dot(strictly_lower, eq01.astype(BF16),
                                              preferred_element_type=F32)
            logit_ref[0, kt] = jnp.where(eq, tied_here, -1.0)
            above = above + _fold_rows(jnp.where(gt, 1.0, 0.0), jnp.add)
            return above, tied_before + jnp.sum(eq01, axis=0, keepdims=True)

        above, _ = lax.fori_loop(0, nk, tie_tile,
                                 (jnp.zeros((SUBLANES, t), F32), jnp.zeros((1, t), F32)))
        room = top_k - jnp.sum(above, axis=0, keepdims=True)

        def rebias_tile(kt, carry):
            k0 = pl.multiple_of(kt * t, t)
            gt = jnp.logical_and(score_ref[kt] > thr, (key_row + k0) <= (qry_col + q0))
            rank = logit_ref[0, kt]
            keep = jnp.logical_or(gt, jnp.logical_and(rank >= 0.0, rank < room))
            bias_ref[kt] = jnp.where(keep, 0.0, NEG_INF)
            return carry

        lax.fori_loop(0, nk, rebias_tile, 0)

    m_ref[...] = jnp.full(m_ref.shape, NEG_INF, F32)
    acc_ref[...] = jnp.zeros(acc_ref.shape, F32)
    log2e = 1.4426950408889634
    qs_ref[...] = (aq_ref[...].astype(F32) * ((HEAD_DIM ** -0.5) * log2e)).astype(BF16)
    key_row_1 = lax.broadcasted_iota(I32, (t, LANE), 0)

    def logit_tile(kt):
        k0 = pl.multiple_of(kt * t, t)
        bias = bias_ref[kt]
        kpos = (key_row_1 + (k0 - q0)).astype(F32)
        for h in range(DSA_HEADS):
            slope = (2.0 ** (-8.0 * (h + 1) / DSA_HEADS)) * log2e
            hs = slice(h * HEAD_DIM, (h + 1) * HEAD_DIM)
            s = lax.dot_general(ak_ref[pl.ds(k0, t), hs], qs_ref[:, hs], NT,
                                preferred_element_type=F32)
            alibi = kpos * slope
            lg = s + jnp.concatenate([alibi] * (t // LANE), axis=1) + bias
            logit_ref[h, kt] = lg
            m_ref[h] = jnp.maximum(m_ref[h], _fold_rows(lg, jnp.maximum))

    _for_tiles_grouped(nk, logit_tile)

    for h in range(DSA_HEADS):
        m_ref[h] = jnp.broadcast_to(jnp.max(m_ref[h], axis=0, keepdims=True), (SUBLANES, t))

    def pv_tile(kt):
        for h in range(DSA_HEADS):
            p = jnp.exp2(logit_ref[h, kt] - m_ref[h][:1])
            acc_ref[h] = acc_ref[h] + jnp.dot(vt_ref[kt, h], p.astype(BF16),
                                              preferred_element_type=F32)

    _for_tiles_grouped(nk, pv_tile)

    for h in range(DSA_HEADS):
        hs = slice(h * HEAD_DIM, (h + 1) * HEAD_DIM)
        acc = acc_ref[h]
        o = (acc[:HEAD_DIM] / acc[HEAD_DIM:HEAD_DIM + 1]).T
        o_ref[:, hs] = (_silu(ag_ref[:, hs].astype(F32)) * o).astype(o_ref.dtype)


def _dsa(z, ikw, batch, seq, top_k):
    t = DSA_T
    nq = seq // t
    width = DSA_HEADS * HEAD_DIM
    wblk = width // LANE
    qspec = lambda off: pl.BlockSpec((t, width), lambda b, i: (b * nq + i, off // wblk))
    kvspec = lambda off: pl.BlockSpec((seq, width), lambda b, i: (b, off // wblk))
    return pl.pallas_call(
        functools.partial(_dsa_kernel, top_k=top_k),
        grid=(batch, nq),
        in_specs=[qspec(CB_IQ),
                  pl.BlockSpec((seq, LANE), lambda b, i: (b, 0)),
                  qspec(CB_AQ), kvspec(CB_AK), kvspec(CB_AV), qspec(CB_AG)],
        out_specs=pl.BlockSpec((t, width), lambda b, i: (b * nq + i, 0)),
        out_shape=jax.ShapeDtypeStruct((batch * seq, width), BF16),
        scratch_shapes=[pltpu.VMEM((seq, LANE), BF16),
                        pltpu.VMEM((seq, LANE), BF16),
                        pltpu.VMEM((nq, DSA_HEADS, VT_ROWS, t), BF16),
                        pltpu.VMEM((LANE, t), F32),
                        pltpu.VMEM((t, width), BF16),
                        pltpu.VMEM((nq, t, t), F32),
                        pltpu.VMEM((nq, t, t), BF16),
                        pltpu.VMEM((nq, t, t), F32),
                        pltpu.VMEM((DSA_HEADS, nq, t, t), F32),
                        pltpu.VMEM((DSA_HEADS, SUBLANES, t), F32),
                        pltpu.VMEM((DSA_HEADS, VT_ROWS, t), F32)],
        compiler_params=pltpu.CompilerParams(
            dimension_semantics=("arbitrary", "arbitrary"), vmem_limit_bytes=VMEM_LIMIT),
        name="dsa",
    )(z, ikw, z, z, z, z)


def _memattn_kernel(q_ref, g_ref, mem_ref, nw_ref, w_ref, o_ref, wbf_ref, kv_ref):
    b = pl.program_id(0)
    i = pl.program_id(1)
    width = MEM_HEADS * MEM_HEAD_DIM

    @pl.when(jnp.logical_and(b == 0, i == 0))
    def _():
        for r in range(0, w_ref.shape[0], PROJ_PREP_ROWS):
            wbf_ref[r:r + PROJ_PREP_ROWS, :] = w_ref[r:r + PROJ_PREP_ROWS, :].astype(BF16)

    @pl.when(i == 0)
    def _():
        x = mem_ref[...]
        ms = jnp.mean(x * x, axis=-1, keepdims=True)
        hn = (x * lax.rsqrt(ms + EPS) * nw_ref[...]).astype(BF16)
        kv_ref[...] = jnp.dot(hn, wbf_ref[...], preferred_element_type=F32).astype(BF16)

    scale = MEM_HEAD_DIM ** -0.5
    for h in range(MEM_HEADS):
        hs = slice(h * MEM_HEAD_DIM, (h + 1) * MEM_HEAD_DIM)
        vs = slice(width + h * MEM_HEAD_DIM, width + (h + 1) * MEM_HEAD_DIM)
        s = lax.dot_general(q_ref[:, hs], kv_ref[:, hs], NT, preferred_element_type=F32) * scale
        p = jnp.exp(s - jnp.max(s, axis=-1, keepdims=True))
        l = jnp.sum(p, axis=-1, keepdims=True)
        o = jnp.dot(p.astype(BF16), kv_ref[:, vs], preferred_element_type=F32) / l
        o_ref[:, hs] = (_silu(g_ref[:, hs].astype(F32)) * o).astype(o_ref.dtype)


def _memattn(z, mem2d, norm_w, w_kv, batch, seq, mem_tokens):
    tl = MEM_TL
    d = mem2d.shape[1]
    width = MEM_HEADS * MEM_HEAD_DIM
    wblk = width // LANE
    nl = seq // tl
    return pl.pallas_call(
        _memattn_kernel,
        grid=(batch, nl),
        in_specs=[pl.BlockSpec((tl, width), lambda b, i: (b * nl + i, CB_MQ // wblk)),
                  pl.BlockSpec((tl, width), lambda b, i: (b * nl + i, CB_MG // wblk)),
                  pl.BlockSpec((mem_tokens, d), lambda b, i: (b, 0)),
                  pl.BlockSpec((1, d), lambda b, i: (0, 0)),
                  pl.BlockSpec((d, 2 * width), lambda b, i: (0, 0), pipeline_mode=SINGLE_BUFFER)],
        out_specs=pl.BlockSpec((tl, width), lambda b, i: (b * nl + i, 0)),
        out_shape=jax.ShapeDtypeStruct((batch * seq, width), BF16),
        scratch_shapes=[pltpu.VMEM((d, 2 * width), BF16),
                        pltpu.VMEM((mem_tokens, 2 * width), BF16)],
        compiler_params=pltpu.CompilerParams(
            dimension_semantics=("arbitrary", "arbitrary"), vmem_limit_bytes=VMEM_LIMIT),
        name="mem_attn",
    )(z, z, mem2d, norm_w.reshape(1, d), w_kv)


def _merge_kernel(yr_ref, yd_ref, ym_ref, wr_ref, wd_ref, wm_ref, gr_ref, gd_ref, gm_ref, o_ref,
                  wbf_ref):
    @pl.when(pl.program_id(1) == 0)
    def _():
        for b, w_ref in enumerate((wr_ref, wd_ref, wm_ref)):
            wbf_ref[b] = w_ref[...].astype(BF16)

    def branch(b, y_ref, g_ref):
        up = jnp.dot(y_ref[...], wbf_ref[b], preferred_element_type=F32)
        return _sigmoid(g_ref[...].astype(F32)) * up

    merged = branch(0, yr_ref, gr_ref) + branch(1, yd_ref, gd_ref) + branch(2, ym_ref, gm_ref)
    o_ref[...] = merged.astype(o_ref.dtype)


def _merge(y_ret, y_dsa, y_mem, w_ret, w_dsa, w_mem, z):
    tm, tn = MERGE_TM, MERGE_TN
    m, kdim = y_ret.shape
    n = w_ret.shape[1]
    gblk = tn // LANE
    yspec = pl.BlockSpec((tm, kdim), lambda j, i: (i, 0))
    wspec = pl.BlockSpec((kdim, tn), lambda j, i: (0, j))
    gspec = lambda off: pl.BlockSpec((tm, tn), lambda j, i: (i, off // gblk + j))
    return pl.pallas_call(
        _merge_kernel,
        grid=(n // tn, m // tm),
        in_specs=[yspec, yspec, yspec, wspec, wspec, wspec,
                  gspec(CB_GRET), gspec(CB_GDSA), gspec(CB_GMEM)],
        out_specs=pl.BlockSpec((tm, tn), lambda j, i: (i, j)),
        out_shape=jax.ShapeDtypeStruct((m, n), BF16),
        scratch_shapes=[pltpu.VMEM((N_BRANCHES, kdim, tn), BF16)],
        compiler_params=pltpu.CompilerParams(
            dimension_semantics=("arbitrary", "arbitrary"), vmem_limit_bytes=VMEM_LIMIT),
        name="merge",
    )(y_ret, y_dsa, y_mem, w_ret, w_dsa, w_mem, z, z, z)


def _out_kernel(m_ref, w_ref, x_ref, nw_ref, o_ref, wbf_ref):
    @pl.when(pl.program_id(0) == 0)
    def _():
        for r in range(0, w_ref.shape[0], PROJ_PREP_ROWS):
            wbf_ref[r:r + PROJ_PREP_ROWS, :] = w_ref[r:r + PROJ_PREP_ROWS, :].astype(BF16)

    out = jnp.dot(m_ref[...], wbf_ref[...], preferred_element_type=F32)
    ms = jnp.mean(out * out, axis=-1, keepdims=True)
    o_ref[...] = x_ref[...] + out * lax.rsqrt(ms + EPS) * nw_ref[...]


def _out_proj(merged, w_o, x2d, post_w):
    tm = OUT_TM
    m, d = x2d.shape
    return pl.pallas_call(
        _out_kernel,
        grid=(m // tm,),
        in_specs=[pl.BlockSpec((tm, d), lambda i: (i, 0)),
                  pl.BlockSpec((d, d), lambda i: (0, 0), pipeline_mode=SINGLE_BUFFER),
                  pl.BlockSpec((tm, d), lambda i: (i, 0)),
                  pl.BlockSpec((1, d), lambda i: (0, 0))],
        out_specs=pl.BlockSpec((tm, d), lambda i: (i, 0)),
        out_shape=jax.ShapeDtypeStruct((m, d), F32),
        scratch_shapes=[pltpu.VMEM((d, d), BF16)],
        compiler_params=pltpu.CompilerParams(
            dimension_semantics=("arbitrary",), vmem_limit_bytes=VMEM_LIMIT),
        name="out_proj_postnorm",
    )(merged, w_o, x2d, post_w.reshape(1, d))


def _layer(x, mem, pre_norm_w, w_in, ret_gn_w, mem_norm_w, w_mem_kv,
           w_up_ret, w_up_dsa, w_up_mem, w_o, post_norm_w):
    batch, seq, d = x.shape
    mem_tokens = mem.shape[1]
    top_k = min(IDX_TOPK_MAX, seq // 4)
    assert w_in.shape == (d, Z_COLS + IDX_TAIL_COLS), w_in.shape
    assert seq % max(DSA_T, RET_C, MEM_TL) == 0 and seq < 2**15, seq
    assert (batch * seq) % max(MERGE_TM, OUT_TM) == 0 and top_k <= DSA_T, (batch, seq)
    x2d = x.reshape(batch * seq, d)

    w_in_t = w_in.T
    w_ikw = jnp.pad(w_in_t[IDX_TAIL_START:IDX_TAIL_START + IDX_TAIL_COLS],
                    ((0, LANE - IDX_TAIL_COLS), (0, 0))).astype(BF16)
    h, ikw = _prenorm(x2d, pre_norm_w, w_ikw)
    z = _project(h, w_in_t)

    log_g = jnp.log1p(-(2.0 ** (-5.0 - jnp.arange(RET_HEADS, dtype=F32))))
    y_ret = _retention(z, log_g, ret_gn_w, batch, seq)
    y_dsa = _dsa(z, ikw, batch, seq, top_k)
    y_mem = _memattn(z, mem.reshape(batch * mem_tokens, d), mem_norm_w, w_mem_kv, batch, seq,
                     mem_tokens)

    merged = _merge(y_ret, y_dsa, y_mem, w_up_ret, w_up_dsa, w_up_mem, z)
    out = _out_proj(merged, w_o, x2d, post_norm_w)
    return out.reshape(batch, seq, d)


def kernel(x, mem, pre_norm_w, w_in, ret_gn_w, mem_norm_w, w_mem_kv, w_up_ret, w_up_dsa,
           w_up_mem, w_o, post_norm_w):
    for layer in range(w_in.shape[0]):
        x = _layer(x, mem, pre_norm_w[layer], w_in[layer], ret_gn_w[layer], mem_norm_w[layer],
                   w_mem_kv[layer], w_up_ret[layer], w_up_dsa[layer], w_up_mem[layer],
                   w_o[layer], post_norm_w[layer])
    return x
```

```python
import functools

import jax
import jax.numpy as jnp
from jax import lax
from jax.experimental import pallas as pl
from jax.experimental.pallas import tpu as pltpu

F32 = jnp.float32
BF16 = jnp.bfloat16
I32 = jnp.int32

LANE = 128
SUBLANES = 8
N_BRANCHES = 3
RET_HEADS = 8
HEAD_DIM = 128
DSA_HEADS = 8
IDX_HEADS = 16
IDX_DIM = 64
IDX_TOPK_MAX = 256
MEM_HEADS = 4
MEM_HEAD_DIM = 256
NEG_INF = -1e30
EPS = 1e-6
INT_MIN = -2**31

CB_RQ, CB_RK, CB_RV, CB_RG = 0, 8, 16, 24
CB_AQ, CB_AK, CB_AV, CB_AG = 32, 40, 48, 56
CB_IQ = 64
CB_MQ, CB_MG = 72, 80
CB_GRET, CB_GDSA, CB_GMEM = 88, 104, 120
N_COL_BLOCKS = 136
Z_COLS = N_COL_BLOCKS * LANE
IDX_TAIL_START = (CB_IQ + IDX_HEADS * IDX_DIM // LANE) * LANE
IDX_TAIL_COLS = IDX_DIM + IDX_HEADS

PROJ_TM = 2048
PROJ_TN = 1024
PROJ_ALIGNED_TILES = IDX_TAIL_START // PROJ_TN
PROJ_ROWS = 1024
PROJ_PREP_ROWS = 256
PRENORM_TM = 1024
MEM_TL = 1024
MERGE_TM, MERGE_TN = 512, 1024
OUT_TM = 512
VMEM_LIMIT = 56 * 1024 * 1024
SINGLE_BUFFER = pl.Buffered(1)

NT = (((1,), (1,)), ((), ()))
TN = (((0,), (0,)), ((), ()))


def _sigmoid(v):
    return 1.0 / (1.0 + jnp.exp(-v))


def _silu(v):
    return v * _sigmoid(v)


def _prenorm_kernel(x_ref, w_ref, wikw_ref, h_ref, ikw_ref):
    x = x_ref[...]
    ms = jnp.mean(x * x, axis=-1, keepdims=True)
    h = (x * lax.rsqrt(ms + EPS) * w_ref[...]).astype(h_ref.dtype)
    h_ref[...] = h
    ikw_ref[...] = lax.dot_general(h, wikw_ref[...], NT,
                                   preferred_element_type=F32).astype(ikw_ref.dtype)


def _prenorm(x2d, w, w_ikw):
    m, d = x2d.shape
    tm = min(PRENORM_TM, m)
    return pl.pallas_call(
        _prenorm_kernel,
        grid=(m // tm,),
        in_specs=[pl.BlockSpec((tm, d), lambda i: (i, 0)),
                  pl.BlockSpec((1, d), lambda i: (0, 0)),
                  pl.BlockSpec((LANE, d), lambda i: (0, 0))],
        out_specs=[pl.BlockSpec((tm, d), lambda i: (i, 0)),
                   pl.BlockSpec((tm, LANE), lambda i: (i, 0))],
        out_shape=[jax.ShapeDtypeStruct((m, d), BF16),
                   jax.ShapeDtypeStruct((m, LANE), BF16)],
        compiler_params=pltpu.CompilerParams(dimension_semantics=("arbitrary",)),
        name="prenorm_idx",
    )(x2d, w.reshape(1, d), w_ikw)


def _proj_kernel(h_ref, w_ref, z_ref, wbf_ref):
    @pl.when(pl.program_id(1) == 0)
    def _():
        for r in range(0, w_ref.shape[0], PROJ_PREP_ROWS):
            wbf_ref[r:r + PROJ_PREP_ROWS, :] = w_ref[r:r + PROJ_PREP_ROWS, :].astype(BF16)

    w = wbf_ref[...]
    for r in range(0, h_ref.shape[0], PROJ_ROWS):
        z_ref[r:r + PROJ_ROWS, :] = lax.dot_general(
            h_ref[r:r + PROJ_ROWS, :], w, NT, preferred_element_type=F32).astype(z_ref.dtype)


def _project(h, w_in_t):
    m, d = h.shape
    tm = min(PROJ_TM, m)

    def w_rows(j, i):
        skip = jnp.where(j >= PROJ_ALIGNED_TILES, IDX_TAIL_COLS // SUBLANES, 0)
        return ((j * (PROJ_TN // SUBLANES) + skip) * SUBLANES, 0)

    return pl.pallas_call(
        _proj_kernel,
        grid=(Z_COLS // PROJ_TN, m // tm),
        in_specs=[pl.BlockSpec((tm, d), lambda j, i: (i, 0)),
                  pl.BlockSpec((pl.Element(PROJ_TN), pl.Element(d)), w_rows)],
        out_specs=pl.BlockSpec((tm, PROJ_TN), lambda j, i: (i, j)),
        out_shape=jax.ShapeDtypeStruct((m, Z_COLS), BF16),
        scratch_shapes=[pltpu.VMEM((PROJ_TN, d), BF16)],
        compiler_params=pltpu.CompilerParams(
            dimension_semantics=("arbitrary", "arbitrary"), vmem_limit_bytes=VMEM_LIMIT),
        name="in_proj",
    )(h, w_in_t)


RET_C = 256
RET_HEADS_PER_STEP = 4


def _retention_kernel(lg_ref, q_ref, k_ref, v_ref, g_ref, gw_ref, o_ref):
    c = RET_C
    seq = q_ref.shape[0]
    scale = HEAD_DIM ** -0.5
    ri = lax.broadcasted_iota(I32, (c, c), 0)
    ci = lax.broadcasted_iota(I32, (c, c), 1)
    diff = (ri - ci).astype(F32)
    idx = lax.broadcasted_iota(I32, (c, 1), 0).astype(F32)

    for hh in range(RET_HEADS_PER_STEP):
        hs = slice(hh * HEAD_DIM, (hh + 1) * HEAD_DIM)
        lg = lg_ref[pl.program_id(1) * RET_HEADS_PER_STEP + hh]
        decay = jnp.where(diff >= 0, jnp.exp(lg * jnp.maximum(diff, 0.0)), 0.0) * scale
        q_dec = jnp.exp(lg * (idx + 1.0))
        k_dec = jnp.exp(lg * (c - 1.0 - idx)) * scale
        chunk_dec = jnp.exp(lg * jnp.full((1, HEAD_DIM), float(c), F32))
        gw = gw_ref[:, hs]
        state = jnp.zeros((HEAD_DIM, HEAD_DIM), F32)
        for i in range(seq // c):
            r = i * c
            qi = q_ref[pl.ds(r, c), hs]
            ki = k_ref[pl.ds(r, c), hs]
            vi = v_ref[pl.ds(r, c), hs]
            inner = lax.dot_general(qi, ki, NT, preferred_element_type=F32) * decay
            o = (jnp.dot(inner.astype(BF16), vi, preferred_element_type=F32)
                 + jnp.dot(qi, state.astype(BF16), preferred_element_type=F32) * q_dec)
            vs = (vi.astype(F32) * k_dec).astype(BF16)
            state = state * chunk_dec + lax.dot_general(ki, vs, TN, preferred_element_type=F32)
            mu = jnp.mean(o, axis=-1, keepdims=True)
            oc = o - mu
            var = jnp.mean(oc * oc, axis=-1, keepdims=True)
            on = oc * lax.rsqrt(var + 1e-5) * gw
            g = g_ref[pl.ds(r, c), hs].astype(F32)
            o_ref[pl.ds(r, c), hs] = (_silu(g) * on).astype(o_ref.dtype)


def _retention(z, log_g, gn_w, batch, seq):
    hp = RET_HEADS_PER_STEP
    width = hp * HEAD_DIM
    blk = lambda off: pl.BlockSpec((seq, width), lambda b, h: (b, off // hp + h))
    return pl.pallas_call(
        _retention_kernel,
        grid=(batch, RET_HEADS // hp),
        in_specs=[pl.BlockSpec(memory_space=pltpu.SMEM),
                  blk(CB_RQ), blk(CB_RK), blk(CB_RV), blk(CB_RG),
                  pl.BlockSpec((1, width), lambda b, h: (0, h))],
        out_specs=pl.BlockSpec((seq, width), lambda b, h: (b, h)),
        out_shape=jax.ShapeDtypeStruct((batch * seq, RET_HEADS * HEAD_DIM), BF16),
        compiler_params=pltpu.CompilerParams(
            dimension_semantics=("arbitrary", "arbitrary"), vmem_limit_bytes=VMEM_LIMIT),
        name="retention",
    )(log_g, z, z, z, z, gn_w.reshape(1, -1))


DSA_T = 256
I16 = jnp.int16
PACKED_ROWS = 2 * SUBLANES
VT_ROWS = HEAD_DIM + PACKED_ROWS


def _fold_rows(x, op, rows=SUBLANES):
    while x.shape[0] > rows:
        half = x.shape[0] // 2
        x = op(x[:half], x[half:])
    return x


def _for_tiles_grouped(nk, step):
    def quad(i, carry):
        for u in range(4):
            step(4 * i + u)
        return carry

    lax.fori_loop(0, nk >> 2, quad, 0)
    done = (nk >> 2) << 2

    @pl.when((nk & 2) == 2)
    def _():
        step(done)
        step(done + 1)

    @pl.when((nk & 1) == 1)
    def _():
        step(nk - 1)


def _fold_tiles_grouped(nk, step, carry):
    def quad(i, c):
        for u in range(4):
            c = step(4 * i + u, c)
        return c

    carry = lax.fori_loop(0, nk >> 2, quad, carry)
    done = (nk >> 2) << 2
    carry = lax.cond((nk & 2) == 2, lambda c: step(done + 1, step(done, c)), lambda c: c, carry)
    return lax.cond((nk & 1) == 1, lambda c: step(nk - 1, c), lambda c: c, carry)


ORD_NEG_INF = -2**31 + 0x7FFFFF


def _flip(v):
    return v ^ ((v >> 31) & 0x7FFFFFFF)


def _ordinal_to_f32(o):
    return lax.bitcast_convert_type(_flip(jnp.maximum(o, ORD_NEG_INF)), F32)


def _prefix_to_bf16_bits(p):
    return _flip(jnp.maximum(p, ORD_NEG_INF >> 16) << 16) & -65536


def _dsa_kernel(iq_ref, ikw_ref, aq_ref, ak_ref, av_ref, ag_ref, o_ref,
                ika_ref, ikb_ref, vt_ref, wt_ref, qs_ref, score_ref, score16_ref, bias_ref,
                logit_ref,
                m_ref, acc_ref, *, top_k):
    t = DSA_T
    qb = pl.program_id(1)
    nk = qb + 1
    q0 = pl.multiple_of(qb * t, t)
    nq = vt_ref.shape[0]

    @pl.when(qb == 0)
    def _():
        ikw = ikw_ref[...].astype(F32)
        lane = lax.broadcasted_iota(I32, ikw.shape, 1)
        a = jnp.where(lane < IDX_DIM, ikw, 0.0)
        ika_ref[...] = a.astype(BF16)
        ikb_ref[...] = pltpu.roll(a, IDX_DIM, axis=1).astype(BF16)

        def vt_tile(kt, carry):
            k0 = pl.multiple_of(kt * t, t)
            for h in range(DSA_HEADS):
                hs = slice(h * HEAD_DIM, (h + 1) * HEAD_DIM)
                vt_ref[kt, h, :HEAD_DIM, :] = av_ref[pl.ds(k0, t), hs].astype(F32).T.astype(BF16)
                vt_ref[kt, h, HEAD_DIM:, :] = jnp.ones((VT_ROWS - HEAD_DIM, t), BF16)
            return carry

        lax.fori_loop(0, nq, vt_tile, 0)

    wt_ref[...] = ikw_ref[pl.ds(q0, t), :].astype(F32).T

    key_row = lax.broadcasted_iota(I32, (t, t), 0)
    qry_col = lax.broadcasted_iota(I32, (t, t), 1)

    def score_tile(kt):
        k0 = pl.multiple_of(kt * t, t)
        ka = ika_ref[pl.ds(k0, t), :]
        kb = ikb_ref[pl.ds(k0, t), :]
        acc = jnp.zeros((t, t), F32)
        for j in range(IDX_HEADS // 2):
            qj = iq_ref[:, j * LANE:(j + 1) * LANE]
            for half, kx in enumerate((ka, kb)):
                hh = IDX_DIM + 2 * j + half
                s = lax.dot_general(kx, qj, NT, preferred_element_type=F32)
                acc = acc + jnp.maximum(s, 0.0) * wt_ref[hh:hh + 1, :]
        causal = (key_row + k0) <= (qry_col + q0)
        score = jnp.where(causal, acc, -jnp.inf)
        score_ref[kt] = score
        score16_ref[kt] = score.astype(BF16)

    _for_tiles_grouped(nk, score_tile)

    one16, zero16 = jnp.ones((), I16), jnp.zeros((), I16)

    def prefix_step(i, pfx):
        cand_pfx = pfx + lax.shift_left(jnp.int32(1), 15 - i)
        cand = lax.bitcast_convert_type(_prefix_to_bf16_bits(cand_pfx), F32)
        cand16 = jnp.broadcast_to(cand, (PACKED_ROWS, t)).astype(BF16)[:1]

        def count_tile(kt, cnt):
            ge = jnp.where(score16_ref[kt] >= cand16, one16, zero16)
            return cnt + _fold_rows(ge, jnp.add, PACKED_ROWS)

        cnt = _fold_tiles_grouped(nk, count_tile, jnp.zeros((PACKED_ROWS, t), I16))
        total = jnp.sum(cnt.astype(I32), axis=0, keepdims=True)
        return jnp.where(total >= top_k, cand_pfx, pfx)

    pfx = lax.fori_loop(0, 16, prefix_step, jnp.full((1, t), -2**15, I32))

    def bisect_step(i, bounds):
        lo, hi = bounds
        mid = lo + ((hi - lo) >> 1)
        cand = _ordinal_to_f32(mid)

        def count_tile(kt, cnt):
            return cnt + _fold_rows(jnp.where(score_ref[kt] >= cand, 1, 0), jnp.add)

        cnt = _fold_tiles_grouped(nk, count_tile, jnp.zeros((SUBLANES, t), I32))
        enough = jnp.sum(cnt, axis=0, keepdims=True) >= top_k
        return jnp.where(enough, mid, lo), jnp.where(enough, hi, mid)

    lo, _ = lax.fori_loop(0, 17, bisect_step, (_flip(_prefix_to_bf16_bits(pfx - 1)),
                                               _flip(_prefix_to_bf16_bits(pfx + 1))))
    thr = _ordinal_to_f32(lo)

    def bias_tile(kt, kept):
        k0 = pl.multiple_of(kt * t, t)
        keep = jnp.logical_and(score_ref[kt] >= thr, (key_row + k0) <= (qry_col + q0))
        bias_ref[kt] = jnp.where(keep, 0.0, NEG_INF)
        return kept + _fold_rows(jnp.where(keep, 1, 0), jnp.add)

    kept = lax.fori_loop(0, nk, bias_tile, jnp.zeros((SUBLANES, t), I32))
    kept = jnp.sum(kept, axis=0, keepdims=True)

    @pl.when(jnp.max(kept) > top_k)
    def _():
        strictly_lower = (qry_col < key_row).astype(BF16)

        def tie_tile(kt, carry):
            above, tied_before = carry
            k0 = pl.multiple_of(kt * t, t)
            causal = (key_row + k0) <= (qry_col + q0)
            score = score_ref[kt]
            gt = jnp.logical_and(score > thr, causal)
            eq = jnp.logical_and(score == thr, causal)
            eq01 = jnp.where(eq, 1.0, 0.0)
            tied_here = tied_before + jnp.dot(strictly_lower, eq01.astype(BF16),
                                              preferred_element_type=F32)
            logit_ref[0, kt] = jnp.where(eq, tied_here, -1.0)
            above = above + _fold_rows(jnp.where(gt, 1.0, 0.0), jnp.add)
            return above, tied_before + jnp.sum(eq01, axis=0, keepdims=True)

        above, _ = lax.fori_loop(0, nk, tie_tile,
                                 (jnp.zeros((SUBLANES, t), F32), jnp.zeros((1, t), F32)))
        room = top_k - jnp.sum(above, axis=0, keepdims=True)

        def rebias_tile(kt, carry):
            k0 = pl.multiple_of(kt * t, t)
            gt = jnp.logical_and(score_ref[kt] > thr, (key_row + k0) <= (qry_col + q0))
            rank = logit_ref[0, kt]
            keep = jnp.logical_or(gt, jnp.logical_and(rank >= 0.0, rank < room))
            bias_ref[kt] = jnp.where(keep, 0.0, NEG_INF)
            return carry

        lax.fori_loop(0, nk, rebias_tile, 0)

    m_ref[...] = jnp.full(m_ref.shape, NEG_INF, F32)
    acc_ref[...] = jnp.zeros(acc_ref.shape, F32)
    log2e = 1.4426950408889634
    qs_ref[...] = (aq_ref[...].astype(F32) * ((HEAD_DIM ** -0.5) * log2e)).astype(BF16)
    key_row_1 = lax.broadcasted_iota(I32, (t, LANE), 0)

    def logit_tile(kt):
        k0 = pl.multiple_of(kt * t, t)
        bias = bias_ref[kt]
        kpos = (key_row_1 + (k0 - q0)).astype(F32)
        for h in range(DSA_HEADS):
            slope = (2.0 ** (-8.0 * (h + 1) / DSA_HEADS)) * log2e
            hs = slice(h * HEAD_DIM, (h + 1) * HEAD_DIM)
            s = lax.dot_general(ak_ref[pl.ds(k0, t), hs], qs_ref[:, hs], NT,
                                preferred_element_type=F32)
            alibi = kpos * slope
            lg = s + jnp.concatenate([alibi] * (t // LANE), axis=1) + bias
            logit_ref[h, kt] = lg
            m_ref[h] = jnp.maximum(m_ref[h], _fold_rows(lg, jnp.maximum))

    _for_tiles_grouped(nk, logit_tile)

    for h in range(DSA_HEADS):
        m_ref[h] = jnp.broadcast_to(jnp.max(m_ref[h], axis=0, keepdims=True), (SUBLANES, t))

    def pv_tile(kt):
        for h in range(DSA_HEADS):
            p = jnp.exp2(logit_ref[h, kt] - m_ref[h][:1])
            acc_ref[h] = acc_ref[h] + jnp.dot(vt_ref[kt, h], p.astype(BF16),
                                              preferred_element_type=F32)

    _for_tiles_grouped(nk, pv_tile)

    for h in range(DSA_HEADS):
        hs = slice(h * HEAD_DIM, (h + 1) * HEAD_DIM)
        acc = acc_ref[h]
        o = (acc[:HEAD_DIM] / acc[HEAD_DIM:HEAD_DIM + 1]).T
        o_ref[:, hs] = (_silu(ag_ref[:, hs].astype(F32)) * o).astype(o_ref.dtype)


def _dsa(z, ikw, batch, seq, top_k):
    t = DSA_T
    nq = seq // t
    width = DSA_HEADS * HEAD_DIM
    wblk = width // LANE
    qspec = lambda off: pl.BlockSpec((t, width), lambda b, i: (b * nq + i, off // wblk))
    kvspec = lambda off: pl.BlockSpec((seq, width), lambda b, i: (b, off // wblk))
    return pl.pallas_call(
        functools.partial(_dsa_kernel, top_k=top_k),
        grid=(batch, nq),
        in_specs=[qspec(CB_IQ),
                  pl.BlockSpec((seq, LANE), lambda b, i: (b, 0)),
                  qspec(CB_AQ), kvspec(CB_AK), kvspec(CB_AV), qspec(CB_AG)],
        out_specs=pl.BlockSpec((t, width), lambda b, i: (b * nq + i, 0)),
        out_shape=jax.ShapeDtypeStruct((batch * seq, width), BF16),
        scratch_shapes=[pltpu.VMEM((seq, LANE), BF16),
                        pltpu.VMEM((seq, LANE), BF16),
                        pltpu.VMEM((nq, DSA_HEADS, VT_ROWS, t), BF16),
                        pltpu.VMEM((LANE, t), F32),
                        pltpu.VMEM((t, width), BF16),
                        pltpu.VMEM((nq, t, t), F32),
                        pltpu.VMEM((nq, t, t), BF16),
                        pltpu.VMEM((nq, t, t), F32),
                        pltpu.VMEM((DSA_HEADS, nq, t, t), F32),
                        pltpu.VMEM((DSA_HEADS, SUBLANES, t), F32),
                        pltpu.VMEM((DSA_HEADS, VT_ROWS, t), F32)],
        compiler_params=pltpu.CompilerParams(
            dimension_semantics=("arbitrary", "arbitrary"), vmem_limit_bytes=VMEM_LIMIT),
        name="dsa",
    )(z, ikw, z, z, z, z)


def _memattn_kernel(q_ref, g_ref, mem_ref, nw_ref, w_ref, o_ref, wbf_ref, kv_ref):
    b = pl.program_id(0)
    i = pl.program_id(1)
    width = MEM_HEADS * MEM_HEAD_DIM

    @pl.when(jnp.logical_and(b == 0, i == 0))
    def _():
        for r in range(0, w_ref.shape[0], PROJ_PREP_ROWS):
            wbf_ref[r:r + PROJ_PREP_ROWS, :] = w_ref[r:r + PROJ_PREP_ROWS, :].astype(BF16)

    @pl.when(i == 0)
    def _():
        x = mem_ref[...]
        ms = jnp.mean(x * x, axis=-1, keepdims=True)
        hn = (x * lax.rsqrt(ms + EPS) * nw_ref[...]).astype(BF16)
        kv_ref[...] = jnp.dot(hn, wbf_ref[...], preferred_element_type=F32).astype(BF16)

    scale = MEM_HEAD_DIM ** -0.5
    for h in range(MEM_HEADS):
        hs = slice(h * MEM_HEAD_DIM, (h + 1) * MEM_HEAD_DIM)
        vs = slice(width + h * MEM_HEAD_DIM, width + (h + 1) * MEM_HEAD_DIM)
        s = lax.dot_general(q_ref[:, hs], kv_ref[:, hs], NT, preferred_element_type=F32) * scale
        p = jnp.exp(s - jnp.max(s, axis=-1, keepdims=True))
        l = jnp.sum(p, axis=-1, keepdims=True)
        o = jnp.dot(p.astype(BF16), kv_ref[:, vs], preferred_element_type=F32) / l
        o_ref[:, hs] = (_silu(g_ref[:, hs].astype(F32)) * o).astype(o_ref.dtype)


def _memattn(z, mem2d, norm_w, w_kv, batch, seq, mem_tokens):
    tl = MEM_TL
    d = mem2d.shape[1]
    width = MEM_HEADS * MEM_HEAD_DIM
    wblk = width // LANE
    nl = seq // tl
    return pl.pallas_call(
        _memattn_kernel,
        grid=(batch, nl),
        in_specs=[pl.BlockSpec((tl, width), lambda b, i: (b * nl + i, CB_MQ // wblk)),
                  pl.BlockSpec((tl, width), lambda b, i: (b * nl + i, CB_MG // wblk)),
                  pl.BlockSpec((mem_tokens, d), lambda b, i: (b, 0)),
                  pl.BlockSpec((1, d), lambda b, i: (0, 0)),
                  pl.BlockSpec((d, 2 * width), lambda b, i: (0, 0), pipeline_mode=SINGLE_BUFFER)],
        out_specs=pl.BlockSpec((tl, width), lambda b, i: (b * nl + i, 0)),
        out_shape=jax.ShapeDtypeStruct((batch * seq, width), BF16),
        scratch_shapes=[pltpu.VMEM((d, 2 * width), BF16),
                        pltpu.VMEM((mem_tokens, 2 * width), BF16)],
        compiler_params=pltpu.CompilerParams(
            dimension_semantics=("arbitrary", "arbitrary"), vmem_limit_bytes=VMEM_LIMIT),
        name="mem_attn",
    )(z, z, mem2d, norm_w.reshape(1, d), w_kv)


def _merge_kernel(yr_ref, yd_ref, ym_ref, wr_ref, wd_ref, wm_ref, gr_ref, gd_ref, gm_ref, o_ref,
                  wbf_ref):
    @pl.when(pl.program_id(1) == 0)
    def _():
        for b, w_ref in enumerate((wr_ref, wd_ref, wm_ref)):
            wbf_ref[b] = w_ref[...].astype(BF16)

    def branch(b, y_ref, g_ref):
        up = jnp.dot(y_ref[...], wbf_ref[b], preferred_element_type=F32)
        return _sigmoid(g_ref[...].astype(F32)) * up

    merged = branch(0, yr_ref, gr_ref) + branch(1, yd_ref, gd_ref) + branch(2, ym_ref, gm_ref)
    o_ref[...] = merged.astype(o_ref.dtype)


def _merge(y_ret, y_dsa, y_mem, w_ret, w_dsa, w_mem, z):
    tm, tn = MERGE_TM, MERGE_TN
    m, kdim = y_ret.shape
    n = w_ret.shape[1]
    gblk = tn // LANE
    yspec = pl.BlockSpec((tm, kdim), lambda j, i: (i, 0))
    wspec = pl.BlockSpec((kdim, tn), lambda j, i: (0, j))
    gspec = lambda off: pl.BlockSpec((tm, tn), lambda j, i: (i, off // gblk + j))
    return pl.pallas_call(
        _merge_kernel,
        grid=(n // tn, m // tm),
        in_specs=[yspec, yspec, yspec, wspec, wspec, wspec,
                  gspec(CB_GRET), gspec(CB_GDSA), gspec(CB_GMEM)],
        out_specs=pl.BlockSpec((tm, tn), lambda j, i: (i, j)),
        out_shape=jax.ShapeDtypeStruct((m, n), BF16),
        scratch_shapes=[pltpu.VMEM((N_BRANCHES, kdim, tn), BF16)],
        compiler_params=pltpu.CompilerParams(
            dimension_semantics=("arbitrary", "arbitrary"), vmem_limit_bytes=VMEM_LIMIT),
        name="merge",
    )(y_ret, y_dsa, y_mem, w_ret, w_dsa, w_mem, z, z, z)


def _out_kernel(m_ref, w_ref, x_ref, nw_ref, o_ref, wbf_ref):
    @pl.when(pl.program_id(0) == 0)
    def _():
        for r in range(0, w_ref.shape[0], PROJ_PREP_ROWS):
            wbf_ref[r:r + PROJ_PREP_ROWS, :] = w_ref[r:r + PROJ_PREP_ROWS, :].astype(BF16)

    out = jnp.dot(m_ref[...], wbf_ref[...], preferred_element_type=F32)
    ms = jnp.mean(out * out, axis=-1, keepdims=True)
    o_ref[...] = x_ref[...] + out * lax.rsqrt(ms + EPS) * nw_ref[...]


def _out_proj(merged, w_o, x2d, post_w):
    tm = OUT_TM
    m, d = x2d.shape
    return pl.pallas_call(
        _out_kernel,
        grid=(m // tm,),
        in_specs=[pl.BlockSpec((tm, d), lambda i: (i, 0)),
                  pl.BlockSpec((d, d), lambda i: (0, 0), pipeline_mode=SINGLE_BUFFER),
                  pl.BlockSpec((tm, d), lambda i: (i, 0)),
                  pl.BlockSpec((1, d), lambda i: (0, 0))],
        out_specs=pl.BlockSpec((tm, d), lambda i: (i, 0)),
        out_shape=jax.ShapeDtypeStruct((m, d), F32),
        scratch_shapes=[pltpu.VMEM((d, d), BF16)],
        compiler_params=pltpu.CompilerParams(
            dimension_semantics=("arbitrary",), vmem_limit_bytes=VMEM_LIMIT),
        name="out_proj_postnorm",
    )(merged, w_o, x2d, post_w.reshape(1, d))


def _layer(x, mem, pre_norm_w, w_in, ret_gn_w, mem_norm_w, w_mem_kv,
           w_up_ret, w_up_dsa, w_up_mem, w_o, post_norm_w):
    batch, seq, d = x.shape
    mem_tokens = mem.shape[1]
    top_k = min(IDX_TOPK_MAX, seq // 4)
    assert w_in.shape == (d, Z_COLS + IDX_TAIL_COLS), w_in.shape
    assert seq % max(DSA_T, RET_C, MEM_TL) == 0 and seq < 2**15, seq
    assert (batch * seq) % max(MERGE_TM, OUT_TM) == 0 and top_k <= DSA_T, (batch, seq)
    x2d = x.reshape(batch * seq, d)

    w_in_t = w_in.T
    w_ikw = jnp.pad(w_in_t[IDX_TAIL_START:IDX_TAIL_START + IDX_TAIL_COLS],
                    ((0, LANE - IDX_TAIL_COLS), (0, 0))).astype(BF16)
    h, ikw = _prenorm(x2d, pre_norm_w, w_ikw)
    z = _project(h, w_in_t)

    log_g = jnp.log1p(-(2.0 ** (-5.0 - jnp.arange(RET_HEADS, dtype=F32))))
    y_ret = _retention(z, log_g, ret_gn_w, batch, seq)
    y_dsa = _dsa(z, ikw, batch, seq, top_k)
    y_mem = _memattn(z, mem.reshape(batch * mem_tokens, d), mem_norm_w, w_mem_kv, batch, seq,
                     mem_tokens)

    merged = _merge(y_ret, y_dsa, y_mem, w_up_ret, w_up_dsa, w_up_mem, z)
    out = _out_proj(merged, w_o, x2d, post_norm_w)
    return out.reshape(batch, seq, d)


def kernel(x, mem, pre_norm_w, w_in, ret_gn_w, mem_norm_w, w_mem_kv, w_up_ret, w_up_dsa,
           w_up_mem, w_o, post_norm_w):
    for layer in range(w_in.shape[0]):
        x = _layer(x, mem, pre_norm_w[layer], w_in[layer], ret_gn_w[layer], mem_norm_w[layer],
                   w_mem_kv[layer], w_up_ret[layer], w_up_dsa[layer], w_up_mem[layer],
                   w_o[layer], post_norm_w[layer])
    return x
```

```python
import functools

import jax
import jax.numpy as jnp
from jax import lax
from jax.experimental import pallas as pl
from jax.experimental.pallas import tpu as pltpu

F32 = jnp.float32
BF16 = jnp.bfloat16
I32 = jnp.int32

LANE = 128
SUBLANES = 8
N_BRANCHES = 3
RET_HEADS = 8
HEAD_DIM = 128
DSA_HEADS = 8
IDX_HEADS = 16
IDX_DIM = 64
IDX_TOPK_MAX = 256
MEM_HEADS = 4
MEM_HEAD_DIM = 256
NEG_INF = -1e30
EPS = 1e-6
INT_MIN = -2**31

CB_RQ, CB_RK, CB_RV, CB_RG = 0, 8, 16, 24
CB_AQ, CB_AK, CB_AV, CB_AG = 32, 40, 48, 56
CB_IQ = 64
CB_MQ, CB_MG = 72, 80
CB_GRET, CB_GDSA, CB_GMEM = 88, 104, 120
N_COL_BLOCKS = 136
Z_COLS = N_COL_BLOCKS * LANE
IDX_TAIL_START = (CB_IQ + IDX_HEADS * IDX_DIM // LANE) * LANE
IDX_TAIL_COLS = IDX_DIM + IDX_HEADS

PROJ_TM = 2048
PROJ_TN = 1024
PROJ_ALIGNED_TILES = IDX_TAIL_START // PROJ_TN
PROJ_ROWS = 1024
PROJ_PREP_ROWS = 256
PRENORM_TM = 1024
MEM_TL = 1024
MERGE_TM, MERGE_TN = 512, 1024
OUT_TM = 512
VMEM_LIMIT = 56 * 1024 * 1024
SINGLE_BUFFER = pl.Buffered(1)

NT = (((1,), (1,)), ((), ()))
TN = (((0,), (0,)), ((), ()))


def _sigmoid(v):
    return 1.0 / (1.0 + jnp.exp(-v))


def _silu(v):
    return v * _sigmoid(v)


def _prenorm_kernel(x_ref, w_ref, wikw_ref, h_ref, ikw_ref):
    x = x_ref[...]
    ms = jnp.mean(x * x, axis=-1, keepdims=True)
    h = (x * lax.rsqrt(ms + EPS) * w_ref[...]).astype(h_ref.dtype)
    h_ref[...] = h
    ikw_ref[...] = lax.dot_general(h, wikw_ref[...], NT,
                                   preferred_element_type=F32).astype(ikw_ref.dtype)


def _prenorm(x2d, w, w_ikw):
    m, d = x2d.shape
    tm = min(PRENORM_TM, m)
    return pl.pallas_call(
        _prenorm_kernel,
        grid=(m // tm,),
        in_specs=[pl.BlockSpec((tm, d), lambda i: (i, 0)),
                  pl.BlockSpec((1, d), lambda i: (0, 0)),
                  pl.BlockSpec((LANE, d), lambda i: (0, 0))],
        out_specs=[pl.BlockSpec((tm, d), lambda i: (i, 0)),
                   pl.BlockSpec((tm, LANE), lambda i: (i, 0))],
        out_shape=[jax.ShapeDtypeStruct((m, d), BF16),
                   jax.ShapeDtypeStruct((m, LANE), BF16)],
        compiler_params=pltpu.CompilerParams(dimension_semantics=("arbitrary",)),
        name="prenorm_idx",
    )(x2d, w.reshape(1, d), w_ikw)


def _proj_kernel(h_ref, w_hbm, z_ref, wf32_ref, wbf_ref, sem, *, n_row_tiles):
    j = pl.program_id(0)
    i = pl.program_id(1)
    n_tiles = pl.num_programs(0)

    def tile_copy(tile, slot):
        skip = jnp.where(tile >= PROJ_ALIGNED_TILES, IDX_TAIL_COLS // SUBLANES, 0)
        start = pl.multiple_of((tile * (PROJ_TN // SUBLANES) + skip) * SUBLANES, SUBLANES)
        return pltpu.make_async_copy(w_hbm.at[pl.ds(start, PROJ_TN), :], wf32_ref.at[slot],
                                     sem.at[slot])

    @pl.when(jnp.logical_and(j == 0, i == 0))
    def _():
        tile_copy(0, 0).start()
        tile_copy(1, 1).start()
        tile_copy(0, 0).wait()
        for r in range(0, PROJ_TN, PROJ_PREP_ROWS):
            wbf_ref[0, r:r + PROJ_PREP_ROWS, :] = wf32_ref[0, r:r + PROJ_PREP_ROWS, :].astype(BF16)
        tile_copy(1, 1).wait()

    @pl.when(jnp.logical_and(i == 0, jnp.logical_and(j >= 1, j + 1 < n_tiles)))
    def _():
        tile_copy(j + 1, (j + 1) % 2).wait()

    @pl.when(jnp.logical_and(i == 0, j + 2 < n_tiles))
    def _():
        tile_copy(j + 2, j % 2).start()

    nxt = (j + 1) % 2
    part = PROJ_TN // n_row_tiles
    rows = pl.ds(pl.multiple_of(i * part, part), part)
    wbf_ref[nxt, rows, :] = wf32_ref[nxt, rows, :].astype(BF16)

    w = wbf_ref[j % 2]
    for r in range(0, h_ref.shape[0], PROJ_ROWS):
        z_ref[r:r + PROJ_ROWS, :] = lax.dot_general(
            h_ref[r:r + PROJ_ROWS, :], w, NT, preferred_element_type=F32).astype(z_ref.dtype)


def _project(h, w_in_t):
    m, d = h.shape
    tm = min(PROJ_TM, m)
    n_row_tiles = m // tm
    assert Z_COLS // PROJ_TN >= 2 and PROJ_TN % (n_row_tiles * PACKED_ROWS) == 0
    return pl.pallas_call(
        functools.partial(_proj_kernel, n_row_tiles=n_row_tiles),
        grid=(Z_COLS // PROJ_TN, n_row_tiles),
        in_specs=[pl.BlockSpec((tm, d), lambda j, i: (i, 0)),
                  pl.BlockSpec(memory_space=pl.ANY)],
        out_specs=pl.BlockSpec((tm, PROJ_TN), lambda j, i: (i, j)),
        out_shape=jax.ShapeDtypeStruct((m, Z_COLS), BF16),
        scratch_shapes=[pltpu.VMEM((2, PROJ_TN, d), F32),
                        pltpu.VMEM((2, PROJ_TN, d), BF16),
                        pltpu.SemaphoreType.DMA((2,))],
        compiler_params=pltpu.CompilerParams(
            dimension_semantics=("arbitrary", "arbitrary"), vmem_limit_bytes=VMEM_LIMIT),
        name="in_proj",
    )(h, w_in_t)


RET_C = 256
RET_HEADS_PER_STEP = 4


def _retention_kernel(lg_ref, q_ref, k_ref, v_ref, g_ref, gw_ref, o_ref):
    c = RET_C
    seq = q_ref.shape[0]
    scale = HEAD_DIM ** -0.5
    ri = lax.broadcasted_iota(I32, (c, c), 0)
    ci = lax.broadcasted_iota(I32, (c, c), 1)
    diff = (ri - ci).astype(F32)
    idx = lax.broadcasted_iota(I32, (c, 1), 0).astype(F32)

    for hh in range(RET_HEADS_PER_STEP):
        hs = slice(hh * HEAD_DIM, (hh + 1) * HEAD_DIM)
        lg = lg_ref[pl.program_id(1) * RET_HEADS_PER_STEP + hh]
        decay = jnp.where(diff >= 0, jnp.exp(lg * jnp.maximum(diff, 0.0)), 0.0) * scale
        q_dec = jnp.exp(lg * (idx + 1.0))
        k_dec = jnp.exp(lg * (c - 1.0 - idx)) * scale
        chunk_dec = jnp.exp(lg * jnp.full((1, HEAD_DIM), float(c), F32))
        gw = gw_ref[:, hs]
        state = jnp.zeros((HEAD_DIM, HEAD_DIM), F32)
        for i in range(seq // c):
            r = i * c
            qi = q_ref[pl.ds(r, c), hs]
            ki = k_ref[pl.ds(r, c), hs]
            vi = v_ref[pl.ds(r, c), hs]
            inner = lax.dot_general(qi, ki, NT, preferred_element_type=F32) * decay
            o = (jnp.dot(inner.astype(BF16), vi, preferred_element_type=F32)
                 + jnp.dot(qi, state.astype(BF16), preferred_element_type=F32) * q_dec)
            vs = (vi.astype(F32) * k_dec).astype(BF16)
            state = state * chunk_dec + lax.dot_general(ki, vs, TN, preferred_element_type=F32)
            mu = jnp.mean(o, axis=-1, keepdims=True)
            oc = o - mu
            var = jnp.mean(oc * oc, axis=-1, keepdims=True)
            on = oc * lax.rsqrt(var + 1e-5) * gw
            g = g_ref[pl.ds(r, c), hs].astype(F32)
            o_ref[pl.ds(r, c), hs] = (_silu(g) * on).astype(o_ref.dtype)


def _retention(z, log_g, gn_w, batch, seq):
    hp = RET_HEADS_PER_STEP
    width = hp * HEAD_DIM
    blk = lambda off: pl.BlockSpec((seq, width), lambda b, h: (b, off // hp + h))
    return pl.pallas_call(
        _retention_kernel,
        grid=(batch, RET_HEADS // hp),
        in_specs=[pl.BlockSpec(memory_space=pltpu.SMEM),
                  blk(CB_RQ), blk(CB_RK), blk(CB_RV), blk(CB_RG),
                  pl.BlockSpec((1, width), lambda b, h: (0, h))],
        out_specs=pl.BlockSpec((seq, width), lambda b, h: (b, h)),
        out_shape=jax.ShapeDtypeStruct((batch * seq, RET_HEADS * HEAD_DIM), BF16),
        compiler_params=pltpu.CompilerParams(
            dimension_semantics=("arbitrary", "arbitrary"), vmem_limit_bytes=VMEM_LIMIT),
        name="retention",
    )(log_g, z, z, z, z, gn_w.reshape(1, -1))


DSA_T = 256
I16 = jnp.int16
PACKED_ROWS = 2 * SUBLANES
VT_ROWS = HEAD_DIM + PACKED_ROWS


def _fold_rows(x, op, rows=SUBLANES):
    while x.shape[0] > rows:
        half = x.shape[0] // 2
        x = op(x[:half], x[half:])
    return x


def _for_tiles_grouped(nk, step):
    def quad(i, carry):
        for u in range(4):
            step(4 * i + u)
        return carry

    lax.fori_loop(0, nk >> 2, quad, 0)
    done = (nk >> 2) << 2

    @pl.when((nk & 2) == 2)
    def _():
        step(done)
        step(done + 1)

    @pl.when((nk & 1) == 1)
    def _():
        step(nk - 1)


def _fold_tiles_grouped(nk, step, carry):
    def quad(i, c):
        for u in range(4):
            c = step(4 * i + u, c)
        return c

    carry = lax.fori_loop(0, nk >> 2, quad, carry)
    done = (nk >> 2) << 2
    carry = lax.cond((nk & 2) == 2, lambda c: step(done + 1, step(done, c)), lambda c: c, carry)
    return lax.cond((nk & 1) == 1, lambda c: step(nk - 1, c), lambda c: c, carry)


ORD_NEG_INF = -2**31 + 0x7FFFFF


def _flip(v):
    return v ^ ((v >> 31) & 0x7FFFFFFF)


def _ordinal_to_f32(o):
    return lax.bitcast_convert_type(_flip(jnp.maximum(o, ORD_NEG_INF)), F32)


def _prefix_to_bf16_bits(p):
    return _flip(jnp.maximum(p, ORD_NEG_INF >> 16) << 16) & -65536


def _dsa_kernel(iq_ref, ikw_ref, aq_ref, ak_ref, av_ref, ag_ref, o_ref,
                ika_ref, ikb_ref, vt_ref, wt_ref, qs_ref, score_ref, score16_ref, bias_ref,
                logit_ref,
                m_ref, acc_ref, *, top_k):
    t = DSA_T
    qb = pl.program_id(1)
    nk = qb + 1
    q0 = pl.multiple_of(qb * t, t)
    nq = vt_ref.shape[0]

    @pl.when(qb == 0)
    def _():
        ikw = ikw_ref[...].astype(F32)
        lane = lax.broadcasted_iota(I32, ikw.shape, 1)
        a = jnp.where(lane < IDX_DIM, ikw, 0.0)
        ika_ref[...] = a.astype(BF16)
        ikb_ref[...] = pltpu.roll(a, IDX_DIM, axis=1).astype(BF16)

        def vt_tile(kt, carry):
            k0 = pl.multiple_of(kt * t, t)
            for h in range(DSA_HEADS):
                hs = slice(h * HEAD_DIM, (h + 1) * HEAD_DIM)
                vt_ref[kt, h, :HEAD_DIM, :] = av_ref[pl.ds(k0, t), hs].astype(F32).T.astype(BF16)
                vt_ref[kt, h, HEAD_DIM:, :] = jnp.ones((VT_ROWS - HEAD_DIM, t), BF16)
            return carry

        lax.fori_loop(0, nq, vt_tile, 0)

    wt_ref[...] = ikw_ref[pl.ds(q0, t), :].astype(F32).T

    key_row = lax.broadcasted_iota(I32, (t, t), 0)
    qry_col = lax.broadcasted_iota(I32, (t, t), 1)

    def score_tile(kt):
        k0 = pl.multiple_of(kt * t, t)
        ka = ika_ref[pl.ds(k0, t), :]
        kb = ikb_ref[pl.ds(k0, t), :]
        acc = jnp.zeros((t, t), F32)
        for j in range(IDX_HEADS // 2):
            qj = iq_ref[:, j * LANE:(j + 1) * LANE]
            for half, kx in enumerate((ka, kb)):
                hh = IDX_DIM + 2 * j + half
                s = lax.dot_general(kx, qj, NT, preferred_element_type=F32)
                acc = acc + jnp.maximum(s, 0.0) * wt_ref[hh:hh + 1, :]
        causal = (key_row + k0) <= (qry_col + q0)
        score = jnp.where(causal, acc, -jnp.inf)
        score_ref[kt] = score
        score16_ref[kt] = score.astype(BF16)

    _for_tiles_grouped(nk, score_tile)

    one16, zero16 = jnp.ones((), I16), jnp.zeros((), I16)

    def prefix_step(i, pfx):
        cand_pfx = pfx + lax.shift_left(jnp.int32(1), 15 - i)
        cand = lax.bitcast_convert_type(_prefix_to_bf16_bits(cand_pfx), F32)
        cand16 = jnp.broadcast_to(cand, (PACKED_ROWS, t)).astype(BF16)[:1]

        def count_tile(kt, cnt):
            ge = jnp.where(score16_ref[kt] >= cand16, one16, zero16)
            return cnt + _fold_rows(ge, jnp.add, PACKED_ROWS)

        cnt = _fold_tiles_grouped(nk, count_tile, jnp.zeros((PACKED_ROWS, t), I16))
        total = jnp.sum(cnt.astype(I32), axis=0, keepdims=True)
        return jnp.where(total >= top_k, cand_pfx, pfx)

    pfx = lax.fori_loop(0, 16, prefix_step, jnp.full((1, t), -2**15, I32))

    def bisect_step(i, bounds):
        lo, hi = bounds
        mid = lo + ((hi - lo) >> 1)
        cand = _ordinal_to_f32(mid)

        def count_tile(kt, cnt):
            return cnt + _fold_rows(jnp.where(score_ref[kt] >= cand, 1, 0), jnp.add)

        cnt = _fold_tiles_grouped(nk, count_tile, jnp.zeros((SUBLANES, t), I32))
        enough = jnp.sum(cnt, axis=0, keepdims=True) >= top_k
        return jnp.where(enough, mid, lo), jnp.where(enough, hi, mid)

    lo, _ = lax.fori_loop(0, 17, bisect_step, (_flip(_prefix_to_bf16_bits(pfx - 1)),
                                               _flip(_prefix_to_bf16_bits(pfx + 1))))
    thr = _ordinal_to_f32(lo)

    def bias_tile(kt, kept):
        k0 = pl.multiple_of(kt * t, t)
        keep = jnp.logical_and(score_ref[kt] >= thr, (key_row + k0) <= (qry_col + q0))
        bias_ref[kt] = jnp.where(keep, 0.0, NEG_INF)
        return kept + _fold_rows(jnp.where(keep, 1, 0), jnp.add)

    kept = lax.fori_loop(0, nk, bias_tile, jnp.zeros((SUBLANES, t), I32))
    kept = jnp.sum(kept, axis=0, keepdims=True)

    @pl.when(jnp.max(kept) > top_k)
    def _():
        strictly_lower = (qry_col < key_row).astype(BF16)

        def tie_tile(kt, carry):
            above, tied_before = carry
            k0 = pl.multiple_of(kt * t, t)
            causal = (key_row + k0) <= (qry_col + q0)
            score = score_ref[kt]
            gt = jnp.logical_and(score > thr, causal)
            eq = jnp.logical_and(score == thr, causal)
            eq01 = jnp.where(eq, 1.0, 0.0)
            tied_here = tied_before + jnp.dot(strictly_lower, eq01.astype(BF16),
                                              preferred_element_type=F32)
            logit_ref[0, kt] = jnp.where(eq, tied_here, -1.0)
            above = above + _fold_rows(jnp.where(gt, 1.0, 0.0), jnp.add)
            return above, tied_before + jnp.sum(eq01, axis=0, keepdims=True)

        above, _ = lax.fori_loop(0, nk, tie_tile,
                                 (jnp.zeros((SUBLANES, t), F32), jnp.zeros((1, t), F32)))
        room = top_k - jnp.sum(above, axis=0, keepdims=True)

        def rebias_tile(kt, carry):
            k0 = pl.multiple_of(kt * t, t)
            gt = jnp.logical_and(score_ref[kt] > thr, (key_row + k0) <= (qry_col + q0))
            rank = logit_ref[0, kt]
            keep = jnp.logical_or(gt, jnp.logical_and(rank >= 0.0, rank < room))
            bias_ref[kt] = jnp.where(keep, 0.0, NEG_INF)
            return carry

        lax.fori_loop(0, nk, rebias_tile, 0)

    m_ref[...] = jnp.full(m_ref.shape, NEG_INF, F32)
    acc_ref[...] = jnp.zeros(acc_ref.shape, F32)
    log2e = 1.4426950408889634
    qs_ref[...] = (aq_ref[...].astype(F32) * ((HEAD_DIM ** -0.5) * log2e)).astype(BF16)
    key_row_1 = lax.broadcasted_iota(I32, (t, LANE), 0)

    def logit_tile(kt):
        k0 = pl.multiple_of(kt * t, t)
        bias = bias_ref[kt]
        kpos = (key_row_1 + (k0 - q0)).astype(F32)
        for h in range(DSA_HEADS):
            slope = (2.0 ** (-8.0 * (h + 1) / DSA_HEADS)) * log2e
            hs = slice(h * HEAD_DIM, (h + 1) * HEAD_DIM)
            s = lax.dot_general(ak_ref[pl.ds(k0, t), hs], qs_ref[:, hs], NT,
                                preferred_element_type=F32)
            alibi = kpos * slope
            lg = s + jnp.concatenate([alibi] * (t // LANE), axis=1) + bias
            logit_ref[h, kt] = lg
            m_ref[h] = jnp.maximum(m_ref[h], _fold_rows(lg, jnp.maximum))

    _for_tiles_grouped(nk, logit_tile)

    for h in range(DSA_HEADS):
        m_ref[h] = jnp.broadcast_to(jnp.max(m_ref[h], axis=0, keepdims=True), (SUBLANES, t))

    def pv_tile(kt):
        for h in range(DSA_HEADS):
            p = jnp.exp2(logit_ref[h, kt] - m_ref[h][:1])
            acc_ref[h] = acc_ref[h] + jnp.dot(vt_ref[kt, h], p.astype(BF16),
                                              preferred_element_type=F32)

    _for_tiles_grouped(nk, pv_tile)

    for h in range(DSA_HEADS):
        hs = slice(h * HEAD_DIM, (h + 1) * HEAD_DIM)
        acc = acc_ref[h]
        o = (acc[:HEAD_DIM] / acc[HEAD_DIM:HEAD_DIM + 1]).T
        o_ref[:, hs] = (_silu(ag_ref[:, hs].astype(F32)) * o).astype(o_ref.dtype)


def _dsa(z, ikw, batch, seq, top_k):
    t = DSA_T
    nq = seq // t
    width = DSA_HEADS * HEAD_DIM
    wblk = width // LANE
    qspec = lambda off: pl.BlockSpec((t, width), lambda b, i: (b * nq + i, off // wblk))
    kvspec = lambda off: pl.BlockSpec((seq, width), lambda b, i: (b, off // wblk))
    return pl.pallas_call(
        functools.partial(_dsa_kernel, top_k=top_k),
        grid=(batch, nq),
        in_specs=[qspec(CB_IQ),
                  pl.BlockSpec((seq, LANE), lambda b, i: (b, 0)),
                  qspec(CB_AQ), kvspec(CB_AK), kvspec(CB_AV), qspec(CB_AG)],
        out_specs=pl.BlockSpec((t, width), lambda b, i: (b * nq + i, 0)),
        out_shape=jax.ShapeDtypeStruct((batch * seq, width), BF16),
        scratch_shapes=[pltpu.VMEM((seq, LANE), BF16),
                        pltpu.VMEM((seq, LANE), BF16),
                        pltpu.VMEM((nq, DSA_HEADS, VT_ROWS, t), BF16),
                        pltpu.VMEM((LANE, t), F32),
                        pltpu.VMEM((t, width), BF16),
                        pltpu.VMEM((nq, t, t), F32),
                        pltpu.VMEM((nq, t, t), BF16),
                        pltpu.VMEM((nq, t, t), F32),
                        pltpu.VMEM((DSA_HEADS, nq, t, t), F32),
                        pltpu.VMEM((DSA_HEADS, SUBLANES, t), F32),
                        pltpu.VMEM((DSA_HEADS, VT_ROWS, t), F32)],
        compiler_params=pltpu.CompilerParams(
            dimension_semantics=("arbitrary", "arbitrary"), vmem_limit_bytes=VMEM_LIMIT),
        name="dsa",
    )(z, ikw, z, z, z, z)


def _memattn_kernel(q_ref, g_ref, mem_ref, nw_ref, w_ref, o_ref, wbf_ref, kv_ref):
    b = pl.program_id(0)
    i = pl.program_id(1)
    width = MEM_HEADS * MEM_HEAD_DIM

    @pl.when(jnp.logical_and(b == 0, i == 0))
    def _():
        for r in range(0, w_ref.shape[0], PROJ_PREP_ROWS):
            wbf_ref[r:r + PROJ_PREP_ROWS, :] = w_ref[r:r + PROJ_PREP_ROWS, :].astype(BF16)

    @pl.when(i == 0)
    def _():
        x = mem_ref[...]
        ms = jnp.mean(x * x, axis=-1, keepdims=True)
        hn = (x * lax.rsqrt(ms + EPS) * nw_ref[...]).astype(BF16)
        kv_ref[...] = jnp.dot(hn, wbf_ref[...], preferred_element_type=F32).astype(BF16)

    scale = MEM_HEAD_DIM ** -0.5
    for h in range(MEM_HEADS):
        hs = slice(h * MEM_HEAD_DIM, (h + 1) * MEM_HEAD_DIM)
        vs = slice(width + h * MEM_HEAD_DIM, width + (h + 1) * MEM_HEAD_DIM)
        s = lax.dot_general(q_ref[:, hs], kv_ref[:, hs], NT, preferred_element_type=F32) * scale
        p = jnp.exp(s - jnp.max(s, axis=-1, keepdims=True))
        l = jnp.sum(p, axis=-1, keepdims=True)
        o = jnp.dot(p.astype(BF16), kv_ref[:, vs], preferred_element_type=F32) / l
        o_ref[:, hs] = (_silu(g_ref[:, hs].astype(F32)) * o).astype(o_ref.dtype)


def _memattn(z, mem2d, norm_w, w_kv, batch, seq, mem_tokens):
    tl = MEM_TL
    d = mem2d.shape[1]
    width = MEM_HEADS * MEM_HEAD_DIM
    wblk = width // LANE
    nl = seq // tl
    return pl.pallas_call(
        _memattn_kernel,
        grid=(batch, nl),
        in_specs=[pl.BlockSpec((tl, width), lambda b, i: (b * nl + i, CB_MQ // wblk)),
                  pl.BlockSpec((tl, width), lambda b, i: (b * nl + i, CB_MG // wblk)),
                  pl.BlockSpec((mem_tokens, d), lambda b, i: (b, 0)),
                  pl.BlockSpec((1, d), lambda b, i: (0, 0)),
                  pl.BlockSpec((d, 2 * width), lambda b, i: (0, 0), pipeline_mode=SINGLE_BUFFER)],
        out_specs=pl.BlockSpec((tl, width), lambda b, i: (b * nl + i, 0)),
        out_shape=jax.ShapeDtypeStruct((batch * seq, width), BF16),
        scratch_shapes=[pltpu.VMEM((d, 2 * width), BF16),
                        pltpu.VMEM((mem_tokens, 2 * width), BF16)],
        compiler_params=pltpu.CompilerParams(
            dimension_semantics=("arbitrary", "arbitrary"), vmem_limit_bytes=VMEM_LIMIT),
        name="mem_attn",
    )(z, z, mem2d, norm_w.reshape(1, d), w_kv)


def _merge_kernel(yr_ref, yd_ref, ym_ref, wr_ref, wd_ref, wm_ref, gr_ref, gd_ref, gm_ref, o_ref,
                  wbf_ref):
    @pl.when(pl.program_id(1) == 0)
    def _():
        for b, w_ref in enumerate((wr_ref, wd_ref, wm_ref)):
            wbf_ref[b] = w_ref[...].astype(BF16)

    def branch(b, y_ref, g_ref):
        up = jnp.dot(y_ref[...], wbf_ref[b], preferred_element_type=F32)
        return _sigmoid(g_ref[...].astype(F32)) * up

    merged = branch(0, yr_ref, gr_ref) + branch(1, yd_ref, gd_ref) + branch(2, ym_ref, gm_ref)
    o_ref[...] = merged.astype(o_ref.dtype)


def _merge(y_ret, y_dsa, y_mem, w_ret, w_dsa, w_mem, z):
    tm, tn = MERGE_TM, MERGE_TN
    m, kdim = y_ret.shape
    n = w_ret.shape[1]
    gblk = tn // LANE
    yspec = pl.BlockSpec((tm, kdim), lambda j, i: (i, 0))
    wspec = pl.BlockSpec((kdim, tn), lambda j, i: (0, j))
    gspec = lambda off: pl.BlockSpec((tm, tn), lambda j, i: (i, off // gblk + j))
    return pl.pallas_call(
        _merge_kernel,
        grid=(n // tn, m // tm),
        in_specs=[yspec, yspec, yspec, wspec, wspec, wspec,
                  gspec(CB_GRET), gspec(CB_GDSA), gspec(CB_GMEM)],
        out_specs=pl.BlockSpec((tm, tn), lambda j, i: (i, j)),
        out_shape=jax.ShapeDtypeStruct((m, n), BF16),
        scratch_shapes=[pltpu.VMEM((N_BRANCHES, kdim, tn), BF16)],
        compiler_params=pltpu.CompilerParams(
            dimension_semantics=("arbitrary", "arbitrary"), vmem_limit_bytes=VMEM_LIMIT),
        name="merge",
    )(y_ret, y_dsa, y_mem, w_ret, w_dsa, w_mem, z, z, z)


def _out_kernel(m_ref, w_ref, x_ref, nw_ref, o_ref, wbf_ref):
    @pl.when(pl.program_id(0) == 0)
    def _():
        for r in range(0, w_ref.shape[0], PROJ_PREP_ROWS):
            wbf_ref[r:r + PROJ_PREP_ROWS, :] = w_ref[r:r + PROJ_PREP_ROWS, :].astype(BF16)

    out = jnp.dot(m_ref[...], wbf_ref[...], preferred_element_type=F32)
    ms = jnp.mean(out * out, axis=-1, keepdims=True)
    o_ref[...] = x_ref[...] + out * lax.rsqrt(ms + EPS) * nw_ref[...]


def _out_proj(merged, w_o, x2d, post_w):
    tm = OUT_TM
    m, d = x2d.shape
    return pl.pallas_call(
        _out_kernel,
        grid=(m // tm,),
        in_specs=[pl.BlockSpec((tm, d), lambda i: (i, 0)),
                  pl.BlockSpec((d, d), lambda i: (0, 0), pipeline_mode=SINGLE_BUFFER),
                  pl.BlockSpec((tm, d), lambda i: (i, 0)),
                  pl.BlockSpec((1, d), lambda i: (0, 0))],
        out_specs=pl.BlockSpec((tm, d), lambda i: (i, 0)),
        out_shape=jax.ShapeDtypeStruct((m, d), F32),
        scratch_shapes=[pltpu.VMEM((d, d), BF16)],
        compiler_params=pltpu.CompilerParams(
            dimension_semantics=("arbitrary",), vmem_limit_bytes=VMEM_LIMIT),
        name="out_proj_postnorm",
    )(merged, w_o, x2d, post_w.reshape(1, d))


def _layer(x, mem, pre_norm_w, w_in, ret_gn_w, mem_norm_w, w_mem_kv,
           w_up_ret, w_up_dsa, w_up_mem, w_o, post_norm_w):
    batch, seq, d = x.shape
    mem_tokens = mem.shape[1]
    top_k = min(IDX_TOPK_MAX, seq // 4)
    assert w_in.shape == (d, Z_COLS + IDX_TAIL_COLS), w_in.shape
    assert seq % max(DSA_T, RET_C, MEM_TL) == 0 and seq < 2**15, seq
    assert (batch * seq) % max(MERGE_TM, OUT_TM) == 0 and top_k <= DSA_T, (batch, seq)
    x2d = x.reshape(batch * seq, d)

    w_in_t = w_in.T
    w_ikw = jnp.pad(w_in_t[IDX_TAIL_START:IDX_TAIL_START + IDX_TAIL_COLS],
                    ((0, LANE - IDX_TAIL_COLS), (0, 0))).astype(BF16)
    h, ikw = _prenorm(x2d, pre_norm_w, w_ikw)
    z = _project(h, w_in_t)

    log_g = jnp.log1p(-(2.0 ** (-5.0 - jnp.arange(RET_HEADS, dtype=F32))))
    y_ret = _retention(z, log_g, ret_gn_w, batch, seq)
    y_dsa = _dsa(z, ikw, batch, seq, top_k)
    y_mem = _memattn(z, mem.reshape(batch * mem_tokens, d), mem_norm_w, w_mem_kv, batch, seq,
                     mem_tokens)

    merged = _merge(y_ret, y_dsa, y_mem, w_up_ret, w_up_dsa, w_up_mem, z)
    out = _out_proj(merged, w_o, x2d, post_norm_w)
    return out.reshape(batch, seq, d)


def kernel(x, mem, pre_norm_w, w_in, ret_gn_w, mem_norm_w, w_mem_kv, w_up_ret, w_up_dsa,
           w_up_mem, w_o, post_norm_w):
    for layer in range(w_in.shape[0]):
        x = _layer(x, mem, pre_norm_w[layer], w_in[layer], ret_gn_w[layer], mem_norm_w[layer],
                   w_mem_kv[layer], w_up_ret[layer], w_up_dsa[layer], w_up_mem[layer],
                   w_o[layer], post_norm_w[layer])
    return x
```

```python
import functools

import jax
import jax.numpy as jnp
from jax import lax
from jax.experimental import pallas as pl
from jax.experimental.pallas import tpu as pltpu

F32 = jnp.float32
BF16 = jnp.bfloat16
I32 = jnp.int32

LANE = 128
SUBLANES = 8
N_BRANCHES = 3
RET_HEADS = 8
HEAD_DIM = 128
DSA_HEADS = 8
IDX_HEADS = 16
IDX_DIM = 64
IDX_TOPK_MAX = 256
MEM_HEADS = 4
MEM_HEAD_DIM = 256
NEG_INF = -1e30
EPS = 1e-6
INT_MIN = -2**31

CB_RQ, CB_RK, CB_RV, CB_RG = 0, 8, 16, 24
CB_AQ, CB_AK, CB_AV, CB_AG = 32, 40, 48, 56
CB_IQ = 64
CB_MQ, CB_MG = 72, 80
CB_GRET, CB_GDSA, CB_GMEM = 88, 104, 120
N_COL_BLOCKS = 136
Z_COLS = N_COL_BLOCKS * LANE
IDX_TAIL_START = (CB_IQ + IDX_HEADS * IDX_DIM // LANE) * LANE
IDX_TAIL_COLS = IDX_DIM + IDX_HEADS

PROJ_TM = 2048
PROJ_TN = 1024
PROJ_ALIGNED_TILES = IDX_TAIL_START // PROJ_TN
PROJ_ROWS = 1024
PROJ_PREP_ROWS = 256
PRENORM_TM = 1024
MEM_TL = 1024
MERGE_TM, MERGE_TN = 512, 1024
OUT_TM = 512
VMEM_LIMIT = 56 * 1024 * 1024
SINGLE_BUFFER = pl.Buffered(1)

NT = (((1,), (1,)), ((), ()))
TN = (((0,), (0,)), ((), ()))


def _sigmoid(v):
    return 1.0 / (1.0 + jnp.exp(-v))


def _silu(v):
    return v * _sigmoid(v)


def _prenorm_kernel(x_ref, w_ref, wikw_ref, h_ref, ikw_ref):
    x = x_ref[...]
    ms = jnp.mean(x * x, axis=-1, keepdims=True)
    h = (x * lax.rsqrt(ms + EPS) * w_ref[...]).astype(h_ref.dtype)
    h_ref[...] = h
    ikw_ref[...] = lax.dot_general(h, wikw_ref[...], NT,
                                   preferred_element_type=F32).astype(ikw_ref.dtype)


def _prenorm(x2d, w, w_ikw):
    m, d = x2d.shape
    tm = min(PRENORM_TM, m)
    return pl.pallas_call(
        _prenorm_kernel,
        grid=(m // tm,),
        in_specs=[pl.BlockSpec((tm, d), lambda i: (i, 0)),
                  pl.BlockSpec((1, d), lambda i: (0, 0)),
                  pl.BlockSpec((LANE, d), lambda i: (0, 0))],
        out_specs=[pl.BlockSpec((tm, d), lambda i: (i, 0)),
                   pl.BlockSpec((tm, LANE), lambda i: (i, 0))],
        out_shape=[jax.ShapeDtypeStruct((m, d), BF16),
                   jax.ShapeDtypeStruct((m, LANE), BF16)],
        compiler_params=pltpu.CompilerParams(dimension_semantics=("arbitrary",)),
        name="prenorm_idx",
    )(x2d, w.reshape(1, d), w_ikw)


def _proj_kernel(h_ref, w_ref, z_ref, wbf_ref):
    @pl.when(pl.program_id(1) == 0)
    def _():
        for r in range(0, w_ref.shape[0], PROJ_PREP_ROWS):
            wbf_ref[r:r + PROJ_PREP_ROWS, :] = w_ref[r:r + PROJ_PREP_ROWS, :].astype(BF16)

    w = wbf_ref[...]
    for r in range(0, h_ref.shape[0], PROJ_ROWS):
        z_ref[r:r + PROJ_ROWS, :] = lax.dot_general(
            h_ref[r:r + PROJ_ROWS, :], w, NT, preferred_element_type=F32).astype(z_ref.dtype)


def _project(h, w_in_t):
    m, d = h.shape
    tm = min(PROJ_TM, m)

    def w_rows(j, i):
        skip = jnp.where(j >= PROJ_ALIGNED_TILES, IDX_TAIL_COLS // SUBLANES, 0)
        return ((j * (PROJ_TN // SUBLANES) + skip) * SUBLANES, 0)

    return pl.pallas_call(
        _proj_kernel,
        grid=(Z_COLS // PROJ_TN, m // tm),
        in_specs=[pl.BlockSpec((tm, d), lambda j, i: (i, 0)),
                  pl.BlockSpec((pl.Element(PROJ_TN), pl.Element(d)), w_rows)],
        out_specs=pl.BlockSpec((tm, PROJ_TN), lambda j, i: (i, j)),
        out_shape=jax.ShapeDtypeStruct((m, Z_COLS), BF16),
        scratch_shapes=[pltpu.VMEM((PROJ_TN, d), BF16)],
        compiler_params=pltpu.CompilerParams(
            dimension_semantics=("arbitrary", "arbitrary"), vmem_limit_bytes=VMEM_LIMIT),
        name="in_proj",
    )(h, w_in_t)


RET_C = 256
RET_HEADS_PER_STEP = 4


def _retention_kernel(lg_ref, q_ref, k_ref, v_ref, g_ref, gw_ref, o_ref):
    c = RET_C
    seq = q_ref.shape[0]
    scale = HEAD_DIM ** -0.5
    ri = lax.broadcasted_iota(I32, (c, c), 0)
    ci = lax.broadcasted_iota(I32, (c, c), 1)
    diff = (ri - ci).astype(F32)
    idx = lax.broadcasted_iota(I32, (c, 1), 0).astype(F32)

    for hh in range(RET_HEADS_PER_STEP):
        hs = slice(hh * HEAD_DIM, (hh + 1) * HEAD_DIM)
        lg = lg_ref[pl.program_id(1) * RET_HEADS_PER_STEP + hh]
        decay = jnp.where(diff >= 0, jnp.exp(lg * jnp.maximum(diff, 0.0)), 0.0) * scale
        q_dec = jnp.exp(lg * (idx + 1.0))
        k_dec = jnp.exp(lg * (c - 1.0 - idx)) * scale
        chunk_dec = jnp.exp(lg * jnp.full((1, HEAD_DIM), float(c), F32))
        gw = gw_ref[:, hs]
        state = jnp.zeros((HEAD_DIM, HEAD_DIM), F32)
        for i in range(seq // c):
            r = i * c
            qi = q_ref[pl.ds(r, c), hs]
            ki = k_ref[pl.ds(r, c), hs]
            vi = v_ref[pl.ds(r, c), hs]
            inner = lax.dot_general(qi, ki, NT, preferred_element_type=F32) * decay
            o = (jnp.dot(inner.astype(BF16), vi, preferred_element_type=F32)
                 + jnp.dot(qi, state.astype(BF16), preferred_element_type=F32) * q_dec)
            vs = (vi.astype(F32) * k_dec).astype(BF16)
            state = state * chunk_dec + lax.dot_general(ki, vs, TN, preferred_element_type=F32)
            mu = jnp.mean(o, axis=-1, keepdims=True)
            oc = o - mu
            var = jnp.mean(oc * oc, axis=-1, keepdims=True)
            on = oc * lax.rsqrt(var + 1e-5) * gw
            g = g_ref[pl.ds(r, c), hs].astype(F32)
            o_ref[pl.ds(r, c), hs] = (_silu(g) * on).astype(o_ref.dtype)


def _retention(z, log_g, gn_w, batch, seq):
    hp = RET_HEADS_PER_STEP
    width = hp * HEAD_DIM
    blk = lambda off: pl.BlockSpec((seq, width), lambda b, h: (b, off // hp + h))
    return pl.pallas_call(
        _retention_kernel,
        grid=(batch, RET_HEADS // hp),
        in_specs=[pl.BlockSpec(memory_space=pltpu.SMEM),
                  blk(CB_RQ), blk(CB_RK), blk(CB_RV), blk(CB_RG),
                  pl.BlockSpec((1, width), lambda b, h: (0, h))],
        out_specs=pl.BlockSpec((seq, width), lambda b, h: (b, h)),
        out_shape=jax.ShapeDtypeStruct((batch * seq, RET_HEADS * HEAD_DIM), BF16),
        compiler_params=pltpu.CompilerParams(
            dimension_semantics=("arbitrary", "arbitrary"), vmem_limit_bytes=VMEM_LIMIT),
        name="retention",
    )(log_g, z, z, z, z, gn_w.reshape(1, -1))


DSA_T = 256
I16 = jnp.int16
PACKED_ROWS = 2 * SUBLANES
VT_ROWS = HEAD_DIM + PACKED_ROWS


def _fold_rows(x, op, rows=SUBLANES):
    while x.shape[0] > rows:
        half = x.shape[0] // 2
        x = op(x[:half], x[half:])
    return x


def _for_tiles_grouped(nk, step):
    def quad(i, carry):
        for u in range(4):
            step(4 * i + u)
        return carry

    lax.fori_loop(0, nk >> 2, quad, 0)
    done = (nk >> 2) << 2

    @pl.when((nk & 2) == 2)
    def _():
        step(done)
        step(done + 1)

    @pl.when((nk & 1) == 1)
    def _():
        step(nk - 1)


def _fold_tiles_grouped(nk, step, carry):
    def quad(i, c):
        for u in range(4):
            c = step(4 * i + u, c)
        return c

    carry = lax.fori_loop(0, nk >> 2, quad, carry)
    done = (nk >> 2) << 2
    carry = lax.cond((nk & 2) == 2, lambda c: step(done + 1, step(done, c)), lambda c: c, carry)
    return lax.cond((nk & 1) == 1, lambda c: step(nk - 1, c), lambda c: c, carry)


ORD_NEG_INF = -2**31 + 0x7FFFFF


def _flip(v):
    return v ^ ((v >> 31) & 0x7FFFFFFF)


def _ordinal_to_f32(o):
    return lax.bitcast_convert_type(_flip(jnp.maximum(o, ORD_NEG_INF)), F32)


def _prefix_to_bf16_bits(p):
    return _flip(jnp.maximum(p, ORD_NEG_INF >> 16) << 16) & -65536


def _dsa_kernel(iq_ref, ikw_ref, aq_ref, ak_ref, av_ref, ag_ref, o_ref,
                ika_ref, ikb_ref, vt_ref, wt_ref, qs_ref, score_ref, score16_ref, bias_ref,
                logit_ref,
                m_ref, acc_ref, *, top_k):
    t = DSA_T
    qb = pl.program_id(1)
    nk = qb + 1
    q0 = pl.multiple_of(qb * t, t)
    nq = vt_ref.shape[0]

    @pl.when(qb == 0)
    def _():
        ikw = ikw_ref[...].astype(F32)
        lane = lax.broadcasted_iota(I32, ikw.shape, 1)
        a = jnp.where(lane < IDX_DIM, ikw, 0.0)
        ika_ref[...] = a.astype(BF16)
        ikb_ref[...] = pltpu.roll(a, IDX_DIM, axis=1).astype(BF16)

        def vt_tile(kt, carry):
            k0 = pl.multiple_of(kt * t, t)
            for h in range(DSA_HEADS):
                hs = slice(h * HEAD_DIM, (h + 1) * HEAD_DIM)
                vt_ref[kt, h, :HEAD_DIM, :] = av_ref[pl.ds(k0, t), hs].astype(F32).T.astype(BF16)
                vt_ref[kt, h, HEAD_DIM:, :] = jnp.ones((VT_ROWS - HEAD_DIM, t), BF16)
            return carry

        lax.fori_loop(0, nq, vt_tile, 0)

    wt_ref[...] = ikw_ref[pl.ds(q0, t), :].astype(F32).T

    key_row = lax.broadcasted_iota(I32, (t, t), 0)
    qry_col = lax.broadcasted_iota(I32, (t, t), 1)

    def score_tile(kt):
        k0 = pl.multiple_of(kt * t, t)
        ka = ika_ref[pl.ds(k0, t), :]
        kb = ikb_ref[pl.ds(k0, t), :]
        acc = jnp.zeros((t, t), F32)
        for j in range(IDX_HEADS // 2):
            qj = iq_ref[:, j * LANE:(j + 1) * LANE]
            for half, kx in enumerate((ka, kb)):
                hh = IDX_DIM + 2 * j + half
                s = lax.dot_general(kx, qj, NT, preferred_element_type=F32)
                acc = acc + jnp.maximum(s, 0.0) * wt_ref[hh:hh + 1, :]
        causal = (key_row + k0) <= (qry_col + q0)
        score = jnp.where(causal, acc, -jnp.inf)
        score_ref[kt] = score
        score16_ref[kt] = score.astype(BF16)

    _for_tiles_grouped(nk, score_tile)

    one16, zero16 = jnp.ones((), I16), jnp.zeros((), I16)

    def prefix_step(i, pfx):
        cand_pfx = pfx + lax.shift_left(jnp.int32(1), 15 - i)
        cand = lax.bitcast_convert_type(_prefix_to_bf16_bits(cand_pfx), F32)
        cand16 = jnp.broadcast_to(cand, (PACKED_ROWS, t)).astype(BF16)[:1]

        def count_tile(kt, cnt):
            ge = jnp.where(score16_ref[kt] >= cand16, one16, zero16)
            return cnt + _fold_rows(ge, jnp.add, PACKED_ROWS)

        cnt = _fold_tiles_grouped(nk, count_tile, jnp.zeros((PACKED_ROWS, t), I16))
        total = jnp.sum(cnt.astype(I32), axis=0, keepdims=True)
        return jnp.where(total >= top_k, cand_pfx, pfx)

    pfx = lax.fori_loop(0, 16, prefix_step, jnp.full((1, t), -2**15, I32))

    def bisect_step(i, state):
        lo, hi, kept = state
        mid = lo + ((hi - lo) >> 1)
        cand = _ordinal_to_f32(mid)

        def count_tile(kt, cnt):
            return cnt + _fold_rows(jnp.where(score_ref[kt] >= cand, 1, 0), jnp.add)

        cnt = _fold_tiles_grouped(nk, count_tile, jnp.zeros((SUBLANES, t), I32))
        total = jnp.sum(cnt, axis=0, keepdims=True)
        enough = total >= top_k
        return (jnp.where(enough, mid, lo), jnp.where(enough, hi, mid),
                jnp.where(enough, total, kept))

    lo, _, kept = lax.fori_loop(
        0, 17, bisect_step, (_flip(_prefix_to_bf16_bits(pfx - 1)),
                             _flip(_prefix_to_bf16_bits(pfx + 1)),
                             jnp.full((1, t), top_k + 1, I32)))
    thr = _ordinal_to_f32(lo)

    def bias_tile(kt, carry):
        bias_ref[kt] = jnp.where(score_ref[kt] >= thr, 0.0, NEG_INF)
        return carry

    lax.fori_loop(0, qb, bias_tile, 0)
    keep = jnp.logical_and(score_ref[qb] >= thr, key_row <= qry_col)
    bias_ref[qb] = jnp.where(keep, 0.0, NEG_INF)

    @pl.when(jnp.max(kept) > top_k)
    def _():
        strictly_lower = (qry_col < key_row).astype(BF16)

        def tie_tile(kt, carry):
            above, tied_before = carry
            k0 = pl.multiple_of(kt * t, t)
            causal = (key_row + k0) <= (qry_col + q0)
            score = score_ref[kt]
            gt = jnp.logical_and(score > thr, causal)
            eq = jnp.logical_and(score == thr, causal)
            eq01 = jnp.where(eq, 1.0, 0.0)
            tied_here = tied_before + jnp.dot(strictly_lower, eq01.astype(BF16),
                                              preferred_element_type=F32)
            logit_ref[0, kt] = jnp.where(eq, tied_here, -1.0)
            above = above + _fold_rows(jnp.where(gt, 1.0, 0.0), jnp.add)
            return above, tied_before + jnp.sum(eq01, axis=0, keepdims=True)

        above, _ = lax.fori_loop(0, nk, tie_tile,
                                 (jnp.zeros((SUBLANES, t), F32), jnp.zeros((1, t), F32)))
        room = top_k - jnp.sum(above, axis=0, keepdims=True)

        def rebias_tile(kt, carry):
            k0 = pl.multiple_of(kt * t, t)
            gt = jnp.logical_and(score_ref[kt] > thr, (key_row + k0) <= (qry_col + q0))
            rank = logit_ref[0, kt]
            keep = jnp.logical_or(gt, jnp.logical_and(rank >= 0.0, rank < room))
            bias_ref[kt] = jnp.where(keep, 0.0, NEG_INF)
            return carry

        lax.fori_loop(0, nk, rebias_tile, 0)

    m_ref[...] = jnp.full(m_ref.shape, NEG_INF, F32)
    acc_ref[...] = jnp.zeros(acc_ref.shape, F32)
    log2e = 1.4426950408889634
    qs_ref[...] = (aq_ref[...].astype(F32) * ((HEAD_DIM ** -0.5) * log2e)).astype(BF16)
    key_row_1 = lax.broadcasted_iota(I32, (t, LANE), 0)

    def logit_tile(kt):
        k0 = pl.multiple_of(kt * t, t)
        bias = bias_ref[kt]
        kpos = (key_row_1 + (k0 - q0)).astype(F32)
        for h in range(DSA_HEADS):
            slope = (2.0 ** (-8.0 * (h + 1) / DSA_HEADS)) * log2e
            hs = slice(h * HEAD_DIM, (h + 1) * HEAD_DIM)
            s = lax.dot_general(ak_ref[pl.ds(k0, t), hs], qs_ref[:, hs], NT,
                                preferred_element_type=F32)
            alibi = kpos * slope
            lg = s + jnp.concatenate([alibi] * (t // LANE), axis=1) + bias
            logit_ref[h, kt] = lg
            m_ref[h] = jnp.maximum(m_ref[h], _fold_rows(lg, jnp.maximum))

    _for_tiles_grouped(nk, logit_tile)

    for h in range(DSA_HEADS):
        m_ref[h] = jnp.broadcast_to(jnp.max(m_ref[h], axis=0, keepdims=True), (SUBLANES, t))

    def pv_tile(kt):
        for h in range(DSA_HEADS):
            p = jnp.exp2(logit_ref[h, kt] - m_ref[h][:1])
            acc_ref[h] = acc_ref[h] + jnp.dot(vt_ref[kt, h], p.astype(BF16),
                                              preferred_element_type=F32)

    _for_tiles_grouped(nk, pv_tile)

    for h in range(DSA_HEADS):
        hs = slice(h * HEAD_DIM, (h + 1) * HEAD_DIM)
        acc = acc_ref[h]
        o = (acc[:HEAD_DIM] / acc[HEAD_DIM:HEAD_DIM + 1]).T
        o_ref[:, hs] = (_silu(ag_ref[:, hs].astype(F32)) * o).astype(o_ref.dtype)


def _dsa(z, ikw, batch, seq, top_k):
    t = DSA_T
    nq = seq // t
    width = DSA_HEADS * HEAD_DIM
    wblk = width // LANE
    qspec = lambda off: pl.BlockSpec((t, width), lambda b, i: (b * nq + i, off // wblk))
    kvspec = lambda off: pl.BlockSpec((seq, width), lambda b, i: (b, off // wblk))
    return pl.pallas_call(
        functools.partial(_dsa_kernel, top_k=top_k),
        grid=(batch, nq),
        in_specs=[qspec(CB_IQ),
                  pl.BlockSpec((seq, LANE), lambda b, i: (b, 0)),
                  qspec(CB_AQ), kvspec(CB_AK), kvspec(CB_AV), qspec(CB_AG)],
        out_specs=pl.BlockSpec((t, width), lambda b, i: (b * nq + i, 0)),
        out_shape=jax.ShapeDtypeStruct((batch * seq, width), BF16),
        scratch_shapes=[pltpu.VMEM((seq, LANE), BF16),
                        pltpu.VMEM((seq, LANE), BF16),
                        pltpu.VMEM((nq, DSA_HEADS, VT_ROWS, t), BF16),
                        pltpu.VMEM((LANE, t), F32),
                        pltpu.VMEM((t, width), BF16),
                        pltpu.VMEM((nq, t, t), F32),
                        pltpu.VMEM((nq, t, t), BF16),
                        pltpu.VMEM((nq, t, t), F32),
                        pltpu.VMEM((DSA_HEADS, nq, t, t), F32),
                        pltpu.VMEM((DSA_HEADS, SUBLANES, t), F32),
                        pltpu.VMEM((DSA_HEADS, VT_ROWS, t), F32)],
        compiler_params=pltpu.CompilerParams(
            dimension_semantics=("arbitrary", "arbitrary"), vmem_limit_bytes=VMEM_LIMIT),
        name="dsa",
    )(z, ikw, z, z, z, z)


def _memattn_kernel(q_ref, g_ref, mem_ref, nw_ref, w_ref, o_ref, wbf_ref, kv_ref):
    b = pl.program_id(0)
    i = pl.program_id(1)
    width = MEM_HEADS * MEM_HEAD_DIM

    @pl.when(jnp.logical_and(b == 0, i == 0))
    def _():
        for r in range(0, w_ref.shape[0], PROJ_PREP_ROWS):
            wbf_ref[r:r + PROJ_PREP_ROWS, :] = w_ref[r:r + PROJ_PREP_ROWS, :].astype(BF16)

    @pl.when(i == 0)
    def _():
        x = mem_ref[...]
        ms = jnp.mean(x * x, axis=-1, keepdims=True)
        hn = (x * lax.rsqrt(ms + EPS) * nw_ref[...]).astype(BF16)
        kv_ref[...] = jnp.dot(hn, wbf_ref[...], preferred_element_type=F32).astype(BF16)

    scale = MEM_HEAD_DIM ** -0.5
    for h in range(MEM_HEADS):
        hs = slice(h * MEM_HEAD_DIM, (h + 1) * MEM_HEAD_DIM)
        vs = slice(width + h * MEM_HEAD_DIM, width + (h + 1) * MEM_HEAD_DIM)
        s = lax.dot_general(q_ref[:, hs], kv_ref[:, hs], NT, preferred_element_type=F32) * scale
        p = jnp.exp(s - jnp.max(s, axis=-1, keepdims=True))
        l = jnp.sum(p, axis=-1, keepdims=True)
        o = jnp.dot(p.astype(BF16), kv_ref[:, vs], preferred_element_type=F32) / l
        o_ref[:, hs] = (_silu(g_ref[:, hs].astype(F32)) * o).astype(o_ref.dtype)


def _memattn(z, mem2d, norm_w, w_kv, batch, seq, mem_tokens):
    tl = MEM_TL
    d = mem2d.shape[1]
    width = MEM_HEADS * MEM_HEAD_DIM
    wblk = width // LANE
    nl = seq // tl
    return pl.pallas_call(
        _memattn_kernel,
        grid=(batch, nl),
        in_specs=[pl.BlockSpec((tl, width), lambda b, i: (b * nl + i, CB_MQ // wblk)),
                  pl.BlockSpec((tl, width), lambda b, i: (b * nl + i, CB_MG // wblk)),
                  pl.BlockSpec((mem_tokens, d), lambda b, i: (b, 0)),
                  pl.BlockSpec((1, d), lambda b, i: (0, 0)),
                  pl.BlockSpec((d, 2 * width), lambda b, i: (0, 0), pipeline_mode=SINGLE_BUFFER)],
        out_specs=pl.BlockSpec((tl, width), lambda b, i: (b * nl + i, 0)),
        out_shape=jax.ShapeDtypeStruct((batch * seq, width), BF16),
        scratch_shapes=[pltpu.VMEM((d, 2 * width), BF16),
                        pltpu.VMEM((mem_tokens, 2 * width), BF16)],
        compiler_params=pltpu.CompilerParams(
            dimension_semantics=("arbitrary", "arbitrary"), vmem_limit_bytes=VMEM_LIMIT),
        name="mem_attn",
    )(z, z, mem2d, norm_w.reshape(1, d), w_kv)


def _merge_kernel(yr_ref, yd_ref, ym_ref, wr_ref, wd_ref, wm_ref, gr_ref, gd_ref, gm_ref, o_ref,
                  wbf_ref):
    @pl.when(pl.program_id(1) == 0)
    def _():
        for b, w_ref in enumerate((wr_ref, wd_ref, wm_ref)):
            wbf_ref[b] = w_ref[...].astype(BF16)

    def branch(b, y_ref, g_ref):
        up = jnp.dot(y_ref[...], wbf_ref[b], preferred_element_type=F32)
        return _sigmoid(g_ref[...].astype(F32)) * up

    merged = branch(0, yr_ref, gr_ref) + branch(1, yd_ref, gd_ref) + branch(2, ym_ref, gm_ref)
    o_ref[...] = merged.astype(o_ref.dtype)


def _merge(y_ret, y_dsa, y_mem, w_ret, w_dsa, w_mem, z):
    tm, tn = MERGE_TM, MERGE_TN
    m, kdim = y_ret.shape
    n = w_ret.shape[1]
    gblk = tn // LANE
    yspec = pl.BlockSpec((tm, kdim), lambda j, i: (i, 0))
    wspec = pl.BlockSpec((kdim, tn), lambda j, i: (0, j))
    gspec = lambda off: pl.BlockSpec((tm, tn), lambda j, i: (i, off // gblk + j))
    return pl.pallas_call(
        _merge_kernel,
        grid=(n // tn, m // tm),
        in_specs=[yspec, yspec, yspec, wspec, wspec, wspec,
                  gspec(CB_GRET), gspec(CB_GDSA), gspec(CB_GMEM)],
        out_specs=pl.BlockSpec((tm, tn), lambda j, i: (i, j)),
        out_shape=jax.ShapeDtypeStruct((m, n), BF16),
        scratch_shapes=[pltpu.VMEM((N_BRANCHES, kdim, tn), BF16)],
        compiler_params=pltpu.CompilerParams(
            dimension_semantics=("arbitrary", "arbitrary"), vmem_limit_bytes=VMEM_LIMIT),
        name="merge",
    )(y_ret, y_dsa, y_mem, w_ret, w_dsa, w_mem, z, z, z)


def _out_kernel(m_ref, w_ref, x_ref, nw_ref, o_ref, wbf_ref):
    @pl.when(pl.program_id(0) == 0)
    def _():
        for r in range(0, w_ref.shape[0], PROJ_PREP_ROWS):
            wbf_ref[r:r + PROJ_PREP_ROWS, :] = w_ref[r:r + PROJ_PREP_ROWS, :].astype(BF16)

    out = jnp.dot(m_ref[...], wbf_ref[...], preferred_element_type=F32)
    ms = jnp.mean(out * out, axis=-1, keepdims=True)
    o_ref[...] = x_ref[...] + out * lax.rsqrt(ms + EPS) * nw_ref[...]


def _out_proj(merged, w_o, x2d, post_w):
    tm = OUT_TM
    m, d = x2d.shape
    return pl.pallas_call(
        _out_kernel,
        grid=(m // tm,),
        in_specs=[pl.BlockSpec((tm, d), lambda i: (i, 0)),
                  pl.BlockSpec((d, d), lambda i: (0, 0), pipeline_mode=SINGLE_BUFFER),
                  pl.BlockSpec((tm, d), lambda i: (i, 0)),
                  pl.BlockSpec((1, d), lambda i: (0, 0))],
        out_specs=pl.BlockSpec((tm, d), lambda i: (i, 0)),
        out_shape=jax.ShapeDtypeStruct((m, d), F32),
        scratch_shapes=[pltpu.VMEM((d, d), BF16)],
        compiler_params=pltpu.CompilerParams(
            dimension_semantics=("arbitrary",), vmem_limit_bytes=VMEM_LIMIT),
        name="out_proj_postnorm",
    )(merged, w_o, x2d, post_w.reshape(1, d))


def _layer(x, mem, pre_norm_w, w_in, ret_gn_w, mem_norm_w, w_mem_kv,
           w_up_ret, w_up_dsa, w_up_mem, w_o, post_norm_w):
    batch, seq, d = x.shape
    mem_tokens = mem.shape[1]
    top_k = min(IDX_TOPK_MAX, seq // 4)
    assert w_in.shape == (d, Z_COLS + IDX_TAIL_COLS), w_in.shape
    assert seq % max(DSA_T, RET_C, MEM_TL) == 0 and seq < 2**15, seq
    assert (batch * seq) % max(MERGE_TM, OUT_TM) == 0 and top_k <= DSA_T, (batch, seq)
    x2d = x.reshape(batch * seq, d)

    w_in_t = w_in.T
    w_ikw = jnp.pad(w_in_t[IDX_TAIL_START:IDX_TAIL_START + IDX_TAIL_COLS],
                    ((0, LANE - IDX_TAIL_COLS), (0, 0))).astype(BF16)
    h, ikw = _prenorm(x2d, pre_norm_w, w_ikw)
    z = _project(h, w_in_t)

    log_g = jnp.log1p(-(2.0 ** (-5.0 - jnp.arange(RET_HEADS, dtype=F32))))
    y_ret = _retention(z, log_g, ret_gn_w, batch, seq)
    y_dsa = _dsa(z, ikw, batch, seq, top_k)
    y_mem = _memattn(z, mem.reshape(batch * mem_tokens, d), mem_norm_w, w_mem_kv, batch, seq,
                     mem_tokens)

    merged = _merge(y_ret, y_dsa, y_mem, w_up_ret, w_up_dsa, w_up_mem, z)
    out = _out_proj(merged, w_o, x2d, post_norm_w)
    return out.reshape(batch, seq, d)


def kernel(x, mem, pre_norm_w, w_in, ret_gn_w, mem_norm_w, w_mem_kv, w_up_ret, w_up_dsa,
           w_up_mem, w_o, post_norm_w):
    for layer in range(w_in.shape[0]):
        x = _layer(x, mem, pre_norm_w[layer], w_in[layer], ret_gn_w[layer], mem_norm_w[layer],
                   w_mem_kv[layer], w_up_ret[layer], w_up_dsa[layer], w_up_mem[layer],
                   w_o[layer], post_norm_w[layer])
    return x
```

```python
import functools

import jax
import jax.numpy as jnp
from jax import lax
from jax.experimental import pallas as pl
from jax.experimental.pallas import tpu as pltpu

F32 = jnp.float32
BF16 = jnp.bfloat16
I32 = jnp.int32

LANE = 128
SUBLANES = 8
N_BRANCHES = 3
RET_HEADS = 8
HEAD_DIM = 128
DSA_HEADS = 8
IDX_HEADS = 16
IDX_DIM = 64
IDX_TOPK_MAX = 256
MEM_HEADS = 4
MEM_HEAD_DIM = 256
NEG_INF = -1e30
EPS = 1e-6
INT_MIN = -2**31

CB_RQ, CB_RK, CB_RV, CB_RG = 0, 8, 16, 24
CB_AQ, CB_AK, CB_AV, CB_AG = 32, 40, 48, 56
CB_IQ = 64
CB_MQ, CB_MG = 72, 80
CB_GRET, CB_GDSA, CB_GMEM = 88, 104, 120
N_COL_BLOCKS = 136
Z_COLS = N_COL_BLOCKS * LANE
IDX_TAIL_START = (CB_IQ + IDX_HEADS * IDX_DIM // LANE) * LANE
IDX_TAIL_COLS = IDX_DIM + IDX_HEADS

PROJ_TM = 2048
PROJ_TN = 1024
PROJ_ALIGNED_TILES = IDX_TAIL_START // PROJ_TN
PROJ_ROWS = 1024
PROJ_PREP_ROWS = 256
PRENORM_TM = 1024
MEM_TL = 1024
MERGE_TM, MERGE_TN = 512, 1024
OUT_TM = 512
VMEM_LIMIT = 56 * 1024 * 1024
SINGLE_BUFFER = pl.Buffered(1)

NT = (((1,), (1,)), ((), ()))
TN = (((0,), (0,)), ((), ()))


def _sigmoid(v):
    return 1.0 / (1.0 + jnp.exp(-v))


def _silu(v):
    return v * _sigmoid(v)


def _prenorm_kernel(x_ref, w_ref, wikw_ref, h_ref, ikw_ref):
    x = x_ref[...]
    ms = jnp.mean(x * x, axis=-1, keepdims=True)
    h = (x * lax.rsqrt(ms + EPS) * w_ref[...]).astype(h_ref.dtype)
    h_ref[...] = h
    ikw_ref[...] = lax.dot_general(h, wikw_ref[...], NT,
                                   preferred_element_type=F32).astype(ikw_ref.dtype)


def _prenorm(x2d, w, w_ikw):
    m, d = x2d.shape
    tm = min(PRENORM_TM, m)
    return pl.pallas_call(
        _prenorm_kernel,
        grid=(m // tm,),
        in_specs=[pl.BlockSpec((tm, d), lambda i: (i, 0)),
                  pl.BlockSpec((1, d), lambda i: (0, 0)),
                  pl.BlockSpec((LANE, d), lambda i: (0, 0))],
        out_specs=[pl.BlockSpec((tm, d), lambda i: (i, 0)),
                   pl.BlockSpec((tm, LANE), lambda i: (i, 0))],
        out_shape=[jax.ShapeDtypeStruct((m, d), BF16),
                   jax.ShapeDtypeStruct((m, LANE), BF16)],
        compiler_params=pltpu.CompilerParams(dimension_semantics=("arbitrary",)),
        name="prenorm_idx",
    )(x2d, w.reshape(1, d), w_ikw)


def _proj_kernel(h_ref, w_ref, z_ref, wbf_ref):
    @pl.when(pl.program_id(1) == 0)
    def _():
        for r in range(0, w_ref.shape[0], PROJ_PREP_ROWS):
            wbf_ref[r:r + PROJ_PREP_ROWS, :] = w_ref[r:r + PROJ_PREP_ROWS, :].astype(BF16)

    w = wbf_ref[...]
    for r in range(0, h_ref.shape[0], PROJ_ROWS):
        z_ref[r:r + PROJ_ROWS, :] = lax.dot_general(
            h_ref[r:r + PROJ_ROWS, :], w, NT, preferred_element_type=F32).astype(z_ref.dtype)


def _project(h, w_in_t):
    m, d = h.shape
    tm = min(PROJ_TM, m)

    def w_rows(j, i):
        skip = jnp.where(j >= PROJ_ALIGNED_TILES, IDX_TAIL_COLS // SUBLANES, 0)
        return ((j * (PROJ_TN // SUBLANES) + skip) * SUBLANES, 0)

    return pl.pallas_call(
        _proj_kernel,
        grid=(Z_COLS // PROJ_TN, m // tm),
        in_specs=[pl.BlockSpec((tm, d), lambda j, i: (i, 0)),
                  pl.BlockSpec((pl.Element(PROJ_TN), pl.Element(d)), w_rows)],
        out_specs=pl.BlockSpec((tm, PROJ_TN), lambda j, i: (i, j)),
        out_shape=jax.ShapeDtypeStruct((m, Z_COLS), BF16),
        scratch_shapes=[pltpu.VMEM((PROJ_TN, d), BF16)],
        compiler_params=pltpu.CompilerParams(
            dimension_semantics=("arbitrary", "arbitrary"), vmem_limit_bytes=VMEM_LIMIT),
        name="in_proj",
    )(h, w_in_t)


RET_C = 256
RET_HEADS_PER_STEP = 4


def _retention_kernel(lg_ref, q_ref, k_ref, v_ref, g_ref, gw_ref, o_ref):
    c = RET_C
    seq = q_ref.shape[0]
    scale = HEAD_DIM ** -0.5
    ri = lax.broadcasted_iota(I32, (c, c), 0)
    ci = lax.broadcasted_iota(I32, (c, c), 1)
    diff = (ri - ci).astype(F32)
    idx = lax.broadcasted_iota(I32, (c, 1), 0).astype(F32)

    for hh in range(RET_HEADS_PER_STEP):
        hs = slice(hh * HEAD_DIM, (hh + 1) * HEAD_DIM)
        lg = lg_ref[pl.program_id(1) * RET_HEADS_PER_STEP + hh]
        decay = jnp.where(diff >= 0, jnp.exp(lg * jnp.maximum(diff, 0.0)), 0.0) * scale
        q_dec = jnp.exp(lg * (idx + 1.0))
        k_dec = jnp.exp(lg * (c - 1.0 - idx)) * scale
        chunk_dec = jnp.exp(lg * jnp.full((1, HEAD_DIM), float(c), F32))
        gw = gw_ref[:, hs]
        state = jnp.zeros((HEAD_DIM, HEAD_DIM), F32)
        for i in range(seq // c):
            r = i * c
            qi = q_ref[pl.ds(r, c), hs]
            ki = k_ref[pl.ds(r, c), hs]
            vi = v_ref[pl.ds(r, c), hs]
            inner = lax.dot_general(qi, ki, NT, preferred_element_type=F32) * decay
            o = (jnp.dot(inner.astype(BF16), vi, preferred_element_type=F32)
                 + jnp.dot(qi, state.astype(BF16), preferred_element_type=F32) * q_dec)
            vs = (vi.astype(F32) * k_dec).astype(BF16)
            state = state * chunk_dec + lax.dot_general(ki, vs, TN, preferred_element_type=F32)
            mu = jnp.mean(o, axis=-1, keepdims=True)
            oc = o - mu
            var = jnp.mean(oc * oc, axis=-1, keepdims=True)
            on = oc * lax.rsqrt(var + 1e-5) * gw
            g = g_ref[pl.ds(r, c), hs].astype(F32)
            o_ref[pl.ds(r, c), hs] = (_silu(g) * on).astype(o_ref.dtype)


def _retention(z, log_g, gn_w, batch, seq):
    hp = RET_HEADS_PER_STEP
    width = hp * HEAD_DIM
    blk = lambda off: pl.BlockSpec((seq, width), lambda b, h: (b, off // hp + h))
    return pl.pallas_call(
        _retention_kernel,
        grid=(batch, RET_HEADS // hp),
        in_specs=[pl.BlockSpec(memory_space=pltpu.SMEM),
                  blk(CB_RQ), blk(CB_RK), blk(CB_RV), blk(CB_RG),
                  pl.BlockSpec((1, width), lambda b, h: (0, h))],
        out_specs=pl.BlockSpec((seq, width), lambda b, h: (b, h)),
        out_shape=jax.ShapeDtypeStruct((batch * seq, RET_HEADS * HEAD_DIM), BF16),
        compiler_params=pltpu.CompilerParams(
            dimension_semantics=("arbitrary", "arbitrary"), vmem_limit_bytes=VMEM_LIMIT),
        name="retention",
    )(log_g, z, z, z, z, gn_w.reshape(1, -1))


DSA_T = 256
I16 = jnp.int16
PACKED_ROWS = 2 * SUBLANES
VT_ROWS = HEAD_DIM + PACKED_ROWS


def _fold_rows(x, op, rows=SUBLANES):
    while x.shape[0] > rows:
        half = x.shape[0] // 2
        x = op(x[:half], x[half:])
    return x


def _for_tiles_grouped(nk, step):
    def quad(i, carry):
        for u in range(4):
            step(4 * i + u)
        return carry

    lax.fori_loop(0, nk >> 2, quad, 0)
    done = (nk >> 2) << 2

    @pl.when((nk & 2) == 2)
    def _():
        step(done)
        step(done + 1)

    @pl.when((nk & 1) == 1)
    def _():
        step(nk - 1)


def _fold_tiles_grouped(nk, step, carry):
    def quad(i, c):
        for u in range(4):
            c = step(4 * i + u, c)
        return c

    carry = lax.fori_loop(0, nk >> 2, quad, carry)
    done = (nk >> 2) << 2
    carry = lax.cond((nk & 2) == 2, lambda c: step(done + 1, step(done, c)), lambda c: c, carry)
    return lax.cond((nk & 1) == 1, lambda c: step(nk - 1, c), lambda c: c, carry)


ORD_NEG_INF = -2**31 + 0x7FFFFF


def _flip(v):
    return v ^ ((v >> 31) & 0x7FFFFFFF)


def _ordinal_to_f32(o):
    return lax.bitcast_convert_type(_flip(jnp.maximum(o, ORD_NEG_INF)), F32)


def _prefix_to_bf16_bits(p):
    return _flip(jnp.maximum(p, ORD_NEG_INF >> 16) << 16) & -65536


def _dsa_kernel(iq_ref, ikw_ref, aq_ref, ak_ref, av_ref, ag_ref, o_ref,
                ika_ref, ikb_ref, vt_ref, wt_ref, qs_ref, score_ref, score16_ref, bias_ref,
                logit_ref,
                m_ref, acc_ref, *, top_k):
    t = DSA_T
    qb = pl.program_id(1)
    nk = qb + 1
    q0 = pl.multiple_of(qb * t, t)
    nq = vt_ref.shape[0]

    @pl.when(qb == 0)
    def _():
        ikw = ikw_ref[...].astype(F32)
        lane = lax.broadcasted_iota(I32, ikw.shape, 1)
        a = jnp.where(lane < IDX_DIM, ikw, 0.0)
        ika_ref[...] = a.astype(BF16)
        ikb_ref[...] = pltpu.roll(a, IDX_DIM, axis=1).astype(BF16)

        def vt_tile(kt, carry):
            k0 = pl.multiple_of(kt * t, t)
            for h in range(DSA_HEADS):
                hs = slice(h * HEAD_DIM, (h + 1) * HEAD_DIM)
                vt_ref[kt, h, :HEAD_DIM, :] = av_ref[pl.ds(k0, t), hs].astype(F32).T.astype(BF16)
                vt_ref[kt, h, HEAD_DIM:, :] = jnp.ones((VT_ROWS - HEAD_DIM, t), BF16)
            return carry

        lax.fori_loop(0, nq, vt_tile, 0)

    wt_ref[...] = ikw_ref[pl.ds(q0, t), :].astype(F32).T

    key_row = lax.broadcasted_iota(I32, (t, t), 0)
    qry_col = lax.broadcasted_iota(I32, (t, t), 1)

    def score_tile(kt):
        k0 = pl.multiple_of(kt * t, t)
        ka = ika_ref[pl.ds(k0, t), :]
        kb = ikb_ref[pl.ds(k0, t), :]
        acc = jnp.zeros((t, t), F32)
        for j in range(IDX_HEADS // 2):
            qj = iq_ref[:, j * LANE:(j + 1) * LANE]
            for half, kx in enumerate((ka, kb)):
                hh = IDX_DIM + 2 * j + half
                s = lax.dot_general(kx, qj, NT, preferred_element_type=F32)
                acc = acc + jnp.maximum(s, 0.0) * wt_ref[hh:hh + 1, :]
        causal = (key_row + k0) <= (qry_col + q0)
        score = jnp.where(causal, acc, -jnp.inf)
        score_ref[kt] = score
        score16_ref[kt] = score.astype(BF16)

    _for_tiles_grouped(nk, score_tile)

    one16, zero16 = jnp.ones((), I16), jnp.zeros((), I16)

    def prefix_step(i, pfx):
        cand_pfx = pfx + lax.shift_left(jnp.int32(1), 15 - i)
        cand = lax.bitcast_convert_type(_prefix_to_bf16_bits(cand_pfx), F32)
        cand16 = jnp.broadcast_to(cand, (PACKED_ROWS, t)).astype(BF16)[:1]

        def count_tile(kt, cnt):
            ge = jnp.where(score16_ref[kt] >= cand16, one16, zero16)
            return cnt + _fold_rows(ge, jnp.add, PACKED_ROWS)

        cnt = _fold_tiles_grouped(nk, count_tile, jnp.zeros((PACKED_ROWS, t), I16))
        total = jnp.sum(cnt.astype(I32), axis=0, keepdims=True)
        return jnp.where(total >= top_k, cand_pfx, pfx)

    pfx = lax.fori_loop(0, 16, prefix_step, jnp.full((1, t), -2**15, I32))

    def bisect_step(i, state):
        lo, hi, kept = state
        mid = lo + ((hi - lo) >> 1)
        cand = _ordinal_to_f32(mid)

        def count_tile(kt, cnt):
            return cnt + _fold_rows(jnp.where(score_ref[kt] >= cand, 1, 0), jnp.add)

        cnt = _fold_tiles_grouped(nk, count_tile, jnp.zeros((SUBLANES, t), I32))
        total = jnp.sum(cnt, axis=0, keepdims=True)
        enough = total >= top_k
        return (jnp.where(enough, mid, lo), jnp.where(enough, hi, mid),
                jnp.where(enough, total, kept))

    lo, _, kept = lax.fori_loop(
        0, 17, bisect_step, (_flip(_prefix_to_bf16_bits(pfx - 1)),
                             _flip(_prefix_to_bf16_bits(pfx + 1)),
                             jnp.full((1, t), top_k + 1, I32)))
    thr = _ordinal_to_f32(lo)

    def bias_tile(kt, carry):
        bias_ref[kt] = jnp.where(score_ref[kt] >= thr, 0.0, NEG_INF)
        return carry

    lax.fori_loop(0, qb, bias_tile, 0)
    keep = jnp.logical_and(score_ref[qb] >= thr, key_row <= qry_col)
    bias_ref[qb] = jnp.where(keep, 0.0, NEG_INF)

    @pl.when(jnp.max(kept) > top_k)
    def _():
        strictly_lower = (qry_col < key_row).astype(BF16)

        def tie_tile(kt, carry):
            above, tied_before = carry
            k0 = pl.multiple_of(kt * t, t)
            causal = (key_row + k0) <= (qry_col + q0)
            score = score_ref[kt]
            gt = jnp.logical_and(score > thr, causal)
            eq = jnp.logical_and(score == thr, causal)
            eq01 = jnp.where(eq, 1.0, 0.0)
            tied_here = tied_before + jnp.dot(strictly_lower, eq01.astype(BF16),
                                              preferred_element_type=F32)
            logit_ref[0, kt] = jnp.where(eq, tied_here, -1.0)
            above = above + _fold_rows(jnp.where(gt, 1.0, 0.0), jnp.add)
            return above, tied_before + jnp.sum(eq01, axis=0, keepdims=True)

        above, _ = lax.fori_loop(0, nk, tie_tile,
                                 (jnp.zeros((SUBLANES, t), F32), jnp.zeros((1, t), F32)))
        room = top_k - jnp.sum(above, axis=0, keepdims=True)

        def rebias_tile(kt, carry):
            k0 = pl.multiple_of(kt * t, t)
            gt = jnp.logical_and(score_ref[kt] > thr, (key_row + k0) <= (qry_col + q0))
            rank = logit_ref[0, kt]
            keep = jnp.logical_or(gt, jnp.logical_and(rank >= 0.0, rank < room))
            bias_ref[kt] = jnp.where(keep, 0.0, NEG_INF)
            return carry

        lax.fori_loop(0, nk, rebias_tile, 0)

    m_ref[...] = jnp.full(m_ref.shape, NEG_INF, F32)
    acc_ref[...] = jnp.zeros(acc_ref.shape, F32)
    log2e = 1.4426950408889634
    qs_ref[...] = (aq_ref[...].astype(F32) * ((HEAD_DIM ** -0.5) * log2e)).astype(BF16)
    key_row_1 = lax.broadcasted_iota(I32, (t, LANE), 0)

    def logit_tile(kt):
        k0 = pl.multiple_of(kt * t, t)
        bias = bias_ref[kt]
        kpos = (key_row_1 + (k0 - q0)).astype(F32)
        for h in range(DSA_HEADS):
            slope = (2.0 ** (-8.0 * (h + 1) / DSA_HEADS)) * log2e
            hs = slice(h * HEAD_DIM, (h + 1) * HEAD_DIM)
            s = lax.dot_general(ak_ref[pl.ds(k0, t), hs], qs_ref[:, hs], NT,
                                preferred_element_type=F32)
            alibi = kpos * slope
            lg = s + jnp.concatenate([alibi] * (t // LANE), axis=1) + bias
            logit_ref[h, kt] = lg
            m_ref[h] = jnp.maximum(m_ref[h], _fold_rows(lg, jnp.maximum))

    _for_tiles_grouped(nk, logit_tile)

    for h in range(DSA_HEADS):
        m_ref[h] = jnp.broadcast_to(jnp.max(m_ref[h], axis=0, keepdims=True), (SUBLANES, t))

    def pv_tile(kt):
        for h in range(DSA_HEADS):
            p = jnp.exp2(logit_ref[h, kt] - m_ref[h][:1])
            acc_ref[h] = acc_ref[h] + jnp.dot(vt_ref[kt, h], p.astype(BF16),
                                              preferred_element_type=F32)

    _for_tiles_grouped(nk, pv_tile)

    for h in range(DSA_HEADS):
        hs = slice(h * HEAD_DIM, (h + 1) * HEAD_DIM)
        acc = acc_ref[h]
        o = (acc[:HEAD_DIM] / acc[HEAD_DIM:HEAD_DIM + 1]).T
        o_ref[:, hs] = (_silu(ag_ref[:, hs].astype(F32)) * o).astype(o_ref.dtype)


def _dsa(z, ikw, batch, seq, top_k):
    t = DSA_T
    nq = seq // t
    width = DSA_HEADS * HEAD_DIM
    wblk = width // LANE
    qspec = lambda off: pl.BlockSpec((t, width), lambda b, i: (b * nq + i, off // wblk))
    kvspec = lambda off: pl.BlockSpec((seq, width), lambda b, i: (b, off // wblk))
    return pl.pallas_call(
        functools.partial(_dsa_kernel, top_k=top_k),
        grid=(batch, nq),
        in_specs=[qspec(CB_IQ),
                  pl.BlockSpec((seq, LANE), lambda b, i: (b, 0)),
                  qspec(CB_AQ), kvspec(CB_AK), kvspec(CB_AV), qspec(CB_AG)],
        out_specs=pl.BlockSpec((t, width), lambda b, i: (b * nq + i, 0)),
        out_shape=jax.ShapeDtypeStruct((batch * seq, width), BF16),
        scratch_shapes=[pltpu.VMEM((seq, LANE), BF16),
                        pltpu.VMEM((seq, LANE), BF16),
                        pltpu.VMEM((nq, DSA_HEADS, VT_ROWS, t), BF16),
                        pltpu.VMEM((LANE, t), F32),
                        pltpu.VMEM((t, width), BF16),
                        pltpu.VMEM((nq, t, t), F32),
                        pltpu.VMEM((nq, t, t), BF16),
                        pltpu.VMEM((nq, t, t), F32),
                        pltpu.VMEM((DSA_HEADS, nq, t, t), F32),
                        pltpu.VMEM((DSA_HEADS, SUBLANES, t), F32),
                        pltpu.VMEM((DSA_HEADS, VT_ROWS, t), F32)],
        compiler_params=pltpu.CompilerParams(
            dimension_semantics=("arbitrary", "arbitrary"), vmem_limit_bytes=VMEM_LIMIT),
        name="dsa",
    )(z, ikw, z, z, z, z)


def _memattn_kernel(q_ref, g_ref, mem_ref, nw_ref, w_ref, o_ref, wbf_ref, kv_ref):
    b = pl.program_id(0)
    i = pl.program_id(1)
    width = MEM_HEADS * MEM_HEAD_DIM

    @pl.when(jnp.logical_and(b == 0, i == 0))
    def _():
        for r in range(0, w_ref.shape[0], PROJ_PREP_ROWS):
            wbf_ref[r:r + PROJ_PREP_ROWS, :] = w_ref[r:r + PROJ_PREP_ROWS, :].astype(BF16)

    @pl.when(i == 0)
    def _():
        x = mem_ref[...]
        ms = jnp.mean(x * x, axis=-1, keepdims=True)
        hn = (x * lax.rsqrt(ms + EPS) * nw_ref[...]).astype(BF16)
        kv_ref[...] = jnp.dot(hn, wbf_ref[...], preferred_element_type=F32).astype(BF16)

    scale = MEM_HEAD_DIM ** -0.5
    for h in range(MEM_HEADS):
        hs = slice(h * MEM_HEAD_DIM, (h + 1) * MEM_HEAD_DIM)
        vs = slice(width + h * MEM_HEAD_DIM, width + (h + 1) * MEM_HEAD_DIM)
        s = lax.dot_general(q_ref[:, hs], kv_ref[:, hs], NT, preferred_element_type=F32) * scale
        p = jnp.exp(s - jnp.max(s, axis=-1, keepdims=True))
        l = jnp.sum(p, axis=-1, keepdims=True)
        o = jnp.dot(p.astype(BF16), kv_ref[:, vs], preferred_element_type=F32) / l
        o_ref[:, hs] = (_silu(g_ref[:, hs].astype(F32)) * o).astype(o_ref.dtype)


def _memattn(z, mem2d, norm_w, w_kv, batch, seq, mem_tokens):
    tl = MEM_TL
    d = mem2d.shape[1]
    width = MEM_HEADS * MEM_HEAD_DIM
    wblk = width // LANE
    nl = seq // tl
    return pl.pallas_call(
        _memattn_kernel,
        grid=(batch, nl),
        in_specs=[pl.BlockSpec((tl, width), lambda b, i: (b * nl + i, CB_MQ // wblk)),
                  pl.BlockSpec((tl, width), lambda b, i: (b * nl + i, CB_MG // wblk)),
                  pl.BlockSpec((mem_tokens, d), lambda b, i: (b, 0)),
                  pl.BlockSpec((1, d), lambda b, i: (0, 0)),
                  pl.BlockSpec((d, 2 * width), lambda b, i: (0, 0), pipeline_mode=SINGLE_BUFFER)],
        out_specs=pl.BlockSpec((tl, width), lambda b, i: (b * nl + i, 0)),
        out_shape=jax.ShapeDtypeStruct((batch * seq, width), BF16),
        scratch_shapes=[pltpu.VMEM((d, 2 * width), BF16),
                        pltpu.VMEM((mem_tokens, 2 * width), BF16)],
        compiler_params=pltpu.CompilerParams(
            dimension_semantics=("arbitrary", "arbitrary"), vmem_limit_bytes=VMEM_LIMIT),
        name="mem_attn",
    )(z, z, mem2d, norm_w.reshape(1, d), w_kv)


def _merge_kernel(yr_ref, yd_ref, ym_ref, wr_ref, wd_ref, wm_ref, gr_ref, gd_ref, gm_ref, o_ref,
                  wbf_ref):
    j = pl.program_id(1)
    tn = o_ref.shape[1]

    @pl.when(jnp.logical_and(pl.program_id(0) == 0, j == 0))
    def _():
        for b, w_ref in enumerate((wr_ref, wd_ref, wm_ref)):
            for jj in range(wbf_ref.shape[1]):
                wbf_ref[b, jj] = w_ref[:, jj * tn:(jj + 1) * tn].astype(BF16)

    def branch(b, y_ref, g_ref):
        up = jnp.dot(y_ref[...], wbf_ref[b, j], preferred_element_type=F32)
        return _sigmoid(g_ref[...].astype(F32)) * up

    merged = branch(0, yr_ref, gr_ref) + branch(1, yd_ref, gd_ref) + branch(2, ym_ref, gm_ref)
    o_ref[...] = merged.astype(o_ref.dtype)


def _merge(y_ret, y_dsa, y_mem, w_ret, w_dsa, w_mem, z):
    tm, tn = MERGE_TM, MERGE_TN
    m, kdim = y_ret.shape
    n = w_ret.shape[1]
    gblk = tn // LANE
    yspec = pl.BlockSpec((tm, kdim), lambda i, j: (i, 0))
    wspec = pl.BlockSpec((kdim, n), lambda i, j: (0, 0), pipeline_mode=SINGLE_BUFFER)
    gspec = lambda off: pl.BlockSpec((tm, tn), lambda i, j: (i, off // gblk + j))
    return pl.pallas_call(
        _merge_kernel,
        grid=(m // tm, n // tn),
        in_specs=[yspec, yspec, yspec, wspec, wspec, wspec,
                  gspec(CB_GRET), gspec(CB_GDSA), gspec(CB_GMEM)],
        out_specs=pl.BlockSpec((tm, tn), lambda i, j: (i, j)),
        out_shape=jax.ShapeDtypeStruct((m, n), BF16),
        scratch_shapes=[pltpu.VMEM((N_BRANCHES, n // tn, kdim, tn), BF16)],
        compiler_params=pltpu.CompilerParams(
            dimension_semantics=("arbitrary", "arbitrary"), vmem_limit_bytes=VMEM_LIMIT),
        name="merge",
    )(y_ret, y_dsa, y_mem, w_ret, w_dsa, w_mem, z, z, z)


def _out_kernel(m_ref, w_ref, x_ref, nw_ref, o_ref, wbf_ref):
    @pl.when(pl.program_id(0) == 0)
    def _():
        for r in range(0, w_ref.shape[0], PROJ_PREP_ROWS):
            wbf_ref[r:r + PROJ_PREP_ROWS, :] = w_ref[r:r + PROJ_PREP_ROWS, :].astype(BF16)

    out = jnp.dot(m_ref[...], wbf_ref[...], preferred_element_type=F32)
    ms = jnp.mean(out * out, axis=-1, keepdims=True)
    o_ref[...] = x_ref[...] + out * lax.rsqrt(ms + EPS) * nw_ref[...]


def _out_proj(merged, w_o, x2d, post_w):
    tm = OUT_TM
    m, d = x2d.shape
    return pl.pallas_call(
        _out_kernel,
        grid=(m // tm,),
        in_specs=[pl.BlockSpec((tm, d), lambda i: (i, 0)),
                  pl.BlockSpec((d, d), lambda i: (0, 0), pipeline_mode=SINGLE_BUFFER),
                  pl.BlockSpec((tm, d), lambda i: (i, 0)),
                  pl.BlockSpec((1, d), lambda i: (0, 0))],
        out_specs=pl.BlockSpec((tm, d), lambda i: (i, 0)),
        out_shape=jax.ShapeDtypeStruct((m, d), F32),
        scratch_shapes=[pltpu.VMEM((d, d), BF16)],
        compiler_params=pltpu.CompilerParams(
            dimension_semantics=("arbitrary",), vmem_limit_bytes=VMEM_LIMIT),
        name="out_proj_postnorm",
    )(merged, w_o, x2d, post_w.reshape(1, d))


def _layer(x, mem, pre_norm_w, w_in, ret_gn_w, mem_norm_w, w_mem_kv,
           w_up_ret, w_up_dsa, w_up_mem, w_o, post_norm_w):
    batch, seq, d = x.shape
    mem_tokens = mem.shape[1]
    top_k = min(IDX_TOPK_MAX, seq // 4)
    assert w_in.shape == (d, Z_COLS + IDX_TAIL_COLS), w_in.shape
    assert seq % max(DSA_T, RET_C, MEM_TL) == 0 and seq < 2**15, seq
    assert (batch * seq) % max(MERGE_TM, OUT_TM) == 0 and top_k <= DSA_T, (batch, seq)
    x2d = x.reshape(batch * seq, d)

    w_in_t = w_in.T
    w_ikw = jnp.pad(w_in_t[IDX_TAIL_START:IDX_TAIL_START + IDX_TAIL_COLS],
                    ((0, LANE - IDX_TAIL_COLS), (0, 0))).astype(BF16)
    h, ikw = _prenorm(x2d, pre_norm_w, w_ikw)
    z = _project(h, w_in_t)

    log_g = jnp.log1p(-(2.0 ** (-5.0 - jnp.arange(RET_HEADS, dtype=F32))))
    y_ret = _retention(z, log_g, ret_gn_w, batch, seq)
    y_dsa = _dsa(z, ikw, batch, seq, top_k)
    y_mem = _memattn(z, mem.reshape(batch * mem_tokens, d), mem_norm_w, w_mem_kv, batch, seq,
                     mem_tokens)

    merged = _merge(y_ret, y_dsa, y_mem, w_up_ret, w_up_dsa, w_up_mem, z)
    out = _out_proj(merged, w_o, x2d, post_norm_w)
    return out.reshape(batch, seq, d)


def kernel(x, mem, pre_norm_w, w_in, ret_gn_w, mem_norm_w, w_mem_kv, w_up_ret, w_up_dsa,
           w_up_mem, w_o, post_norm_w):
    for layer in range(w_in.shape[0]):
        x = _layer(x, mem, pre_norm_w[layer], w_in[layer], ret_gn_w[layer], mem_norm_w[layer],
                   w_mem_kv[layer], w_up_ret[layer], w_up_dsa[layer], w_up_mem[layer],
                   w_o[layer], post_norm_w[layer])
    return x
```

```python
import functools

import jax
import jax.numpy as jnp
from jax import lax
from jax.experimental import pallas as pl
from jax.experimental.pallas import tpu as pltpu

F32 = jnp.float32
BF16 = jnp.bfloat16
I32 = jnp.int32

LANE = 128
SUBLANES = 8
N_BRANCHES = 3
RET_HEADS = 8
HEAD_DIM = 128
DSA_HEADS = 8
IDX_HEADS = 16
IDX_DIM = 64
IDX_TOPK_MAX = 256
MEM_HEADS = 4
MEM_HEAD_DIM = 256
NEG_INF = -1e30
EPS = 1e-6
INT_MIN = -2**31

CB_RQ, CB_RK, CB_RV, CB_RG = 0, 8, 16, 24
CB_AQ, CB_AK, CB_AV, CB_AG = 32, 40, 48, 56
CB_IQ = 64
CB_MQ, CB_MG = 72, 80
CB_GRET, CB_GDSA, CB_GMEM = 88, 104, 120
N_COL_BLOCKS = 136
Z_COLS = N_COL_BLOCKS * LANE
IDX_TAIL_START = (CB_IQ + IDX_HEADS * IDX_DIM // LANE) * LANE
IDX_TAIL_COLS = IDX_DIM + IDX_HEADS

PROJ_TM = 2048
PROJ_TN = 1024
PROJ_ALIGNED_TILES = IDX_TAIL_START // PROJ_TN
PROJ_ROWS = 1024
PROJ_PREP_ROWS = 256
PRENORM_TM = 1024
MEM_TL = 1024
MERGE_TM, MERGE_TN = 512, 1024
OUT_TM = 512
VMEM_LIMIT = 56 * 1024 * 1024
SINGLE_BUFFER = pl.Buffered(1)

NT = (((1,), (1,)), ((), ()))
TN = (((0,), (0,)), ((), ()))


def _sigmoid(v):
    return 1.0 / (1.0 + jnp.exp(-v))


def _silu(v):
    return v * _sigmoid(v)


def _prenorm_kernel(x_ref, w_ref, wikw_ref, h_ref, ikw_ref):
    x = x_ref[...]
    ms = jnp.mean(x * x, axis=-1, keepdims=True)
    h = (x * lax.rsqrt(ms + EPS) * w_ref[...]).astype(h_ref.dtype)
    h_ref[...] = h
    ikw_ref[...] = lax.dot_general(h, wikw_ref[...], NT,
                                   preferred_element_type=F32).astype(ikw_ref.dtype)


def _prenorm(x2d, w, w_ikw):
    m, d = x2d.shape
    tm = min(PRENORM_TM, m)
    return pl.pallas_call(
        _prenorm_kernel,
        grid=(m // tm,),
        in_specs=[pl.BlockSpec((tm, d), lambda i: (i, 0)),
                  pl.BlockSpec((1, d), lambda i: (0, 0)),
                  pl.BlockSpec((LANE, d), lambda i: (0, 0))],
        out_specs=[pl.BlockSpec((tm, d), lambda i: (i, 0)),
                   pl.BlockSpec((tm, LANE), lambda i: (i, 0))],
        out_shape=[jax.ShapeDtypeStruct((m, d), BF16),
                   jax.ShapeDtypeStruct((m, LANE), BF16)],
        compiler_params=pltpu.CompilerParams(dimension_semantics=("arbitrary",)),
        name="prenorm_idx",
    )(x2d, w.reshape(1, d), w_ikw)


def _proj_kernel(h_ref, w_ref, z_ref, wbf_ref):
    @pl.when(pl.program_id(1) == 0)
    def _():
        for r in range(0, w_ref.shape[0], PROJ_PREP_ROWS):
            wbf_ref[r:r + PROJ_PREP_ROWS, :] = w_ref[r:r + PROJ_PREP_ROWS, :].astype(BF16)

    w = wbf_ref[...]
    for r in range(0, h_ref.shape[0], PROJ_ROWS):
        z_ref[r:r + PROJ_ROWS, :] = lax.dot_general(
            h_ref[r:r + PROJ_ROWS, :], w, NT, preferred_element_type=F32).astype(z_ref.dtype)


def _project(h, w_in_t):
    m, d = h.shape
    tm = min(PROJ_TM, m)

    def w_rows(j, i):
        skip = jnp.where(j >= PROJ_ALIGNED_TILES, IDX_TAIL_COLS // SUBLANES, 0)
        return ((j * (PROJ_TN // SUBLANES) + skip) * SUBLANES, 0)

    return pl.pallas_call(
        _proj_kernel,
        grid=(Z_COLS // PROJ_TN, m // tm),
        in_specs=[pl.BlockSpec((tm, d), lambda j, i: (i, 0)),
                  pl.BlockSpec((pl.Element(PROJ_TN), pl.Element(d)), w_rows)],
        out_specs=pl.BlockSpec((tm, PROJ_TN), lambda j, i: (i, j)),
        out_shape=jax.ShapeDtypeStruct((m, Z_COLS), BF16),
        scratch_shapes=[pltpu.VMEM((PROJ_TN, d), BF16)],
        compiler_params=pltpu.CompilerParams(
            dimension_semantics=("arbitrary", "arbitrary"), vmem_limit_bytes=VMEM_LIMIT),
        name="in_proj",
    )(h, w_in_t)


RET_C = 256
RET_HEADS_PER_STEP = 4


def _retention_kernel(lg_ref, q_ref, k_ref, v_ref, g_ref, gw_ref, o_ref):
    c = RET_C
    seq = q_ref.shape[0]
    scale = HEAD_DIM ** -0.5
    ri = lax.broadcasted_iota(I32, (c, c), 0)
    ci = lax.broadcasted_iota(I32, (c, c), 1)
    diff = (ri - ci).astype(F32)
    idx = lax.broadcasted_iota(I32, (c, 1), 0).astype(F32)

    for hh in range(RET_HEADS_PER_STEP):
        hs = slice(hh * HEAD_DIM, (hh + 1) * HEAD_DIM)
        lg = lg_ref[pl.program_id(1) * RET_HEADS_PER_STEP + hh]
        decay = jnp.where(diff >= 0, jnp.exp(lg * jnp.maximum(diff, 0.0)), 0.0) * scale
        q_dec = jnp.exp(lg * (idx + 1.0))
        k_dec = jnp.exp(lg * (c - 1.0 - idx)) * scale
        chunk_dec = jnp.exp(lg * jnp.full((1, HEAD_DIM), float(c), F32))
        gw = gw_ref[:, hs]
        state = jnp.zeros((HEAD_DIM, HEAD_DIM), F32)
        for i in range(seq // c):
            r = i * c
            qi = q_ref[pl.ds(r, c), hs]
            ki = k_ref[pl.ds(r, c), hs]
            vi = v_ref[pl.ds(r, c), hs]
            inner = lax.dot_general(qi, ki, NT, preferred_element_type=F32) * decay
            o = (jnp.dot(inner.astype(BF16), vi, preferred_element_type=F32)
                 + jnp.dot(qi, state.astype(BF16), preferred_element_type=F32) * q_dec)
            vs = (vi.astype(F32) * k_dec).astype(BF16)
            state = state * chunk_dec + lax.dot_general(ki, vs, TN, preferred_element_type=F32)
            mu = jnp.mean(o, axis=-1, keepdims=True)
            oc = o - mu
            var = jnp.mean(oc * oc, axis=-1, keepdims=True)
            on = oc * lax.rsqrt(var + 1e-5) * gw
            g = g_ref[pl.ds(r, c), hs].astype(F32)
            o_ref[pl.ds(r, c), hs] = (_silu(g) * on).astype(o_ref.dtype)


def _retention(z, log_g, gn_w, batch, seq):
    hp = RET_HEADS_PER_STEP
    width = hp * HEAD_DIM
    blk = lambda off: pl.BlockSpec((seq, width), lambda b, h: (b, off // hp + h))
    return pl.pallas_call(
        _retention_kernel,
        grid=(batch, RET_HEADS // hp),
        in_specs=[pl.BlockSpec(memory_space=pltpu.SMEM),
                  blk(CB_RQ), blk(CB_RK), blk(CB_RV), blk(CB_RG),
                  pl.BlockSpec((1, width), lambda b, h: (0, h))],
        out_specs=pl.BlockSpec((seq, width), lambda b, h: (b, h)),
        out_shape=jax.ShapeDtypeStruct((batch * seq, RET_HEADS * HEAD_DIM), BF16),
        compiler_params=pltpu.CompilerParams(
            dimension_semantics=("arbitrary", "arbitrary"), vmem_limit_bytes=VMEM_LIMIT),
        name="retention",
    )(log_g, z, z, z, z, gn_w.reshape(1, -1))


DSA_T = 256
I16 = jnp.int16
PACKED_ROWS = 2 * SUBLANES
VT_ROWS = HEAD_DIM + PACKED_ROWS


def _fold_rows(x, op, rows=SUBLANES):
    while x.shape[0] > rows:
        half = x.shape[0] // 2
        x = op(x[:half], x[half:])
    return x


def _for_tiles_grouped(nk, step):
    def quad(i, carry):
        for u in range(4):
            step(4 * i + u)
        return carry

    lax.fori_loop(0, nk >> 2, quad, 0)
    done = (nk >> 2) << 2

    @pl.when((nk & 2) == 2)
    def _():
        step(done)
        step(done + 1)

    @pl.when((nk & 1) == 1)
    def _():
        step(nk - 1)


def _fold_tiles_grouped(nk, step, carry):
    def quad(i, c):
        for u in range(4):
            c = step(4 * i + u, c)
        return c

    carry = lax.fori_loop(0, nk >> 2, quad, carry)
    done = (nk >> 2) << 2
    carry = lax.cond((nk & 2) == 2, lambda c: step(done + 1, step(done, c)), lambda c: c, carry)
    return lax.cond((nk & 1) == 1, lambda c: step(nk - 1, c), lambda c: c, carry)


ORD_NEG_INF = -2**31 + 0x7FFFFF


def _flip(v):
    return v ^ ((v >> 31) & 0x7FFFFFFF)


def _ordinal_to_f32(o):
    return lax.bitcast_convert_type(_flip(jnp.maximum(o, ORD_NEG_INF)), F32)


def _prefix_to_bf16_bits(p):
    return _flip(jnp.maximum(p, ORD_NEG_INF >> 16) << 16) & -65536


def _dsa_kernel(iq_ref, ikw_ref, aq_ref, ak_ref, av_ref, ag_ref, wr_ref, wd_ref, wm_ref, wo_ref,
                o_ref, wrb_ref, wdb_ref, wmb_ref, wob_ref,
                ika_ref, ikb_ref, vt_ref, wt_ref, qs_ref, score_ref, score16_ref, bias_ref,
                logit_ref,
                m_ref, acc_ref, *, top_k):
    t = DSA_T
    qb = pl.program_id(1)
    nk = qb + 1
    q0 = pl.multiple_of(qb * t, t)
    nq = vt_ref.shape[0]

    for w_ref, wb_ref in ((wr_ref, wrb_ref), (wd_ref, wdb_ref), (wm_ref, wmb_ref)):
        for jj in range(wb_ref.shape[0]):
            wb_ref[jj] = w_ref[:, jj * MERGE_TN:(jj + 1) * MERGE_TN].astype(BF16)
    wob_ref[...] = wo_ref[...].astype(BF16)

    @pl.when(qb == 0)
    def _():
        ikw = ikw_ref[...].astype(F32)
        lane = lax.broadcasted_iota(I32, ikw.shape, 1)
        a = jnp.where(lane < IDX_DIM, ikw, 0.0)
        ika_ref[...] = a.astype(BF16)
        ikb_ref[...] = pltpu.roll(a, IDX_DIM, axis=1).astype(BF16)

        def vt_tile(kt, carry):
            k0 = pl.multiple_of(kt * t, t)
            for h in range(DSA_HEADS):
                hs = slice(h * HEAD_DIM, (h + 1) * HEAD_DIM)
                vt_ref[kt, h, :HEAD_DIM, :] = av_ref[pl.ds(k0, t), hs].astype(F32).T.astype(BF16)
                vt_ref[kt, h, HEAD_DIM:, :] = jnp.ones((VT_ROWS - HEAD_DIM, t), BF16)
            return carry

        lax.fori_loop(0, nq, vt_tile, 0)

    wt_ref[...] = ikw_ref[pl.ds(q0, t), :].astype(F32).T

    key_row = lax.broadcasted_iota(I32, (t, t), 0)
    qry_col = lax.broadcasted_iota(I32, (t, t), 1)

    def score_tile(kt):
        k0 = pl.multiple_of(kt * t, t)
        ka = ika_ref[pl.ds(k0, t), :]
        kb = ikb_ref[pl.ds(k0, t), :]
        acc = jnp.zeros((t, t), F32)
        for j in range(IDX_HEADS // 2):
            qj = iq_ref[:, j * LANE:(j + 1) * LANE]
            for half, kx in enumerate((ka, kb)):
                hh = IDX_DIM + 2 * j + half
                s = lax.dot_general(kx, qj, NT, preferred_element_type=F32)
                acc = acc + jnp.maximum(s, 0.0) * wt_ref[hh:hh + 1, :]
        causal = (key_row + k0) <= (qry_col + q0)
        score = jnp.where(causal, acc, -jnp.inf)
        score_ref[kt] = score
        score16_ref[kt] = score.astype(BF16)

    _for_tiles_grouped(nk, score_tile)

    one16, zero16 = jnp.ones((), I16), jnp.zeros((), I16)

    def prefix_step(i, pfx):
        cand_pfx = pfx + lax.shift_left(jnp.int32(1), 15 - i)
        cand = lax.bitcast_convert_type(_prefix_to_bf16_bits(cand_pfx), F32)
        cand16 = jnp.broadcast_to(cand, (PACKED_ROWS, t)).astype(BF16)[:1]

        def count_tile(kt, cnt):
            ge = jnp.where(score16_ref[kt] >= cand16, one16, zero16)
            return cnt + _fold_rows(ge, jnp.add, PACKED_ROWS)

        cnt = _fold_tiles_grouped(nk, count_tile, jnp.zeros((PACKED_ROWS, t), I16))
        total = jnp.sum(cnt.astype(I32), axis=0, keepdims=True)
        return jnp.where(total >= top_k, cand_pfx, pfx)

    pfx = lax.fori_loop(0, 16, prefix_step, jnp.full((1, t), -2**15, I32))

    def bisect_step(i, state):
        lo, hi, kept = state
        mid = lo + ((hi - lo) >> 1)
        cand = _ordinal_to_f32(mid)

        def count_tile(kt, cnt):
            return cnt + _fold_rows(jnp.where(score_ref[kt] >= cand, 1, 0), jnp.add)

        cnt = _fold_tiles_grouped(nk, count_tile, jnp.zeros((SUBLANES, t), I32))
        total = jnp.sum(cnt, axis=0, keepdims=True)
        enough = total >= top_k
        return (jnp.where(enough, mid, lo), jnp.where(enough, hi, mid),
                jnp.where(enough, total, kept))

    lo, _, kept = lax.fori_loop(
        0, 17, bisect_step, (_flip(_prefix_to_bf16_bits(pfx - 1)),
                             _flip(_prefix_to_bf16_bits(pfx + 1)),
                             jnp.full((1, t), top_k + 1, I32)))
    thr = _ordinal_to_f32(lo)

    def bias_tile(kt, carry):
        bias_ref[kt] = jnp.where(score_ref[kt] >= thr, 0.0, NEG_INF)
        return carry

    lax.fori_loop(0, qb, bias_tile, 0)
    keep = jnp.logical_and(score_ref[qb] >= thr, key_row <= qry_col)
    bias_ref[qb] = jnp.where(keep, 0.0, NEG_INF)

    @pl.when(jnp.max(kept) > top_k)
    def _():
        strictly_lower = (qry_col < key_row).astype(BF16)

        def tie_tile(kt, carry):
            above, tied_before = carry
            k0 = pl.multiple_of(kt * t, t)
            causal = (key_row + k0) <= (qry_col + q0)
            score = score_ref[kt]
            gt = jnp.logical_and(score > thr, causal)
            eq = jnp.logical_and(score == thr, causal)
            eq01 = jnp.where(eq, 1.0, 0.0)
            tied_here = tied_before + jnp.dot(strictly_lower, eq01.astype(BF16),
                                              preferred_element_type=F32)
            logit_ref[0, kt] = jnp.where(eq, tied_here, -1.0)
            above = above + _fold_rows(jnp.where(gt, 1.0, 0.0), jnp.add)
            return above, tied_before + jnp.sum(eq01, axis=0, keepdims=True)

        above, _ = lax.fori_loop(0, nk, tie_tile,
                                 (jnp.zeros((SUBLANES, t), F32), jnp.zeros((1, t), F32)))
        room = top_k - jnp.sum(above, axis=0, keepdims=True)

        def rebias_tile(kt, carry):
            k0 = pl.multiple_of(kt * t, t)
            gt = jnp.logical_and(score_ref[kt] > thr, (key_row + k0) <= (qry_col + q0))
            rank = logit_ref[0, kt]
            keep = jnp.logical_or(gt, jnp.logical_and(rank >= 0.0, rank < room))
            bias_ref[kt] = jnp.where(keep, 0.0, NEG_INF)
            return carry

        lax.fori_loop(0, nk, rebias_tile, 0)

    m_ref[...] = jnp.full(m_ref.shape, NEG_INF, F32)
    acc_ref[...] = jnp.zeros(acc_ref.shape, F32)
    log2e = 1.4426950408889634
    qs_ref[...] = (aq_ref[...].astype(F32) * ((HEAD_DIM ** -0.5) * log2e)).astype(BF16)
    key_row_1 = lax.broadcasted_iota(I32, (t, LANE), 0)

    def logit_tile(kt):
        k0 = pl.multiple_of(kt * t, t)
        bias = bias_ref[kt]
        kpos = (key_row_1 + (k0 - q0)).astype(F32)
        for h in range(DSA_HEADS):
            slope = (2.0 ** (-8.0 * (h + 1) / DSA_HEADS)) * log2e
            hs = slice(h * HEAD_DIM, (h + 1) * HEAD_DIM)
            s = lax.dot_general(ak_ref[pl.ds(k0, t), hs], qs_ref[:, hs], NT,
                                preferred_element_type=F32)
            alibi = kpos * slope
            lg = s + jnp.concatenate([alibi] * (t // LANE), axis=1) + bias
            logit_ref[h, kt] = lg
            m_ref[h] = jnp.maximum(m_ref[h], _fold_rows(lg, jnp.maximum))

    _for_tiles_grouped(nk, logit_tile)

    for h in range(DSA_HEADS):
        m_ref[h] = jnp.broadcast_to(jnp.max(m_ref[h], axis=0, keepdims=True), (SUBLANES, t))

    def pv_tile(kt):
        for h in range(DSA_HEADS):
            p = jnp.exp2(logit_ref[h, kt] - m_ref[h][:1])
            acc_ref[h] = acc_ref[h] + jnp.dot(vt_ref[kt, h], p.astype(BF16),
                                              preferred_element_type=F32)

    _for_tiles_grouped(nk, pv_tile)

    for h in range(DSA_HEADS):
        hs = slice(h * HEAD_DIM, (h + 1) * HEAD_DIM)
        acc = acc_ref[h]
        o = (acc[:HEAD_DIM] / acc[HEAD_DIM:HEAD_DIM + 1]).T
        o_ref[:, hs] = (_silu(ag_ref[:, hs].astype(F32)) * o).astype(o_ref.dtype)


def _dsa(z, ikw, w_ups, w_o, batch, seq, top_k):
    t = DSA_T
    nq = seq // t
    steps = batch * nq
    kdim, n = w_ups[0].shape
    d = w_o.shape[0]
    up_rows, o_rows, nj = kdim // steps, d // steps, n // MERGE_TN
    assert up_rows % PACKED_ROWS == 0 and o_rows % PACKED_ROWS == 0, (kdim, d, steps)
    upspec = pl.BlockSpec((up_rows, n), lambda b, i: (b * nq + i, 0))
    upbspec = pl.BlockSpec((nj, up_rows, MERGE_TN), lambda b, i: (0, b * nq + i, 0))
    width = DSA_HEADS * HEAD_DIM
    wblk = width // LANE
    qspec = lambda off: pl.BlockSpec((t, width), lambda b, i: (b * nq + i, off // wblk))
    kvspec = lambda off: pl.BlockSpec((seq, width), lambda b, i: (b, off // wblk))
    return pl.pallas_call(
        functools.partial(_dsa_kernel, top_k=top_k),
        grid=(batch, nq),
        in_specs=[qspec(CB_IQ),
                  pl.BlockSpec((seq, LANE), lambda b, i: (b, 0)),
                  qspec(CB_AQ), kvspec(CB_AK), kvspec(CB_AV), qspec(CB_AG),
                  upspec, upspec, upspec,
                  pl.BlockSpec((o_rows, d), lambda b, i: (b * nq + i, 0))],
        out_specs=[pl.BlockSpec((t, width), lambda b, i: (b * nq + i, 0)),
                   upbspec, upbspec, upbspec,
                   pl.BlockSpec((o_rows, d), lambda b, i: (b * nq + i, 0))],
        out_shape=[jax.ShapeDtypeStruct((batch * seq, width), BF16)]
        + [jax.ShapeDtypeStruct((nj, kdim, MERGE_TN), BF16)] * N_BRANCHES
        + [jax.ShapeDtypeStruct((d, d), BF16)],
        scratch_shapes=[pltpu.VMEM((seq, LANE), BF16),
                        pltpu.VMEM((seq, LANE), BF16),
                        pltpu.VMEM((nq, DSA_HEADS, VT_ROWS, t), BF16),
                        pltpu.VMEM((LANE, t), F32),
                        pltpu.VMEM((t, width), BF16),
                        pltpu.VMEM((nq, t, t), F32),
                        pltpu.VMEM((nq, t, t), BF16),
                        pltpu.VMEM((nq, t, t), F32),
                        pltpu.VMEM((DSA_HEADS, nq, t, t), F32),
                        pltpu.VMEM((DSA_HEADS, SUBLANES, t), F32),
                        pltpu.VMEM((DSA_HEADS, VT_ROWS, t), F32)],
        compiler_params=pltpu.CompilerParams(
            dimension_semantics=("arbitrary", "arbitrary"), vmem_limit_bytes=VMEM_LIMIT),
        name="dsa",
    )(z, ikw, z, z, z, z, *w_ups, w_o)


def _memattn_kernel(q_ref, g_ref, mem_ref, nw_ref, w_ref, o_ref, wbf_ref, kv_ref):
    b = pl.program_id(0)
    i = pl.program_id(1)
    width = MEM_HEADS * MEM_HEAD_DIM

    @pl.when(jnp.logical_and(b == 0, i == 0))
    def _():
        for r in range(0, w_ref.shape[0], PROJ_PREP_ROWS):
            wbf_ref[r:r + PROJ_PREP_ROWS, :] = w_ref[r:r + PROJ_PREP_ROWS, :].astype(BF16)

    @pl.when(i == 0)
    def _():
        x = mem_ref[...]
        ms = jnp.mean(x * x, axis=-1, keepdims=True)
        hn = (x * lax.rsqrt(ms + EPS) * nw_ref[...]).astype(BF16)
        kv_ref[...] = jnp.dot(hn, wbf_ref[...], preferred_element_type=F32).astype(BF16)

    scale = MEM_HEAD_DIM ** -0.5
    for h in range(MEM_HEADS):
        hs = slice(h * MEM_HEAD_DIM, (h + 1) * MEM_HEAD_DIM)
        vs = slice(width + h * MEM_HEAD_DIM, width + (h + 1) * MEM_HEAD_DIM)
        s = lax.dot_general(q_ref[:, hs], kv_ref[:, hs], NT, preferred_element_type=F32) * scale
        p = jnp.exp(s - jnp.max(s, axis=-1, keepdims=True))
        l = jnp.sum(p, axis=-1, keepdims=True)
        o = jnp.dot(p.astype(BF16), kv_ref[:, vs], preferred_element_type=F32) / l
        o_ref[:, hs] = (_silu(g_ref[:, hs].astype(F32)) * o).astype(o_ref.dtype)


def _memattn(z, mem2d, norm_w, w_kv, batch, seq, mem_tokens):
    tl = MEM_TL
    d = mem2d.shape[1]
    width = MEM_HEADS * MEM_HEAD_DIM
    wblk = width // LANE
    nl = seq // tl
    return pl.pallas_call(
        _memattn_kernel,
        grid=(batch, nl),
        in_specs=[pl.BlockSpec((tl, width), lambda b, i: (b * nl + i, CB_MQ // wblk)),
                  pl.BlockSpec((tl, width), lambda b, i: (b * nl + i, CB_MG // wblk)),
                  pl.BlockSpec((mem_tokens, d), lambda b, i: (b, 0)),
                  pl.BlockSpec((1, d), lambda b, i: (0, 0)),
                  pl.BlockSpec((d, 2 * width), lambda b, i: (0, 0), pipeline_mode=SINGLE_BUFFER)],
        out_specs=pl.BlockSpec((tl, width), lambda b, i: (b * nl + i, 0)),
        out_shape=jax.ShapeDtypeStruct((batch * seq, width), BF16),
        scratch_shapes=[pltpu.VMEM((d, 2 * width), BF16),
                        pltpu.VMEM((mem_tokens, 2 * width), BF16)],
        compiler_params=pltpu.CompilerParams(
            dimension_semantics=("arbitrary", "arbitrary"), vmem_limit_bytes=VMEM_LIMIT),
        name="mem_attn",
    )(z, z, mem2d, norm_w.reshape(1, d), w_kv)


def _merge_kernel(yr_ref, yd_ref, ym_ref, wr_ref, wd_ref, wm_ref, gr_ref, gd_ref, gm_ref, o_ref):
    j = pl.program_id(1)

    def branch(y_ref, w_ref, g_ref):
        up = jnp.dot(y_ref[...], w_ref[j], preferred_element_type=F32)
        return _sigmoid(g_ref[...].astype(F32)) * up

    merged = (branch(yr_ref, wr_ref, gr_ref) + branch(yd_ref, wd_ref, gd_ref)
              + branch(ym_ref, wm_ref, gm_ref))
    o_ref[...] = merged.astype(o_ref.dtype)


def _merge(y_ret, y_dsa, y_mem, w_ret, w_dsa, w_mem, z):
    tm = MERGE_TM
    m, kdim = y_ret.shape
    nj, _, tn = w_ret.shape
    gblk = tn // LANE
    yspec = pl.BlockSpec((tm, kdim), lambda i, j: (i, 0))
    wspec = pl.BlockSpec((nj, kdim, tn), lambda i, j: (0, 0, 0), pipeline_mode=SINGLE_BUFFER)
    gspec = lambda off: pl.BlockSpec((tm, tn), lambda i, j: (i, off // gblk + j))
    return pl.pallas_call(
        _merge_kernel,
        grid=(m // tm, nj),
        in_specs=[yspec, yspec, yspec, wspec, wspec, wspec,
                  gspec(CB_GRET), gspec(CB_GDSA), gspec(CB_GMEM)],
        out_specs=pl.BlockSpec((tm, tn), lambda i, j: (i, j)),
        out_shape=jax.ShapeDtypeStruct((m, nj * tn), BF16),
        compiler_params=pltpu.CompilerParams(
            dimension_semantics=("arbitrary", "arbitrary"), vmem_limit_bytes=VMEM_LIMIT),
        name="merge",
    )(y_ret, y_dsa, y_mem, w_ret, w_dsa, w_mem, z, z, z)


def _out_kernel(m_ref, w_ref, x_ref, nw_ref, o_ref):
    out = jnp.dot(m_ref[...], w_ref[...], preferred_element_type=F32)
    ms = jnp.mean(out * out, axis=-1, keepdims=True)
    o_ref[...] = x_ref[...] + out * lax.rsqrt(ms + EPS) * nw_ref[...]


def _out_proj(merged, w_o, x2d, post_w):
    tm = OUT_TM
    m, d = x2d.shape
    return pl.pallas_call(
        _out_kernel,
        grid=(m // tm,),
        in_specs=[pl.BlockSpec((tm, d), lambda i: (i, 0)),
                  pl.BlockSpec((d, d), lambda i: (0, 0), pipeline_mode=SINGLE_BUFFER),
                  pl.BlockSpec((tm, d), lambda i: (i, 0)),
                  pl.BlockSpec((1, d), lambda i: (0, 0))],
        out_specs=pl.BlockSpec((tm, d), lambda i: (i, 0)),
        out_shape=jax.ShapeDtypeStruct((m, d), F32),
        compiler_params=pltpu.CompilerParams(
            dimension_semantics=("arbitrary",), vmem_limit_bytes=VMEM_LIMIT),
        name="out_proj_postnorm",
    )(merged, w_o, x2d, post_w.reshape(1, d))


def _layer(x, mem, pre_norm_w, w_in, ret_gn_w, mem_norm_w, w_mem_kv,
           w_up_ret, w_up_dsa, w_up_mem, w_o, post_norm_w):
    batch, seq, d = x.shape
    mem_tokens = mem.shape[1]
    top_k = min(IDX_TOPK_MAX, seq // 4)
    assert w_in.shape == (d, Z_COLS + IDX_TAIL_COLS), w_in.shape
    assert seq % max(DSA_T, RET_C, MEM_TL) == 0 and seq < 2**15, seq
    assert (batch * seq) % max(MERGE_TM, OUT_TM) == 0 and top_k <= DSA_T, (batch, seq)
    x2d = x.reshape(batch * seq, d)

    w_in_t = w_in.T
    w_ikw = jnp.pad(w_in_t[IDX_TAIL_START:IDX_TAIL_START + IDX_TAIL_COLS],
                    ((0, LANE - IDX_TAIL_COLS), (0, 0))).astype(BF16)
    h, ikw = _prenorm(x2d, pre_norm_w, w_ikw)
    z = _project(h, w_in_t)

    log_g = jnp.log1p(-(2.0 ** (-5.0 - jnp.arange(RET_HEADS, dtype=F32))))
    y_ret = _retention(z, log_g, ret_gn_w, batch, seq)
    y_dsa, wb_ret, wb_dsa, wb_mem, wb_o = _dsa(z, ikw, (w_up_ret, w_up_dsa, w_up_mem), w_o,
                                               batch, seq, top_k)
    y_mem = _memattn(z, mem.reshape(batch * mem_tokens, d), mem_norm_w, w_mem_kv, batch, seq,
                     mem_tokens)

    merged = _merge(y_ret, y_dsa, y_mem, wb_ret, wb_dsa, wb_mem, z)
    out = _out_proj(merged, wb_o, x2d, post_norm_w)
    return out.reshape(batch, seq, d)


def kernel(x, mem, pre_norm_w, w_in, ret_gn_w, mem_norm_w, w_mem_kv, w_up_ret, w_up_dsa,
           w_up_mem, w_o, post_norm_w):
    for layer in range(w_in.shape[0]):
        x = _layer(x, mem, pre_norm_w[layer], w_in[layer], ret_gn_w[layer], mem_norm_w[layer],
                   w_mem_kv[layer], w_up_ret[layer], w_up_dsa[layer], w_up_mem[layer],
                   w_o[layer], post_norm_w[layer])
    return x
```

```python
import functools

import jax
import jax.numpy as jnp
from jax import lax
from jax.experimental import pallas as pl
from jax.experimental.pallas import tpu as pltpu

F32 = jnp.float32
BF16 = jnp.bfloat16
I32 = jnp.int32

LANE = 128
SUBLANES = 8
N_BRANCHES = 3
RET_HEADS = 8
HEAD_DIM = 128
DSA_HEADS = 8
IDX_HEADS = 16
IDX_DIM = 64
IDX_TOPK_MAX = 256
MEM_HEADS = 4
MEM_HEAD_DIM = 256
NEG_INF = -1e30
EPS = 1e-6
INT_MIN = -2**31

CB_RQ, CB_RK, CB_RV, CB_RG = 0, 8, 16, 24
CB_AQ, CB_AK, CB_AV, CB_AG = 32, 40, 48, 56
CB_IQ = 64
CB_MQ, CB_MG = 72, 80
CB_GRET, CB_GDSA, CB_GMEM = 88, 104, 120
N_COL_BLOCKS = 136
Z_COLS = N_COL_BLOCKS * LANE
IDX_TAIL_START = (CB_IQ + IDX_HEADS * IDX_DIM // LANE) * LANE
IDX_TAIL_COLS = IDX_DIM + IDX_HEADS

PROJ_TM = 2048
PROJ_TN = 1024
PROJ_ALIGNED_TILES = IDX_TAIL_START // PROJ_TN
PROJ_ROWS = 1024
PROJ_PREP_ROWS = 256
PRENORM_TM = 1024
MEM_TL = 1024
MERGE_TM, MERGE_TN = 1024, 1024
OUT_TM = 512
VMEM_LIMIT = 56 * 1024 * 1024
SINGLE_BUFFER = pl.Buffered(1)

NT = (((1,), (1,)), ((), ()))
TN = (((0,), (0,)), ((), ()))


def _sigmoid(v):
    return 1.0 / (1.0 + jnp.exp(-v))


def _silu(v):
    return v * _sigmoid(v)


def _prenorm_kernel(x_ref, w_ref, wikw_ref, h_ref, ikw_ref):
    x = x_ref[...]
    ms = jnp.mean(x * x, axis=-1, keepdims=True)
    h = (x * lax.rsqrt(ms + EPS) * w_ref[...]).astype(h_ref.dtype)
    h_ref[...] = h
    ikw_ref[...] = lax.dot_general(h, wikw_ref[...], NT,
                                   preferred_element_type=F32).astype(ikw_ref.dtype)


def _prenorm(x2d, w, w_ikw):
    m, d = x2d.shape
    tm = min(PRENORM_TM, m)
    return pl.pallas_call(
        _prenorm_kernel,
        grid=(m // tm,),
        in_specs=[pl.BlockSpec((tm, d), lambda i: (i, 0)),
                  pl.BlockSpec((1, d), lambda i: (0, 0)),
                  pl.BlockSpec((LANE, d), lambda i: (0, 0))],
        out_specs=[pl.BlockSpec((tm, d), lambda i: (i, 0)),
                   pl.BlockSpec((tm, LANE), lambda i: (i, 0))],
        out_shape=[jax.ShapeDtypeStruct((m, d), BF16),
                   jax.ShapeDtypeStruct((m, LANE), BF16)],
        compiler_params=pltpu.CompilerParams(dimension_semantics=("arbitrary",)),
        name="prenorm_idx",
    )(x2d, w.reshape(1, d), w_ikw)


def _proj_kernel(h_ref, w_ref, z_ref, wbf_ref):
    @pl.when(pl.program_id(1) == 0)
    def _():
        for r in range(0, w_ref.shape[0], PROJ_PREP_ROWS):
            wbf_ref[r:r + PROJ_PREP_ROWS, :] = w_ref[r:r + PROJ_PREP_ROWS, :].astype(BF16)

    w = wbf_ref[...]
    for r in range(0, h_ref.shape[0], PROJ_ROWS):
        z_ref[r:r + PROJ_ROWS, :] = lax.dot_general(
            h_ref[r:r + PROJ_ROWS, :], w, NT, preferred_element_type=F32).astype(z_ref.dtype)


def _project(h, w_in_t):
    m, d = h.shape
    tm = min(PROJ_TM, m)

    def w_rows(j, i):
        skip = jnp.where(j >= PROJ_ALIGNED_TILES, IDX_TAIL_COLS // SUBLANES, 0)
        return ((j * (PROJ_TN // SUBLANES) + skip) * SUBLANES, 0)

    return pl.pallas_call(
        _proj_kernel,
        grid=(Z_COLS // PROJ_TN, m // tm),
        in_specs=[pl.BlockSpec((tm, d), lambda j, i: (i, 0)),
                  pl.BlockSpec((pl.Element(PROJ_TN), pl.Element(d)), w_rows)],
        out_specs=pl.BlockSpec((tm, PROJ_TN), lambda j, i: (i, j)),
        out_shape=jax.ShapeDtypeStruct((m, Z_COLS), BF16),
        scratch_shapes=[pltpu.VMEM((PROJ_TN, d), BF16)],
        compiler_params=pltpu.CompilerParams(
            dimension_semantics=("arbitrary", "arbitrary"), vmem_limit_bytes=VMEM_LIMIT),
        name="in_proj",
    )(h, w_in_t)


RET_C = 256
RET_HEADS_PER_STEP = 4


def _retention_kernel(lg_ref, q_ref, k_ref, v_ref, g_ref, gw_ref, wr_ref, wd_ref, wm_ref, wo_ref,
                      o_ref, wrb_ref, wdb_ref, wmb_ref, wob_ref):
    for w_ref, wb_ref in ((wr_ref, wrb_ref), (wd_ref, wdb_ref), (wm_ref, wmb_ref)):
        for jj in range(wb_ref.shape[0]):
            wb_ref[jj] = w_ref[:, jj * MERGE_TN:(jj + 1) * MERGE_TN].astype(BF16)
    wob_ref[...] = wo_ref[...].astype(BF16)

    c = RET_C
    seq = q_ref.shape[0]
    scale = HEAD_DIM ** -0.5
    ri = lax.broadcasted_iota(I32, (c, c), 0)
    ci = lax.broadcasted_iota(I32, (c, c), 1)
    diff = (ri - ci).astype(F32)
    idx = lax.broadcasted_iota(I32, (c, 1), 0).astype(F32)

    for hh in range(RET_HEADS_PER_STEP):
        hs = slice(hh * HEAD_DIM, (hh + 1) * HEAD_DIM)
        lg = lg_ref[pl.program_id(1) * RET_HEADS_PER_STEP + hh]
        decay = jnp.where(diff >= 0, jnp.exp(lg * jnp.maximum(diff, 0.0)), 0.0) * scale
        q_dec = jnp.exp(lg * (idx + 1.0))
        k_dec = jnp.exp(lg * (c - 1.0 - idx)) * scale
        chunk_dec = jnp.exp(lg * jnp.full((1, HEAD_DIM), float(c), F32))
        gw = gw_ref[:, hs]
        state = jnp.zeros((HEAD_DIM, HEAD_DIM), F32)
        for i in range(seq // c):
            r = i * c
            qi = q_ref[pl.ds(r, c), hs]
            ki = k_ref[pl.ds(r, c), hs]
            vi = v_ref[pl.ds(r, c), hs]
            inner = lax.dot_general(qi, ki, NT, preferred_element_type=F32) * decay
            o = (jnp.dot(inner.astype(BF16), vi, preferred_element_type=F32)
                 + jnp.dot(qi, state.astype(BF16), preferred_element_type=F32) * q_dec)
            vs = (vi.astype(F32) * k_dec).astype(BF16)
            state = state * chunk_dec + lax.dot_general(ki, vs, TN, preferred_element_type=F32)
            mu = jnp.mean(o, axis=-1, keepdims=True)
            oc = o - mu
            var = jnp.mean(oc * oc, axis=-1, keepdims=True)
            on = oc * lax.rsqrt(var + 1e-5) * gw
            g = g_ref[pl.ds(r, c), hs].astype(F32)
            o_ref[pl.ds(r, c), hs] = (_silu(g) * on).astype(o_ref.dtype)


def _retention(z, log_g, gn_w, w_ups, w_o, batch, seq):
    hp = RET_HEADS_PER_STEP
    width = hp * HEAD_DIM
    hsteps = RET_HEADS // hp
    steps = batch * hsteps
    kdim, n = w_ups[0].shape
    d = w_o.shape[0]
    up_rows, o_rows, nj = kdim // steps, d // steps, n // MERGE_TN
    assert up_rows % PACKED_ROWS == 0 and o_rows % PACKED_ROWS == 0, (kdim, d, steps)
    blk = lambda off: pl.BlockSpec((seq, width), lambda b, h: (b, off // hp + h))
    upspec = pl.BlockSpec((up_rows, n), lambda b, h: (b * hsteps + h, 0))
    upbspec = pl.BlockSpec((nj, up_rows, MERGE_TN), lambda b, h: (0, b * hsteps + h, 0))
    ospec = pl.BlockSpec((o_rows, d), lambda b, h: (b * hsteps + h, 0))
    return pl.pallas_call(
        _retention_kernel,
        grid=(batch, hsteps),
        in_specs=[pl.BlockSpec(memory_space=pltpu.SMEM),
                  blk(CB_RQ), blk(CB_RK), blk(CB_RV), blk(CB_RG),
                  pl.BlockSpec((1, width), lambda b, h: (0, h)),
                  upspec, upspec, upspec, ospec],
        out_specs=[pl.BlockSpec((seq, width), lambda b, h: (b, h)),
                   upbspec, upbspec, upbspec, ospec],
        out_shape=[jax.ShapeDtypeStruct((batch * seq, RET_HEADS * HEAD_DIM), BF16)]
        + [jax.ShapeDtypeStruct((nj, kdim, MERGE_TN), BF16)] * N_BRANCHES
        + [jax.ShapeDtypeStruct((d, d), BF16)],
        compiler_params=pltpu.CompilerParams(
            dimension_semantics=("arbitrary", "arbitrary"), vmem_limit_bytes=VMEM_LIMIT),
        name="retention",
    )(log_g, z, z, z, z, gn_w.reshape(1, -1), *w_ups, w_o)


DSA_T = 256
I16 = jnp.int16
PACKED_ROWS = 2 * SUBLANES
VT_ROWS = HEAD_DIM + PACKED_ROWS


def _fold_rows(x, op, rows=SUBLANES):
    while x.shape[0] > rows:
        half = x.shape[0] // 2
        x = op(x[:half], x[half:])
    return x


def _for_tiles_grouped(nk, step):
    def quad(i, carry):
        for u in range(4):
            step(4 * i + u)
        return carry

    lax.fori_loop(0, nk >> 2, quad, 0)
    done = (nk >> 2) << 2

    @pl.when((nk & 2) == 2)
    def _():
        step(done)
        step(done + 1)

    @pl.when((nk & 1) == 1)
    def _():
        step(nk - 1)


def _fold_tiles_grouped(nk, step, carry):
    def quad(i, c):
        for u in range(4):
            c = step(4 * i + u, c)
        return c

    carry = lax.fori_loop(0, nk >> 2, quad, carry)
    done = (nk >> 2) << 2
    carry = lax.cond((nk & 2) == 2, lambda c: step(done + 1, step(done, c)), lambda c: c, carry)
    return lax.cond((nk & 1) == 1, lambda c: step(nk - 1, c), lambda c: c, carry)


ORD_NEG_INF = -2**31 + 0x7FFFFF


def _flip(v):
    return v ^ ((v >> 31) & 0x7FFFFFFF)


def _ordinal_to_f32(o):
    return lax.bitcast_convert_type(_flip(jnp.maximum(o, ORD_NEG_INF)), F32)


def _prefix_to_bf16_bits(p):
    return _flip(jnp.maximum(p, ORD_NEG_INF >> 16) << 16) & -65536


def _dsa_kernel(iq_ref, ikw_ref, aq_ref, ak_ref, av_ref, ag_ref, o_ref,
                ika_ref, ikb_ref, vt_ref, wt_ref, qs_ref, score_ref, score16_ref, bias_ref,
                logit_ref,
                m_ref, acc_ref, *, top_k):
    t = DSA_T
    qb = pl.program_id(1)
    nk = qb + 1
    q0 = pl.multiple_of(qb * t, t)
    nq = vt_ref.shape[0]

    @pl.when(qb == 0)
    def _():
        ikw = ikw_ref[...].astype(F32)
        lane = lax.broadcasted_iota(I32, ikw.shape, 1)
        a = jnp.where(lane < IDX_DIM, ikw, 0.0)
        ika_ref[...] = a.astype(BF16)
        ikb_ref[...] = pltpu.roll(a, IDX_DIM, axis=1).astype(BF16)

        def vt_tile(kt, carry):
            k0 = pl.multiple_of(kt * t, t)
            for h in range(DSA_HEADS):
                hs = slice(h * HEAD_DIM, (h + 1) * HEAD_DIM)
                vt_ref[kt, h, :HEAD_DIM, :] = av_ref[pl.ds(k0, t), hs].astype(F32).T.astype(BF16)
                vt_ref[kt, h, HEAD_DIM:, :] = jnp.ones((VT_ROWS - HEAD_DIM, t), BF16)
            return carry

        lax.fori_loop(0, nq, vt_tile, 0)

    wt_ref[...] = ikw_ref[pl.ds(q0, t), :].astype(F32).T

    key_row = lax.broadcasted_iota(I32, (t, t), 0)
    qry_col = lax.broadcasted_iota(I32, (t, t), 1)

    def score_tile(kt):
        k0 = pl.multiple_of(kt * t, t)
        ka = ika_ref[pl.ds(k0, t), :]
        kb = ikb_ref[pl.ds(k0, t), :]
        acc = jnp.zeros((t, t), F32)
        for j in range(IDX_HEADS // 2):
            qj = iq_ref[:, j * LANE:(j + 1) * LANE]
            for half, kx in enumerate((ka, kb)):
                hh = IDX_DIM + 2 * j + half
                s = lax.dot_general(kx, qj, NT, preferred_element_type=F32)
                acc = acc + jnp.maximum(s, 0.0) * wt_ref[hh:hh + 1, :]
        causal = (key_row + k0) <= (qry_col + q0)
        score = jnp.where(causal, acc, -jnp.inf)
        score_ref[kt] = score
        score16_ref[kt] = score.astype(BF16)

    _for_tiles_grouped(nk, score_tile)

    one16, zero16 = jnp.ones((), I16), jnp.zeros((), I16)

    def prefix_step(i, pfx):
        cand_pfx = pfx + lax.shift_left(jnp.int32(1), 15 - i)
        cand = lax.bitcast_convert_type(_prefix_to_bf16_bits(cand_pfx), F32)
        cand16 = jnp.broadcast_to(cand, (PACKED_ROWS, t)).astype(BF16)[:1]

        def count_tile(kt, cnt):
            ge = jnp.where(score16_ref[kt] >= cand16, one16, zero16)
            return cnt + _fold_rows(ge, jnp.add, PACKED_ROWS)

        cnt = _fold_tiles_grouped(nk, count_tile, jnp.zeros((PACKED_ROWS, t), I16))
        total = jnp.sum(cnt.astype(I32), axis=0, keepdims=True)
        return jnp.where(total >= top_k, cand_pfx, pfx)

    pfx = lax.fori_loop(0, 16, prefix_step, jnp.full((1, t), -2**15, I32))

    def bisect_step(i, state):
        lo, hi, kept = state
        mid = lo + ((hi - lo) >> 1)
        cand = _ordinal_to_f32(mid)

        def count_tile(kt, cnt):
            return cnt + _fold_rows(jnp.where(score_ref[kt] >= cand, 1, 0), jnp.add)

        cnt = _fold_tiles_grouped(nk, count_tile, jnp.zeros((SUBLANES, t), I32))
        total = jnp.sum(cnt, axis=0, keepdims=True)
        enough = total >= top_k
        return (jnp.where(enough, mid, lo), jnp.where(enough, hi, mid),
                jnp.where(enough, total, kept))

    lo, _, kept = lax.fori_loop(
        0, 17, bisect_step, (_flip(_prefix_to_bf16_bits(pfx - 1)),
                             _flip(_prefix_to_bf16_bits(pfx + 1)),
                             jnp.full((1, t), top_k + 1, I32)))
    thr = _ordinal_to_f32(lo)

    def bias_tile(kt, carry):
        bias_ref[kt] = jnp.where(score_ref[kt] >= thr, 0.0, NEG_INF)
        return carry

    lax.fori_loop(0, qb, bias_tile, 0)
    keep = jnp.logical_and(score_ref[qb] >= thr, key_row <= qry_col)
    bias_ref[qb] = jnp.where(keep, 0.0, NEG_INF)

    @pl.when(jnp.max(kept) > top_k)
    def _():
        strictly_lower = (qry_col < key_row).astype(BF16)

        def tie_tile(kt, carry):
            above, tied_before = carry
            k0 = pl.multiple_of(kt * t, t)
            causal = (key_row + k0) <= (qry_col + q0)
            score = score_ref[kt]
            gt = jnp.logical_and(score > thr, causal)
            eq = jnp.logical_and(score == thr, causal)
            eq01 = jnp.where(eq, 1.0, 0.0)
            tied_here = tied_before + jnp.dot(strictly_lower, eq01.astype(BF16),
                                              preferred_element_type=F32)
            logit_ref[0, kt] = jnp.where(eq, tied_here, -1.0)
            above = above + _fold_rows(jnp.where(gt, 1.0, 0.0), jnp.add)
            return above, tied_before + jnp.sum(eq01, axis=0, keepdims=True)

        above, _ = lax.fori_loop(0, nk, tie_tile,
                                 (jnp.zeros((SUBLANES, t), F32), jnp.zeros((1, t), F32)))
        room = top_k - jnp.sum(above, axis=0, keepdims=True)

        def rebias_tile(kt, carry):
            k0 = pl.multiple_of(kt * t, t)
            gt = jnp.logical_and(score_ref[kt] > thr, (key_row + k0) <= (qry_col + q0))
            rank = logit_ref[0, kt]
            keep = jnp.logical_or(gt, jnp.logical_and(rank >= 0.0, rank < room))
            bias_ref[kt] = jnp.where(keep, 0.0, NEG_INF)
            return carry

        lax.fori_loop(0, nk, rebias_tile, 0)

    m_ref[...] = jnp.full(m_ref.shape, NEG_INF, F32)
    acc_ref[...] = jnp.zeros(acc_ref.shape, F32)
    log2e = 1.4426950408889634
    qs_ref[...] = (aq_ref[...].astype(F32) * ((HEAD_DIM ** -0.5) * log2e)).astype(BF16)
    key_row_1 = lax.broadcasted_iota(I32, (t, LANE), 0)

    def logit_tile(kt):
        k0 = pl.multiple_of(kt * t, t)
        bias = bias_ref[kt]
        kpos = (key_row_1 + (k0 - q0)).astype(F32)
        for h in range(DSA_HEADS):
            slope = (2.0 ** (-8.0 * (h + 1) / DSA_HEADS)) * log2e
            hs = slice(h * HEAD_DIM, (h + 1) * HEAD_DIM)
            s = lax.dot_general(ak_ref[pl.ds(k0, t), hs], qs_ref[:, hs], NT,
                                preferred_element_type=F32)
            alibi = kpos * slope
            lg = s + jnp.concatenate([alibi] * (t // LANE), axis=1) + bias
            logit_ref[h, kt] = lg
            m_ref[h] = jnp.maximum(m_ref[h], _fold_rows(lg, jnp.maximum))

    _for_tiles_grouped(nk, logit_tile)

    for h in range(DSA_HEADS):
        m_ref[h] = jnp.broadcast_to(jnp.max(m_ref[h], axis=0, keepdims=True), (SUBLANES, t))

    def pv_tile(kt):
        for h in range(DSA_HEADS):
            p = jnp.exp2(logit_ref[h, kt] - m_ref[h][:1])
            acc_ref[h] = acc_ref[h] + jnp.dot(vt_ref[kt, h], p.astype(BF16),
                                              preferred_element_type=F32)

    _for_tiles_grouped(nk, pv_tile)

    for h in range(DSA_HEADS):
        hs = slice(h * HEAD_DIM, (h + 1) * HEAD_DIM)
        acc = acc_ref[h]
        o = (acc[:HEAD_DIM] / acc[HEAD_DIM:HEAD_DIM + 1]).T
        o_ref[:, hs] = (_silu(ag_ref[:, hs].astype(F32)) * o).astype(o_ref.dtype)


def _dsa(z, ikw, batch, seq, top_k):
    t = DSA_T
    nq = seq // t
    width = DSA_HEADS * HEAD_DIM
    wblk = width // LANE
    qspec = lambda off: pl.BlockSpec((t, width), lambda b, i: (b * nq + i, off // wblk))
    kvspec = lambda off: pl.BlockSpec((seq, width), lambda b, i: (b, off // wblk))
    return pl.pallas_call(
        functools.partial(_dsa_kernel, top_k=top_k),
        grid=(batch, nq),
        in_specs=[qspec(CB_IQ),
                  pl.BlockSpec((seq, LANE), lambda b, i: (b, 0)),
                  qspec(CB_AQ), kvspec(CB_AK), kvspec(CB_AV), qspec(CB_AG)],
        out_specs=pl.BlockSpec((t, width), lambda b, i: (b * nq + i, 0)),
        out_shape=jax.ShapeDtypeStruct((batch * seq, width), BF16),
        scratch_shapes=[pltpu.VMEM((seq, LANE), BF16),
                        pltpu.VMEM((seq, LANE), BF16),
                        pltpu.VMEM((nq, DSA_HEADS, VT_ROWS, t), BF16),
                        pltpu.VMEM((LANE, t), F32),
                        pltpu.VMEM((t, width), BF16),
                        pltpu.VMEM((nq, t, t), F32),
                        pltpu.VMEM((nq, t, t), BF16),
                        pltpu.VMEM((nq, t, t), F32),
                        pltpu.VMEM((DSA_HEADS, nq, t, t), F32),
                        pltpu.VMEM((DSA_HEADS, SUBLANES, t), F32),
                        pltpu.VMEM((DSA_HEADS, VT_ROWS, t), F32)],
        compiler_params=pltpu.CompilerParams(
            dimension_semantics=("arbitrary", "arbitrary"), vmem_limit_bytes=VMEM_LIMIT),
        name="dsa",
    )(z, ikw, z, z, z, z)


def _memattn_kernel(q_ref, g_ref, mem_ref, nw_ref, w_ref, o_ref, wbf_ref, kv_ref):
    b = pl.program_id(0)
    i = pl.program_id(1)
    width = MEM_HEADS * MEM_HEAD_DIM

    @pl.when(jnp.logical_and(b == 0, i == 0))
    def _():
        for r in range(0, w_ref.shape[0], PROJ_PREP_ROWS):
            wbf_ref[r:r + PROJ_PREP_ROWS, :] = w_ref[r:r + PROJ_PREP_ROWS, :].astype(BF16)

    @pl.when(i == 0)
    def _():
        x = mem_ref[...]
        ms = jnp.mean(x * x, axis=-1, keepdims=True)
        hn = (x * lax.rsqrt(ms + EPS) * nw_ref[...]).astype(BF16)
        kv_ref[...] = jnp.dot(hn, wbf_ref[...], preferred_element_type=F32).astype(BF16)

    scale = MEM_HEAD_DIM ** -0.5
    for h in range(MEM_HEADS):
        hs = slice(h * MEM_HEAD_DIM, (h + 1) * MEM_HEAD_DIM)
        vs = slice(width + h * MEM_HEAD_DIM, width + (h + 1) * MEM_HEAD_DIM)
        s = lax.dot_general(q_ref[:, hs], kv_ref[:, hs], NT, preferred_element_type=F32) * scale
        p = jnp.exp(s - jnp.max(s, axis=-1, keepdims=True))
        l = jnp.sum(p, axis=-1, keepdims=True)
        o = jnp.dot(p.astype(BF16), kv_ref[:, vs], preferred_element_type=F32) / l
        o_ref[:, hs] = (_silu(g_ref[:, hs].astype(F32)) * o).astype(o_ref.dtype)


def _memattn(z, mem2d, norm_w, w_kv, batch, seq, mem_tokens):
    tl = MEM_TL
    d = mem2d.shape[1]
    width = MEM_HEADS * MEM_HEAD_DIM
    wblk = width // LANE
    nl = seq // tl
    return pl.pallas_call(
        _memattn_kernel,
        grid=(batch, nl),
        in_specs=[pl.BlockSpec((tl, width), lambda b, i: (b * nl + i, CB_MQ // wblk)),
                  pl.BlockSpec((tl, width), lambda b, i: (b * nl + i, CB_MG // wblk)),
                  pl.BlockSpec((mem_tokens, d), lambda b, i: (b, 0)),
                  pl.BlockSpec((1, d), lambda b, i: (0, 0)),
                  pl.BlockSpec((d, 2 * width), lambda b, i: (0, 0), pipeline_mode=SINGLE_BUFFER)],
        out_specs=pl.BlockSpec((tl, width), lambda b, i: (b * nl + i, 0)),
        out_shape=jax.ShapeDtypeStruct((batch * seq, width), BF16),
        scratch_shapes=[pltpu.VMEM((d, 2 * width), BF16),
                        pltpu.VMEM((mem_tokens, 2 * width), BF16)],
        compiler_params=pltpu.CompilerParams(
            dimension_semantics=("arbitrary", "arbitrary"), vmem_limit_bytes=VMEM_LIMIT),
        name="mem_attn",
    )(z, z, mem2d, norm_w.reshape(1, d), w_kv)


def _merge_kernel(yr_ref, yd_ref, ym_ref, wr_ref, wd_ref, wm_ref, gr_ref, gd_ref, gm_ref, o_ref):
    j = pl.program_id(1)

    def branch(y_ref, w_ref, g_ref):
        up = jnp.dot(y_ref[...], w_ref[j], preferred_element_type=F32)
        return _sigmoid(g_ref[...].astype(F32)) * up

    merged = (branch(yr_ref, wr_ref, gr_ref) + branch(yd_ref, wd_ref, gd_ref)
              + branch(ym_ref, wm_ref, gm_ref))
    o_ref[...] = merged.astype(o_ref.dtype)


def _merge(y_ret, y_dsa, y_mem, w_ret, w_dsa, w_mem, z):
    tm = MERGE_TM
    m, kdim = y_ret.shape
    nj, _, tn = w_ret.shape
    gblk = tn // LANE
    yspec = pl.BlockSpec((tm, kdim), lambda i, j: (i, 0))
    wspec = pl.BlockSpec((nj, kdim, tn), lambda i, j: (0, 0, 0), pipeline_mode=SINGLE_BUFFER)
    gspec = lambda off: pl.BlockSpec((tm, tn), lambda i, j: (i, off // gblk + j))
    return pl.pallas_call(
        _merge_kernel,
        grid=(m // tm, nj),
        in_specs=[yspec, yspec, yspec, wspec, wspec, wspec,
                  gspec(CB_GRET), gspec(CB_GDSA), gspec(CB_GMEM)],
        out_specs=pl.BlockSpec((tm, tn), lambda i, j: (i, j)),
        out_shape=jax.ShapeDtypeStruct((m, nj * tn), BF16),
        compiler_params=pltpu.CompilerParams(
            dimension_semantics=("arbitrary", "arbitrary"), vmem_limit_bytes=VMEM_LIMIT),
        name="merge",
    )(y_ret, y_dsa, y_mem, w_ret, w_dsa, w_mem, z, z, z)


def _out_kernel(m_ref, w_ref, x_ref, nw_ref, o_ref):
    out = jnp.dot(m_ref[...], w_ref[...], preferred_element_type=F32)
    ms = jnp.mean(out * out, axis=-1, keepdims=True)
    o_ref[...] = x_ref[...] + out * lax.rsqrt(ms + EPS) * nw_ref[...]


def _out_proj(merged, w_o, x2d, post_w):
    tm = OUT_TM
    m, d = x2d.shape
    return pl.pallas_call(
        _out_kernel,
        grid=(m // tm,),
        in_specs=[pl.BlockSpec((tm, d), lambda i: (i, 0)),
                  pl.BlockSpec((d, d), lambda i: (0, 0), pipeline_mode=SINGLE_BUFFER),
                  pl.BlockSpec((tm, d), lambda i: (i, 0)),
                  pl.BlockSpec((1, d), lambda i: (0, 0))],
        out_specs=pl.BlockSpec((tm, d), lambda i: (i, 0)),
        out_shape=jax.ShapeDtypeStruct((m, d), F32),
        compiler_params=pltpu.CompilerParams(
            dimension_semantics=("arbitrary",), vmem_limit_bytes=VMEM_LIMIT),
        name="out_proj_postnorm",
    )(merged, w_o, x2d, post_w.reshape(1, d))


def _layer(x, mem, pre_norm_w, w_in, ret_gn_w, mem_norm_w, w_mem_kv,
           w_up_ret, w_up_dsa, w_up_mem, w_o, post_norm_w):
    batch, seq, d = x.shape
    mem_tokens = mem.shape[1]
    top_k = min(IDX_TOPK_MAX, seq // 4)
    assert w_in.shape == (d, Z_COLS + IDX_TAIL_COLS), w_in.shape
    assert seq % max(DSA_T, RET_C, MEM_TL) == 0 and seq < 2**15, seq
    assert (batch * seq) % max(MERGE_TM, OUT_TM) == 0 and top_k <= DSA_T, (batch, seq)
    x2d = x.reshape(batch * seq, d)

    w_in_t = w_in.T
    w_ikw = jnp.pad(w_in_t[IDX_TAIL_START:IDX_TAIL_START + IDX_TAIL_COLS],
                    ((0, LANE - IDX_TAIL_COLS), (0, 0))).astype(BF16)
    h, ikw = _prenorm(x2d, pre_norm_w, w_ikw)
    z = _project(h, w_in_t)

    log_g = jnp.log1p(-(2.0 ** (-5.0 - jnp.arange(RET_HEADS, dtype=F32))))
    y_ret, wb_ret, wb_dsa, wb_mem, wb_o = _retention(
        z, log_g, ret_gn_w, (w_up_ret, w_up_dsa, w_up_mem), w_o, batch, seq)
    y_dsa = _dsa(z, ikw, batch, seq, top_k)
    y_mem = _memattn(z, mem.reshape(batch * mem_tokens, d), mem_norm_w, w_mem_kv, batch, seq,
                     mem_tokens)

    merged = _merge(y_ret, y_dsa, y_mem, wb_ret, wb_dsa, wb_mem, z)
    out = _out_proj(merged, wb_o, x2d, post_norm_w)
    return out.reshape(batch, seq, d)


def kernel(x, mem, pre_norm_w, w_in, ret_gn_w, mem_norm_w, w_mem_kv, w_up_ret, w_up_dsa,
           w_up_mem, w_o, post_norm_w):
    for layer in range(w_in.shape[0]):
        x = _layer(x, mem, pre_norm_w[layer], w_in[layer], ret_gn_w[layer], mem_norm_w[layer],
                   w_mem_kv[layer], w_up_ret[layer], w_up_dsa[layer], w_up_mem[layer],
                   w_o[layer], post_norm_w[layer])
    return x
```

```python
import functools

import jax
import jax.numpy as jnp
from jax import lax
from jax.experimental import pallas as pl
from jax.experimental.pallas import tpu as pltpu

F32 = jnp.float32
BF16 = jnp.bfloat16
I32 = jnp.int32

LANE = 128
SUBLANES = 8
N_BRANCHES = 3
RET_HEADS = 8
HEAD_DIM = 128
DSA_HEADS = 8
IDX_HEADS = 16
IDX_DIM = 64
IDX_TOPK_MAX = 256
MEM_HEADS = 4
MEM_HEAD_DIM = 256
NEG_INF = -1e30
EPS = 1e-6
INT_MIN = -2**31

CB_RQ, CB_RK, CB_RV, CB_RG = 0, 8, 16, 24
CB_AQ, CB_AK, CB_AV, CB_AG = 32, 40, 48, 56
CB_IQ = 64
CB_MQ, CB_MG = 72, 80
CB_GRET, CB_GDSA, CB_GMEM = 88, 104, 120
N_COL_BLOCKS = 136
Z_COLS = N_COL_BLOCKS * LANE
IDX_TAIL_START = (CB_IQ + IDX_HEADS * IDX_DIM // LANE) * LANE
IDX_TAIL_COLS = IDX_DIM + IDX_HEADS

PROJ_TM = 2048
PROJ_TN = 1024
PROJ_ALIGNED_TILES = IDX_TAIL_START // PROJ_TN
PROJ_ROWS = 1024
PROJ_PREP_ROWS = 256
PRENORM_TM = 1024
MEM_TL = 1024
MERGE_TM, MERGE_TN = 1024, 1024
OUT_TM = 512
VMEM_LIMIT = 56 * 1024 * 1024
SINGLE_BUFFER = pl.Buffered(1)

NT = (((1,), (1,)), ((), ()))
TN = (((0,), (0,)), ((), ()))


def _sigmoid(v):
    return 0.5 * jnp.tanh(0.5 * v) + 0.5


def _silu(v):
    return v * _sigmoid(v)


def _prenorm_kernel(x_ref, w_ref, wikw_ref, h_ref, ikw_ref):
    x = x_ref[...]
    ms = jnp.mean(x * x, axis=-1, keepdims=True)
    h = (x * lax.rsqrt(ms + EPS) * w_ref[...]).astype(h_ref.dtype)
    h_ref[...] = h
    ikw_ref[...] = lax.dot_general(h, wikw_ref[...], NT,
                                   preferred_element_type=F32).astype(ikw_ref.dtype)


def _prenorm(x2d, w, w_ikw):
    m, d = x2d.shape
    tm = min(PRENORM_TM, m)
    return pl.pallas_call(
        _prenorm_kernel,
        grid=(m // tm,),
        in_specs=[pl.BlockSpec((tm, d), lambda i: (i, 0)),
                  pl.BlockSpec((1, d), lambda i: (0, 0)),
                  pl.BlockSpec((LANE, d), lambda i: (0, 0))],
        out_specs=[pl.BlockSpec((tm, d), lambda i: (i, 0)),
                   pl.BlockSpec((tm, LANE), lambda i: (i, 0))],
        out_shape=[jax.ShapeDtypeStruct((m, d), BF16),
                   jax.ShapeDtypeStruct((m, LANE), BF16)],
        compiler_params=pltpu.CompilerParams(dimension_semantics=("arbitrary",)),
        name="prenorm_idx",
    )(x2d, w.reshape(1, d), w_ikw)


def _proj_kernel(h_ref, w_ref, z_ref, wbf_ref):
    @pl.when(pl.program_id(1) == 0)
    def _():
        for r in range(0, w_ref.shape[0], PROJ_PREP_ROWS):
            wbf_ref[r:r + PROJ_PREP_ROWS, :] = w_ref[r:r + PROJ_PREP_ROWS, :].astype(BF16)

    w = wbf_ref[...]
    for r in range(0, h_ref.shape[0], PROJ_ROWS):
        z_ref[r:r + PROJ_ROWS, :] = lax.dot_general(
            h_ref[r:r + PROJ_ROWS, :], w, NT, preferred_element_type=F32).astype(z_ref.dtype)


def _project(h, w_in_t):
    m, d = h.shape
    tm = min(PROJ_TM, m)

    def w_rows(j, i):
        skip = jnp.where(j >= PROJ_ALIGNED_TILES, IDX_TAIL_COLS // SUBLANES, 0)
        return ((j * (PROJ_TN // SUBLANES) + skip) * SUBLANES, 0)

    return pl.pallas_call(
        _proj_kernel,
        grid=(Z_COLS // PROJ_TN, m // tm),
        in_specs=[pl.BlockSpec((tm, d), lambda j, i: (i, 0)),
                  pl.BlockSpec((pl.Element(PROJ_TN), pl.Element(d)), w_rows)],
        out_specs=pl.BlockSpec((tm, PROJ_TN), lambda j, i: (i, j)),
        out_shape=jax.ShapeDtypeStruct((m, Z_COLS), BF16),
        scratch_shapes=[pltpu.VMEM((PROJ_TN, d), BF16)],
        compiler_params=pltpu.CompilerParams(
            dimension_semantics=("arbitrary", "arbitrary"), vmem_limit_bytes=VMEM_LIMIT),
        name="in_proj",
    )(h, w_in_t)


RET_C = 256
RET_HEADS_PER_STEP = 4


def _retention_kernel(lg_ref, q_ref, k_ref, v_ref, g_ref, gw_ref, wr_ref, wd_ref, wm_ref, wo_ref,
                      o_ref, wrb_ref, wdb_ref, wmb_ref, wob_ref):
    for w_ref, wb_ref in ((wr_ref, wrb_ref), (wd_ref, wdb_ref), (wm_ref, wmb_ref)):
        for jj in range(wb_ref.shape[0]):
            wb_ref[jj] = w_ref[:, jj * MERGE_TN:(jj + 1) * MERGE_TN].astype(BF16)
    wob_ref[...] = wo_ref[...].astype(BF16)

    c = RET_C
    seq = q_ref.shape[0]
    scale = HEAD_DIM ** -0.5
    ri = lax.broadcasted_iota(I32, (c, c), 0)
    ci = lax.broadcasted_iota(I32, (c, c), 1)
    diff = (ri - ci).astype(F32)
    idx = lax.broadcasted_iota(I32, (c, 1), 0).astype(F32)

    for hh in range(RET_HEADS_PER_STEP):
        hs = slice(hh * HEAD_DIM, (hh + 1) * HEAD_DIM)
        lg = lg_ref[pl.program_id(1) * RET_HEADS_PER_STEP + hh]
        decay = jnp.where(diff >= 0, jnp.exp(lg * jnp.maximum(diff, 0.0)), 0.0) * scale
        q_dec = jnp.exp(lg * (idx + 1.0))
        k_dec = jnp.exp(lg * (c - 1.0 - idx)) * scale
        chunk_dec = jnp.exp(lg * jnp.full((1, HEAD_DIM), float(c), F32))
        gw = gw_ref[:, hs]
        state = jnp.zeros((HEAD_DIM, HEAD_DIM), F32)
        for i in range(seq // c):
            r = i * c
            qi = q_ref[pl.ds(r, c), hs]
            ki = k_ref[pl.ds(r, c), hs]
            vi = v_ref[pl.ds(r, c), hs]
            inner = lax.dot_general(qi, ki, NT, preferred_element_type=F32) * decay
            o = (jnp.dot(inner.astype(BF16), vi, preferred_element_type=F32)
                 + jnp.dot(qi, state.astype(BF16), preferred_element_type=F32) * q_dec)
            vs = (vi.astype(F32) * k_dec).astype(BF16)
            state = state * chunk_dec + lax.dot_general(ki, vs, TN, preferred_element_type=F32)
            mu = jnp.mean(o, axis=-1, keepdims=True)
            oc = o - mu
            var = jnp.mean(oc * oc, axis=-1, keepdims=True)
            on = oc * lax.rsqrt(var + 1e-5) * gw
            g = g_ref[pl.ds(r, c), hs].astype(F32)
            o_ref[pl.ds(r, c), hs] = (_silu(g) * on).astype(o_ref.dtype)


def _retention(z, log_g, gn_w, w_ups, w_o, batch, seq):
    hp = RET_HEADS_PER_STEP
    width = hp * HEAD_DIM
    hsteps = RET_HEADS // hp
    steps = batch * hsteps
    kdim, n = w_ups[0].shape
    d = w_o.shape[0]
    up_rows, o_rows, nj = kdim // steps, d // steps, n // MERGE_TN
    assert up_rows % PACKED_ROWS == 0 and o_rows % PACKED_ROWS == 0, (kdim, d, steps)
    blk = lambda off: pl.BlockSpec((seq, width), lambda b, h: (b, off // hp + h))
    upspec = pl.BlockSpec((up_rows, n), lambda b, h: (b * hsteps + h, 0))
    upbspec = pl.BlockSpec((nj, up_rows, MERGE_TN), lambda b, h: (0, b * hsteps + h, 0))
    ospec = pl.BlockSpec((o_rows, d), lambda b, h: (b * hsteps + h, 0))
    return pl.pallas_call(
        _retention_kernel,
        grid=(batch, hsteps),
        in_specs=[pl.BlockSpec(memory_space=pltpu.SMEM),
                  blk(CB_RQ), blk(CB_RK), blk(CB_RV), blk(CB_RG),
                  pl.BlockSpec((1, width), lambda b, h: (0, h)),
                  upspec, upspec, upspec, ospec],
        out_specs=[pl.BlockSpec((seq, width), lambda b, h: (b, h)),
                   upbspec, upbspec, upbspec, ospec],
        out_shape=[jax.ShapeDtypeStruct((batch * seq, RET_HEADS * HEAD_DIM), BF16)]
        + [jax.ShapeDtypeStruct((nj, kdim, MERGE_TN), BF16)] * N_BRANCHES
        + [jax.ShapeDtypeStruct((d, d), BF16)],
        compiler_params=pltpu.CompilerParams(
            dimension_semantics=("arbitrary", "arbitrary"), vmem_limit_bytes=VMEM_LIMIT),
        name="retention",
    )(log_g, z, z, z, z, gn_w.reshape(1, -1), *w_ups, w_o)


DSA_T = 256
I16 = jnp.int16
PACKED_ROWS = 2 * SUBLANES
VT_ROWS = HEAD_DIM + PACKED_ROWS


def _fold_rows(x, op, rows=SUBLANES):
    while x.shape[0] > rows:
        half = x.shape[0] // 2
        x = op(x[:half], x[half:])
    return x


def _for_tiles_grouped(nk, step):
    def quad(i, carry):
        for u in range(4):
            step(4 * i + u)
        return carry

    lax.fori_loop(0, nk >> 2, quad, 0)
    done = (nk >> 2) << 2

    @pl.when((nk & 2) == 2)
    def _():
        step(done)
        step(done + 1)

    @pl.when((nk & 1) == 1)
    def _():
        step(nk - 1)


def _fold_tiles_grouped(nk, step, carry):
    def quad(i, c):
        for u in range(4):
            c = step(4 * i + u, c)
        return c

    carry = lax.fori_loop(0, nk >> 2, quad, carry)
    done = (nk >> 2) << 2
    carry = lax.cond((nk & 2) == 2, lambda c: step(done + 1, step(done, c)), lambda c: c, carry)
    return lax.cond((nk & 1) == 1, lambda c: step(nk - 1, c), lambda c: c, carry)


ORD_NEG_INF = -2**31 + 0x7FFFFF


def _flip(v):
    return v ^ ((v >> 31) & 0x7FFFFFFF)


def _ordinal_to_f32(o):
    return lax.bitcast_convert_type(_flip(jnp.maximum(o, ORD_NEG_INF)), F32)


def _prefix_to_bf16_bits(p):
    return _flip(jnp.maximum(p, ORD_NEG_INF >> 16) << 16) & -65536


def _dsa_kernel(iq_ref, ikw_ref, aq_ref, ak_ref, av_ref, ag_ref, o_ref,
                ika_ref, ikb_ref, vt_ref, wt_ref, qs_ref, score_ref, score16_ref, bias_ref,
                logit_ref,
                m_ref, acc_ref, *, top_k):
    t = DSA_T
    qb = pl.program_id(1)
    nk = qb + 1
    q0 = pl.multiple_of(qb * t, t)
    nq = vt_ref.shape[0]

    @pl.when(qb == 0)
    def _():
        ikw = ikw_ref[...].astype(F32)
        lane = lax.broadcasted_iota(I32, ikw.shape, 1)
        a = jnp.where(lane < IDX_DIM, ikw, 0.0)
        ika_ref[...] = a.astype(BF16)
        ikb_ref[...] = pltpu.roll(a, IDX_DIM, axis=1).astype(BF16)

        def vt_tile(kt, carry):
            k0 = pl.multiple_of(kt * t, t)
            for h in range(DSA_HEADS):
                hs = slice(h * HEAD_DIM, (h + 1) * HEAD_DIM)
                vt_ref[kt, h, :HEAD_DIM, :] = av_ref[pl.ds(k0, t), hs].astype(F32).T.astype(BF16)
                vt_ref[kt, h, HEAD_DIM:, :] = jnp.ones((VT_ROWS - HEAD_DIM, t), BF16)
            return carry

        lax.fori_loop(0, nq, vt_tile, 0)

    wt_ref[...] = ikw_ref[pl.ds(q0, t), :].astype(F32).T

    key_row = lax.broadcasted_iota(I32, (t, t), 0)
    qry_col = lax.broadcasted_iota(I32, (t, t), 1)

    def score_tile(kt):
        k0 = pl.multiple_of(kt * t, t)
        ka = ika_ref[pl.ds(k0, t), :]
        kb = ikb_ref[pl.ds(k0, t), :]
        acc = jnp.zeros((t, t), F32)
        for j in range(IDX_HEADS // 2):
            qj = iq_ref[:, j * LANE:(j + 1) * LANE]
            for half, kx in enumerate((ka, kb)):
                hh = IDX_DIM + 2 * j + half
                s = lax.dot_general(kx, qj, NT, preferred_element_type=F32)
                acc = acc + jnp.maximum(s, 0.0) * wt_ref[hh:hh + 1, :]
        causal = (key_row + k0) <= (qry_col + q0)
        score = jnp.where(causal, acc, -jnp.inf)
        score_ref[kt] = score
        score16_ref[kt] = score.astype(BF16)

    _for_tiles_grouped(nk, score_tile)

    one16, zero16 = jnp.ones((), I16), jnp.zeros((), I16)

    def prefix_step(i, pfx):
        cand_pfx = pfx + lax.shift_left(jnp.int32(1), 15 - i)
        cand = lax.bitcast_convert_type(_prefix_to_bf16_bits(cand_pfx), F32)
        cand16 = jnp.broadcast_to(cand, (PACKED_ROWS, t)).astype(BF16)[:1]

        def count_tile(kt, cnt):
            ge = jnp.where(score16_ref[kt] >= cand16, one16, zero16)
            return cnt + _fold_rows(ge, jnp.add, PACKED_ROWS)

        cnt = _fold_tiles_grouped(nk, count_tile, jnp.zeros((PACKED_ROWS, t), I16))
        total = jnp.sum(cnt.astype(I32), axis=0, keepdims=True)
        return jnp.where(total >= top_k, cand_pfx, pfx)

    pfx = lax.fori_loop(0, 16, prefix_step, jnp.full((1, t), -2**15, I32))

    def bisect_step(i, state):
        lo, hi, kept = state
        mid = lo + ((hi - lo) >> 1)
        cand = _ordinal_to_f32(mid)

        def count_tile(kt, cnt):
            return cnt + _fold_rows(jnp.where(score_ref[kt] >= cand, 1, 0), jnp.add)

        cnt = _fold_tiles_grouped(nk, count_tile, jnp.zeros((SUBLANES, t), I32))
        total = jnp.sum(cnt, axis=0, keepdims=True)
        enough = total >= top_k
        return (jnp.where(enough, mid, lo), jnp.where(enough, hi, mid),
                jnp.where(enough, total, kept))

    lo, _, kept = lax.fori_loop(
        0, 17, bisect_step, (_flip(_prefix_to_bf16_bits(pfx - 1)),
                             _flip(_prefix_to_bf16_bits(pfx + 1)),
                             jnp.full((1, t), top_k + 1, I32)))
    thr = _ordinal_to_f32(lo)

    def bias_tile(kt, carry):
        bias_ref[kt] = jnp.where(score_ref[kt] >= thr, 0.0, NEG_INF)
        return carry

    lax.fori_loop(0, qb, bias_tile, 0)
    keep = jnp.logical_and(score_ref[qb] >= thr, key_row <= qry_col)
    bias_ref[qb] = jnp.where(keep, 0.0, NEG_INF)

    @pl.when(jnp.max(kept) > top_k)
    def _():
        strictly_lower = (qry_col < key_row).astype(BF16)

        def tie_tile(kt, carry):
            above, tied_before = carry
            k0 = pl.multiple_of(kt * t, t)
            causal = (key_row + k0) <= (qry_col + q0)
            score = score_ref[kt]
            gt = jnp.logical_and(score > thr, causal)
            eq = jnp.logical_and(score == thr, causal)
            eq01 = jnp.where(eq, 1.0, 0.0)
            tied_here = tied_before + jnp.dot(strictly_lower, eq01.astype(BF16),
                                              preferred_element_type=F32)
            logit_ref[0, kt] = jnp.where(eq, tied_here, -1.0)
            above = above + _fold_rows(jnp.where(gt, 1.0, 0.0), jnp.add)
            return above, tied_before + jnp.sum(eq01, axis=0, keepdims=True)

        above, _ = lax.fori_loop(0, nk, tie_tile,
                                 (jnp.zeros((SUBLANES, t), F32), jnp.zeros((1, t), F32)))
        room = top_k - jnp.sum(above, axis=0, keepdims=True)

        def rebias_tile(kt, carry):
            k0 = pl.multiple_of(kt * t, t)
            gt = jnp.logical_and(score_ref[kt] > thr, (key_row + k0) <= (qry_col + q0))
            rank = logit_ref[0, kt]
            keep = jnp.logical_or(gt, jnp.logical_and(rank >= 0.0, rank < room))
            bias_ref[kt] = jnp.where(keep, 0.0, NEG_INF)
            return carry

        lax.fori_loop(0, nk, rebias_tile, 0)

    m_ref[...] = jnp.full(m_ref.shape, NEG_INF, F32)
    acc_ref[...] = jnp.zeros(acc_ref.shape, F32)
    log2e = 1.4426950408889634
    qs_ref[...] = (aq_ref[...].astype(F32) * ((HEAD_DIM ** -0.5) * log2e)).astype(BF16)
    key_row_1 = lax.broadcasted_iota(I32, (t, LANE), 0)

    def logit_tile(kt):
        k0 = pl.multiple_of(kt * t, t)
        bias = bias_ref[kt]
        kpos = (key_row_1 + (k0 - q0)).astype(F32)
        for h in range(DSA_HEADS):
            slope = (2.0 ** (-8.0 * (h + 1) / DSA_HEADS)) * log2e
            hs = slice(h * HEAD_DIM, (h + 1) * HEAD_DIM)
            s = lax.dot_general(ak_ref[pl.ds(k0, t), hs], qs_ref[:, hs], NT,
                                preferred_element_type=F32)
            alibi = kpos * slope
            lg = s + jnp.concatenate([alibi] * (t // LANE), axis=1) + bias
            logit_ref[h, kt] = lg
            m_ref[h] = jnp.maximum(m_ref[h], _fold_rows(lg, jnp.maximum))

    _for_tiles_grouped(nk, logit_tile)

    for h in range(DSA_HEADS):
        m_ref[h] = jnp.broadcast_to(jnp.max(m_ref[h], axis=0, keepdims=True), (SUBLANES, t))

    def pv_tile(kt):
        for h in range(DSA_HEADS):
            p = jnp.exp2(logit_ref[h, kt] - m_ref[h][:1])
            acc_ref[h] = acc_ref[h] + jnp.dot(vt_ref[kt, h], p.astype(BF16),
                                              preferred_element_type=F32)

    _for_tiles_grouped(nk, pv_tile)

    for h in range(DSA_HEADS):
        hs = slice(h * HEAD_DIM, (h + 1) * HEAD_DIM)
        acc = acc_ref[h]
        o = (acc[:HEAD_DIM] / acc[HEAD_DIM:HEAD_DIM + 1]).T
        o_ref[:, hs] = (_silu(ag_ref[:, hs].astype(F32)) * o).astype(o_ref.dtype)


def _dsa(z, ikw, batch, seq, top_k):
    t = DSA_T
    nq = seq // t
    width = DSA_HEADS * HEAD_DIM
    wblk = width // LANE
    qspec = lambda off: pl.BlockSpec((t, width), lambda b, i: (b * nq + i, off // wblk))
    kvspec = lambda off: pl.BlockSpec((seq, width), lambda b, i: (b, off // wblk))
    return pl.pallas_call(
        functools.partial(_dsa_kernel, top_k=top_k),
        grid=(batch, nq),
        in_specs=[qspec(CB_IQ),
                  pl.BlockSpec((seq, LANE), lambda b, i: (b, 0)),
                  qspec(CB_AQ), kvspec(CB_AK), kvspec(CB_AV), qspec(CB_AG)],
        out_specs=pl.BlockSpec((t, width), lambda b, i: (b * nq + i, 0)),
        out_shape=jax.ShapeDtypeStruct((batch * seq, width), BF16),
        scratch_shapes=[pltpu.VMEM((seq, LANE), BF16),
                        pltpu.VMEM((seq, LANE), BF16),
                        pltpu.VMEM((nq, DSA_HEADS, VT_ROWS, t), BF16),
                        pltpu.VMEM((LANE, t), F32),
                        pltpu.VMEM((t, width), BF16),
                        pltpu.VMEM((nq, t, t), F32),
                        pltpu.VMEM((nq, t, t), BF16),
                        pltpu.VMEM((nq, t, t), F32),
                        pltpu.VMEM((DSA_HEADS, nq, t, t), F32),
                        pltpu.VMEM((DSA_HEADS, SUBLANES, t), F32),
                        pltpu.VMEM((DSA_HEADS, VT_ROWS, t), F32)],
        compiler_params=pltpu.CompilerParams(
            dimension_semantics=("arbitrary", "arbitrary"), vmem_limit_bytes=VMEM_LIMIT),
        name="dsa",
    )(z, ikw, z, z, z, z)


def _memattn_kernel(q_ref, g_ref, mem_ref, nw_ref, w_ref, o_ref, wbf_ref, kv_ref):
    b = pl.program_id(0)
    i = pl.program_id(1)
    width = MEM_HEADS * MEM_HEAD_DIM

    @pl.when(jnp.logical_and(b == 0, i == 0))
    def _():
        for r in range(0, w_ref.shape[0], PROJ_PREP_ROWS):
            wbf_ref[r:r + PROJ_PREP_ROWS, :] = w_ref[r:r + PROJ_PREP_ROWS, :].astype(BF16)

    @pl.when(i == 0)
    def _():
        x = mem_ref[...]
        ms = jnp.mean(x * x, axis=-1, keepdims=True)
        hn = (x * lax.rsqrt(ms + EPS) * nw_ref[...]).astype(BF16)
        kv_ref[...] = jnp.dot(hn, wbf_ref[...], preferred_element_type=F32).astype(BF16)

    scale = MEM_HEAD_DIM ** -0.5
    for h in range(MEM_HEADS):
        hs = slice(h * MEM_HEAD_DIM, (h + 1) * MEM_HEAD_DIM)
        vs = slice(width + h * MEM_HEAD_DIM, width + (h + 1) * MEM_HEAD_DIM)
        s = lax.dot_general(q_ref[:, hs], kv_ref[:, hs], NT, preferred_element_type=F32) * scale
        p = jnp.exp(s - jnp.max(s, axis=-1, keepdims=True))
        l = jnp.sum(p, axis=-1, keepdims=True)
        o = jnp.dot(p.astype(BF16), kv_ref[:, vs], preferred_element_type=F32) / l
        o_ref[:, hs] = (_silu(g_ref[:, hs].astype(F32)) * o).astype(o_ref.dtype)


def _memattn(z, mem2d, norm_w, w_kv, batch, seq, mem_tokens):
    tl = MEM_TL
    d = mem2d.shape[1]
    width = MEM_HEADS * MEM_HEAD_DIM
    wblk = width // LANE
    nl = seq // tl
    return pl.pallas_call(
        _memattn_kernel,
        grid=(batch, nl),
        in_specs=[pl.BlockSpec((tl, width), lambda b, i: (b * nl + i, CB_MQ // wblk)),
                  pl.BlockSpec((tl, width), lambda b, i: (b * nl + i, CB_MG // wblk)),
                  pl.BlockSpec((mem_tokens, d), lambda b, i: (b, 0)),
                  pl.BlockSpec((1, d), lambda b, i: (0, 0)),
                  pl.BlockSpec((d, 2 * width), lambda b, i: (0, 0), pipeline_mode=SINGLE_BUFFER)],
        out_specs=pl.BlockSpec((tl, width), lambda b, i: (b * nl + i, 0)),
        out_shape=jax.ShapeDtypeStruct((batch * seq, width), BF16),
        scratch_shapes=[pltpu.VMEM((d, 2 * width), BF16),
                        pltpu.VMEM((mem_tokens, 2 * width), BF16)],
        compiler_params=pltpu.CompilerParams(
            dimension_semantics=("arbitrary", "arbitrary"), vmem_limit_bytes=VMEM_LIMIT),
        name="mem_attn",
    )(z, z, mem2d, norm_w.reshape(1, d), w_kv)


def _merge_kernel(yr_ref, yd_ref, ym_ref, wr_ref, wd_ref, wm_ref, gr_ref, gd_ref, gm_ref, o_ref):
    j = pl.program_id(1)

    def branch(y_ref, w_ref, g_ref):
        up = jnp.dot(y_ref[...], w_ref[j], preferred_element_type=F32)
        return _sigmoid(g_ref[...].astype(F32)) * up

    merged = (branch(yr_ref, wr_ref, gr_ref) + branch(yd_ref, wd_ref, gd_ref)
              + branch(ym_ref, wm_ref, gm_ref))
    o_ref[...] = merged.astype(o_ref.dtype)


def _merge(y_ret, y_dsa, y_mem, w_ret, w_dsa, w_mem, z):
    tm = MERGE_TM
    m, kdim = y_ret.shape
    nj, _, tn = w_ret.shape
    gblk = tn // LANE
    yspec = pl.BlockSpec((tm, kdim), lambda i, j: (i, 0))
    wspec = pl.BlockSpec((nj, kdim, tn), lambda i, j: (0, 0, 0), pipeline_mode=SINGLE_BUFFER)
    gspec = lambda off: pl.BlockSpec((tm, tn), lambda i, j: (i, off // gblk + j))
    return pl.pallas_call(
        _merge_kernel,
        grid=(m // tm, nj),
        in_specs=[yspec, yspec, yspec, wspec, wspec, wspec,
                  gspec(CB_GRET), gspec(CB_GDSA), gspec(CB_GMEM)],
        out_specs=pl.BlockSpec((tm, tn), lambda i, j: (i, j)),
        out_shape=jax.ShapeDtypeStruct((m, nj * tn), BF16),
        compiler_params=pltpu.CompilerParams(
            dimension_semantics=("arbitrary", "arbitrary"), vmem_limit_bytes=VMEM_LIMIT),
        name="merge",
    )(y_ret, y_dsa, y_mem, w_ret, w_dsa, w_mem, z, z, z)


def _out_kernel(m_ref, w_ref, x_ref, nw_ref, o_ref):
    out = jnp.dot(m_ref[...], w_ref[...], preferred_element_type=F32)
    ms = jnp.mean(out * out, axis=-1, keepdims=True)
    o_ref[...] = x_ref[...] + out * lax.rsqrt(ms + EPS) * nw_ref[...]


def _out_proj(merged, w_o, x2d, post_w):
    tm = OUT_TM
    m, d = x2d.shape
    return pl.pallas_call(
        _out_kernel,
        grid=(m // tm,),
        in_specs=[pl.BlockSpec((tm, d), lambda i: (i, 0)),
                  pl.BlockSpec((d, d), lambda i: (0, 0), pipeline_mode=SINGLE_BUFFER),
                  pl.BlockSpec((tm, d), lambda i: (i, 0)),
                  pl.BlockSpec((1, d), lambda i: (0, 0))],
        out_specs=pl.BlockSpec((tm, d), lambda i: (i, 0)),
        out_shape=jax.ShapeDtypeStruct((m, d), F32),
        compiler_params=pltpu.CompilerParams(
            dimension_semantics=("arbitrary",), vmem_limit_bytes=VMEM_LIMIT),
        name="out_proj_postnorm",
    )(merged, w_o, x2d, post_w.reshape(1, d))


def _layer(x, mem, pre_norm_w, w_in, ret_gn_w, mem_norm_w, w_mem_kv,
           w_up_ret, w_up_dsa, w_up_mem, w_o, post_norm_w):
    batch, seq, d = x.shape
    mem_tokens = mem.shape[1]
    top_k = min(IDX_TOPK_MAX, seq // 4)
    assert w_in.shape == (d, Z_COLS + IDX_TAIL_COLS), w_in.shape
    assert seq % max(DSA_T, RET_C, MEM_TL) == 0 and seq < 2**15, seq
    assert (batch * seq) % max(MERGE_TM, OUT_TM) == 0 and top_k <= DSA_T, (batch, seq)
    x2d = x.reshape(batch * seq, d)

    w_in_t = w_in.T
    w_ikw = jnp.pad(w_in_t[IDX_TAIL_START:IDX_TAIL_START + IDX_TAIL_COLS],
                    ((0, LANE - IDX_TAIL_COLS), (0, 0))).astype(BF16)
    h, ikw = _prenorm(x2d, pre_norm_w, w_ikw)
    z = _project(h, w_in_t)

    log_g = jnp.log1p(-(2.0 ** (-5.0 - jnp.arange(RET_HEADS, dtype=F32))))
    y_ret, wb_ret, wb_dsa, wb_mem, wb_o = _retention(
        z, log_g, ret_gn_w, (w_up_ret, w_up_dsa, w_up_mem), w_o, batch, seq)
    y_dsa = _dsa(z, ikw, batch, seq, top_k)
    y_mem = _memattn(z, mem.reshape(batch * mem_tokens, d), mem_norm_w, w_mem_kv, batch, seq,
                     mem_tokens)

    merged = _merge(y_ret, y_dsa, y_mem, wb_ret, wb_dsa, wb_mem, z)
    out = _out_proj(merged, wb_o, x2d, post_norm_w)
    return out.reshape(batch, seq, d)


def kernel(x, mem, pre_norm_w, w_in, ret_gn_w, mem_norm_w, w_mem_kv, w_up_ret, w_up_dsa,
           w_up_mem, w_o, post_norm_w):
    for layer in range(w_in.shape[0]):
        x = _layer(x, mem, pre_norm_w[layer], w_in[layer], ret_gn_w[layer], mem_norm_w[layer],
                   w_mem_kv[layer], w_up_ret[layer], w_up_dsa[layer], w_up_mem[layer],
                   w_o[layer], post_norm_w[layer])
    return x
```

```python
import functools

import jax
import jax.numpy as jnp
from jax import lax
from jax.experimental import pallas as pl
from jax.experimental.pallas import tpu as pltpu

F32 = jnp.float32
BF16 = jnp.bfloat16
I32 = jnp.int32

LANE = 128
SUBLANES = 8
N_BRANCHES = 3
RET_HEADS = 8
HEAD_DIM = 128
DSA_HEADS = 8
IDX_HEADS = 16
IDX_DIM = 64
IDX_TOPK_MAX = 256
MEM_HEADS = 4
MEM_HEAD_DIM = 256
NEG_INF = -1e30
EPS = 1e-6
INT_MIN = -2**31

CB_RQ, CB_RK, CB_RV, CB_RG = 0, 8, 16, 24
CB_AQ, CB_AK, CB_AV, CB_AG = 32, 40, 48, 56
CB_IQ = 64
CB_MQ, CB_MG = 72, 80
CB_GRET, CB_GDSA, CB_GMEM = 88, 104, 120
N_COL_BLOCKS = 136
Z_COLS = N_COL_BLOCKS * LANE
IDX_TAIL_START = (CB_IQ + IDX_HEADS * IDX_DIM // LANE) * LANE
IDX_TAIL_COLS = IDX_DIM + IDX_HEADS

PROJ_TM = 2048
PROJ_TN = 1024
PROJ_ALIGNED_TILES = IDX_TAIL_START // PROJ_TN
PROJ_ROWS = 1024
PROJ_PREP_ROWS = 256
PRENORM_TM = 1024
MEM_TL = 1024
MERGE_TM, MERGE_TN = 1024, 1024
OUT_TM = 512
VMEM_LIMIT = 56 * 1024 * 1024
SINGLE_BUFFER = pl.Buffered(1)

NT = (((1,), (1,)), ((), ()))
TN = (((0,), (0,)), ((), ()))


def _sigmoid(v):
    return 0.5 * jnp.tanh(0.5 * v) + 0.5


def _silu(v):
    return v * _sigmoid(v)


def _prenorm_kernel(x_ref, w_ref, wikw_ref, h_ref, ikw_ref):
    x = x_ref[...]
    ms = jnp.mean(x * x, axis=-1, keepdims=True)
    h = (x * lax.rsqrt(ms + EPS) * w_ref[...]).astype(h_ref.dtype)
    h_ref[...] = h
    ikw_ref[...] = lax.dot_general(h, wikw_ref[...], NT,
                                   preferred_element_type=F32).astype(ikw_ref.dtype)


def _prenorm(x2d, w, w_ikw):
    m, d = x2d.shape
    tm = min(PRENORM_TM, m)
    return pl.pallas_call(
        _prenorm_kernel,
        grid=(m // tm,),
        in_specs=[pl.BlockSpec((tm, d), lambda i: (i, 0)),
                  pl.BlockSpec((1, d), lambda i: (0, 0)),
                  pl.BlockSpec((LANE, d), lambda i: (0, 0))],
        out_specs=[pl.BlockSpec((tm, d), lambda i: (i, 0)),
                   pl.BlockSpec((tm, LANE), lambda i: (i, 0))],
        out_shape=[jax.ShapeDtypeStruct((m, d), BF16),
                   jax.ShapeDtypeStruct((m, LANE), BF16)],
        compiler_params=pltpu.CompilerParams(dimension_semantics=("arbitrary",)),
        name="prenorm_idx",
    )(x2d, w.reshape(1, d), w_ikw)


def _proj_kernel(h_ref, w_ref, z_ref, wbf_ref):
    @pl.when(pl.program_id(1) == 0)
    def _():
        for r in range(0, w_ref.shape[0], PROJ_PREP_ROWS):
            wbf_ref[r:r + PROJ_PREP_ROWS, :] = w_ref[r:r + PROJ_PREP_ROWS, :].astype(BF16)

    w = wbf_ref[...]
    for r in range(0, h_ref.shape[0], PROJ_ROWS):
        z_ref[r:r + PROJ_ROWS, :] = lax.dot_general(
            h_ref[r:r + PROJ_ROWS, :], w, NT, preferred_element_type=F32).astype(z_ref.dtype)


def _project(h, w_in_t):
    m, d = h.shape
    tm = min(PROJ_TM, m)

    def w_rows(j, i):
        skip = jnp.where(j >= PROJ_ALIGNED_TILES, IDX_TAIL_COLS // SUBLANES, 0)
        return ((j * (PROJ_TN // SUBLANES) + skip) * SUBLANES, 0)

    return pl.pallas_call(
        _proj_kernel,
        grid=(Z_COLS // PROJ_TN, m // tm),
        in_specs=[pl.BlockSpec((tm, d), lambda j, i: (i, 0)),
                  pl.BlockSpec((pl.Element(PROJ_TN), pl.Element(d)), w_rows)],
        out_specs=pl.BlockSpec((tm, PROJ_TN), lambda j, i: (i, j)),
        out_shape=jax.ShapeDtypeStruct((m, Z_COLS), BF16),
        scratch_shapes=[pltpu.VMEM((PROJ_TN, d), BF16)],
        compiler_params=pltpu.CompilerParams(
            dimension_semantics=("arbitrary", "arbitrary"), vmem_limit_bytes=VMEM_LIMIT),
        name="in_proj",
    )(h, w_in_t)


RET_C = 256
RET_HEADS_PER_STEP = 4


def _retention_kernel(lg_ref, q_ref, k_ref, v_ref, g_ref, gw_ref, wr_ref, wd_ref, wm_ref,
                      o_ref, wrb_ref, wdb_ref, wmb_ref):
    for w_ref, wb_ref in ((wr_ref, wrb_ref), (wd_ref, wdb_ref), (wm_ref, wmb_ref)):
        for jj in range(wb_ref.shape[0]):
            wb_ref[jj] = w_ref[:, jj * MERGE_TN:(jj + 1) * MERGE_TN].astype(BF16)

    c = RET_C
    seq = q_ref.shape[0]
    scale = HEAD_DIM ** -0.5
    ri = lax.broadcasted_iota(I32, (c, c), 0)
    ci = lax.broadcasted_iota(I32, (c, c), 1)
    diff = (ri - ci).astype(F32)
    idx = lax.broadcasted_iota(I32, (c, 1), 0).astype(F32)

    for hh in range(RET_HEADS_PER_STEP):
        hs = slice(hh * HEAD_DIM, (hh + 1) * HEAD_DIM)
        lg = lg_ref[pl.program_id(1) * RET_HEADS_PER_STEP + hh]
        decay = jnp.where(diff >= 0, jnp.exp(lg * jnp.maximum(diff, 0.0)), 0.0) * scale
        q_dec = jnp.exp(lg * (idx + 1.0))
        k_dec = jnp.exp(lg * (c - 1.0 - idx)) * scale
        chunk_dec = jnp.exp(lg * jnp.full((1, HEAD_DIM), float(c), F32))
        gw = gw_ref[:, hs]
        state = jnp.zeros((HEAD_DIM, HEAD_DIM), F32)
        for i in range(seq // c):
            r = i * c
            qi = q_ref[pl.ds(r, c), hs]
            ki = k_ref[pl.ds(r, c), hs]
            vi = v_ref[pl.ds(r, c), hs]
            inner = lax.dot_general(qi, ki, NT, preferred_element_type=F32) * decay
            o = (jnp.dot(inner.astype(BF16), vi, preferred_element_type=F32)
                 + jnp.dot(qi, state.astype(BF16), preferred_element_type=F32) * q_dec)
            vs = (vi.astype(F32) * k_dec).astype(BF16)
            state = state * chunk_dec + lax.dot_general(ki, vs, TN, preferred_element_type=F32)
            mu = jnp.mean(o, axis=-1, keepdims=True)
            oc = o - mu
            var = jnp.mean(oc * oc, axis=-1, keepdims=True)
            on = oc * lax.rsqrt(var + 1e-5) * gw
            g = g_ref[pl.ds(r, c), hs].astype(F32)
            o_ref[pl.ds(r, c), hs] = (_silu(g) * on).astype(o_ref.dtype)


def _retention(z, log_g, gn_w, w_ups, batch, seq):
    hp = RET_HEADS_PER_STEP
    width = hp * HEAD_DIM
    hsteps = RET_HEADS // hp
    steps = batch * hsteps
    kdim, n = w_ups[0].shape
    up_rows, nj = kdim // steps, n // MERGE_TN
    assert up_rows % PACKED_ROWS == 0, (kdim, steps)
    blk = lambda off: pl.BlockSpec((seq, width), lambda b, h: (b, off // hp + h))
    upspec = pl.BlockSpec((up_rows, n), lambda b, h: (b * hsteps + h, 0))
    upbspec = pl.BlockSpec((nj, up_rows, MERGE_TN), lambda b, h: (0, b * hsteps + h, 0))
    return pl.pallas_call(
        _retention_kernel,
        grid=(batch, hsteps),
        in_specs=[pl.BlockSpec(memory_space=pltpu.SMEM),
                  blk(CB_RQ), blk(CB_RK), blk(CB_RV), blk(CB_RG),
                  pl.BlockSpec((1, width), lambda b, h: (0, h)),
                  upspec, upspec, upspec],
        out_specs=[pl.BlockSpec((seq, width), lambda b, h: (b, h)),
                   upbspec, upbspec, upbspec],
        out_shape=[jax.ShapeDtypeStruct((batch * seq, RET_HEADS * HEAD_DIM), BF16)]
        + [jax.ShapeDtypeStruct((nj, kdim, MERGE_TN), BF16)] * N_BRANCHES,
        compiler_params=pltpu.CompilerParams(
            dimension_semantics=("arbitrary", "arbitrary"), vmem_limit_bytes=VMEM_LIMIT),
        name="retention",
    )(log_g, z, z, z, z, gn_w.reshape(1, -1), *w_ups)


DSA_T = 256
I16 = jnp.int16
PACKED_ROWS = 2 * SUBLANES
VT_ROWS = HEAD_DIM + PACKED_ROWS


def _fold_rows(x, op, rows=SUBLANES):
    while x.shape[0] > rows:
        half = x.shape[0] // 2
        x = op(x[:half], x[half:])
    return x


def _for_tiles_grouped(nk, step):
    def quad(i, carry):
        for u in range(4):
            step(4 * i + u)
        return carry

    lax.fori_loop(0, nk >> 2, quad, 0)
    done = (nk >> 2) << 2

    @pl.when((nk & 2) == 2)
    def _():
        step(done)
        step(done + 1)

    @pl.when((nk & 1) == 1)
    def _():
        step(nk - 1)


def _fold_tiles_grouped(nk, step, carry):
    def quad(i, c):
        for u in range(4):
            c = step(4 * i + u, c)
        return c

    carry = lax.fori_loop(0, nk >> 2, quad, carry)
    done = (nk >> 2) << 2
    carry = lax.cond((nk & 2) == 2, lambda c: step(done + 1, step(done, c)), lambda c: c, carry)
    return lax.cond((nk & 1) == 1, lambda c: step(nk - 1, c), lambda c: c, carry)


ORD_NEG_INF = -2**31 + 0x7FFFFF


def _flip(v):
    return v ^ ((v >> 31) & 0x7FFFFFFF)


def _ordinal_to_f32(o):
    return lax.bitcast_convert_type(_flip(jnp.maximum(o, ORD_NEG_INF)), F32)


def _prefix_to_bf16_bits(p):
    return _flip(jnp.maximum(p, ORD_NEG_INF >> 16) << 16) & -65536


def _dsa_kernel(iq_ref, ikw_ref, aq_ref, ak_ref, av_ref, ag_ref, wo_ref, o_ref, wob_ref,
                ika_ref, ikb_ref, vt_ref, wt_ref, qs_ref, score_ref, score16_ref, bias_ref,
                logit_ref,
                m_ref, acc_ref, *, top_k):
    t = DSA_T
    qb = pl.program_id(1)
    nk = qb + 1
    q0 = pl.multiple_of(qb * t, t)
    nq = vt_ref.shape[0]

    wob_ref[...] = wo_ref[...].astype(BF16)

    @pl.when(qb == 0)
    def _():
        ikw = ikw_ref[...].astype(F32)
        lane = lax.broadcasted_iota(I32, ikw.shape, 1)
        a = jnp.where(lane < IDX_DIM, ikw, 0.0)
        ika_ref[...] = a.astype(BF16)
        ikb_ref[...] = pltpu.roll(a, IDX_DIM, axis=1).astype(BF16)

        def vt_tile(kt, carry):
            k0 = pl.multiple_of(kt * t, t)
            for h in range(DSA_HEADS):
                hs = slice(h * HEAD_DIM, (h + 1) * HEAD_DIM)
                vt_ref[kt, h, :HEAD_DIM, :] = av_ref[pl.ds(k0, t), hs].astype(F32).T.astype(BF16)
                vt_ref[kt, h, HEAD_DIM:, :] = jnp.ones((VT_ROWS - HEAD_DIM, t), BF16)
            return carry

        lax.fori_loop(0, nq, vt_tile, 0)

    wt_ref[...] = ikw_ref[pl.ds(q0, t), :].astype(F32).T

    key_row = lax.broadcasted_iota(I32, (t, t), 0)
    qry_col = lax.broadcasted_iota(I32, (t, t), 1)

    def score_tile(kt):
        k0 = pl.multiple_of(kt * t, t)
        ka = ika_ref[pl.ds(k0, t), :]
        kb = ikb_ref[pl.ds(k0, t), :]
        acc = jnp.zeros((t, t), F32)
        for j in range(IDX_HEADS // 2):
            qj = iq_ref[:, j * LANE:(j + 1) * LANE]
            for half, kx in enumerate((ka, kb)):
                hh = IDX_DIM + 2 * j + half
                s = lax.dot_general(kx, qj, NT, preferred_element_type=F32)
                acc = acc + jnp.maximum(s, 0.0) * wt_ref[hh:hh + 1, :]
        causal = (key_row + k0) <= (qry_col + q0)
        score = jnp.where(causal, acc, -jnp.inf)
        score_ref[kt] = score
        score16_ref[kt] = score.astype(BF16)

    _for_tiles_grouped(nk, score_tile)

    one16, zero16 = jnp.ones((), I16), jnp.zeros((), I16)

    def prefix_step(i, pfx):
        cand_pfx = pfx + lax.shift_left(jnp.int32(1), 15 - i)
        cand = lax.bitcast_convert_type(_prefix_to_bf16_bits(cand_pfx), F32)
        cand16 = jnp.broadcast_to(cand, (PACKED_ROWS, t)).astype(BF16)[:1]

        def count_tile(kt, cnt):
            ge = jnp.where(score16_ref[kt] >= cand16, one16, zero16)
            return cnt + _fold_rows(ge, jnp.add, PACKED_ROWS)

        cnt = _fold_tiles_grouped(nk, count_tile, jnp.zeros((PACKED_ROWS, t), I16))
        total = jnp.sum(cnt.astype(I32), axis=0, keepdims=True)
        return jnp.where(total >= top_k, cand_pfx, pfx)

    pfx = lax.fori_loop(0, 16, prefix_step, jnp.full((1, t), -2**15, I32))

    def bisect_step(i, state):
        lo, hi, kept = state
        mid = lo + ((hi - lo) >> 1)
        cand = _ordinal_to_f32(mid)

        def count_tile(kt, cnt):
            return cnt + _fold_rows(jnp.where(score_ref[kt] >= cand, 1, 0), jnp.add)

        cnt = _fold_tiles_grouped(nk, count_tile, jnp.zeros((SUBLANES, t), I32))
        total = jnp.sum(cnt, axis=0, keepdims=True)
        enough = total >= top_k
        return (jnp.where(enough, mid, lo), jnp.where(enough, hi, mid),
                jnp.where(enough, total, kept))

    lo, _, kept = lax.fori_loop(
        0, 17, bisect_step, (_flip(_prefix_to_bf16_bits(pfx - 1)),
                             _flip(_prefix_to_bf16_bits(pfx + 1)),
                             jnp.full((1, t), top_k + 1, I32)))
    thr = _ordinal_to_f32(lo)

    def bias_tile(kt, carry):
        bias_ref[kt] = jnp.where(score_ref[kt] >= thr, 0.0, NEG_INF)
        return carry

    lax.fori_loop(0, qb, bias_tile, 0)
    keep = jnp.logical_and(score_ref[qb] >= thr, key_row <= qry_col)
    bias_ref[qb] = jnp.where(keep, 0.0, NEG_INF)

    @pl.when(jnp.max(kept) > top_k)
    def _():
        strictly_lower = (qry_col < key_row).astype(BF16)

        def tie_tile(kt, carry):
            above, tied_before = carry
            k0 = pl.multiple_of(kt * t, t)
            causal = (key_row + k0) <= (qry_col + q0)
            score = score_ref[kt]
            gt = jnp.logical_and(score > thr, causal)
            eq = jnp.logical_and(score == thr, causal)
            eq01 = jnp.where(eq, 1.0, 0.0)
            tied_here = tied_before + jnp.dot(strictly_lower, eq01.astype(BF16),
                                              preferred_element_type=F32)
            logit_ref[0, kt] = jnp.where(eq, tied_here, -1.0)
            above = above + _fold_rows(jnp.where(gt, 1.0, 0.0), jnp.add)
            return above, tied_before + jnp.sum(eq01, axis=0, keepdims=True)

        above, _ = lax.fori_loop(0, nk, tie_tile,
                                 (jnp.zeros((SUBLANES, t), F32), jnp.zeros((1, t), F32)))
        room = top_k - jnp.sum(above, axis=0, keepdims=True)

        def rebias_tile(kt, carry):
            k0 = pl.multiple_of(kt * t, t)
            gt = jnp.logical_and(score_ref[kt] > thr, (key_row + k0) <= (qry_col + q0))
            rank = logit_ref[0, kt]
            keep = jnp.logical_or(gt, jnp.logical_and(rank >= 0.0, rank < room))
            bias_ref[kt] = jnp.where(keep, 0.0, NEG_INF)
            return carry

        lax.fori_loop(0, nk, rebias_tile, 0)

    m_ref[...] = jnp.full(m_ref.shape, NEG_INF, F32)
    acc_ref[...] = jnp.zeros(acc_ref.shape, F32)
    log2e = 1.4426950408889634
    qs_ref[...] = (aq_ref[...].astype(F32) * ((HEAD_DIM ** -0.5) * log2e)).astype(BF16)
    key_row_1 = lax.broadcasted_iota(I32, (t, LANE), 0)

    def logit_tile(kt):
        k0 = pl.multiple_of(kt * t, t)
        bias = bias_ref[kt]
        kpos = (key_row_1 + (k0 - q0)).astype(F32)
        for h in range(DSA_HEADS):
            slope = (2.0 ** (-8.0 * (h + 1) / DSA_HEADS)) * log2e
            hs = slice(h * HEAD_DIM, (h + 1) * HEAD_DIM)
            s = lax.dot_general(ak_ref[pl.ds(k0, t), hs], qs_ref[:, hs], NT,
                                preferred_element_type=F32)
            alibi = kpos * slope
            lg = s + jnp.concatenate([alibi] * (t // LANE), axis=1) + bias
            logit_ref[h, kt] = lg
            m_ref[h] = jnp.maximum(m_ref[h], _fold_rows(lg, jnp.maximum))

    _for_tiles_grouped(nk, logit_tile)

    for h in range(DSA_HEADS):
        m_ref[h] = jnp.broadcast_to(jnp.max(m_ref[h], axis=0, keepdims=True), (SUBLANES, t))

    def pv_tile(kt):
        for h in range(DSA_HEADS):
            p = jnp.exp2(logit_ref[h, kt] - m_ref[h][:1])
            acc_ref[h] = acc_ref[h] + jnp.dot(vt_ref[kt, h], p.astype(BF16),
                                              preferred_element_type=F32)

    _for_tiles_grouped(nk, pv_tile)

    for h in range(DSA_HEADS):
        hs = slice(h * HEAD_DIM, (h + 1) * HEAD_DIM)
        acc = acc_ref[h]
        o = (acc[:HEAD_DIM] / acc[HEAD_DIM:HEAD_DIM + 1]).T
        o_ref[:, hs] = (_silu(ag_ref[:, hs].astype(F32)) * o).astype(o_ref.dtype)


def _dsa(z, ikw, w_o, batch, seq, top_k):
    t = DSA_T
    nq = seq // t
    width = DSA_HEADS * HEAD_DIM
    wblk = width // LANE
    d = w_o.shape[0]
    o_rows = d // (batch * nq)
    assert o_rows % PACKED_ROWS == 0, (d, batch, nq)
    ospec = pl.BlockSpec((o_rows, d), lambda b, i: (b * nq + i, 0))
    qspec = lambda off: pl.BlockSpec((t, width), lambda b, i: (b * nq + i, off // wblk))
    kvspec = lambda off: pl.BlockSpec((seq, width), lambda b, i: (b, off // wblk))
    return pl.pallas_call(
        functools.partial(_dsa_kernel, top_k=top_k),
        grid=(batch, nq),
        in_specs=[qspec(CB_IQ),
                  pl.BlockSpec((seq, LANE), lambda b, i: (b, 0)),
                  qspec(CB_AQ), kvspec(CB_AK), kvspec(CB_AV), qspec(CB_AG), ospec],
        out_specs=[pl.BlockSpec((t, width), lambda b, i: (b * nq + i, 0)), ospec],
        out_shape=[jax.ShapeDtypeStruct((batch * seq, width), BF16),
                   jax.ShapeDtypeStruct((d, d), BF16)],
        scratch_shapes=[pltpu.VMEM((seq, LANE), BF16),
                        pltpu.VMEM((seq, LANE), BF16),
                        pltpu.VMEM((nq, DSA_HEADS, VT_ROWS, t), BF16),
                        pltpu.VMEM((LANE, t), F32),
                        pltpu.VMEM((t, width), BF16),
                        pltpu.VMEM((nq, t, t), F32),
                        pltpu.VMEM((nq, t, t), BF16),
                        pltpu.VMEM((nq, t, t), F32),
                        pltpu.VMEM((DSA_HEADS, nq, t, t), F32),
                        pltpu.VMEM((DSA_HEADS, SUBLANES, t), F32),
                        pltpu.VMEM((DSA_HEADS, VT_ROWS, t), F32)],
        compiler_params=pltpu.CompilerParams(
            dimension_semantics=("arbitrary", "arbitrary"), vmem_limit_bytes=VMEM_LIMIT),
        name="dsa",
    )(z, ikw, z, z, z, z, w_o)


def _memattn_kernel(q_ref, g_ref, mem_ref, nw_ref, w_ref, o_ref, wbf_ref, kv_ref):
    b = pl.program_id(0)
    i = pl.program_id(1)
    width = MEM_HEADS * MEM_HEAD_DIM

    @pl.when(jnp.logical_and(b == 0, i == 0))
    def _():
        for r in range(0, w_ref.shape[0], PROJ_PREP_ROWS):
            wbf_ref[r:r + PROJ_PREP_ROWS, :] = w_ref[r:r + PROJ_PREP_ROWS, :].astype(BF16)

    @pl.when(i == 0)
    def _():
        x = mem_ref[...]
        ms = jnp.mean(x * x, axis=-1, keepdims=True)
        hn = (x * lax.rsqrt(ms + EPS) * nw_ref[...]).astype(BF16)
        kv_ref[...] = jnp.dot(hn, wbf_ref[...], preferred_element_type=F32).astype(BF16)

    scale = MEM_HEAD_DIM ** -0.5
    for h in range(MEM_HEADS):
        hs = slice(h * MEM_HEAD_DIM, (h + 1) * MEM_HEAD_DIM)
        vs = slice(width + h * MEM_HEAD_DIM, width + (h + 1) * MEM_HEAD_DIM)
        s = lax.dot_general(q_ref[:, hs], kv_ref[:, hs], NT, preferred_element_type=F32) * scale
        p = jnp.exp(s - jnp.max(s, axis=-1, keepdims=True))
        l = jnp.sum(p, axis=-1, keepdims=True)
        o = jnp.dot(p.astype(BF16), kv_ref[:, vs], preferred_element_type=F32) / l
        o_ref[:, hs] = (_silu(g_ref[:, hs].astype(F32)) * o).astype(o_ref.dtype)


def _memattn(z, mem2d, norm_w, w_kv, batch, seq, mem_tokens):
    tl = MEM_TL
    d = mem2d.shape[1]
    width = MEM_HEADS * MEM_HEAD_DIM
    wblk = width // LANE
    nl = seq // tl
    return pl.pallas_call(
        _memattn_kernel,
        grid=(batch, nl),
        in_specs=[pl.BlockSpec((tl, width), lambda b, i: (b * nl + i, CB_MQ // wblk)),
                  pl.BlockSpec((tl, width), lambda b, i: (b * nl + i, CB_MG // wblk)),
                  pl.BlockSpec((mem_tokens, d), lambda b, i: (b, 0)),
                  pl.BlockSpec((1, d), lambda b, i: (0, 0)),
                  pl.BlockSpec((d, 2 * width), lambda b, i: (0, 0), pipeline_mode=SINGLE_BUFFER)],
        out_specs=pl.BlockSpec((tl, width), lambda b, i: (b * nl + i, 0)),
        out_shape=jax.ShapeDtypeStruct((batch * seq, width), BF16),
        scratch_shapes=[pltpu.VMEM((d, 2 * width), BF16),
                        pltpu.VMEM((mem_tokens, 2 * width), BF16)],
        compiler_params=pltpu.CompilerParams(
            dimension_semantics=("arbitrary", "arbitrary"), vmem_limit_bytes=VMEM_LIMIT),
        name="mem_attn",
    )(z, z, mem2d, norm_w.reshape(1, d), w_kv)


def _merge_kernel(yr_ref, yd_ref, ym_ref, wr_ref, wd_ref, wm_ref, gr_ref, gd_ref, gm_ref, o_ref):
    j = pl.program_id(1)

    def branch(y_ref, w_ref, g_ref):
        up = jnp.dot(y_ref[...], w_ref[j], preferred_element_type=F32)
        return _sigmoid(g_ref[...].astype(F32)) * up

    merged = (branch(yr_ref, wr_ref, gr_ref) + branch(yd_ref, wd_ref, gd_ref)
              + branch(ym_ref, wm_ref, gm_ref))
    o_ref[...] = merged.astype(o_ref.dtype)


def _merge(y_ret, y_dsa, y_mem, w_ret, w_dsa, w_mem, z):
    tm = MERGE_TM
    m, kdim = y_ret.shape
    nj, _, tn = w_ret.shape
    gblk = tn // LANE
    yspec = pl.BlockSpec((tm, kdim), lambda i, j: (i, 0))
    wspec = pl.BlockSpec((nj, kdim, tn), lambda i, j: (0, 0, 0), pipeline_mode=SINGLE_BUFFER)
    gspec = lambda off: pl.BlockSpec((tm, tn), lambda i, j: (i, off // gblk + j))
    return pl.pallas_call(
        _merge_kernel,
        grid=(m // tm, nj),
        in_specs=[yspec, yspec, yspec, wspec, wspec, wspec,
                  gspec(CB_GRET), gspec(CB_GDSA), gspec(CB_GMEM)],
        out_specs=pl.BlockSpec((tm, tn), lambda i, j: (i, j)),
        out_shape=jax.ShapeDtypeStruct((m, nj * tn), BF16),
        compiler_params=pltpu.CompilerParams(
            dimension_semantics=("arbitrary", "arbitrary"), vmem_limit_bytes=VMEM_LIMIT),
        name="merge",
    )(y_ret, y_dsa, y_mem, w_ret, w_dsa, w_mem, z, z, z)


def _out_kernel(m_ref, w_ref, x_ref, nw_ref, o_ref):
    out = jnp.dot(m_ref[...], w_ref[...], preferred_element_type=F32)
    ms = jnp.mean(out * out, axis=-1, keepdims=True)
    o_ref[...] = x_ref[...] + out * lax.rsqrt(ms + EPS) * nw_ref[...]


def _out_proj(merged, w_o, x2d, post_w):
    tm = OUT_TM
    m, d = x2d.shape
    return pl.pallas_call(
        _out_kernel,
        grid=(m // tm,),
        in_specs=[pl.BlockSpec((tm, d), lambda i: (i, 0)),
                  pl.BlockSpec((d, d), lambda i: (0, 0), pipeline_mode=SINGLE_BUFFER),
                  pl.BlockSpec((tm, d), lambda i: (i, 0)),
                  pl.BlockSpec((1, d), lambda i: (0, 0))],
        out_specs=pl.BlockSpec((tm, d), lambda i: (i, 0)),
        out_shape=jax.ShapeDtypeStruct((m, d), F32),
        compiler_params=pltpu.CompilerParams(
            dimension_semantics=("arbitrary",), vmem_limit_bytes=VMEM_LIMIT),
        name="out_proj_postnorm",
    )(merged, w_o, x2d, post_w.reshape(1, d))


def _layer(x, mem, pre_norm_w, w_in, ret_gn_w, mem_norm_w, w_mem_kv,
           w_up_ret, w_up_dsa, w_up_mem, w_o, post_norm_w):
    batch, seq, d = x.shape
    mem_tokens = mem.shape[1]
    top_k = min(IDX_TOPK_MAX, seq // 4)
    assert w_in.shape == (d, Z_COLS + IDX_TAIL_COLS), w_in.shape
    assert seq % max(DSA_T, RET_C, MEM_TL) == 0 and seq < 2**15, seq
    assert (batch * seq) % max(MERGE_TM, OUT_TM) == 0 and top_k <= DSA_T, (batch, seq)
    x2d = x.reshape(batch * seq, d)

    w_in_t = w_in.T
    w_ikw = jnp.pad(w_in_t[IDX_TAIL_START:IDX_TAIL_START + IDX_TAIL_COLS],
                    ((0, LANE - IDX_TAIL_COLS), (0, 0))).astype(BF16)
    h, ikw = _prenorm(x2d, pre_norm_w, w_ikw)
    z = _project(h, w_in_t)

    log_g = jnp.log1p(-(2.0 ** (-5.0 - jnp.arange(RET_HEADS, dtype=F32))))
    y_ret, wb_ret, wb_dsa, wb_mem = _retention(
        z, log_g, ret_gn_w, (w_up_ret, w_up_dsa, w_up_mem), batch, seq)
    y_dsa, wb_o = _dsa(z, ikw, w_o, batch, seq, top_k)
    y_mem = _memattn(z, mem.reshape(batch * mem_tokens, d), mem_norm_w, w_mem_kv, batch, seq,
                     mem_tokens)

    merged = _merge(y_ret, y_dsa, y_mem, wb_ret, wb_dsa, wb_mem, z)
    out = _out_proj(merged, wb_o, x2d, post_norm_w)
    return out.reshape(batch, seq, d)


def kernel(x, mem, pre_norm_w, w_in, ret_gn_w, mem_norm_w, w_mem_kv, w_up_ret, w_up_dsa,
           w_up_mem, w_o, post_norm_w):
    for layer in range(w_in.shape[0]):
        x = _layer(x, mem, pre_norm_w[layer], w_in[layer], ret_gn_w[layer], mem_norm_w[layer],
                   w_mem_kv[layer], w_up_ret[layer], w_up_dsa[layer], w_up_mem[layer],
                   w_o[layer], post_norm_w[layer])
    return x
```

```python
import functools

import jax
import jax.numpy as jnp
from jax import lax
from jax.experimental import pallas as pl
from jax.experimental.pallas import tpu as pltpu

F32 = jnp.float32
BF16 = jnp.bfloat16
I32 = jnp.int32

LANE = 128
SUBLANES = 8
N_BRANCHES = 3
RET_HEADS = 8
HEAD_DIM = 128
DSA_HEADS = 8
IDX_HEADS = 16
IDX_DIM = 64
IDX_TOPK_MAX = 256
MEM_HEADS = 4
MEM_HEAD_DIM = 256
NEG_INF = -1e30
EPS = 1e-6
INT_MIN = -2**31

CB_RQ, CB_RK, CB_RV, CB_RG = 0, 8, 16, 24
CB_AQ, CB_AK, CB_AV, CB_AG = 32, 40, 48, 56
CB_IQ = 64
CB_MQ, CB_MG = 72, 80
CB_GRET, CB_GDSA, CB_GMEM = 88, 104, 120
N_COL_BLOCKS = 136
Z_COLS = N_COL_BLOCKS * LANE
IDX_TAIL_START = (CB_IQ + IDX_HEADS * IDX_DIM // LANE) * LANE
IDX_TAIL_COLS = IDX_DIM + IDX_HEADS

PROJ_TM = 2048
PROJ_TN = 1024
PROJ_ALIGNED_TILES = IDX_TAIL_START // PROJ_TN
PROJ_ROWS = 1024
PROJ_PREP_ROWS = 256
PRENORM_TM = 1024
MEM_TL = 1024
MERGE_TM, MERGE_TN = 1024, 1024
OUT_TM = 512
VMEM_LIMIT = 56 * 1024 * 1024
SINGLE_BUFFER = pl.Buffered(1)

NT = (((1,), (1,)), ((), ()))
TN = (((0,), (0,)), ((), ()))


def _sigmoid(v):
    return 0.5 * jnp.tanh(0.5 * v) + 0.5


def _silu(v):
    return v * _sigmoid(v)


def _prenorm_kernel(x_ref, w_ref, wikw_ref, h_ref, ikw_ref):
    x = x_ref[...]
    ms = jnp.mean(x * x, axis=-1, keepdims=True)
    h = (x * lax.rsqrt(ms + EPS) * w_ref[...]).astype(h_ref.dtype)
    h_ref[...] = h
    ikw_ref[...] = lax.dot_general(h, wikw_ref[...], NT,
                                   preferred_element_type=F32).astype(ikw_ref.dtype)


def _prenorm(x2d, w, w_ikw):
    m, d = x2d.shape
    tm = min(PRENORM_TM, m)
    return pl.pallas_call(
        _prenorm_kernel,
        grid=(m // tm,),
        in_specs=[pl.BlockSpec((tm, d), lambda i: (i, 0)),
                  pl.BlockSpec((1, d), lambda i: (0, 0)),
                  pl.BlockSpec((LANE, d), lambda i: (0, 0))],
        out_specs=[pl.BlockSpec((tm, d), lambda i: (i, 0)),
                   pl.BlockSpec((tm, LANE), lambda i: (i, 0))],
        out_shape=[jax.ShapeDtypeStruct((m, d), BF16),
                   jax.ShapeDtypeStruct((m, LANE), BF16)],
        compiler_params=pltpu.CompilerParams(dimension_semantics=("arbitrary",)),
        name="prenorm_idx",
    )(x2d, w.reshape(1, d), w_ikw)


def _proj_kernel(h_ref, w_ref, z_ref, wbf_ref):
    @pl.when(pl.program_id(1) == 0)
    def _():
        for r in range(0, w_ref.shape[0], PROJ_PREP_ROWS):
            wbf_ref[r:r + PROJ_PREP_ROWS, :] = w_ref[r:r + PROJ_PREP_ROWS, :].astype(BF16)

    w = wbf_ref[...]
    for r in range(0, h_ref.shape[0], PROJ_ROWS):
        z_ref[r:r + PROJ_ROWS, :] = lax.dot_general(
            h_ref[r:r + PROJ_ROWS, :], w, NT, preferred_element_type=F32).astype(z_ref.dtype)


def _project(h, w_in_t):
    m, d = h.shape
    tm = min(PROJ_TM, m)

    def w_rows(j, i):
        skip = jnp.where(j >= PROJ_ALIGNED_TILES, IDX_TAIL_COLS // SUBLANES, 0)
        return ((j * (PROJ_TN // SUBLANES) + skip) * SUBLANES, 0)

    return pl.pallas_call(
        _proj_kernel,
        grid=(Z_COLS // PROJ_TN, m // tm),
        in_specs=[pl.BlockSpec((tm, d), lambda j, i: (i, 0)),
                  pl.BlockSpec((pl.Element(PROJ_TN), pl.Element(d)), w_rows)],
        out_specs=pl.BlockSpec((tm, PROJ_TN), lambda j, i: (i, j)),
        out_shape=jax.ShapeDtypeStruct((m, Z_COLS), BF16),
        scratch_shapes=[pltpu.VMEM((PROJ_TN, d), BF16)],
        compiler_params=pltpu.CompilerParams(
            dimension_semantics=("arbitrary", "arbitrary"), vmem_limit_bytes=VMEM_LIMIT),
        name="in_proj",
    )(h, w_in_t)


RET_C = 256
RET_HEADS_PER_STEP = 4


def _retention_kernel(lg_ref, q_ref, k_ref, v_ref, g_ref, gw_ref, wr_ref, wd_ref, wm_ref, wo_ref,
                      o_ref, wrb_ref, wdb_ref, wmb_ref, wob_ref):
    for w_ref, wb_ref in ((wr_ref, wrb_ref), (wd_ref, wdb_ref), (wm_ref, wmb_ref)):
        for jj in range(wb_ref.shape[0]):
            wb_ref[jj] = w_ref[:, jj * MERGE_TN:(jj + 1) * MERGE_TN].astype(BF16)
    wob_ref[...] = wo_ref[...].astype(BF16)

    c = RET_C
    seq = q_ref.shape[0]
    scale = HEAD_DIM ** -0.5
    ri = lax.broadcasted_iota(I32, (c, c), 0)
    ci = lax.broadcasted_iota(I32, (c, c), 1)
    diff = (ri - ci).astype(F32)
    idx = lax.broadcasted_iota(I32, (c, 1), 0).astype(F32)

    for hh in range(RET_HEADS_PER_STEP):
        hs = slice(hh * HEAD_DIM, (hh + 1) * HEAD_DIM)
        lg = lg_ref[pl.program_id(1) * RET_HEADS_PER_STEP + hh]
        decay = jnp.where(diff >= 0, jnp.exp(lg * jnp.maximum(diff, 0.0)), 0.0) * scale
        q_dec = jnp.exp(lg * (idx + 1.0))
        k_dec = jnp.exp(lg * (c - 1.0 - idx)) * scale
        chunk_dec = jnp.exp(lg * jnp.full((1, HEAD_DIM), float(c), F32))
        gw = gw_ref[:, hs]
        state = jnp.zeros((HEAD_DIM, HEAD_DIM), F32)
        for i in range(seq // c):
            r = i * c
            qi = q_ref[pl.ds(r, c), hs]
            ki = k_ref[pl.ds(r, c), hs]
            vi = v_ref[pl.ds(r, c), hs]
            inner = lax.dot_general(qi, ki, NT, preferred_element_type=F32) * decay
            o = (jnp.dot(inner.astype(BF16), vi, preferred_element_type=F32)
                 + jnp.dot(qi, state.astype(BF16), preferred_element_type=F32) * q_dec)
            vs = (vi.astype(F32) * k_dec).astype(BF16)
            state = state * chunk_dec + lax.dot_general(ki, vs, TN, preferred_element_type=F32)
            mu = jnp.mean(o, axis=-1, keepdims=True)
            oc = o - mu
            var = jnp.mean(oc * oc, axis=-1, keepdims=True)
            on = oc * lax.rsqrt(var + 1e-5) * gw
            g = g_ref[pl.ds(r, c), hs].astype(F32)
            o_ref[pl.ds(r, c), hs] = (_silu(g) * on).astype(o_ref.dtype)


def _retention(z, log_g, gn_w, w_ups, w_o, batch, seq):
    hp = RET_HEADS_PER_STEP
    width = hp * HEAD_DIM
    hsteps = RET_HEADS // hp
    steps = batch * hsteps
    kdim, n = w_ups[0].shape
    d = w_o.shape[0]
    up_rows, o_rows, nj = kdim // steps, d // steps, n // MERGE_TN
    assert up_rows % PACKED_ROWS == 0 and o_rows % PACKED_ROWS == 0, (kdim, d, steps)
    blk = lambda off: pl.BlockSpec((seq, width), lambda b, h: (b, off // hp + h))
    upspec = pl.BlockSpec((up_rows, n), lambda b, h: (b * hsteps + h, 0))
    upbspec = pl.BlockSpec((nj, up_rows, MERGE_TN), lambda b, h: (0, b * hsteps + h, 0))
    ospec = pl.BlockSpec((o_rows, d), lambda b, h: (b * hsteps + h, 0))
    return pl.pallas_call(
        _retention_kernel,
        grid=(batch, hsteps),
        in_specs=[pl.BlockSpec(memory_space=pltpu.SMEM),
                  blk(CB_RQ), blk(CB_RK), blk(CB_RV), blk(CB_RG),
                  pl.BlockSpec((1, width), lambda b, h: (0, h)),
                  upspec, upspec, upspec, ospec],
        out_specs=[pl.BlockSpec((seq, width), lambda b, h: (b, h)),
                   upbspec, upbspec, upbspec, ospec],
        out_shape=[jax.ShapeDtypeStruct((batch * seq, RET_HEADS * HEAD_DIM), BF16)]
        + [jax.ShapeDtypeStruct((nj, kdim, MERGE_TN), BF16)] * N_BRANCHES
        + [jax.ShapeDtypeStruct((d, d), BF16)],
        compiler_params=pltpu.CompilerParams(
            dimension_semantics=("arbitrary", "arbitrary"), vmem_limit_bytes=VMEM_LIMIT),
        name="retention",
    )(log_g, z, z, z, z, gn_w.reshape(1, -1), *w_ups, w_o)


DSA_T = 256
I16 = jnp.int16
PACKED_ROWS = 2 * SUBLANES
VT_ROWS = HEAD_DIM + PACKED_ROWS


def _fold_rows(x, op, rows=SUBLANES):
    while x.shape[0] > rows:
        half = x.shape[0] // 2
        x = op(x[:half], x[half:])
    return x


def _for_tiles_grouped(nk, step):
    def quad(i, carry):
        for u in range(4):
            step(4 * i + u)
        return carry

    lax.fori_loop(0, nk >> 2, quad, 0)
    done = (nk >> 2) << 2

    @pl.when((nk & 2) == 2)
    def _():
        step(done)
        step(done + 1)

    @pl.when((nk & 1) == 1)
    def _():
        step(nk - 1)


def _fold_tiles_grouped(nk, step, carry):
    def quad(i, c):
        for u in range(4):
            c = step(4 * i + u, c)
        return c

    carry = lax.fori_loop(0, nk >> 2, quad, carry)
    done = (nk >> 2) << 2
    carry = lax.cond((nk & 2) == 2, lambda c: step(done + 1, step(done, c)), lambda c: c, carry)
    return lax.cond((nk & 1) == 1, lambda c: step(nk - 1, c), lambda c: c, carry)


ORD_NEG_INF = -2**31 + 0x7FFFFF


def _flip(v):
    return v ^ ((v >> 31) & 0x7FFFFFFF)


def _ordinal_to_f32(o):
    return lax.bitcast_convert_type(_flip(jnp.maximum(o, ORD_NEG_INF)), F32)


def _prefix_to_bf16_bits(p):
    return _flip(jnp.maximum(p, ORD_NEG_INF >> 16) << 16) & -65536


def _dsa_kernel(iq_ref, ikw_ref, aq_ref, ak_ref, av_ref, ag_ref, o_ref,
                ika_ref, ikb_ref, vt_ref, wt_ref, qs_ref, score_ref, score16_ref, bias_ref,
                logit_ref,
                m_ref, acc_ref, *, top_k):
    t = DSA_T
    qb = pl.program_id(1)
    nk = qb + 1
    q0 = pl.multiple_of(qb * t, t)
    nq = vt_ref.shape[0]

    @pl.when(qb == 0)
    def _():
        ikw = ikw_ref[...].astype(F32)
        lane = lax.broadcasted_iota(I32, ikw.shape, 1)
        a = jnp.where(lane < IDX_DIM, ikw, 0.0)
        ika_ref[...] = a.astype(BF16)
        ikb_ref[...] = pltpu.roll(a, IDX_DIM, axis=1).astype(BF16)

        def vt_tile(kt, carry):
            k0 = pl.multiple_of(kt * t, t)
            for h in range(DSA_HEADS):
                hs = slice(h * HEAD_DIM, (h + 1) * HEAD_DIM)
                vt_ref[kt, h, :HEAD_DIM, :] = av_ref[pl.ds(k0, t), hs].astype(F32).T.astype(BF16)
                vt_ref[kt, h, HEAD_DIM:, :] = jnp.ones((VT_ROWS - HEAD_DIM, t), BF16)
            return carry

        lax.fori_loop(0, nq, vt_tile, 0)

    wt_ref[...] = ikw_ref[pl.ds(q0, t), :].astype(F32).T

    key_row = lax.broadcasted_iota(I32, (t, t), 0)
    qry_col = lax.broadcasted_iota(I32, (t, t), 1)

    def score_tile(kt):
        k0 = pl.multiple_of(kt * t, t)
        ka = ika_ref[pl.ds(k0, t), :]
        kb = ikb_ref[pl.ds(k0, t), :]
        acc = jnp.zeros((t, t), F32)
        for j in range(IDX_HEADS // 2):
            qj = iq_ref[:, j * LANE:(j + 1) * LANE]
            for half, kx in enumerate((ka, kb)):
                hh = IDX_DIM + 2 * j + half
                s = lax.dot_general(kx, qj, NT, preferred_element_type=F32)
                acc = acc + jnp.maximum(s, 0.0) * wt_ref[hh:hh + 1, :]
        causal = (key_row + k0) <= (qry_col + q0)
        score = jnp.where(causal, acc, -jnp.inf)
        score_ref[kt] = score
        score16_ref[kt] = score.astype(BF16)

    _for_tiles_grouped(nk, score_tile)

    one16, zero16 = jnp.ones((), I16), jnp.zeros((), I16)

    def prefix_step(i, pfx):
        cand_pfx = pfx + lax.shift_left(jnp.int32(1), 15 - i)
        cand = lax.bitcast_convert_type(_prefix_to_bf16_bits(cand_pfx), F32)
        cand16 = jnp.broadcast_to(cand, (PACKED_ROWS, t)).astype(BF16)[:1]

        def count_tile(kt, cnt):
            ge = jnp.where(score16_ref[kt] >= cand16, one16, zero16)
            return cnt + _fold_rows(ge, jnp.add, PACKED_ROWS)

        cnt = _fold_tiles_grouped(nk, count_tile, jnp.zeros((PACKED_ROWS, t), I16))
        total = jnp.sum(cnt.astype(I32), axis=0, keepdims=True)
        return jnp.where(total >= top_k, cand_pfx, pfx)

    pfx = lax.fori_loop(0, 16, prefix_step, jnp.full((1, t), -2**15, I32))

    def bisect_step(i, state):
        lo, hi, kept = state
        mid = lo + ((hi - lo) >> 1)
        cand = _ordinal_to_f32(mid)

        def count_tile(kt, cnt):
            return cnt + _fold_rows(jnp.where(score_ref[kt] >= cand, 1, 0), jnp.add)

        cnt = _fold_tiles_grouped(nk, count_tile, jnp.zeros((SUBLANES, t), I32))
        total = jnp.sum(cnt, axis=0, keepdims=True)
        enough = total >= top_k
        return (jnp.where(enough, mid, lo), jnp.where(enough, hi, mid),
                jnp.where(enough, total, kept))

    lo, _, kept = lax.fori_loop(
        0, 17, bisect_step, (_flip(_prefix_to_bf16_bits(pfx - 1)),
                             _flip(_prefix_to_bf16_bits(pfx + 1)),
                             jnp.full((1, t), top_k + 1, I32)))
    thr = _ordinal_to_f32(lo)

    def bias_tile(kt, carry):
        bias_ref[kt] = jnp.where(score_ref[kt] >= thr, 0.0, NEG_INF)
        return carry

    lax.fori_loop(0, qb, bias_tile, 0)
    keep = jnp.logical_and(score_ref[qb] >= thr, key_row <= qry_col)
    bias_ref[qb] = jnp.where(keep, 0.0, NEG_INF)

    @pl.when(jnp.max(kept) > top_k)
    def _():
        strictly_lower = (qry_col < key_row).astype(BF16)

        def tie_tile(kt, carry):
            above, tied_before = carry
            k0 = pl.multiple_of(kt * t, t)
            causal = (key_row + k0) <= (qry_col + q0)
            score = score_ref[kt]
            gt = jnp.logical_and(score > thr, causal)
            eq = jnp.logical_and(score == thr, causal)
            eq01 = jnp.where(eq, 1.0, 0.0)
            tied_here = tied_before + jnp.dot(strictly_lower, eq01.astype(BF16),
                                              preferred_element_type=F32)
            logit_ref[0, kt] = jnp.where(eq, tied_here, -1.0)
            above = above + _fold_rows(jnp.where(gt, 1.0, 0.0), jnp.add)
            return above, tied_before + jnp.sum(eq01, axis=0, keepdims=True)

        above, _ = lax.fori_loop(0, nk, tie_tile,
                                 (jnp.zeros((SUBLANES, t), F32), jnp.zeros((1, t), F32)))
        room = top_k - jnp.sum(above, axis=0, keepdims=True)

        def rebias_tile(kt, carry):
            k0 = pl.multiple_of(kt * t, t)
            gt = jnp.logical_and(score_ref[kt] > thr, (key_row + k0) <= (qry_col + q0))
            rank = logit_ref[0, kt]
            keep = jnp.logical_or(gt, jnp.logical_and(rank >= 0.0, rank < room))
            bias_ref[kt] = jnp.where(keep, 0.0, NEG_INF)
            return carry

        lax.fori_loop(0, nk, rebias_tile, 0)

    m_ref[...] = jnp.full(m_ref.shape, NEG_INF, F32)
    acc_ref[...] = jnp.zeros(acc_ref.shape, F32)
    log2e = 1.4426950408889634
    qs_ref[...] = (aq_ref[...].astype(F32) * ((HEAD_DIM ** -0.5) * log2e)).astype(BF16)
    key_row_1 = lax.broadcasted_iota(I32, (t, LANE), 0)

    def logit_tile(kt):
        k0 = pl.multiple_of(kt * t, t)
        bias = bias_ref[kt]
        kpos = (key_row_1 + (k0 - q0)).astype(F32)
        for h in range(DSA_HEADS):
            slope = (2.0 ** (-8.0 * (h + 1) / DSA_HEADS)) * log2e
            hs = slice(h * HEAD_DIM, (h + 1) * HEAD_DIM)
            s = lax.dot_general(ak_ref[pl.ds(k0, t), hs], qs_ref[:, hs], NT,
                                preferred_element_type=F32)
            alibi = kpos * slope
            lg = s + jnp.concatenate([alibi] * (t // LANE), axis=1) + bias
            logit_ref[h, kt] = lg
            m_ref[h] = jnp.maximum(m_ref[h], _fold_rows(lg, jnp.maximum))

    _for_tiles_grouped(nk, logit_tile)

    for h in range(DSA_HEADS):
        m_ref[h] = jnp.broadcast_to(jnp.max(m_ref[h], axis=0, keepdims=True), (SUBLANES, t))

    def pv_tile(kt):
        for h in range(DSA_HEADS):
            p = jnp.exp2(logit_ref[h, kt] - m_ref[h][:1])
            acc_ref[h] = acc_ref[h] + jnp.dot(vt_ref[kt, h], p.astype(BF16),
                                              preferred_element_type=F32)

    _for_tiles_grouped(nk, pv_tile)

    for h in range(DSA_HEADS):
        hs = slice(h * HEAD_DIM, (h + 1) * HEAD_DIM)
        acc = acc_ref[h]
        o = (acc[:HEAD_DIM] / acc[HEAD_DIM:HEAD_DIM + 1]).T
        o_ref[:, hs] = (_silu(ag_ref[:, hs].astype(F32)) * o).astype(o_ref.dtype)


def _dsa(z, ikw, batch, seq, top_k):
    t = DSA_T
    nq = seq // t
    width = DSA_HEADS * HEAD_DIM
    wblk = width // LANE
    qspec = lambda off: pl.BlockSpec((t, width), lambda b, i: (b * nq + i, off // wblk))
    kvspec = lambda off: pl.BlockSpec((seq, width), lambda b, i: (b, off // wblk))
    return pl.pallas_call(
        functools.partial(_dsa_kernel, top_k=top_k),
        grid=(batch, nq),
        in_specs=[qspec(CB_IQ),
                  pl.BlockSpec((seq, LANE), lambda b, i: (b, 0)),
                  qspec(CB_AQ), kvspec(CB_AK), kvspec(CB_AV), qspec(CB_AG)],
        out_specs=pl.BlockSpec((t, width), lambda b, i: (b * nq + i, 0)),
        out_shape=jax.ShapeDtypeStruct((batch * seq, width), BF16),
        scratch_shapes=[pltpu.VMEM((seq, LANE), BF16),
                        pltpu.VMEM((seq, LANE), BF16),
                        pltpu.VMEM((nq, DSA_HEADS, VT_ROWS, t), BF16),
                        pltpu.VMEM((LANE, t), F32),
                        pltpu.VMEM((t, width), BF16),
                        pltpu.VMEM((nq, t, t), F32),
                        pltpu.VMEM((nq, t, t), BF16),
                        pltpu.VMEM((nq, t, t), F32),
                        pltpu.VMEM((DSA_HEADS, nq, t, t), F32),
                        pltpu.VMEM((DSA_HEADS, SUBLANES, t), F32),
                        pltpu.VMEM((DSA_HEADS, VT_ROWS, t), F32)],
        compiler_params=pltpu.CompilerParams(
            dimension_semantics=("arbitrary", "arbitrary"), vmem_limit_bytes=VMEM_LIMIT),
        name="dsa",
    )(z, ikw, z, z, z, z)


def _memattn_kernel(q_ref, g_ref, mem_ref, nw_ref, w_ref, o_ref, wbf_ref, kv_ref):
    b = pl.program_id(0)
    i = pl.program_id(1)
    width = MEM_HEADS * MEM_HEAD_DIM

    @pl.when(jnp.logical_and(b == 0, i == 0))
    def _():
        for r in range(0, w_ref.shape[0], PROJ_PREP_ROWS):
            wbf_ref[r:r + PROJ_PREP_ROWS, :] = w_ref[r:r + PROJ_PREP_ROWS, :].astype(BF16)

    @pl.when(i == 0)
    def _():
        x = mem_ref[...]
        ms = jnp.mean(x * x, axis=-1, keepdims=True)
        hn = (x * lax.rsqrt(ms + EPS) * nw_ref[...]).astype(BF16)
        kv_ref[...] = jnp.dot(hn, wbf_ref[...], preferred_element_type=F32).astype(BF16)

    scale = MEM_HEAD_DIM ** -0.5
    for h in range(MEM_HEADS):
        hs = slice(h * MEM_HEAD_DIM, (h + 1) * MEM_HEAD_DIM)
        vs = slice(width + h * MEM_HEAD_DIM, width + (h + 1) * MEM_HEAD_DIM)
        s = lax.dot_general(q_ref[:, hs], kv_ref[:, hs], NT, preferred_element_type=F32) * scale
        p = jnp.exp(s - jnp.max(s, axis=-1, keepdims=True))
        l = jnp.sum(p, axis=-1, keepdims=True)
        o = jnp.dot(p.astype(BF16), kv_ref[:, vs], preferred_element_type=F32) / l
        o_ref[:, hs] = (_silu(g_ref[:, hs].astype(F32)) * o).astype(o_ref.dtype)


def _memattn(z, mem2d, norm_w, w_kv, batch, seq, mem_tokens):
    tl = MEM_TL
    d = mem2d.shape[1]
    width = MEM_HEADS * MEM_HEAD_DIM
    wblk = width // LANE
    nl = seq // tl
    return pl.pallas_call(
        _memattn_kernel,
        grid=(batch, nl),
        in_specs=[pl.BlockSpec((tl, width), lambda b, i: (b * nl + i, CB_MQ // wblk)),
                  pl.BlockSpec((tl, width), lambda b, i: (b * nl + i, CB_MG // wblk)),
                  pl.BlockSpec((mem_tokens, d), lambda b, i: (b, 0)),
                  pl.BlockSpec((1, d), lambda b, i: (0, 0)),
                  pl.BlockSpec((d, 2 * width), lambda b, i: (0, 0), pipeline_mode=SINGLE_BUFFER)],
        out_specs=pl.BlockSpec((tl, width), lambda b, i: (b * nl + i, 0)),
        out_shape=jax.ShapeDtypeStruct((batch * seq, width), BF16),
        scratch_shapes=[pltpu.VMEM((d, 2 * width), BF16),
                        pltpu.VMEM((mem_tokens, 2 * width), BF16)],
        compiler_params=pltpu.CompilerParams(
            dimension_semantics=("arbitrary", "arbitrary"), vmem_limit_bytes=VMEM_LIMIT),
        name="mem_attn",
    )(z, z, mem2d, norm_w.reshape(1, d), w_kv)


def _merge_kernel(yr_ref, yd_ref, ym_ref, wr_ref, wd_ref, wm_ref, gr_ref, gd_ref, gm_ref, o_ref):
    j = pl.program_id(1)

    def branch(y_ref, w_ref, g_ref):
        up = jnp.dot(y_ref[...], w_ref[j], preferred_element_type=F32)
        return _sigmoid(g_ref[...].astype(F32)) * up

    merged = (branch(yr_ref, wr_ref, gr_ref) + branch(yd_ref, wd_ref, gd_ref)
              + branch(ym_ref, wm_ref, gm_ref))
    o_ref[...] = merged.astype(o_ref.dtype)


def _merge(y_ret, y_dsa, y_mem, w_ret, w_dsa, w_mem, z):
    tm = MERGE_TM
    m, kdim = y_ret.shape
    nj, _, tn = w_ret.shape
    gblk = tn // LANE
    yspec = pl.BlockSpec((tm, kdim), lambda i, j: (i, 0))
    wspec = pl.BlockSpec((nj, kdim, tn), lambda i, j: (0, 0, 0), pipeline_mode=SINGLE_BUFFER)
    gspec = lambda off: pl.BlockSpec((tm, tn), lambda i, j: (i, off // gblk + j))
    return pl.pallas_call(
        _merge_kernel,
        grid=(m // tm, nj),
        in_specs=[yspec, yspec, yspec, wspec, wspec, wspec,
                  gspec(CB_GRET), gspec(CB_GDSA), gspec(CB_GMEM)],
        out_specs=pl.BlockSpec((tm, tn), lambda i, j: (i, j)),
        out_shape=jax.ShapeDtypeStruct((m, nj * tn), BF16),
        compiler_params=pltpu.CompilerParams(
            dimension_semantics=("arbitrary", "arbitrary"), vmem_limit_bytes=VMEM_LIMIT),
        name="merge",
    )(y_ret, y_dsa, y_mem, w_ret, w_dsa, w_mem, z, z, z)


def _out_kernel(m_ref, w_ref, x_ref, nw_ref, o_ref):
    half = m_ref.shape[0] // 2
    for r in (0, half):
        rows = pl.ds(r, half)
        out = jnp.dot(m_ref[rows, :], w_ref[...], preferred_element_type=F32)
        ms = jnp.mean(out * out, axis=-1, keepdims=True)
        o_ref[rows, :] = x_ref[rows, :] + out * lax.rsqrt(ms + EPS) * nw_ref[...]


def _out_proj(merged, w_o, x2d, post_w):
    tm = OUT_TM
    m, d = x2d.shape
    return pl.pallas_call(
        _out_kernel,
        grid=(m // tm,),
        in_specs=[pl.BlockSpec((tm, d), lambda i: (i, 0)),
                  pl.BlockSpec((d, d), lambda i: (0, 0), pipeline_mode=SINGLE_BUFFER),
                  pl.BlockSpec((tm, d), lambda i: (i, 0)),
                  pl.BlockSpec((1, d), lambda i: (0, 0))],
        out_specs=pl.BlockSpec((tm, d), lambda i: (i, 0)),
        out_shape=jax.ShapeDtypeStruct((m, d), F32),
        compiler_params=pltpu.CompilerParams(
            dimension_semantics=("arbitrary",), vmem_limit_bytes=VMEM_LIMIT),
        name="out_proj_postnorm",
    )(merged, w_o, x2d, post_w.reshape(1, d))


def _layer(x, mem, pre_norm_w, w_in, ret_gn_w, mem_norm_w, w_mem_kv,
           w_up_ret, w_up_dsa, w_up_mem, w_o, post_norm_w):
    batch, seq, d = x.shape
    mem_tokens = mem.shape[1]
    top_k = min(IDX_TOPK_MAX, seq // 4)
    assert w_in.shape == (d, Z_COLS + IDX_TAIL_COLS), w_in.shape
    assert seq % max(DSA_T, RET_C, MEM_TL) == 0 and seq < 2**15, seq
    assert (batch * seq) % max(MERGE_TM, OUT_TM) == 0 and top_k <= DSA_T, (batch, seq)
    x2d = x.reshape(batch * seq, d)

    w_in_t = w_in.T
    w_ikw = jnp.pad(w_in_t[IDX_TAIL_START:IDX_TAIL_START + IDX_TAIL_COLS],
                    ((0, LANE - IDX_TAIL_COLS), (0, 0))).astype(BF16)
    h, ikw = _prenorm(x2d, pre_norm_w, w_ikw)
    z = _project(h, w_in_t)

    log_g = jnp.log1p(-(2.0 ** (-5.0 - jnp.arange(RET_HEADS, dtype=F32))))
    y_ret, wb_ret, wb_dsa, wb_mem, wb_o = _retention(
        z, log_g, ret_gn_w, (w_up_ret, w_up_dsa, w_up_mem), w_o, batch, seq)
    y_dsa = _dsa(z, ikw, batch, seq, top_k)
    y_mem = _memattn(z, mem.reshape(batch * mem_tokens, d), mem_norm_w, w_mem_kv, batch, seq,
                     mem_tokens)

    merged = _merge(y_ret, y_dsa, y_mem, wb_ret, wb_dsa, wb_mem, z)
    out = _out_proj(merged, wb_o, x2d, post_norm_w)
    return out.reshape(batch, seq, d)


def kernel(x, mem, pre_norm_w, w_in, ret_gn_w, mem_norm_w, w_mem_kv, w_up_ret, w_up_dsa,
           w_up_mem, w_o, post_norm_w):
    for layer in range(w_in.shape[0]):
        x = _layer(x, mem, pre_norm_w[layer], w_in[layer], ret_gn_w[layer], mem_norm_w[layer],
                   w_mem_kv[layer], w_up_ret[layer], w_up_dsa[layer], w_up_mem[layer],
                   w_o[layer], post_norm_w[layer])
    return x
```

```python
import functools

import jax
import jax.numpy as jnp
from jax import lax
from jax.experimental import pallas as pl
from jax.experimental.pallas import tpu as pltpu

F32 = jnp.float32
BF16 = jnp.bfloat16
I32 = jnp.int32

LANE = 128
SUBLANES = 8
N_BRANCHES = 3
RET_HEADS = 8
HEAD_DIM = 128
DSA_HEADS = 8
IDX_HEADS = 16
IDX_DIM = 64
IDX_TOPK_MAX = 256
MEM_HEADS = 4
MEM_HEAD_DIM = 256
NEG_INF = -1e30
EPS = 1e-6
INT_MIN = -2**31

CB_RQ, CB_RK, CB_RV, CB_RG = 0, 8, 16, 24
CB_AQ, CB_AK, CB_AV, CB_AG = 32, 40, 48, 56
CB_IQ = 64
CB_MQ, CB_MG = 72, 80
CB_GRET, CB_GDSA, CB_GMEM = 88, 104, 120
N_COL_BLOCKS = 136
Z_COLS = N_COL_BLOCKS * LANE
IDX_TAIL_START = (CB_IQ + IDX_HEADS * IDX_DIM // LANE) * LANE
IDX_TAIL_COLS = IDX_DIM + IDX_HEADS

PROJ_TM = 2048
PROJ_TN = 1024
PROJ_ALIGNED_TILES = IDX_TAIL_START // PROJ_TN
PROJ_ROWS = 1024
PROJ_PREP_ROWS = 256
PRENORM_TM = 1024
MEM_TL = 1024
MERGE_TM, MERGE_TN = 1024, 1024
OUT_TM = 512
VMEM_LIMIT = 56 * 1024 * 1024
SINGLE_BUFFER = pl.Buffered(1)

NT = (((1,), (1,)), ((), ()))
TN = (((0,), (0,)), ((), ()))


def _sigmoid(v):
    return 0.5 * jnp.tanh(0.5 * v) + 0.5


def _silu(v):
    return v * _sigmoid(v)


def _prenorm_kernel(x_ref, w_ref, wikw_ref, h_ref, ikw_ref):
    x = x_ref[...]
    ms = jnp.mean(x * x, axis=-1, keepdims=True)
    h = (x * lax.rsqrt(ms + EPS) * w_ref[...]).astype(h_ref.dtype)
    h_ref[...] = h
    ikw_ref[...] = lax.dot_general(h, wikw_ref[...], NT,
                                   preferred_element_type=F32).astype(ikw_ref.dtype)


def _prenorm(x2d, w, w_ikw):
    m, d = x2d.shape
    tm = min(PRENORM_TM, m)
    return pl.pallas_call(
        _prenorm_kernel,
        grid=(m // tm,),
        in_specs=[pl.BlockSpec((tm, d), lambda i: (i, 0)),
                  pl.BlockSpec((1, d), lambda i: (0, 0)),
                  pl.BlockSpec((LANE, d), lambda i: (0, 0))],
        out_specs=[pl.BlockSpec((tm, d), lambda i: (i, 0)),
                   pl.BlockSpec((tm, LANE), lambda i: (i, 0))],
        out_shape=[jax.ShapeDtypeStruct((m, d), BF16),
                   jax.ShapeDtypeStruct((m, LANE), BF16)],
        compiler_params=pltpu.CompilerParams(dimension_semantics=("arbitrary",)),
        name="prenorm_idx",
    )(x2d, w.reshape(1, d), w_ikw)


def _proj_kernel(h_ref, w_ref, z_ref, wbf_ref):
    @pl.when(pl.program_id(1) == 0)
    def _():
        for r in range(0, w_ref.shape[0], PROJ_PREP_ROWS):
            wbf_ref[:, r:r + PROJ_PREP_ROWS] = w_ref[r:r + PROJ_PREP_ROWS, :].T.astype(BF16)

    w = wbf_ref[...]
    for r in range(0, h_ref.shape[0], PROJ_ROWS):
        z_ref[r:r + PROJ_ROWS, :] = jnp.dot(
            h_ref[r:r + PROJ_ROWS, :], w, preferred_element_type=F32).astype(z_ref.dtype)


def _project(h, w_in_t):
    m, d = h.shape
    tm = min(PROJ_TM, m)

    def w_rows(j, i):
        skip = jnp.where(j >= PROJ_ALIGNED_TILES, IDX_TAIL_COLS // SUBLANES, 0)
        return ((j * (PROJ_TN // SUBLANES) + skip) * SUBLANES, 0)

    return pl.pallas_call(
        _proj_kernel,
        grid=(Z_COLS // PROJ_TN, m // tm),
        in_specs=[pl.BlockSpec((tm, d), lambda j, i: (i, 0)),
                  pl.BlockSpec((pl.Element(PROJ_TN), pl.Element(d)), w_rows)],
        out_specs=pl.BlockSpec((tm, PROJ_TN), lambda j, i: (i, j)),
        out_shape=jax.ShapeDtypeStruct((m, Z_COLS), BF16),
        scratch_shapes=[pltpu.VMEM((d, PROJ_TN), BF16)],
        compiler_params=pltpu.CompilerParams(
            dimension_semantics=("arbitrary", "arbitrary"), vmem_limit_bytes=VMEM_LIMIT),
        name="in_proj",
    )(h, w_in_t)


RET_C = 256
RET_HEADS_PER_STEP = 4


def _retention_kernel(lg_ref, q_ref, k_ref, v_ref, g_ref, gw_ref, wr_ref, wd_ref, wm_ref, wo_ref,
                      o_ref, wrb_ref, wdb_ref, wmb_ref, wob_ref):
    for w_ref, wb_ref in ((wr_ref, wrb_ref), (wd_ref, wdb_ref), (wm_ref, wmb_ref)):
        for jj in range(wb_ref.shape[0]):
            wb_ref[jj] = w_ref[:, jj * MERGE_TN:(jj + 1) * MERGE_TN].astype(BF16)
    wob_ref[...] = wo_ref[...].astype(BF16)

    c = RET_C
    seq = q_ref.shape[0]
    scale = HEAD_DIM ** -0.5
    ri = lax.broadcasted_iota(I32, (c, c), 0)
    ci = lax.broadcasted_iota(I32, (c, c), 1)
    diff = (ri - ci).astype(F32)
    idx = lax.broadcasted_iota(I32, (c, 1), 0).astype(F32)

    for hh in range(RET_HEADS_PER_STEP):
        hs = slice(hh * HEAD_DIM, (hh + 1) * HEAD_DIM)
        lg = lg_ref[pl.program_id(1) * RET_HEADS_PER_STEP + hh]
        decay = jnp.where(diff >= 0, jnp.exp(lg * jnp.maximum(diff, 0.0)), 0.0) * scale
        q_dec = jnp.exp(lg * (idx + 1.0))
        k_dec = jnp.exp(lg * (c - 1.0 - idx)) * scale
        chunk_dec = jnp.exp(lg * jnp.full((1, HEAD_DIM), float(c), F32))
        gw = gw_ref[:, hs]
        state = jnp.zeros((HEAD_DIM, HEAD_DIM), F32)
        for i in range(seq // c):
            r = i * c
            qi = q_ref[pl.ds(r, c), hs]
            ki = k_ref[pl.ds(r, c), hs]
            vi = v_ref[pl.ds(r, c), hs]
            inner = lax.dot_general(qi, ki, NT, preferred_element_type=F32) * decay
            o = (jnp.dot(inner.astype(BF16), vi, preferred_element_type=F32)
                 + jnp.dot(qi, state.astype(BF16), preferred_element_type=F32) * q_dec)
            vs = (vi.astype(F32) * k_dec).astype(BF16)
            state = state * chunk_dec + lax.dot_general(ki, vs, TN, preferred_element_type=F32)
            mu = jnp.mean(o, axis=-1, keepdims=True)
            oc = o - mu
            var = jnp.mean(oc * oc, axis=-1, keepdims=True)
            on = oc * lax.rsqrt(var + 1e-5) * gw
            g = g_ref[pl.ds(r, c), hs].astype(F32)
            o_ref[pl.ds(r, c), hs] = (_silu(g) * on).astype(o_ref.dtype)


def _retention(z, log_g, gn_w, w_ups, w_o, batch, seq):
    hp = RET_HEADS_PER_STEP
    width = hp * HEAD_DIM
    hsteps = RET_HEADS // hp
    steps = batch * hsteps
    kdim, n = w_ups[0].shape
    d = w_o.shape[0]
    up_rows, o_rows, nj = kdim // steps, d // steps, n // MERGE_TN
    assert up_rows % PACKED_ROWS == 0 and o_rows % PACKED_ROWS == 0, (kdim, d, steps)
    blk = lambda off: pl.BlockSpec((seq, width), lambda b, h: (b, off // hp + h))
    upspec = pl.BlockSpec((up_rows, n), lambda b, h: (b * hsteps + h, 0))
    upbspec = pl.BlockSpec((nj, up_rows, MERGE_TN), lambda b, h: (0, b * hsteps + h, 0))
    ospec = pl.BlockSpec((o_rows, d), lambda b, h: (b * hsteps + h, 0))
    return pl.pallas_call(
        _retention_kernel,
        grid=(batch, hsteps),
        in_specs=[pl.BlockSpec(memory_space=pltpu.SMEM),
                  blk(CB_RQ), blk(CB_RK), blk(CB_RV), blk(CB_RG),
                  pl.BlockSpec((1, width), lambda b, h: (0, h)),
                  upspec, upspec, upspec, ospec],
        out_specs=[pl.BlockSpec((seq, width), lambda b, h: (b, h)),
                   upbspec, upbspec, upbspec, ospec],
        out_shape=[jax.ShapeDtypeStruct((batch * seq, RET_HEADS * HEAD_DIM), BF16)]
        + [jax.ShapeDtypeStruct((nj, kdim, MERGE_TN), BF16)] * N_BRANCHES
        + [jax.ShapeDtypeStruct((d, d), BF16)],
        compiler_params=pltpu.CompilerParams(
            dimension_semantics=("arbitrary", "arbitrary"), vmem_limit_bytes=VMEM_LIMIT),
        name="retention",
    )(log_g, z, z, z, z, gn_w.reshape(1, -1), *w_ups, w_o)


DSA_T = 256
I16 = jnp.int16
PACKED_ROWS = 2 * SUBLANES
VT_ROWS = HEAD_DIM + PACKED_ROWS


def _fold_rows(x, op, rows=SUBLANES):
    while x.shape[0] > rows:
        half = x.shape[0] // 2
        x = op(x[:half], x[half:])
    return x


def _for_tiles_grouped(nk, step):
    def quad(i, carry):
        for u in range(4):
            step(4 * i + u)
        return carry

    lax.fori_loop(0, nk >> 2, quad, 0)
    done = (nk >> 2) << 2

    @pl.when((nk & 2) == 2)
    def _():
        step(done)
        step(done + 1)

    @pl.when((nk & 1) == 1)
    def _():
        step(nk - 1)


def _fold_tiles_grouped(nk, step, carry):
    def quad(i, c):
        for u in range(4):
            c = step(4 * i + u, c)
        return c

    carry = lax.fori_loop(0, nk >> 2, quad, carry)
    done = (nk >> 2) << 2
    carry = lax.cond((nk & 2) == 2, lambda c: step(done + 1, step(done, c)), lambda c: c, carry)
    return lax.cond((nk & 1) == 1, lambda c: step(nk - 1, c), lambda c: c, carry)


ORD_NEG_INF = -2**31 + 0x7FFFFF


def _flip(v):
    return v ^ ((v >> 31) & 0x7FFFFFFF)


def _ordinal_to_f32(o):
    return lax.bitcast_convert_type(_flip(jnp.maximum(o, ORD_NEG_INF)), F32)


def _prefix_to_bf16_bits(p):
    return _flip(jnp.maximum(p, ORD_NEG_INF >> 16) << 16) & -65536


def _dsa_kernel(iq_ref, ikw_ref, aq_ref, ak_ref, av_ref, ag_ref, o_ref,
                ika_ref, ikb_ref, vt_ref, wt_ref, qs_ref, score_ref, score16_ref, bias_ref,
                logit_ref,
                m_ref, acc_ref, *, top_k):
    t = DSA_T
    qb = pl.program_id(1)
    nk = qb + 1
    q0 = pl.multiple_of(qb * t, t)
    nq = vt_ref.shape[0]

    @pl.when(qb == 0)
    def _():
        ikw = ikw_ref[...].astype(F32)
        lane = lax.broadcasted_iota(I32, ikw.shape, 1)
        a = jnp.where(lane < IDX_DIM, ikw, 0.0)
        ika_ref[...] = a.astype(BF16)
        ikb_ref[...] = pltpu.roll(a, IDX_DIM, axis=1).astype(BF16)

        def vt_tile(kt, carry):
            k0 = pl.multiple_of(kt * t, t)
            for h in range(DSA_HEADS):
                hs = slice(h * HEAD_DIM, (h + 1) * HEAD_DIM)
                vt_ref[kt, h, :HEAD_DIM, :] = av_ref[pl.ds(k0, t), hs].astype(F32).T.astype(BF16)
                vt_ref[kt, h, HEAD_DIM:, :] = jnp.ones((VT_ROWS - HEAD_DIM, t), BF16)
            return carry

        lax.fori_loop(0, nq, vt_tile, 0)

    wt_ref[...] = ikw_ref[pl.ds(q0, t), :].astype(F32).T

    key_row = lax.broadcasted_iota(I32, (t, t), 0)
    qry_col = lax.broadcasted_iota(I32, (t, t), 1)

    def score_tile(kt):
        k0 = pl.multiple_of(kt * t, t)
        ka = ika_ref[pl.ds(k0, t), :]
        kb = ikb_ref[pl.ds(k0, t), :]
        acc = jnp.zeros((t, t), F32)
        for j in range(IDX_HEADS // 2):
            qj = iq_ref[:, j * LANE:(j + 1) * LANE]
            for half, kx in enumerate((ka, kb)):
                hh = IDX_DIM + 2 * j + half
                s = lax.dot_general(kx, qj, NT, preferred_element_type=F32)
                acc = acc + jnp.maximum(s, 0.0) * wt_ref[hh:hh + 1, :]
        causal = (key_row + k0) <= (qry_col + q0)
        score = jnp.where(causal, acc, -jnp.inf)
        score_ref[kt] = score
        score16_ref[kt] = score.astype(BF16)

    _for_tiles_grouped(nk, score_tile)

    one16, zero16 = jnp.ones((), I16), jnp.zeros((), I16)

    def prefix_step(i, pfx):
        cand_pfx = pfx + lax.shift_left(jnp.int32(1), 15 - i)
        cand = lax.bitcast_convert_type(_prefix_to_bf16_bits(cand_pfx), F32)
        cand16 = jnp.broadcast_to(cand, (PACKED_ROWS, t)).astype(BF16)[:1]

        def count_tile(kt, cnt):
            ge = jnp.where(score16_ref[kt] >= cand16, one16, zero16)
            return cnt + _fold_rows(ge, jnp.add, PACKED_ROWS)

        cnt = _fold_tiles_grouped(nk, count_tile, jnp.zeros((PACKED_ROWS, t), I16))
        total = jnp.sum(cnt.astype(I32), axis=0, keepdims=True)
        return jnp.where(total >= top_k, cand_pfx, pfx)

    pfx = lax.fori_loop(0, 16, prefix_step, jnp.full((1, t), -2**15, I32))

    def bisect_step(i, state):
        lo, hi, kept = state
        mid = lo + ((hi - lo) >> 1)
        cand = _ordinal_to_f32(mid)

        def count_tile(kt, cnt):
            return cnt + _fold_rows(jnp.where(score_ref[kt] >= cand, 1, 0), jnp.add)

        cnt = _fold_tiles_grouped(nk, count_tile, jnp.zeros((SUBLANES, t), I32))
        total = jnp.sum(cnt, axis=0, keepdims=True)
        enough = total >= top_k
        return (jnp.where(enough, mid, lo), jnp.where(enough, hi, mid),
                jnp.where(enough, total, kept))

    lo, _, kept = lax.fori_loop(
        0, 17, bisect_step, (_flip(_prefix_to_bf16_bits(pfx - 1)),
                             _flip(_prefix_to_bf16_bits(pfx + 1)),
                             jnp.full((1, t), top_k + 1, I32)))
    thr = _ordinal_to_f32(lo)

    def bias_tile(kt, carry):
        bias_ref[kt] = jnp.where(score_ref[kt] >= thr, 0.0, NEG_INF)
        return carry

    lax.fori_loop(0, qb, bias_tile, 0)
    keep = jnp.logical_and(score_ref[qb] >= thr, key_row <= qry_col)
    bias_ref[qb] = jnp.where(keep, 0.0, NEG_INF)

    @pl.when(jnp.max(kept) > top_k)
    def _():
        strictly_lower = (qry_col < key_row).astype(BF16)

        def tie_tile(kt, carry):
            above, tied_before = carry
            k0 = pl.multiple_of(kt * t, t)
            causal = (key_row + k0) <= (qry_col + q0)
            score = score_ref[kt]
            gt = jnp.logical_and(score > thr, causal)
            eq = jnp.logical_and(score == thr, causal)
            eq01 = jnp.where(eq, 1.0, 0.0)
            tied_here = tied_before + jnp.dot(strictly_lower, eq01.astype(BF16),
                                              preferred_element_type=F32)
            logit_ref[0, kt] = jnp.where(eq, tied_here, -1.0)
            above = above + _fold_rows(jnp.where(gt, 1.0, 0.0), jnp.add)
            return above, tied_before + jnp.sum(eq01, axis=0, keepdims=True)

        above, _ = lax.fori_loop(0, nk, tie_tile,
                                 (jnp.zeros((SUBLANES, t), F32), jnp.zeros((1, t), F32)))
        room = top_k - jnp.sum(above, axis=0, keepdims=True)

        def rebias_tile(kt, carry):
            k0 = pl.multiple_of(kt * t, t)
            gt = jnp.logical_and(score_ref[kt] > thr, (key_row + k0) <= (qry_col + q0))
            rank = logit_ref[0, kt]
            keep = jnp.logical_or(gt, jnp.logical_and(rank >= 0.0, rank < room))
            bias_ref[kt] = jnp.where(keep, 0.0, NEG_INF)
            return carry

        lax.fori_loop(0, nk, rebias_tile, 0)

    m_ref[...] = jnp.full(m_ref.shape, NEG_INF, F32)
    acc_ref[...] = jnp.zeros(acc_ref.shape, F32)
    log2e = 1.4426950408889634
    qs_ref[...] = (aq_ref[...].astype(F32) * ((HEAD_DIM ** -0.5) * log2e)).astype(BF16)
    key_row_1 = lax.broadcasted_iota(I32, (t, LANE), 0)

    def logit_tile(kt):
        k0 = pl.multiple_of(kt * t, t)
        bias = bias_ref[kt]
        kpos = (key_row_1 + (k0 - q0)).astype(F32)
        for h in range(DSA_HEADS):
            slope = (2.0 ** (-8.0 * (h + 1) / DSA_HEADS)) * log2e
            hs = slice(h * HEAD_DIM, (h + 1) * HEAD_DIM)
            s = lax.dot_general(ak_ref[pl.ds(k0, t), hs], qs_ref[:, hs], NT,
                                preferred_element_type=F32)
            alibi = kpos * slope
            lg = s + jnp.concatenate([alibi] * (t // LANE), axis=1) + bias
            logit_ref[h, kt] = lg
            m_ref[h] = jnp.maximum(m_ref[h], _fold_rows(lg, jnp.maximum))

    _for_tiles_grouped(nk, logit_tile)

    for h in range(DSA_HEADS):
        m_ref[h] = jnp.broadcast_to(jnp.max(m_ref[h], axis=0, keepdims=True), (SUBLANES, t))

    def pv_tile(kt):
        for h in range(DSA_HEADS):
            p = jnp.exp2(logit_ref[h, kt] - m_ref[h][:1])
            acc_ref[h] = acc_ref[h] + jnp.dot(vt_ref[kt, h], p.astype(BF16),
                                              preferred_element_type=F32)

    _for_tiles_grouped(nk, pv_tile)

    for h in range(DSA_HEADS):
        hs = slice(h * HEAD_DIM, (h + 1) * HEAD_DIM)
        acc = acc_ref[h]
        o = (acc[:HEAD_DIM] / acc[HEAD_DIM:HEAD_DIM + 1]).T
        o_ref[:, hs] = (_silu(ag_ref[:, hs].astype(F32)) * o).astype(o_ref.dtype)


def _dsa(z, ikw, batch, seq, top_k):
    t = DSA_T
    nq = seq // t
    width = DSA_HEADS * HEAD_DIM
    wblk = width // LANE
    qspec = lambda off: pl.BlockSpec((t, width), lambda b, i: (b * nq + i, off // wblk))
    kvspec = lambda off: pl.BlockSpec((seq, width), lambda b, i: (b, off // wblk))
    return pl.pallas_call(
        functools.partial(_dsa_kernel, top_k=top_k),
        grid=(batch, nq),
        in_specs=[qspec(CB_IQ),
                  pl.BlockSpec((seq, LANE), lambda b, i: (b, 0)),
                  qspec(CB_AQ), kvspec(CB_AK), kvspec(CB_AV), qspec(CB_AG)],
        out_specs=pl.BlockSpec((t, width), lambda b, i: (b * nq + i, 0)),
        out_shape=jax.ShapeDtypeStruct((batch * seq, width), BF16),
        scratch_shapes=[pltpu.VMEM((seq, LANE), BF16),
                        pltpu.VMEM((seq, LANE), BF16),
                        pltpu.VMEM((nq, DSA_HEADS, VT_ROWS, t), BF16),
                        pltpu.VMEM((LANE, t), F32),
                        pltpu.VMEM((t, width), BF16),
                        pltpu.VMEM((nq, t, t), F32),
                        pltpu.VMEM((nq, t, t), BF16),
                        pltpu.VMEM((nq, t, t), F32),
                        pltpu.VMEM((DSA_HEADS, nq, t, t), F32),
                        pltpu.VMEM((DSA_HEADS, SUBLANES, t), F32),
                        pltpu.VMEM((DSA_HEADS, VT_ROWS, t), F32)],
        compiler_params=pltpu.CompilerParams(
            dimension_semantics=("arbitrary", "arbitrary"), vmem_limit_bytes=VMEM_LIMIT),
        name="dsa",
    )(z, ikw, z, z, z, z)


def _memattn_kernel(q_ref, g_ref, mem_ref, nw_ref, w_ref, o_ref, wbf_ref, kv_ref):
    b = pl.program_id(0)
    i = pl.program_id(1)
    width = MEM_HEADS * MEM_HEAD_DIM

    @pl.when(jnp.logical_and(b == 0, i == 0))
    def _():
        for r in range(0, w_ref.shape[0], PROJ_PREP_ROWS):
            wbf_ref[r:r + PROJ_PREP_ROWS, :] = w_ref[r:r + PROJ_PREP_ROWS, :].astype(BF16)

    @pl.when(i == 0)
    def _():
        x = mem_ref[...]
        ms = jnp.mean(x * x, axis=-1, keepdims=True)
        hn = (x * lax.rsqrt(ms + EPS) * nw_ref[...]).astype(BF16)
        kv_ref[...] = jnp.dot(hn, wbf_ref[...], preferred_element_type=F32).astype(BF16)

    scale = MEM_HEAD_DIM ** -0.5
    for h in range(MEM_HEADS):
        hs = slice(h * MEM_HEAD_DIM, (h + 1) * MEM_HEAD_DIM)
        vs = slice(width + h * MEM_HEAD_DIM, width + (h + 1) * MEM_HEAD_DIM)
        s = lax.dot_general(q_ref[:, hs], kv_ref[:, hs], NT, preferred_element_type=F32) * scale
        p = jnp.exp(s - jnp.max(s, axis=-1, keepdims=True))
        l = jnp.sum(p, axis=-1, keepdims=True)
        o = jnp.dot(p.astype(BF16), kv_ref[:, vs], preferred_element_type=F32) / l
        o_ref[:, hs] = (_silu(g_ref[:, hs].astype(F32)) * o).astype(o_ref.dtype)


def _memattn(z, mem2d, norm_w, w_kv, batch, seq, mem_tokens):
    tl = MEM_TL
    d = mem2d.shape[1]
    width = MEM_HEADS * MEM_HEAD_DIM
    wblk = width // LANE
    nl = seq // tl
    return pl.pallas_call(
        _memattn_kernel,
        grid=(batch, nl),
        in_specs=[pl.BlockSpec((tl, width), lambda b, i: (b * nl + i, CB_MQ // wblk)),
                  pl.BlockSpec((tl, width), lambda b, i: (b * nl + i, CB_MG // wblk)),
                  pl.BlockSpec((mem_tokens, d), lambda b, i: (b, 0)),
                  pl.BlockSpec((1, d), lambda b, i: (0, 0)),
                  pl.BlockSpec((d, 2 * width), lambda b, i: (0, 0), pipeline_mode=SINGLE_BUFFER)],
        out_specs=pl.BlockSpec((tl, width), lambda b, i: (b * nl + i, 0)),
        out_shape=jax.ShapeDtypeStruct((batch * seq, width), BF16),
        scratch_shapes=[pltpu.VMEM((d, 2 * width), BF16),
                        pltpu.VMEM((mem_tokens, 2 * width), BF16)],
        compiler_params=pltpu.CompilerParams(
            dimension_semantics=("arbitrary", "arbitrary"), vmem_limit_bytes=VMEM_LIMIT),
        name="mem_attn",
    )(z, z, mem2d, norm_w.reshape(1, d), w_kv)


def _merge_kernel(yr_ref, yd_ref, ym_ref, wr_ref, wd_ref, wm_ref, gr_ref, gd_ref, gm_ref, o_ref):
    j = pl.program_id(1)

    def branch(y_ref, w_ref, g_ref):
        up = jnp.dot(y_ref[...], w_ref[j], preferred_element_type=F32)
        return _sigmoid(g_ref[...].astype(F32)) * up

    merged = (branch(yr_ref, wr_ref, gr_ref) + branch(yd_ref, wd_ref, gd_ref)
              + branch(ym_ref, wm_ref, gm_ref))
    o_ref[...] = merged.astype(o_ref.dtype)


def _merge(y_ret, y_dsa, y_mem, w_ret, w_dsa, w_mem, z):
    tm = MERGE_TM
    m, kdim = y_ret.shape
    nj, _, tn = w_ret.shape
    gblk = tn // LANE
    yspec = pl.BlockSpec((tm, kdim), lambda i, j: (i, 0))
    wspec = pl.BlockSpec((nj, kdim, tn), lambda i, j: (0, 0, 0), pipeline_mode=SINGLE_BUFFER)
    gspec = lambda off: pl.BlockSpec((tm, tn), lambda i, j: (i, off // gblk + j))
    return pl.pallas_call(
        _merge_kernel,
        grid=(m // tm, nj),
        in_specs=[yspec, yspec, yspec, wspec, wspec, wspec,
                  gspec(CB_GRET), gspec(CB_GDSA), gspec(CB_GMEM)],
        out_specs=pl.BlockSpec((tm, tn), lambda i, j: (i, j)),
        out_shape=jax.ShapeDtypeStruct((m, nj * tn), BF16),
        compiler_params=pltpu.CompilerParams(
            dimension_semantics=("arbitrary", "arbitrary"), vmem_limit_bytes=VMEM_LIMIT),
        name="merge",
    )(y_ret, y_dsa, y_mem, w_ret, w_dsa, w_mem, z, z, z)


def _out_kernel(m_ref, w_ref, x_ref, nw_ref, o_ref):
    out = jnp.dot(m_ref[...], w_ref[...], preferred_element_type=F32)
    ms = jnp.mean(out * out, axis=-1, keepdims=True)
    o_ref[...] = x_ref[...] + out * lax.rsqrt(ms + EPS) * nw_ref[...]


def _out_proj(merged, w_o, x2d, post_w):
    tm = OUT_TM
    m, d = x2d.shape
    return pl.pallas_call(
        _out_kernel,
        grid=(m // tm,),
        in_specs=[pl.BlockSpec((tm, d), lambda i: (i, 0)),
                  pl.BlockSpec((d, d), lambda i: (0, 0), pipeline_mode=SINGLE_BUFFER),
                  pl.BlockSpec((tm, d), lambda i: (i, 0)),
                  pl.BlockSpec((1, d), lambda i: (0, 0))],
        out_specs=pl.BlockSpec((tm, d), lambda i: (i, 0)),
        out_shape=jax.ShapeDtypeStruct((m, d), F32),
        compiler_params=pltpu.CompilerParams(
            dimension_semantics=("arbitrary",), vmem_limit_bytes=VMEM_LIMIT),
        name="out_proj_postnorm",
    )(merged, w_o, x2d, post_w.reshape(1, d))


def _layer(x, mem, pre_norm_w, w_in, ret_gn_w, mem_norm_w, w_mem_kv,
           w_up_ret, w_up_dsa, w_up_mem, w_o, post_norm_w):
    batch, seq, d = x.shape
    mem_tokens = mem.shape[1]
    top_k = min(IDX_TOPK_MAX, seq // 4)
    assert w_in.shape == (d, Z_COLS + IDX_TAIL_COLS), w_in.shape
    assert seq % max(DSA_T, RET_C, MEM_TL) == 0 and seq < 2**15, seq
    assert (batch * seq) % max(MERGE_TM, OUT_TM) == 0 and top_k <= DSA_T, (batch, seq)
    x2d = x.reshape(batch * seq, d)

    w_in_t = w_in.T
    w_ikw = jnp.pad(w_in_t[IDX_TAIL_START:IDX_TAIL_START + IDX_TAIL_COLS],
                    ((0, LANE - IDX_TAIL_COLS), (0, 0))).astype(BF16)
    h, ikw = _prenorm(x2d, pre_norm_w, w_ikw)
    z = _project(h, w_in_t)

    log_g = jnp.log1p(-(2.0 ** (-5.0 - jnp.arange(RET_HEADS, dtype=F32))))
    y_ret, wb_ret, wb_dsa, wb_mem, wb_o = _retention(
        z, log_g, ret_gn_w, (w_up_ret, w_up_dsa, w_up_mem), w_o, batch, seq)
    y_dsa = _dsa(z, ikw, batch, seq, top_k)
    y_mem = _memattn(z, mem.reshape(batch * mem_tokens, d), mem_norm_w, w_mem_kv, batch, seq,
                     mem_tokens)

    merged = _merge(y_ret, y_dsa, y_mem, wb_ret, wb_dsa, wb_mem, z)
    out = _out_proj(merged, wb_o, x2d, post_norm_w)
    return out.reshape(batch, seq, d)


def kernel(x, mem, pre_norm_w, w_in, ret_gn_w, mem_norm_w, w_mem_kv, w_up_ret, w_up_dsa,
           w_up_mem, w_o, post_norm_w):
    for layer in range(w_in.shape[0]):
        x = _layer(x, mem, pre_norm_w[layer], w_in[layer], ret_gn_w[layer], mem_norm_w[layer],
                   w_mem_kv[layer], w_up_ret[layer], w_up_dsa[layer], w_up_mem[layer],
                   w_o[layer], post_norm_w[layer])
    return x
```

```python
import functools

import jax
import jax.numpy as jnp
from jax import lax
from jax.experimental import pallas as pl
from jax.experimental.pallas import tpu as pltpu

F32 = jnp.float32
BF16 = jnp.bfloat16
I32 = jnp.int32

LANE = 128
SUBLANES = 8
N_BRANCHES = 3
RET_HEADS = 8
HEAD_DIM = 128
DSA_HEADS = 8
IDX_HEADS = 16
IDX_DIM = 64
IDX_TOPK_MAX = 256
MEM_HEADS = 4
MEM_HEAD_DIM = 256
NEG_INF = -1e30
EPS = 1e-6
INT_MIN = -2**31

CB_RQ, CB_RK, CB_RV, CB_RG = 0, 8, 16, 24
CB_AQ, CB_AK, CB_AV, CB_AG = 32, 40, 48, 56
CB_IQ = 64
CB_MQ, CB_MG = 72, 80
CB_GRET, CB_GDSA, CB_GMEM = 88, 104, 120
N_COL_BLOCKS = 136
Z_COLS = N_COL_BLOCKS * LANE
IDX_TAIL_START = (CB_IQ + IDX_HEADS * IDX_DIM // LANE) * LANE
IDX_TAIL_COLS = IDX_DIM + IDX_HEADS

PROJ_TM = 2048
PROJ_TN = 1024
PROJ_ALIGNED_TILES = IDX_TAIL_START // PROJ_TN
PROJ_ROWS = 1024
PROJ_PREP_ROWS = 256
PRENORM_TM = 1024
MEM_TL = 1024
MERGE_TM, MERGE_TN = 1024, 1024
OUT_TM = 512
VMEM_LIMIT = 56 * 1024 * 1024
SINGLE_BUFFER = pl.Buffered(1)

NT = (((1,), (1,)), ((), ()))
TN = (((0,), (0,)), ((), ()))


def _sigmoid(v):
    return 0.5 * jnp.tanh(0.5 * v) + 0.5


def _silu(v):
    return v * _sigmoid(v)


def _prenorm_kernel(x_ref, w_ref, wikw_ref, h_ref, ikw_ref):
    x = x_ref[...]
    ms = jnp.mean(x * x, axis=-1, keepdims=True)
    h = (x * lax.rsqrt(ms + EPS) * w_ref[...]).astype(h_ref.dtype)
    h_ref[...] = h
    ikw_ref[...] = lax.dot_general(h, wikw_ref[...], NT,
                                   preferred_element_type=F32).astype(ikw_ref.dtype)


def _prenorm(x2d, w, w_ikw):
    m, d = x2d.shape
    tm = min(PRENORM_TM, m)
    return pl.pallas_call(
        _prenorm_kernel,
        grid=(m // tm,),
        in_specs=[pl.BlockSpec((tm, d), lambda i: (i, 0)),
                  pl.BlockSpec((1, d), lambda i: (0, 0)),
                  pl.BlockSpec((LANE, d), lambda i: (0, 0))],
        out_specs=[pl.BlockSpec((tm, d), lambda i: (i, 0)),
                   pl.BlockSpec((tm, LANE), lambda i: (i, 0))],
        out_shape=[jax.ShapeDtypeStruct((m, d), BF16),
                   jax.ShapeDtypeStruct((m, LANE), BF16)],
        compiler_params=pltpu.CompilerParams(dimension_semantics=("arbitrary",)),
        name="prenorm_idx",
    )(x2d, w.reshape(1, d), w_ikw)


def _proj_kernel(h_ref, w_ref, z_ref, wbf_ref):
    @pl.when(pl.program_id(1) == 0)
    def _():
        for r in range(0, w_ref.shape[0], PROJ_PREP_ROWS):
            wbf_ref[r:r + PROJ_PREP_ROWS, :] = w_ref[r:r + PROJ_PREP_ROWS, :].astype(BF16)

    w = wbf_ref[...]
    for r in range(0, h_ref.shape[0], PROJ_ROWS):
        z_ref[r:r + PROJ_ROWS, :] = lax.dot_general(
            h_ref[r:r + PROJ_ROWS, :], w, NT, preferred_element_type=F32).astype(z_ref.dtype)


def _project(h, w_in_t):
    m, d = h.shape
    tm = min(PROJ_TM, m)

    def w_rows(j, i):
        skip = jnp.where(j >= PROJ_ALIGNED_TILES, IDX_TAIL_COLS // SUBLANES, 0)
        return ((j * (PROJ_TN // SUBLANES) + skip) * SUBLANES, 0)

    return pl.pallas_call(
        _proj_kernel,
        grid=(Z_COLS // PROJ_TN, m // tm),
        in_specs=[pl.BlockSpec((tm, d), lambda j, i: (i, 0)),
                  pl.BlockSpec((pl.Element(PROJ_TN), pl.Element(d)), w_rows)],
        out_specs=pl.BlockSpec((tm, PROJ_TN), lambda j, i: (i, j)),
        out_shape=jax.ShapeDtypeStruct((m, Z_COLS), BF16),
        scratch_shapes=[pltpu.VMEM((PROJ_TN, d), BF16)],
        compiler_params=pltpu.CompilerParams(
            dimension_semantics=("arbitrary", "arbitrary"), vmem_limit_bytes=VMEM_LIMIT),
        name="in_proj",
    )(h, w_in_t)


RET_C = 256
RET_HEADS_PER_STEP = 4


def _retention_kernel(lg_ref, q_ref, k_ref, v_ref, g_ref, gw_ref, wr_ref, wd_ref, wm_ref, wo_ref,
                      o_ref, wrb_ref, wdb_ref, wmb_ref, wob_ref):
    for w_ref, wb_ref in ((wr_ref, wrb_ref), (wd_ref, wdb_ref), (wm_ref, wmb_ref)):
        for jj in range(wb_ref.shape[0]):
            wb_ref[jj] = w_ref[:, jj * MERGE_TN:(jj + 1) * MERGE_TN].astype(BF16)
    wob_ref[...] = wo_ref[...].astype(BF16)

    c = RET_C
    seq = q_ref.shape[0]
    scale = HEAD_DIM ** -0.5
    ri = lax.broadcasted_iota(I32, (c, c), 0)
    ci = lax.broadcasted_iota(I32, (c, c), 1)
    diff = (ri - ci).astype(F32)
    idx = lax.broadcasted_iota(I32, (c, 1), 0).astype(F32)

    for hh in range(RET_HEADS_PER_STEP):
        hs = slice(hh * HEAD_DIM, (hh + 1) * HEAD_DIM)
        lg = lg_ref[pl.program_id(1) * RET_HEADS_PER_STEP + hh]
        decay = jnp.where(diff >= 0, jnp.exp(lg * jnp.maximum(diff, 0.0)), 0.0) * scale
        q_dec = jnp.exp(lg * (idx + 1.0))
        k_dec = jnp.exp(lg * (c - 1.0 - idx)) * scale
        chunk_dec = jnp.exp(lg * jnp.full((1, HEAD_DIM), float(c), F32))
        gw = gw_ref[:, hs]
        state = jnp.zeros((HEAD_DIM, HEAD_DIM), F32)
        for i in range(seq // c):
            r = i * c
            qi = q_ref[pl.ds(r, c), hs]
            ki = k_ref[pl.ds(r, c), hs]
            vi = v_ref[pl.ds(r, c), hs]
            inner = lax.dot_general(qi, ki, NT, preferred_element_type=F32) * decay
            o = (jnp.dot(inner.astype(BF16), vi, preferred_element_type=F32)
                 + jnp.dot(qi, state.astype(BF16), preferred_element_type=F32) * q_dec)
            vs = (vi.astype(F32) * k_dec).astype(BF16)
            state = state * chunk_dec + lax.dot_general(ki, vs, TN, preferred_element_type=F32)
            mu = jnp.mean(o, axis=-1, keepdims=True)
            oc = o - mu
            var = jnp.mean(oc * oc, axis=-1, keepdims=True)
            on = oc * lax.rsqrt(var + 1e-5) * gw
            g = g_ref[pl.ds(r, c), hs].astype(F32)
            o_ref[pl.ds(r, c), hs] = (_silu(g) * on).astype(o_ref.dtype)


def _retention(z, log_g, gn_w, w_ups, w_o, batch, seq):
    hp = RET_HEADS_PER_STEP
    width = hp * HEAD_DIM
    hsteps = RET_HEADS // hp
    steps = batch * hsteps
    kdim, n = w_ups[0].shape
    d = w_o.shape[0]
    up_rows, o_rows, nj = kdim // steps, d // steps, n // MERGE_TN
    assert up_rows % PACKED_ROWS == 0 and o_rows % PACKED_ROWS == 0, (kdim, d, steps)
    blk = lambda off: pl.BlockSpec((seq, width), lambda b, h: (b, off // hp + h))
    upspec = pl.BlockSpec((up_rows, n), lambda b, h: (b * hsteps + h, 0))
    upbspec = pl.BlockSpec((nj, up_rows, MERGE_TN), lambda b, h: (0, b * hsteps + h, 0))
    ospec = pl.BlockSpec((o_rows, d), lambda b, h: (b * hsteps + h, 0))
    return pl.pallas_call(
        _retention_kernel,
        grid=(batch, hsteps),
        in_specs=[pl.BlockSpec(memory_space=pltpu.SMEM),
                  blk(CB_RQ), blk(CB_RK), blk(CB_RV), blk(CB_RG),
                  pl.BlockSpec((1, width), lambda b, h: (0, h)),
                  upspec, upspec, upspec, ospec],
        out_specs=[pl.BlockSpec((seq, width), lambda b, h: (b, h)),
                   upbspec, upbspec, upbspec, ospec],
        out_shape=[jax.ShapeDtypeStruct((batch * seq, RET_HEADS * HEAD_DIM), BF16)]
        + [jax.ShapeDtypeStruct((nj, kdim, MERGE_TN), BF16)] * N_BRANCHES
        + [jax.ShapeDtypeStruct((d, d), BF16)],
        compiler_params=pltpu.CompilerParams(
            dimension_semantics=("arbitrary", "arbitrary"), vmem_limit_bytes=VMEM_LIMIT),
        name="retention",
    )(log_g, z, z, z, z, gn_w.reshape(1, -1), *w_ups, w_o)


DSA_T = 256
I16 = jnp.int16
PACKED_ROWS = 2 * SUBLANES
VT_ROWS = HEAD_DIM + PACKED_ROWS


def _fold_rows(x, op, rows=SUBLANES):
    while x.shape[0] > rows:
        half = x.shape[0] // 2
        x = op(x[:half], x[half:])
    return x


def _for_tiles_grouped(nk, step):
    def quad(i, carry):
        for u in range(4):
            step(4 * i + u)
        return carry

    lax.fori_loop(0, nk >> 2, quad, 0)
    done = (nk >> 2) << 2

    @pl.when((nk & 2) == 2)
    def _():
        step(done)
        step(done + 1)

    @pl.when((nk & 1) == 1)
    def _():
        step(nk - 1)


def _fold_tiles_grouped(nk, step, carry):
    def quad(i, c):
        for u in range(4):
            c = step(4 * i + u, c)
        return c

    carry = lax.fori_loop(0, nk >> 2, quad, carry)
    done = (nk >> 2) << 2
    carry = lax.cond((nk & 2) == 2, lambda c: step(done + 1, step(done, c)), lambda c: c, carry)
    return lax.cond((nk & 1) == 1, lambda c: step(nk - 1, c), lambda c: c, carry)


ORD_NEG_INF = -2**31 + 0x7FFFFF


def _flip(v):
    return v ^ ((v >> 31) & 0x7FFFFFFF)


def _ordinal_to_f32(o):
    return lax.bitcast_convert_type(_flip(jnp.maximum(o, ORD_NEG_INF)), F32)


def _prefix_to_bf16_bits(p):
    return _flip(jnp.maximum(p, ORD_NEG_INF >> 16) << 16) & -65536


def _dsa_kernel(iq_ref, ikw_ref, aq_ref, ak_ref, av_ref, ag_ref, o_ref,
                ika_ref, ikb_ref, vt_ref, wt_ref, qs_ref, score_ref, score16_ref, bias_ref,
                logit_ref,
                m_ref, acc_ref, *, top_k):
    t = DSA_T
    qb = pl.program_id(1)
    nk = qb + 1
    q0 = pl.multiple_of(qb * t, t)
    nq = vt_ref.shape[0]

    @pl.when(qb == 0)
    def _():
        ikw = ikw_ref[...].astype(F32)
        lane = lax.broadcasted_iota(I32, ikw.shape, 1)
        a = jnp.where(lane < IDX_DIM, ikw, 0.0)
        ika_ref[...] = a.astype(BF16)
        ikb_ref[...] = pltpu.roll(a, IDX_DIM, axis=1).astype(BF16)

        def vt_tile(kt, carry):
            k0 = pl.multiple_of(kt * t, t)
            for h in range(DSA_HEADS):
                hs = slice(h * HEAD_DIM, (h + 1) * HEAD_DIM)
                vt_ref[kt, h, :HEAD_DIM, :] = av_ref[pl.ds(k0, t), hs].astype(F32).T.astype(BF16)
                vt_ref[kt, h, HEAD_DIM:, :] = jnp.ones((VT_ROWS - HEAD_DIM, t), BF16)
            return carry

        lax.fori_loop(0, nq, vt_tile, 0)

    wt_ref[...] = ikw_ref[pl.ds(q0, t), :].astype(F32).T

    key_row = lax.broadcasted_iota(I32, (t, t), 0)
    qry_col = lax.broadcasted_iota(I32, (t, t), 1)

    def score_tile(kt):
        k0 = pl.multiple_of(kt * t, t)
        ka = ika_ref[pl.ds(k0, t), :]
        kb = ikb_ref[pl.ds(k0, t), :]
        acc = jnp.zeros((t, t), F32)
        for j in range(IDX_HEADS // 2):
            qj = iq_ref[:, j * LANE:(j + 1) * LANE]
            for half, kx in enumerate((ka, kb)):
                hh = IDX_DIM + 2 * j + half
                s = lax.dot_general(kx, qj, NT, preferred_element_type=F32)
                acc = acc + jnp.maximum(s, 0.0) * wt_ref[hh:hh + 1, :]
        causal = (key_row + k0) <= (qry_col + q0)
        score = jnp.where(causal, acc, -jnp.inf)
        score_ref[kt] = score
        score16_ref[kt] = score.astype(BF16)

    _for_tiles_grouped(nk, score_tile)

    one16, zero16 = jnp.ones((), I16), jnp.zeros((), I16)

    def prefix_step(i, pfx):
        cand_pfx = pfx + lax.shift_left(jnp.int32(1), 15 - i)
        cand = lax.bitcast_convert_type(_prefix_to_bf16_bits(cand_pfx), F32)
        cand16 = jnp.broadcast_to(cand, (PACKED_ROWS, t)).astype(BF16)[:1]

        def count_tile(kt, cnt):
            ge = jnp.where(score16_ref[kt] >= cand16, one16, zero16)
            return cnt + _fold_rows(ge, jnp.add, PACKED_ROWS)

        cnt = _fold_tiles_grouped(nk, count_tile, jnp.zeros((PACKED_ROWS, t), I16))
        total = jnp.sum(cnt.astype(I32), axis=0, keepdims=True)
        return jnp.where(total >= top_k, cand_pfx, pfx)

    pfx = lax.fori_loop(0, 16, prefix_step, jnp.full((1, t), -2**15, I32))

    def bisect_step(i, state):
        lo, hi, kept = state
        mid = lo + ((hi - lo) >> 1)
        cand = _ordinal_to_f32(mid)

        def count_tile(kt, cnt):
            return cnt + _fold_rows(jnp.where(score_ref[kt] >= cand, 1, 0), jnp.add)

        cnt = _fold_tiles_grouped(nk, count_tile, jnp.zeros((SUBLANES, t), I32))
        total = jnp.sum(cnt, axis=0, keepdims=True)
        enough = total >= top_k
        return (jnp.where(enough, mid, lo), jnp.where(enough, hi, mid),
                jnp.where(enough, total, kept))

    lo, _, kept = lax.fori_loop(
        0, 17, bisect_step, (_flip(_prefix_to_bf16_bits(pfx - 1)),
                             _flip(_prefix_to_bf16_bits(pfx + 1)),
                             jnp.full((1, t), top_k + 1, I32)))
    thr = _ordinal_to_f32(lo)

    def bias_tile(kt, carry):
        bias_ref[kt] = jnp.where(score_ref[kt] >= thr, 0.0, NEG_INF)
        return carry

    lax.fori_loop(0, qb, bias_tile, 0)
    keep = jnp.logical_and(score_ref[qb] >= thr, key_row <= qry_col)
    bias_ref[qb] = jnp.where(keep, 0.0, NEG_INF)

    @pl.when(jnp.max(kept) > top_k)
    def _():
        strictly_lower = (qry_col < key_row).astype(BF16)

        def tie_tile(kt, carry):
            above, tied_before = carry
            k0 = pl.multiple_of(kt * t, t)
            causal = (key_row + k0) <= (qry_col + q0)
            score = score_ref[kt]
            gt = jnp.logical_and(score > thr, causal)
            eq = jnp.logical_and(score == thr, causal)
            eq01 = jnp.where(eq, 1.0, 0.0)
            tied_here = tied_before + jnp.dot(strictly_lower, eq01.astype(BF16),
                                              preferred_element_type=F32)
            logit_ref[0, kt] = jnp.where(eq, tied_here, -1.0)
            above = above + _fold_rows(jnp.where(gt, 1.0, 0.0), jnp.add)
            return above, tied_before + jnp.sum(eq01, axis=0, keepdims=True)

        above, _ = lax.fori_loop(0, nk, tie_tile,
                                 (jnp.zeros((SUBLANES, t), F32), jnp.zeros((1, t), F32)))
        room = top_k - jnp.sum(above, axis=0, keepdims=True)

        def rebias_tile(kt, carry):
            k0 = pl.multiple_of(kt * t, t)
            gt = jnp.logical_and(score_ref[kt] > thr, (key_row + k0) <= (qry_col + q0))
            rank = logit_ref[0, kt]
            keep = jnp.logical_or(gt, jnp.logical_and(rank >= 0.0, rank < room))
            bias_ref[kt] = jnp.where(keep, 0.0, NEG_INF)
            return carry

        lax.fori_loop(0, nk, rebias_tile, 0)

    m_ref[...] = jnp.full(m_ref.shape, NEG_INF, F32)
    acc_ref[...] = jnp.zeros(acc_ref.shape, F32)
    log2e = 1.4426950408889634
    qs_ref[...] = (aq_ref[...].astype(F32) * ((HEAD_DIM ** -0.5) * log2e)).astype(BF16)
    key_row_1 = lax.broadcasted_iota(I32, (t, LANE), 0)

    def logit_tile(kt):
        k0 = pl.multiple_of(kt * t, t)
        bias = bias_ref[kt]
        kpos = (key_row_1 + (k0 - q0)).astype(F32)
        for h in range(DSA_HEADS):
            slope = (2.0 ** (-8.0 * (h + 1) / DSA_HEADS)) * log2e
            hs = slice(h * HEAD_DIM, (h + 1) * HEAD_DIM)
            s = lax.dot_general(ak_ref[pl.ds(k0, t), hs], qs_ref[:, hs], NT,
                                preferred_element_type=F32)
            alibi = kpos * slope
            lg = s + jnp.concatenate([alibi] * (t // LANE), axis=1) + bias
            logit_ref[h, kt] = lg
            m_ref[h] = jnp.maximum(m_ref[h], _fold_rows(lg, jnp.maximum))

    _for_tiles_grouped(nk, logit_tile)

    for h in range(DSA_HEADS):
        m_ref[h] = jnp.broadcast_to(jnp.max(m_ref[h], axis=0, keepdims=True), (SUBLANES, t))

    def pv_tile(kt):
        for h in range(DSA_HEADS):
            p = jnp.exp2(logit_ref[h, kt] - m_ref[h][:1])
            acc_ref[h] = acc_ref[h] + jnp.dot(vt_ref[kt, h], p.astype(BF16),
                                              preferred_element_type=F32)

    _for_tiles_grouped(nk, pv_tile)

    for h in range(DSA_HEADS):
        hs = slice(h * HEAD_DIM, (h + 1) * HEAD_DIM)
        acc = acc_ref[h]
        o = (acc[:HEAD_DIM] / acc[HEAD_DIM:HEAD_DIM + 1]).T
        o_ref[:, hs] = (_silu(ag_ref[:, hs].astype(F32)) * o).astype(o_ref.dtype)


def _dsa(z, ikw, batch, seq, top_k):
    t = DSA_T
    nq = seq // t
    width = DSA_HEADS * HEAD_DIM
    wblk = width // LANE
    qspec = lambda off: pl.BlockSpec((t, width), lambda b, i: (b * nq + i, off // wblk))
    kvspec = lambda off: pl.BlockSpec((seq, width), lambda b, i: (b, off // wblk))
    return pl.pallas_call(
        functools.partial(_dsa_kernel, top_k=top_k),
        grid=(batch, nq),
        in_specs=[qspec(CB_IQ),
                  pl.BlockSpec((seq, LANE), lambda b, i: (b, 0)),
                  qspec(CB_AQ), kvspec(CB_AK), kvspec(CB_AV), qspec(CB_AG)],
        out_specs=pl.BlockSpec((t, width), lambda b, i: (b * nq + i, 0)),
        out_shape=jax.ShapeDtypeStruct((batch * seq, width), BF16),
        scratch_shapes=[pltpu.VMEM((seq, LANE), BF16),
                        pltpu.VMEM((seq, LANE), BF16),
                        pltpu.VMEM((nq, DSA_HEADS, VT_ROWS, t), BF16),
                        pltpu.VMEM((LANE, t), F32),
                        pltpu.VMEM((t, width), BF16),
                        pltpu.VMEM((nq, t, t), F32),
                        pltpu.VMEM((nq, t, t), BF16),
                        pltpu.VMEM((nq, t, t), F32),
                        pltpu.VMEM((DSA_HEADS, nq, t, t), F32),
                        pltpu.VMEM((DSA_HEADS, SUBLANES, t), F32),
                        pltpu.VMEM((DSA_HEADS, VT_ROWS, t), F32)],
        compiler_params=pltpu.CompilerParams(
            dimension_semantics=("arbitrary", "arbitrary"), vmem_limit_bytes=VMEM_LIMIT),
        name="dsa",
    )(z, ikw, z, z, z, z)


def _memattn_kernel(q_ref, g_ref, mem_ref, nw_ref, w_ref, o_ref, wbf_ref, kv_ref):
    b = pl.program_id(0)
    i = pl.program_id(1)
    width = MEM_HEADS * MEM_HEAD_DIM

    @pl.when(jnp.logical_and(b == 0, i == 0))
    def _():
        for r in range(0, w_ref.shape[0], PROJ_PREP_ROWS):
            wbf_ref[r:r + PROJ_PREP_ROWS, :] = w_ref[r:r + PROJ_PREP_ROWS, :].astype(BF16)

    @pl.when(i == 0)
    def _():
        x = mem_ref[...]
        ms = jnp.mean(x * x, axis=-1, keepdims=True)
        hn = (x * lax.rsqrt(ms + EPS) * nw_ref[...]).astype(BF16)
        kv_ref[...] = jnp.dot(hn, wbf_ref[...], preferred_element_type=F32).astype(BF16)

    scale_log2e = MEM_HEAD_DIM ** -0.5 * 1.4426950408889634
    for h in range(MEM_HEADS):
        hs = slice(h * MEM_HEAD_DIM, (h + 1) * MEM_HEAD_DIM)
        vs = slice(width + h * MEM_HEAD_DIM, width + (h + 1) * MEM_HEAD_DIM)
        s = lax.dot_general(q_ref[:, hs], kv_ref[:, hs], NT, preferred_element_type=F32)
        p = jnp.exp2((s - jnp.max(s, axis=-1, keepdims=True)) * scale_log2e)
        l = jnp.sum(p, axis=-1, keepdims=True)
        o = jnp.dot(p.astype(BF16), kv_ref[:, vs], preferred_element_type=F32) * (1.0 / l)
        o_ref[:, hs] = (_silu(g_ref[:, hs].astype(F32)) * o).astype(o_ref.dtype)


def _memattn(z, mem2d, norm_w, w_kv, batch, seq, mem_tokens):
    tl = MEM_TL
    d = mem2d.shape[1]
    width = MEM_HEADS * MEM_HEAD_DIM
    wblk = width // LANE
    nl = seq // tl
    return pl.pallas_call(
        _memattn_kernel,
        grid=(batch, nl),
        in_specs=[pl.BlockSpec((tl, width), lambda b, i: (b * nl + i, CB_MQ // wblk)),
                  pl.BlockSpec((tl, width), lambda b, i: (b * nl + i, CB_MG // wblk)),
                  pl.BlockSpec((mem_tokens, d), lambda b, i: (b, 0)),
                  pl.BlockSpec((1, d), lambda b, i: (0, 0)),
                  pl.BlockSpec((d, 2 * width), lambda b, i: (0, 0), pipeline_mode=SINGLE_BUFFER)],
        out_specs=pl.BlockSpec((tl, width), lambda b, i: (b * nl + i, 0)),
        out_shape=jax.ShapeDtypeStruct((batch * seq, width), BF16),
        scratch_shapes=[pltpu.VMEM((d, 2 * width), BF16),
                        pltpu.VMEM((mem_tokens, 2 * width), BF16)],
        compiler_params=pltpu.CompilerParams(
            dimension_semantics=("arbitrary", "arbitrary"), vmem_limit_bytes=VMEM_LIMIT),
        name="mem_attn",
    )(z, z, mem2d, norm_w.reshape(1, d), w_kv)


def _merge_kernel(yr_ref, yd_ref, ym_ref, wr_ref, wd_ref, wm_ref, gr_ref, gd_ref, gm_ref, o_ref):
    j = pl.program_id(1)

    def branch(y_ref, w_ref, g_ref):
        up = jnp.dot(y_ref[...], w_ref[j], preferred_element_type=F32)
        return _sigmoid(g_ref[...].astype(F32)) * up

    merged = (branch(yr_ref, wr_ref, gr_ref) + branch(yd_ref, wd_ref, gd_ref)
              + branch(ym_ref, wm_ref, gm_ref))
    o_ref[...] = merged.astype(o_ref.dtype)


def _merge(y_ret, y_dsa, y_mem, w_ret, w_dsa, w_mem, z):
    tm = MERGE_TM
    m, kdim = y_ret.shape
    nj, _, tn = w_ret.shape
    gblk = tn // LANE
    yspec = pl.BlockSpec((tm, kdim), lambda i, j: (i, 0))
    wspec = pl.BlockSpec((nj, kdim, tn), lambda i, j: (0, 0, 0), pipeline_mode=SINGLE_BUFFER)
    gspec = lambda off: pl.BlockSpec((tm, tn), lambda i, j: (i, off // gblk + j))
    return pl.pallas_call(
        _merge_kernel,
        grid=(m // tm, nj),
        in_specs=[yspec, yspec, yspec, wspec, wspec, wspec,
                  gspec(CB_GRET), gspec(CB_GDSA), gspec(CB_GMEM)],
        out_specs=pl.BlockSpec((tm, tn), lambda i, j: (i, j)),
        out_shape=jax.ShapeDtypeStruct((m, nj * tn), BF16),
        compiler_params=pltpu.CompilerParams(
            dimension_semantics=("arbitrary", "arbitrary"), vmem_limit_bytes=VMEM_LIMIT),
        name="merge",
    )(y_ret, y_dsa, y_mem, w_ret, w_dsa, w_mem, z, z, z)


def _out_kernel(m_ref, w_ref, x_ref, nw_ref, o_ref):
    out = jnp.dot(m_ref[...], w_ref[...], preferred_element_type=F32)
    ms = jnp.mean(out * out, axis=-1, keepdims=True)
    o_ref[...] = x_ref[...] + out * lax.rsqrt(ms + EPS) * nw_ref[...]


def _out_proj(merged, w_o, x2d, post_w):
    tm = OUT_TM
    m, d = x2d.shape
    return pl.pallas_call(
        _out_kernel,
        grid=(m // tm,),
        in_specs=[pl.BlockSpec((tm, d), lambda i: (i, 0)),
                  pl.BlockSpec((d, d), lambda i: (0, 0), pipeline_mode=SINGLE_BUFFER),
                  pl.BlockSpec((tm, d), lambda i: (i, 0)),
                  pl.BlockSpec((1, d), lambda i: (0, 0))],
        out_specs=pl.BlockSpec((tm, d), lambda i: (i, 0)),
        out_shape=jax.ShapeDtypeStruct((m, d), F32),
        compiler_params=pltpu.CompilerParams(
            dimension_semantics=("arbitrary",), vmem_limit_bytes=VMEM_LIMIT),
        name="out_proj_postnorm",
    )(merged, w_o, x2d, post_w.reshape(1, d))


def _layer(x, mem, pre_norm_w, w_in, ret_gn_w, mem_norm_w, w_mem_kv,
           w_up_ret, w_up_dsa, w_up_mem, w_o, post_norm_w):
    batch, seq, d = x.shape
    mem_tokens = mem.shape[1]
    top_k = min(IDX_TOPK_MAX, seq // 4)
    assert w_in.shape == (d, Z_COLS + IDX_TAIL_COLS), w_in.shape
    assert seq % max(DSA_T, RET_C, MEM_TL) == 0 and seq < 2**15, seq
    assert (batch * seq) % max(MERGE_TM, OUT_TM) == 0 and top_k <= DSA_T, (batch, seq)
    x2d = x.reshape(batch * seq, d)

    w_in_t = w_in.T
    w_ikw = jnp.pad(w_in_t[IDX_TAIL_START:IDX_TAIL_START + IDX_TAIL_COLS],
                    ((0, LANE - IDX_TAIL_COLS), (0, 0))).astype(BF16)
    h, ikw = _prenorm(x2d, pre_norm_w, w_ikw)
    z = _project(h, w_in_t)

    log_g = jnp.log1p(-(2.0 ** (-5.0 - jnp.arange(RET_HEADS, dtype=F32))))
    y_ret, wb_ret, wb_dsa, wb_mem, wb_o = _retention(
        z, log_g, ret_gn_w, (w_up_ret, w_up_dsa, w_up_mem), w_o, batch, seq)
    y_dsa = _dsa(z, ikw, batch, seq, top_k)
    y_mem = _memattn(z, mem.reshape(batch * mem_tokens, d), mem_norm_w, w_mem_kv, batch, seq,
                     mem_tokens)

    merged = _merge(y_ret, y_dsa, y_mem, wb_ret, wb_dsa, wb_mem, z)
    out = _out_proj(merged, wb_o, x2d, post_norm_w)
    return out.reshape(batch, seq, d)


def kernel(x, mem, pre_norm_w, w_in, ret_gn_w, mem_norm_w, w_mem_kv, w_up_ret, w_up_dsa,
           w_up_mem, w_o, post_norm_w):
    for layer in range(w_in.shape[0]):
        x = _layer(x, mem, pre_norm_w[layer], w_in[layer], ret_gn_w[layer], mem_norm_w[layer],
                   w_mem_kv[layer], w_up_ret[layer], w_up_dsa[layer], w_up_mem[layer],
                   w_o[layer], post_norm_w[layer])
    return x
```
